```python
import jax
import jax.numpy as jnp
from jax import lax
import numpy as np

D_MODEL = 2048
BATCH = 32
SEQ = 256
DEPTH = 2
DEC_BATCH = 2
DEC_SEQ = 1024
PAST_LEN = 256

GRID_W = 64
CHUNK = 64
CONV_K = 3
N_EVEN = (DEPTH + 1) // 2
N_ODD = DEPTH // 2
ALPHA = (2 * DEPTH) ** 0.25
BETA_INIT = (8 * DEPTH) ** -0.25
LN_EPS = 1e-5
RMS_EPS = 1e-6

H_A = 8
DK_A = 128
DV_A = 128
CONV_A = 2 * H_A * DK_A + H_A * DV_A
H_B = 4
DK_B = 128
DV_B = 256
EV_SIZES = (CONV_A, H_A * DV_A, H_B * DK_B, H_B * DK_B, H_B * DV_B, H_B * DV_B, 2 * H_A, 2 * H_A, 2 * H_B, 2 * H_B)
N_IN_EV = CONV_A + H_A * DV_A + 2 * H_B * DK_B + 2 * H_B * DV_B + 4 * H_A + 4 * H_B
EV_OUT = H_A * DV_A + H_B * DV_B

EXPAND = 2
D_INNER = EXPAND * D_MODEL
P_C = 64
H_C = D_INNER // P_C
N_C = 128
G_C = 8
CONV_C = D_INNER + 2 * G_C * N_C
N_IN_OD = 2 * D_INNER + 2 * G_C * N_C + 2 * H_C

N_EXP = 64
TOP_K = 8
N_GROUPS = 8
TOPK_GROUPS = 4
D_EXP = 512
D_SHARED = 512
ROUTED_SCALE = 2.5
BLK = 128

kernel_name = "hybrid_bidir_deltanet_mlstm_ssd_moe_step"


def split_points(sizes):
    pts, acc = [], 0
    for s in sizes[:-1]:
        acc += s
        pts.append(acc)
    return pts


def layer_norm(x, g, b):
    xf = x.astype(jnp.float32)
    mu = xf.mean(-1, keepdims=True)
    var = jnp.square(xf - mu).mean(-1, keepdims=True)
    return ((xf - mu) * lax.rsqrt(var + LN_EPS) * g + b).astype(x.dtype)


def rms_norm(x, g):
    xf = x.astype(jnp.float32)
    return xf * lax.rsqrt(jnp.square(xf).mean(-1, keepdims=True) + RMS_EPS) * g


def l2_normalize(x):
    return x * lax.rsqrt(jnp.square(x).sum(-1, keepdims=True) + RMS_EPS)


def swiglu(x, w_gate, w_up, w_down):
    return (jax.nn.silu(x @ w_gate) * (x @ w_up)) @ w_down


def flip_t(a):
    return jnp.flip(a, axis=1)


def to_chunks(a):
    b, l = a.shape[:2]
    return jnp.moveaxis(a.reshape(b, l // CHUNK, CHUNK, *a.shape[2:]), 1, 0)


def from_chunks(a):
    a = jnp.moveaxis(a, 0, 1)
    return a.reshape(a.shape[0], -1, *a.shape[3:])


def short_conv(u, conv_w, conv_b, grid):
    bsz, l, ch = u.shape
    seqs = u.reshape(bsz * (l // GRID_W), GRID_W, ch) if grid else u
    pad = CONV_K // 2
    up = jnp.pad(seqs, ((0, 0), (pad, pad), (0, 0)))
    n = seqs.shape[1]
    out = conv_b + up[:, 0:n] * conv_w[0]
    for t in range(1, CONV_K):
        out = out + up[:, t:t + n] * conv_w[t]
    return out.reshape(bsz, l, ch)


def gated_delta_scan(q, k, v, beta, g, s0):
    dv = v.shape[-1]
    causal = jnp.tril(jnp.ones((CHUNK, CHUNK), dtype=bool))
    strict = jnp.tril(jnp.ones((CHUNK, CHUNK), jnp.float32), -1)
    eye = jnp.eye(CHUNK, dtype=jnp.float32)

    def step(s, inp):
        qc, kc, vc, bc, gc = inp
        gcum = jnp.cumsum(gc, axis=1)
        gh = jnp.swapaxes(gcum, 1, 2)
        decay = jnp.exp(jnp.where(causal, gh[..., :, None] - gh[..., None, :], -jnp.inf))
        kb = kc * bc[..., None]
        lower = jnp.einsum('bihd,bjhd->bhij', kb, kc) * decay * strict
        rhs = jnp.swapaxes(jnp.concatenate([vc * bc[..., None], kb * jnp.exp(gcum)[..., None]], -1), 1, 2)
        sol = lax.linalg.triangular_solve(eye + lower, rhs, left_side=True, lower=True, unit_diagonal=True)
        u, w = sol[..., :dv], sol[..., dv:]
        v_new = u - jnp.einsum('bhck,bhkv->bhcv', w, s)
        attn = jnp.einsum('bihd,bjhd->bhij', qc, kc) * decay
        o = (jnp.einsum('bchk,bhkv->bhcv', qc * jnp.exp(gcum)[..., None], s)
             + jnp.einsum('bhij,bhjv->bhiv', attn, v_new))
        kd = kc * jnp.exp(gcum[:, -1:, :] - gcum)[..., None]
        s = s * jnp.exp(gh[..., -1])[..., None, None] + jnp.einsum('bchk,bhcv->bhkv', kd, v_new)
        return s, jnp.swapaxes(o, 1, 2)

    s, o = lax.scan(step, s0.astype(jnp.float32), tuple(to_chunks(a) for a in (q, k, v, beta, g)))
    return from_chunks(o), s


def mlstm_scan(q, k, v, log_i, log_f, c0, n0, m0):
    causal = jnp.tril(jnp.ones((CHUNK, CHUNK), dtype=bool))

    def step(carry, inp):
        cm, nv, m = carry
        qc, kc, vc, ic, fc = inp
        b = jnp.swapaxes(jnp.cumsum(fc, axis=1), 1, 2)
        ih = jnp.swapaxes(ic, 1, 2)
        dlog = jnp.where(causal, b[..., :, None] - b[..., None, :] + ih[..., None, :], -jnp.inf)
        inter = b + m[..., None]
        m_q = jnp.maximum(inter, dlog.max(-1))
        s = jnp.einsum('bihd,bjhd->bhij', qc, kc) * jnp.exp(dlog - m_q[..., None])
        w_inter = jnp.exp(inter - m_q)
        num = (w_inter[..., None] * jnp.einsum('bchk,bhkv->bhcv', qc, cm)
               + jnp.einsum('bhij,bjhv->bhiv', s, vc))
        den = w_inter * jnp.einsum('bchk,bhk->bhc', qc, nv) + s.sum(-1)
        h = num / jnp.maximum(jnp.abs(den), jnp.exp(-m_q))[..., None]
        b_last = b[..., -1]
        wlog = b_last[..., None] - b + ih
        m_new = jnp.maximum(b_last + m, wlog.max(-1))
        wk = jnp.exp(wlog - m_new[..., None])
        sc = jnp.exp(b_last + m - m_new)
        cm = sc[..., None, None] * cm + jnp.einsum('bhc,bchk,bchv->bhkv', wk, kc, vc)
        nv = sc[..., None] * nv + jnp.einsum('bhc,bchk->bhk', wk, kc)
        return (cm, nv, m_new), jnp.swapaxes(h, 1, 2)

    init = (c0.astype(jnp.float32), n0.astype(jnp.float32), m0.astype(jnp.float32))
    (cm, nv, m), h = lax.scan(step, init, tuple(to_chunks(a) for a in (q, k, v, log_i, log_f)))
    return from_chunks(h), cm, nv, m


def ssd_scan(x, dt, a, bm, cm, s0):
    bsz, l, h, p = x.shape
    g, n = bm.shape[2], bm.shape[3]
    hg = h // g
    ag = a.reshape(g, hg)
    causal5 = jnp.tril(jnp.ones((CHUNK, CHUNK), dtype=bool))[:, :, None, None]

    def step(s, inp):
        xc, dtc, bc, cc = inp
        cum = jnp.cumsum(dtc * ag, axis=1)
        seg = jnp.exp(jnp.where(causal5, cum[:, :, None] - cum[:, None], -jnp.inf))
        xdt = xc * dtc[..., None]
        scores = jnp.einsum('btgn,bsgn->btsg', cc, bc)[..., None] * seg
        y = (jnp.einsum('btsgh,bsghp->btghp', scores, xdt)
             + jnp.einsum('btgn,bghpn->btghp', cc, s) * jnp.exp(cum)[..., None])
        dend = jnp.exp(cum[:, -1:] - cum)
        s = s * jnp.exp(cum[:, -1])[..., None, None] + jnp.einsum('bsgn,bsghp->bghpn', bc, xdt * dend[..., None])
        return s, y

    inputs = (to_chunks(x.reshape(bsz, l, g, hg, p)), to_chunks(dt.reshape(bsz, l, g, hg)), to_chunks(bm), to_chunks(cm))
    s, y = lax.scan(step, s0.astype(jnp.float32).reshape(bsz, g, hg, p, n), inputs)
    return from_chunks(y).reshape(bsz, l, h, p), s.reshape(bsz, h, p, n)


def even_mixer(h, w_in, conv_w, conv_b, dn_a_log, dn_dt_bias, ml_b_i, ml_b_f, dn_norm, ml_norm, w_out,
               dn_state, ml_c, ml_n, ml_m, grid):
    f32 = jnp.float32
    bsz, l, _ = h.shape
    proj = h @ w_in
    qkv_a, z_a, q_b, k_b, v_b, o_b, beta_raw, alpha_raw, i_raw, f_raw = jnp.split(proj, split_points(EV_SIZES), axis=-1)
    qkv_a = jax.nn.silu(short_conv(qkv_a, conv_w, conv_b, grid)).astype(f32)
    q_a, k_a, v_a = jnp.split(qkv_a, [H_A * DK_A, 2 * H_A * DK_A], axis=-1)
    q_a = l2_normalize(q_a.reshape(bsz, l, H_A, DK_A)) * DK_A ** -0.5
    k_a = l2_normalize(k_a.reshape(bsz, l, H_A, DK_A))
    v_a = v_a.reshape(bsz, l, H_A, DV_A)
    beta = jax.nn.sigmoid(beta_raw.astype(f32).reshape(bsz, l, 2, H_A))
    gdec = -jnp.exp(dn_a_log.astype(f32)) * jax.nn.softplus(alpha_raw.astype(f32).reshape(bsz, l, 2, H_A) + dn_dt_bias)
    o_f, s_f = gated_delta_scan(q_a, k_a, v_a, beta[:, :, 0], gdec[:, :, 0], dn_state[:, 0])
    o_r, s_r = gated_delta_scan(flip_t(q_a), flip_t(k_a), flip_t(v_a), flip_t(beta[:, :, 1]), flip_t(gdec[:, :, 1]),
                                dn_state[:, 1])
    out_a = rms_norm(o_f + flip_t(o_r), dn_norm) * jax.nn.silu(z_a.astype(f32).reshape(bsz, l, H_A, DV_A))

    q_m = q_b.astype(f32).reshape(bsz, l, H_B, DK_B) * DK_B ** -0.5
    k_m = k_b.astype(f32).reshape(bsz, l, H_B, DK_B)
    v_m = v_b.astype(f32).reshape(bsz, l, H_B, DV_B)
    log_i = i_raw.astype(f32).reshape(bsz, l, 2, H_B) + ml_b_i
    log_f = jax.nn.log_sigmoid(f_raw.astype(f32).reshape(bsz, l, 2, H_B) + ml_b_f)
    h_f, c_f, n_f, m_f = mlstm_scan(q_m, k_m, v_m, log_i[:, :, 0], log_f[:, :, 0], ml_c[:, 0], ml_n[:, 0], ml_m[:, 0])
    h_r, c_r, n_r, m_r = mlstm_scan(flip_t(q_m), flip_t(k_m), flip_t(v_m), flip_t(log_i[:, :, 1]),
                                    flip_t(log_f[:, :, 1]), ml_c[:, 1], ml_n[:, 1], ml_m[:, 1])
    out_b = rms_norm(h_f + flip_t(h_r), ml_norm) * jax.nn.sigmoid(o_b.astype(f32).reshape(bsz, l, H_B, DV_B))

    mixed = jnp.concatenate([out_a.reshape(bsz, l, -1), out_b.reshape(bsz, l, -1)], axis=-1).astype(h.dtype)
    new_states = (jnp.stack([s_f, s_r], 1), jnp.stack([c_f, c_r], 1), jnp.stack([n_f, n_r], 1), jnp.stack([m_f, m_r], 1))
    return mixed @ w_out, new_states


def odd_mixer(h, w_in, conv_w, conv_b, a_log, dt_bias, d_skip, norm_g, w_out, ssd_state, grid):
    f32 = jnp.float32
    bsz, l, _ = h.shape
    z, xbc, dt_raw = jnp.split(h @ w_in, [D_INNER, D_INNER + CONV_C], axis=-1)
    xbc = jax.nn.silu(short_conv(xbc, conv_w, conv_b, grid)).astype(f32)
    xs, bm, cm = jnp.split(xbc, [D_INNER, D_INNER + G_C * N_C], axis=-1)
    xs = xs.reshape(bsz, l, H_C, P_C)
    bm = bm.reshape(bsz, l, G_C, N_C)
    cm = cm.reshape(bsz, l, G_C, N_C)
    dt = jax.nn.softplus(dt_raw.astype(f32).reshape(bsz, l, 2, H_C) + dt_bias)
    a = -jnp.exp(a_log.astype(f32))
    y_f, s_f = ssd_scan(xs, dt[:, :, 0], a[0], bm, cm, ssd_state[:, 0])
    y_r, s_r = ssd_scan(flip_t(xs), flip_t(dt[:, :, 1]), a[1], flip_t(bm), flip_t(cm), ssd_state[:, 1])
    y = y_f + flip_t(y_r) + d_skip[:, None] * xs
    y = y.reshape(bsz, l, G_C, D_INNER // G_C) * jax.nn.silu(z.astype(f32)).reshape(bsz, l, G_C, D_INNER // G_C)
    y = rms_norm(y, norm_g.reshape(G_C, D_INNER // G_C)).reshape(bsz, l, D_INNER).astype(h.dtype)
    return y @ w_out, (jnp.stack([s_f, s_r], 1),)


def routed_experts(x, top_e, wts, w_gate, w_up, w_down):
    t, _ = x.shape
    n_assign = t * TOP_K
    flat_e = top_e.reshape(-1)
    order = jnp.argsort(flat_e)
    se = flat_e[order]
    stok = (order // TOP_K).astype(jnp.int32)
    sw = wts.reshape(-1)[order]
    counts = jnp.zeros((N_EXP,), jnp.int32).at[flat_e].add(1)
    padded = (counts + BLK - 1) // BLK * BLK
    pad_end = jnp.cumsum(padded)
    pad_start = pad_end - padded
    grp_start = jnp.cumsum(counts) - counts
    dest = pad_start[se] + jnp.arange(n_assign, dtype=jnp.int32) - grp_start[se]
    n_blk = -(-n_assign // BLK) + N_EXP
    slot_tok = jnp.zeros((n_blk * BLK,), jnp.int32).at[dest].set(stok)
    slot_w = jnp.zeros((n_blk * BLK,), wts.dtype).at[dest].set(sw)
    blk_e = jnp.minimum(jnp.searchsorted(pad_end, jnp.arange(n_blk, dtype=jnp.int32) * BLK, side='right'), N_EXP - 1)

    def step(acc, inp):
        tok, w, e = inp
        yb = swiglu(x[tok], w_gate[e], w_up[e], w_down[e])
        return acc.at[tok].add((yb * w[:, None]).astype(acc.dtype)), None

    acc, _ = lax.scan(step, jnp.zeros_like(x), (slot_tok.reshape(n_blk, BLK), slot_w.reshape(n_blk, BLK), blk_e))
    return acc


def moe(h, router_w, router_bias, e_gate, e_up, e_down, s_gate, s_up, s_down):
    bsz, l, d = h.shape
    x = h.reshape(-1, d)
    t = x.shape[0]
    scores = jax.nn.sigmoid((x @ router_w).astype(jnp.float32))
    biased = scores + router_bias
    grp_score = lax.top_k(biased.reshape(t, N_GROUPS, N_EXP // N_GROUPS), 2)[0].sum(-1)
    _, top_g = lax.top_k(grp_score, TOPK_GROUPS)
    gmask = jax.nn.one_hot(top_g, N_GROUPS, dtype=jnp.float32).sum(1) > 0
    masked = jnp.where(jnp.repeat(gmask, N_EXP // N_GROUPS, axis=1), biased, -jnp.inf)
    _, top_e = lax.top_k(masked, TOP_K)
    sel = jnp.take_along_axis(scores, top_e, axis=1)
    wts = sel / sel.sum(-1, keepdims=True) * ROUTED_SCALE
    out = routed_experts(x, top_e, wts, e_gate, e_up, e_down) + swiglu(x, s_gate, s_up, s_down)
    return out.reshape(bsz, l, d)


def trunk_block(x, c_vec, mixer_fn, mixer_params, states, grid, mod_w, mod_b, ln1_g, ln1_b, ln2_g, ln2_b, moe_params):
    mod = (jax.nn.silu(c_vec) @ mod_w + mod_b)[:, None, :]
    sh1, sc1, g1, sh2, sc2, g2 = jnp.split(mod, 6, axis=-1)
    y, new_states = mixer_fn(x * (1 + sc1) + sh1, *mixer_params, *states, grid)
    x = layer_norm(ALPHA * x + g1 * y, ln1_g, ln1_b)
    y = moe(x * (1 + sc2) + sh2, *moe_params)
    x = layer_norm(ALPHA * x + g2 * y, ln2_g, ln2_b)
    return x, new_states


def setup_inputs(seed: int = 0) -> dict:
    key = jax.random.key(seed)
    ks = iter(jax.random.split(key, 64))
    f32 = jnp.float32

    def nrm(shape, scale):
        return scale * jax.random.normal(next(ks), shape, f32)

    def gain(shape):
        return 1.0 + nrm(shape, 0.02)

    def unif(shape, lo, hi):
        return jax.random.uniform(next(ks), shape, f32, lo, hi)

    def dt_bias(shape):
        return jnp.log(jnp.expm1(jnp.exp(unif(shape, np.log(1e-3), np.log(1e-1)))))

    d = D_MODEL
    return {
        'x_prompt': nrm((BATCH, SEQ, d), 1.0),
        'x_sample': nrm((DEC_BATCH, DEC_SEQ, d), 1.0),
        'state_dn': nrm((DEC_BATCH, N_EVEN, 2, H_A, DK_A, DV_A), 0.3),
        'state_ml_C': nrm((DEC_BATCH, N_EVEN, 2, H_B, DK_B, DV_B), 0.3),
        'state_ml_n': nrm((DEC_BATCH, N_EVEN, 2, H_B, DK_B), 0.3),
        'state_ml_m': nrm((DEC_BATCH, N_EVEN, 2, H_B), 1.0),
        'state_ssd': nrm((DEC_BATCH, N_ODD, 2, H_C, P_C, N_C), 0.3),
        'c': nrm((DEC_BATCH, d), 1.0),
        'c_ctx': nrm((d,), 1.0),
        'mod_w': nrm((DEPTH, d, 6 * d), d ** -0.5),
        'mod_b': nrm((DEPTH, 6 * d), 0.02),
        'ln1_g': gain((DEPTH, d)),
        'ln1_b': nrm((DEPTH, d), 0.02),
        'ln2_g': gain((DEPTH, d)),
        'ln2_b': nrm((DEPTH, d), 0.02),
        'router_w': nrm((DEPTH, d, N_EXP), d ** -0.5),
        'router_bias': nrm((DEPTH, N_EXP), 0.01),
        'exp_gate': nrm((DEPTH, N_EXP, d, D_EXP), d ** -0.5),
        'exp_up': nrm((DEPTH, N_EXP, d, D_EXP), d ** -0.5),
        'exp_down': nrm((DEPTH, N_EXP, D_EXP, d), D_EXP ** -0.5 * BETA_INIT),
        'sh_gate': nrm((DEPTH, d, D_SHARED), d ** -0.5),
        'sh_up': nrm((DEPTH, d, D_SHARED), d ** -0.5),
        'sh_down': nrm((DEPTH, D_SHARED, d), D_SHARED ** -0.5 * BETA_INIT),
        'ev_w_in': nrm((N_EVEN, d, N_IN_EV), d ** -0.5),
        'ev_conv_w': nrm((N_EVEN, CONV_K, CONV_A), CONV_K ** -0.5),
        'ev_conv_b': nrm((N_EVEN, CONV_A), 0.02),
        'dn_A_log': jnp.log(unif((N_EVEN, 2, H_A), 1.0, 16.0)),
        'dn_dt_bias': dt_bias((N_EVEN, 2, H_A)),
        'ml_b_i': nrm((N_EVEN, 2, H_B), 0.5) - 1.0,
        'ml_b_f': nrm((N_EVEN, 2, H_B), 0.5) + 3.0,
        'dn_norm': gain((N_EVEN, DV_A)),
        'ml_norm': gain((N_EVEN, DV_B)),
        'ev_w_out': nrm((N_EVEN, EV_OUT, d), EV_OUT ** -0.5 * BETA_INIT),
        'od_w_in': nrm((N_ODD, d, N_IN_OD), d ** -0.5),
        'od_conv_w': nrm((N_ODD, CONV_K, CONV_C), CONV_K ** -0.5),
        'od_conv_b': nrm((N_ODD, CONV_C), 0.02),
        'ssd_A_log': jnp.log(unif((N_ODD, 2, H_C), 1.0, 16.0)),
        'ssd_dt_bias': dt_bias((N_ODD, 2, H_C)),
        'ssd_D': gain((N_ODD, H_C)),
        'ssd_norm': gain((N_ODD, D_INNER)),
        'od_w_out': nrm((N_ODD, D_INNER, d), D_INNER ** -0.5 * BETA_INIT),
    }


def reference(x_prompt, x_sample, state_dn, state_ml_C, state_ml_n, state_ml_m, state_ssd, c, c_ctx,
              mod_w, mod_b, ln1_g, ln1_b, ln2_g, ln2_b, router_w, router_bias, exp_gate, exp_up, exp_down,
              sh_gate, sh_up, sh_down, ev_w_in, ev_conv_w, ev_conv_b, dn_A_log, dn_dt_bias, ml_b_i, ml_b_f,
              dn_norm, ml_norm, ev_w_out, od_w_in, od_conv_w, od_conv_b, ssd_A_log, ssd_dt_bias, ssd_D,
              ssd_norm, od_w_out):
    f32 = jnp.float32
    bp = x_prompt.shape[0]
    xp, xs = x_prompt, x_sample
    ev_states, od_states = [], []
    for l in range(DEPTH):
        j = l // 2
        if l % 2 == 0:
            mixer_fn = even_mixer
            mixer_params = (ev_w_in[j], ev_conv_w[j], ev_conv_b[j], dn_A_log[j], dn_dt_bias[j], ml_b_i[j], ml_b_f[j],
                            dn_norm[j], ml_norm[j], ev_w_out[j])
            st_ctx = (jnp.zeros((bp, 2, H_A, DK_A, DV_A), f32), jnp.zeros((bp, 2, H_B, DK_B, DV_B), f32),
                      jnp.zeros((bp, 2, H_B, DK_B), f32), jnp.zeros((bp, 2, H_B), f32))
            st_lat = (state_dn[:, j], state_ml_C[:, j], state_ml_n[:, j], state_ml_m[:, j])
        else:
            mixer_fn = odd_mixer
            mixer_params = (od_w_in[j], od_conv_w[j], od_conv_b[j], ssd_A_log[j], ssd_dt_bias[j], ssd_D[j],
                            ssd_norm[j], od_w_out[j])
            st_ctx = (jnp.zeros((bp, 2, H_C, P_C, N_C), f32),)
            st_lat = (state_ssd[:, j],)
        moe_params = (router_w[l], router_bias[l], exp_gate[l], exp_up[l], exp_down[l], sh_gate[l], sh_up[l], sh_down[l])
        norm_params = (mod_w[l], mod_b[l], ln1_g[l], ln1_b[l], ln2_g[l], ln2_b[l])
        xp, ctx_new = trunk_block(xp, c_ctx[None], mixer_fn, mixer_params, st_ctx, False, *norm_params, moe_params)
        xs, _ = trunk_block(xs, c, mixer_fn, mixer_params, st_lat, True, *norm_params, moe_params)
        if l % 2 == 0:
            ev_states.append(ctx_new)
        else:
            od_states.append(ctx_new)
    new_dn = jnp.stack([s[0] for s in ev_states], axis=1)
    new_ml_C = jnp.stack([s[1] for s in ev_states], axis=1)
    new_ml_n = jnp.stack([s[2] for s in ev_states], axis=1)
    new_ml_m = jnp.stack([s[3] for s in ev_states], axis=1)
    new_ssd = jnp.stack([s[0] for s in od_states], axis=1)
    return (xp, xs, new_dn, new_ml_C, new_ml_n, new_ml_m, new_ssd)
```

```python
import functools

import jax
import jax.numpy as jnp
from jax import lax
from jax.experimental import pallas as pl
from jax.experimental.pallas import tpu as pltpu

F32 = jnp.float32
BF16 = jnp.bfloat16

D_MODEL = 2048
DEPTH = 2
GRID_W = 64
ALPHA = (2 * DEPTH) ** 0.25
LN_EPS = 1e-5
RMS_EPS = 1e-6

H_A, DK_A, DV_A = 8, 128, 128
H_B, DK_B, DV_B = 4, 128, 256
CONV_A = 2 * H_A * DK_A + H_A * DV_A
EV_MAIN = CONV_A + H_A * DV_A + 2 * H_B * DK_B + 2 * H_B * DV_B
EV_GATES = 4 * H_A + 4 * H_B

D_INNER = 2 * D_MODEL
P_C, N_C, G_C = 64, 128, 8
H_C = D_INNER // P_C
HG_C = H_C // G_C
GW_C = D_INNER // G_C
OD_MAIN = 2 * D_INNER + 2 * G_C * N_C

N_EXP, TOP_K, N_GROUPS, TOPK_GROUPS = 64, 8, 8, 4
D_EXP = 512
ROUTED_SCALE = 2.5

CHUNK = 256
LANES = 128
VMEM_LIMIT = 56 * 1024 * 1024
NEG = -1e30


def _cparams(*sem):
    return pltpu.CompilerParams(dimension_semantics=sem, vmem_limit_bytes=VMEM_LIMIT)


def _bdot(a, b):
    return jnp.dot(a.astype(BF16), b.astype(BF16), preferred_element_type=F32)


def _bdot_nt(a, b):
    return lax.dot_general(a.astype(BF16), b.astype(BF16), (((1,), (1,)), ((), ())), preferred_element_type=F32)


def _bdot_tn(a, b):
    return lax.dot_general(a.astype(BF16), b.astype(BF16), (((0,), (0,)), ((), ())), preferred_element_type=F32)


def _split3(a):
    hi = a.astype(BF16)
    r = a - hi.astype(F32)
    mid = r.astype(BF16)
    lo = (r - mid.astype(F32)).astype(BF16)
    return hi, mid, lo


def _dot_exact_rhs(a, b_exact):
    hi, mid, lo = _split3(a)
    bb = b_exact.astype(BF16)
    d = lambda p: jnp.dot(p, bb, preferred_element_type=F32)
    return d(hi) + d(mid) + d(lo)


def _dot_exact_lhs(a_exact, b):
    hi, mid, lo = _split3(b)
    aa = a_exact.astype(BF16)
    d = lambda p: jnp.dot(aa, p, preferred_element_type=F32)
    return d(hi) + d(mid) + d(lo)


def _silu(x):
    return x * jax.nn.sigmoid(x)


def _softplus(x):
    return jnp.maximum(x, 0.0) + jnp.log(1.0 + jnp.exp(-jnp.abs(x)))


def _group_of_block(i, tm, t_ctx, lat_len):
    return jnp.maximum(i * tm - t_ctx, -1) // lat_len + 1


def _mod_kernel(c_ref, w_ref, b_ref, o_ref):
    c = c_ref[...]
    o_ref[...] = _bdot(_silu(c), w_ref[...]) + b_ref[...]


def compute_mods(cvec, mod_w, mod_b):
    depth, d, n = mod_w.shape
    tn = 512
    return pl.pallas_call(
        _mod_kernel,
        grid=(depth, n // tn),
        in_specs=[pl.BlockSpec((8, d), lambda l, j: (0, 0)),
                  pl.BlockSpec((None, d, tn), lambda l, j: (l, 0, j)),
                  pl.BlockSpec((None, 1, tn), lambda l, j: (l, 0, j))],
        out_specs=pl.BlockSpec((None, 8, tn), lambda l, j: (l, 0, j)),
        out_shape=jax.ShapeDtypeStruct((depth, 8, n), F32),
        compiler_params=_cparams("arbitrary", "arbitrary"),
        name="mod_vectors",
    )(cvec, mod_w, mod_b.reshape(depth, 1, n))


def _modulate_kernel(x_ref, m_ref, o_ref, *, sh, sc):
    o_ref[...] = (x_ref[...] * (1.0 + m_ref[sc:sc + 1, :]) + m_ref[sh:sh + 1, :]).astype(o_ref.dtype)


def modulate(x, mod, sh, sc, t_ctx, lat_len):
    t, d = x.shape
    tm = 512
    return pl.pallas_call(
        functools.partial(_modulate_kernel, sh=sh, sc=sc),
        grid=(t // tm,),
        in_specs=[pl.BlockSpec((tm, d), lambda i: (i, 0)),
                  pl.BlockSpec((None, 6, d), lambda i: (_group_of_block(i, tm, t_ctx, lat_len), 0, 0))],
        out_specs=pl.BlockSpec((tm, d), lambda i: (i, 0)),
        out_shape=jax.ShapeDtypeStruct((t, d), BF16),
        compiler_params=_cparams("arbitrary"),
        name="modulate",
    )(x, mod)


def _matmul_kernel(x_ref, w_ref, o_ref, wbf_ref, *, valid_cols):
    @pl.when(pl.program_id(1) == 0)
    def _():
        wbf_ref[...] = w_ref[...].astype(BF16)

    y = jnp.dot(x_ref[...], wbf_ref[...], preferred_element_type=F32)
    if valid_cols is not None:
        col = lax.broadcasted_iota(jnp.int32, y.shape, 1)
        y = jnp.where(col < valid_cols, y, 0.0)
    o_ref[...] = y.astype(o_ref.dtype)


def matmul(x, w, *, tm, tn, n_out, col_block_off=0, valid_cols=None, out_dtype=F32, name="matmul"):
    m, k = x.shape
    return pl.pallas_call(
        functools.partial(_matmul_kernel, valid_cols=valid_cols),
        grid=(n_out // tn, m // tm),
        in_specs=[pl.BlockSpec((tm, k), lambda j, i: (i, 0)),
                  pl.BlockSpec((k, tn), lambda j, i: (0, j + col_block_off))],
        out_specs=pl.BlockSpec((tm, tn), lambda j, i: (i, j)),
        out_shape=jax.ShapeDtypeStruct((m, n_out), out_dtype),
        scratch_shapes=[pltpu.VMEM((k, tn), BF16)],
        compiler_params=_cparams("arbitrary", "arbitrary"),
        name=name,
    )(x, w)


def _tri_masks(n):
    r = lax.broadcasted_iota(jnp.int32, (n, n), 0)
    c = lax.broadcasted_iota(jnp.int32, (n, n), 1)
    return r, c


def _gate_kernel(raw_ref, p_ref, act_ref, cum_ref, *, mode):
    x = raw_ref[...]
    coef, bias, rev = p_ref[0:1, :], p_ref[1:2, :], p_ref[2:3, :]
    col = lax.broadcasted_iota(jnp.int32, x.shape, 1)
    xb = x + bias
    if mode == "even":
        act = jnp.where(col < 2 * H_A, jax.nn.sigmoid(xb),
                        jnp.where(col < 4 * H_A, coef * _softplus(xb),
                                  jnp.where(col < 4 * H_A + 2 * H_B, xb,
                                            jnp.minimum(xb, 0.0) - jnp.log(1.0 + jnp.exp(-jnp.abs(xb))))))
        to_sum = act
    else:
        act = _softplus(xb)
        to_sum = act * coef
    r, c = _tri_masks(CHUNK)
    lower = jnp.where(c <= r, 1.0, 0.0)
    upper = jnp.where(c >= r, 1.0, 0.0)
    cum_f = _dot_exact_lhs(lower, to_sum)
    cum_r = _dot_exact_lhs(upper, to_sum)
    act_ref[...] = act
    cum_ref[...] = jnp.where(rev > 0.5, cum_r, cum_f)


def gate_prep(raw, params, mode):
    t = raw.shape[0]
    return pl.pallas_call(
        functools.partial(_gate_kernel, mode=mode),
        grid=(t // CHUNK,),
        in_specs=[pl.BlockSpec((CHUNK, LANES), lambda i: (i, 0)),
                  pl.BlockSpec((8, LANES), lambda i: (0, 0))],
        out_specs=[pl.BlockSpec((CHUNK, LANES), lambda i: (i, 0))] * 2,
        out_shape=[jax.ShapeDtypeStruct((t, LANES), F32)] * 2,
        compiler_params=_cparams("arbitrary"),
        name="gate_prep_" + mode,
    )(raw, params)


def _conv_silu(x, cw_ref, cb_ref, period):
    n = x.shape[0]
    row = lax.broadcasted_iota(jnp.int32, x.shape, 0) % period
    prev = jnp.where(row == 0, 0.0, pltpu.roll(x, 1, 0))
    nxt = jnp.where(row == period - 1, 0.0, pltpu.roll(x, n - 1, 0))
    y = cb_ref[...] + prev * cw_ref[0:1, :] + x * cw_ref[1:2, :] + nxt * cw_ref[2:3, :]
    return _silu(y)


def _pick_col(blk, idx):
    lane = lax.broadcasted_iota(jnp.int32, blk.shape, 1)
    return jnp.sum(jnp.where(lane == idx, blk, 0.0), axis=1, keepdims=True)


def _dir_masks(rev):
    r, c = _tri_masks(CHUNK)
    if rev:
        return c >= r, c > r
    return c <= r, c < r


def _tri_inverse(lmat, rev):
    r, c = _tri_masks(CHUNK)
    eye = jnp.where(r == c, 1.0, 0.0)

    def off_block(s):
        same = (r // (2 * s)) == (c // (2 * s))
        r_hi = (r // s) % 2
        c_hi = (c // s) % 2
        sel = (r_hi == 1) & (c_hi == 0) if not rev else (r_hi == 0) & (c_hi == 1)
        return jnp.where(same & sel, lmat, 0.0)

    t = eye - off_block(1)
    s = 2
    while s < CHUNK:
        t = t - _bdot(_bdot(t, off_block(s)), t)
        s *= 2
    return t


def _delta_kernel(*refs, n_chunks, period, has_state):
    (q_ref, k_ref, v_ref, z_ref, cwq, cwk, cwv, cbq, cbk, cbv, act_ref, cum_ref, cumt_ref, norm_ref) = refs[:14]
    rest = refs[14:]
    if has_state:
        s0_ref, o_ref, acc_ref = rest
        sout_ref = None
    else:
        o_ref, sout_ref, acc_ref = rest
    h = pl.program_id(1)

    q = _conv_silu(q_ref[...], cwq, cbq, period)
    k = _conv_silu(k_ref[...], cwk, cbk, period)
    v = _conv_silu(v_ref[...], cwv, cbv, period)
    q = q * lax.rsqrt(jnp.sum(q * q, axis=1, keepdims=True) + RMS_EPS) * (DK_A ** -0.5)
    k = k * lax.rsqrt(jnp.sum(k * k, axis=1, keepdims=True) + RMS_EPS)

    for d in (0, 1):
        rev = d == 1
        m_incl, m_strict = _dir_masks(rev)
        state = s0_ref[d] if has_state else None
        order = range(n_chunks - 1, -1, -1) if rev else range(n_chunks)
        for ci in order:
            sl = slice(ci * CHUNK, (ci + 1) * CHUNK)
            qc, kc, vc = q[sl], k[sl], v[sl]
            beta = _pick_col(act_ref[sl, :], d * H_A + h)
            gcol = _pick_col(cum_ref[sl, :], 2 * H_A + d * H_A + h)
            grow = cumt_ref[pl.ds(2 * H_A + d * H_A + h, 1), sl]
            decay = jnp.exp(jnp.where(m_incl, gcol - grow, NEG))
            kb = kc * beta
            lmat = _bdot_nt(kb, kc) * jnp.where(m_strict, decay, 0.0)
            tinv = _tri_inverse(lmat, rev)
            u = _bdot(tinv, vc * beta)
            if state is not None:
                w = _bdot(tinv, kb * jnp.exp(gcol))
                u = u - _bdot(w, state)
            attn = _bdot_nt(qc, kc) * decay
            o = _bdot(attn, u)
            if state is not None:
                o = o + _bdot(qc * jnp.exp(gcol), state)
            glast = gcol[0:1, :] if rev else gcol[CHUNK - 1:CHUNK, :]
            upd = _bdot_tn(kc * jnp.exp(glast - gcol), u)
            state = upd if state is None else state * jnp.exp(glast) + upd
            if rev:
                acc_ref[sl, :] = acc_ref[sl, :] + o
            else:
                acc_ref[sl, :] = o
        if sout_ref is not None:
            sout_ref[d] = state

    o = acc_ref[...]
    o = o * lax.rsqrt(jnp.mean(o * o, axis=1, keepdims=True) + RMS_EPS) * norm_ref[...]
    o_ref[...] = (o * _silu(z_ref[...])).astype(o_ref.dtype)


def delta_mixer(proj, conv_w, conv_b, act, cum, cumt, norm, state, *, row0, n_seq, seq_len, period):
    assert row0 % seq_len == 0
    n_chunks = seq_len // CHUNK
    rb0 = row0 // seq_len
    has_state = state is not None
    nq = H_A
    row_blk = lambda s, h: (rb0 + s)
    col = lambda off: (lambda s, h: (rb0 + s, off + h))
    cw = lambda off: (lambda s, h: (0, off + h))
    in_specs = [pl.BlockSpec((seq_len, LANES), col(0)), pl.BlockSpec((seq_len, LANES), col(nq)),
                pl.BlockSpec((seq_len, LANES), col(2 * nq)), pl.BlockSpec((seq_len, LANES), col(3 * nq)),
                pl.BlockSpec((3, LANES), cw(0)), pl.BlockSpec((3, LANES), cw(nq)), pl.BlockSpec((3, LANES), cw(2 * nq)),
                pl.BlockSpec((1, LANES), cw(0)), pl.BlockSpec((1, LANES), cw(nq)), pl.BlockSpec((1, LANES), cw(2 * nq)),
                pl.BlockSpec((seq_len, LANES), lambda s, h: (rb0 + s, 0)),
                pl.BlockSpec((seq_len, LANES), lambda s, h: (rb0 + s, 0)),
                pl.BlockSpec((LANES, seq_len), lambda s, h: (0, rb0 + s)),
                pl.BlockSpec((1, DV_A), lambda s, h: (0, 0))]
    args = [proj, proj, proj, proj, conv_w, conv_w, conv_w, conv_b, conv_b, conv_b, act, cum, cumt, norm]
    o_spec = pl.BlockSpec((seq_len, LANES), lambda s, h: (s, h))
    o_shape = jax.ShapeDtypeStruct((n_seq * seq_len, H_A * DV_A), BF16)
    if has_state:
        in_specs.append(pl.BlockSpec((None, 2, None, DK_A, DV_A), lambda s, h: (s, 0, h, 0, 0)))
        args.append(state)
        out_specs, out_shape = o_spec, o_shape
    else:
        out_specs = [o_spec, pl.BlockSpec((None, 2, None, DK_A, DV_A), lambda s, h: (s, 0, h, 0, 0))]
        out_shape = [o_shape, jax.ShapeDtypeStruct((n_seq, 2, H_A, DK_A, DV_A), F32)]
    return pl.pallas_call(
        functools.partial(_delta_kernel, n_chunks=n_chunks, period=period, has_state=has_state),
        grid=(n_seq, H_A),
        in_specs=in_specs, out_specs=out_specs, out_shape=out_shape,
        scratch_shapes=[pltpu.VMEM((seq_len, DV_A), F32)],
        compiler_params=_cparams("arbitrary", "arbitrary"),
        name="delta_lat" if has_state else "delta_ctx",
    )(*args)


def even_gate_params(a_log, dt_bias, b_i, b_f):
    zeros_a = jnp.zeros((2 * H_A,), F32)
    coef = jnp.concatenate([zeros_a, -jnp.exp(a_log.astype(F32)).reshape(-1), jnp.zeros((4 * H_B,), F32)])
    bias = jnp.concatenate([zeros_a, dt_bias.reshape(-1), b_i.reshape(-1), b_f.reshape(-1)]).astype(F32)
    rev = jnp.concatenate([jnp.repeat(jnp.arange(2, dtype=F32), H_A)] * 2 + [jnp.repeat(jnp.arange(2, dtype=F32), H_B)] * 2)
    p = jnp.stack([coef, bias, rev])
    return jnp.pad(p, ((0, 5), (0, LANES - EV_GATES)))


def _mlstm_kernel(*refs, n_chunks, has_state):
    (q_ref, k_ref, v_ref, og_ref, act_ref, actt_ref, cum_ref, cumt_ref, norm_ref) = refs[:9]
    rest = refs[9:]
    if has_state:
        c0_ref, n0_ref, m0_ref, o_ref, acc_ref = rest
    else:
        o_ref, cout_ref, nm_ref, acc_ref = rest
    h = pl.program_id(1)
    i_col0, f_col0 = 4 * H_A, 4 * H_A + 2 * H_B

    for d in (0, 1):
        rev = d == 1
        m_incl, _ = _dir_masks(rev)
        if has_state:
            cm, nv, m = c0_ref[d], n0_ref[d], m0_ref[d]
        else:
            cm, nv, m = None, None, jnp.zeros((1, 1), F32)
        order = range(n_chunks - 1, -1, -1) if rev else range(n_chunks)
        for ci in order:
            sl = slice(ci * CHUNK, (ci + 1) * CHUNK)
            qc = q_ref[sl, :] * (DK_B ** -0.5)
            kc = k_ref[sl, :]
            vc = v_ref[sl, :]
            li_col = _pick_col(act_ref[sl, :], i_col0 + d * H_B + h)
            li_row = actt_ref[pl.ds(i_col0 + d * H_B + h, 1), sl]
            b_col = _pick_col(cum_ref[sl, :], f_col0 + d * H_B + h)
            b_row = cumt_ref[pl.ds(f_col0 + d * H_B + h, 1), sl]
            dlog = jnp.where(m_incl, b_col - b_row + li_row, NEG)
            inter = b_col + m
            m_q = jnp.maximum(inter, jnp.max(dlog, axis=1, keepdims=True))
            s = _bdot_nt(qc, kc) * jnp.exp(dlog - m_q)
            num = _bdot(s, vc)
            den = jnp.sum(s, axis=1, keepdims=True)
            if cm is not None:
                w_inter = jnp.exp(inter - m_q)
                num = num + w_inter * _bdot(qc, cm)
                den = den + w_inter * jnp.sum(qc * nv, axis=1, keepdims=True)
            hout = num / jnp.maximum(jnp.abs(den), jnp.exp(-m_q))
            b_last = b_col[0:1, :] if rev else b_col[CHUNK - 1:CHUNK, :]
            wlog = b_last - b_col + li_col
            m_new = jnp.maximum(b_last + m, jnp.max(wlog, axis=0, keepdims=True))
            kw = kc * jnp.exp(wlog - m_new)
            c_upd = _bdot_tn(kw, vc)
            n_upd = jnp.sum(kw, axis=0, keepdims=True)
            if cm is not None:
                sc = jnp.exp(b_last + m - m_new)
                cm, nv = sc * cm + c_upd, sc * nv + n_upd
            else:
                cm, nv = c_upd, n_upd
            m = m_new
            if rev:
                acc_ref[sl, :] = acc_ref[sl, :] + hout
            else:
                acc_ref[sl, :] = hout
        if not has_state:
            cout_ref[d] = cm
            nm_ref[d, 0:1, :] = nv
            nm_ref[d, 1:2, :] = jnp.broadcast_to(m, (1, DK_B))
            nm_ref[d, 2:8, :] = jnp.zeros((6, DK_B), F32)

    o = acc_ref[...]
    o = o * lax.rsqrt(jnp.mean(o * o, axis=1, keepdims=True) + RMS_EPS) * norm_ref[...]
    o_ref[...] = (o * jax.nn.sigmoid(og_ref[...])).astype(o_ref.dtype)


def mlstm_mixer(proj, act, actt, cum, cumt, norm, state, *, row0, n_seq, seq_len):
    assert row0 % seq_len == 0
    n_chunks = seq_len // CHUNK
    rb0 = row0 // seq_len
    has_state = state is not None
    q0 = (CONV_A + H_A * DV_A) // LANES
    k0 = q0 + H_B
    v0 = (CONV_A + H_A * DV_A + 2 * H_B * DK_B) // DV_B
    o0 = v0 + H_B
    col = lambda off: (lambda s, h: (rb0 + s, off + h))
    in_specs = [pl.BlockSpec((seq_len, DK_B), col(q0)), pl.BlockSpec((seq_len, DK_B), col(k0)),
                pl.BlockSpec((seq_len, DV_B), col(v0)), pl.BlockSpec((seq_len, DV_B), col(o0)),
                pl.BlockSpec((seq_len, LANES), lambda s, h: (rb0 + s, 0)),
                pl.BlockSpec((LANES, seq_len), lambda s, h: (0, rb0 + s)),
                pl.BlockSpec((seq_len, LANES), lambda s, h: (rb0 + s, 0)),
                pl.BlockSpec((LANES, seq_len), lambda s, h: (0, rb0 + s)),
                pl.BlockSpec((1, DV_B), lambda s, h: (0, 0))]
    args = [proj, proj, proj, proj, act, actt, cum, cumt, norm]
    o_spec = pl.BlockSpec((seq_len, DV_B), lambda s, h: (s, h))
    o_shape = jax.ShapeDtypeStruct((n_seq * seq_len, H_B * DV_B), BF16)
    st_idx = lambda s, h: (s, 0, h, 0, 0)
    if has_state:
        c0, n0, m0 = state
        in_specs += [pl.BlockSpec((None, 2, None, DK_B, DV_B), st_idx),
                     pl.BlockSpec((None, 2, None, 1, DK_B), st_idx),
                     pl.BlockSpec((None, 2, None, 1, 1), st_idx)]
        args += [c0, n0.reshape(n_seq, 2, H_B, 1, DK_B), m0.reshape(n_seq, 2, H_B, 1, 1)]
        out_specs, out_shape = o_spec, o_shape
    else:
        out_specs = [o_spec, pl.BlockSpec((None, 2, None, DK_B, DV_B), st_idx),
                     pl.BlockSpec((None, 2, None, 8, DK_B), st_idx)]
        out_shape = [o_shape, jax.ShapeDtypeStruct((n_seq, 2, H_B, DK_B, DV_B), F32),
                     jax.ShapeDtypeStruct((n_seq, 2, H_B, 8, DK_B), F32)]
    return pl.pallas_call(
        functools.partial(_mlstm_kernel, n_chunks=n_chunks, has_state=has_state),
        grid=(n_seq, H_B),
        in_specs=in_specs, out_specs=out_specs, out_shape=out_shape,
        scratch_shapes=[pltpu.VMEM((seq_len, DV_B), F32)],
        compiler_params=_cparams("arbitrary", "arbitrary"),
        name="mlstm_lat" if has_state else "mlstm_ctx",
    )(*args)


def odd_gate_params(a_log, dt_bias):
    coef = -jnp.exp(a_log.astype(F32)).reshape(-1)
    bias = dt_bias.astype(F32).reshape(-1)
    rev = jnp.repeat(jnp.arange(2, dtype=F32), H_C)
    return jnp.pad(jnp.stack([coef, bias, rev]), ((0, 5), (0, 0)))


def _ssd_kernel(*refs, n_chunks, period, has_state):
    (z_ref, x_ref, b_ref, c_ref, cwx, cwb, cwc, cbx, cbb, cbc, dt_ref, cum_ref, cumt_ref, dskip_ref, norm_ref) = refs[:15]
    rest = refs[15:]
    if has_state:
        s0_ref, o_ref, acc_ref = rest
        sout_ref = None
    else:
        o_ref, sout_ref, acc_ref = rest
    g = pl.program_id(1)

    x = _conv_silu(x_ref[...], cwx, cbx, period)
    bm = _conv_silu(b_ref[...], cwb, cbb, period)
    cm = _conv_silu(c_ref[...], cwc, cbc, period)

    er = lax.broadcasted_iota(jnp.int32, (LANES, GW_C), 0)
    ec = lax.broadcasted_iota(jnp.int32, (LANES, GW_C), 1)
    tr = lax.broadcasted_iota(jnp.int32, (GW_C, LANES), 0)
    tc = lax.broadcasted_iota(jnp.int32, (GW_C, LANES), 1)

    for d in (0, 1):
        rev = d == 1
        m_incl, _ = _dir_masks(rev)
        col0 = d * H_C + g * HG_C
        expand = jnp.where(er == col0 + ec // P_C, 1.0, 0.0)
        expand_t = tc == col0 + tr // P_C
        state = s0_ref[d].reshape(GW_C, N_C) if has_state else None
        order = range(n_chunks - 1, -1, -1) if rev else range(n_chunks)
        for ci in order:
            sl = slice(ci * CHUNK, (ci + 1) * CHUNK)
            xc, bc, cc = x[sl], bm[sl], cm[sl]
            cum_blk = cum_ref[sl, :]
            cum_last = cum_blk[0:1, :] if rev else cum_blk[CHUNK - 1:CHUNK, :]
            xdt = xc * _dot_exact_rhs(dt_ref[sl, :], expand)
            scores = _bdot_nt(cc, bc)
            for hh in range(HG_C):
                ccol = _pick_col(cum_blk, col0 + hh)
                crow = cumt_ref[pl.ds(col0 + hh, 1), sl]
                seg = jnp.exp(jnp.where(m_incl, ccol - crow, NEG))
                y = _bdot(scores * seg, xdt[:, hh * P_C:(hh + 1) * P_C])
                cs = slice(hh * P_C, (hh + 1) * P_C)
                if rev:
                    acc_ref[sl, cs] = acc_ref[sl, cs] + y
                else:
                    acc_ref[sl, cs] = y
            if state is not None:
                y_in = _bdot_nt(cc, state) * _dot_exact_rhs(jnp.exp(cum_blk), expand)
                acc_ref[sl, :] = acc_ref[sl, :] + y_in
            dend = _dot_exact_rhs(jnp.exp(jnp.minimum(cum_last - cum_blk, 0.0)), expand)
            upd = _bdot_tn(xdt * dend, bc)
            if state is not None:
                tot = jnp.sum(jnp.where(expand_t, jnp.broadcast_to(cum_last, (GW_C, LANES)), 0.0), axis=1, keepdims=True)
                state = state * jnp.exp(tot) + upd
            else:
                state = upd
        if sout_ref is not None:
            sout_ref[d] = state.reshape(HG_C, P_C, N_C)

    y = acc_ref[...] + dskip_ref[...] * x
    y = y * _silu(z_ref[...])
    y = y * lax.rsqrt(jnp.mean(y * y, axis=1, keepdims=True) + RMS_EPS) * norm_ref[...]
    o_ref[...] = y.astype(o_ref.dtype)


def ssd_mixer(proj, conv_w, conv_b, dt, cum, cumt, dskip, norm, state, *, row0, n_seq, seq_len, period):
    assert row0 % seq_len == 0
    n_chunks = seq_len // CHUNK
    rb0 = row0 // seq_len
    has_state = state is not None
    xb0 = D_INNER // GW_C
    bb0 = 2 * D_INNER // N_C
    cb0 = bb0 + G_C
    wb0 = D_INNER // N_C
    wc0 = wb0 + G_C
    col = lambda off: (lambda s, g: (rb0 + s, off + g))
    cw = lambda off: (lambda s, g: (0, off + g))
    in_specs = [pl.BlockSpec((seq_len, GW_C), col(0)), pl.BlockSpec((seq_len, GW_C), col(xb0)),
                pl.BlockSpec((seq_len, N_C), col(bb0)), pl.BlockSpec((seq_len, N_C), col(cb0)),
                pl.BlockSpec((3, GW_C), cw(0)), pl.BlockSpec((3, N_C), cw(wb0)), pl.BlockSpec((3, N_C), cw(wc0)),
                pl.BlockSpec((1, GW_C), cw(0)), pl.BlockSpec((1, N_C), cw(wb0)), pl.BlockSpec((1, N_C), cw(wc0)),
                pl.BlockSpec((seq_len, LANES), lambda s, g: (rb0 + s, 0)),
                pl.BlockSpec((seq_len, LANES), lambda s, g: (rb0 + s, 0)),
                pl.BlockSpec((LANES, seq_len), lambda s, g: (0, rb0 + s)),
                pl.BlockSpec((1, GW_C), cw(0)), pl.BlockSpec((1, GW_C), cw(0))]
    args = [proj, proj, proj, proj, conv_w, conv_w, conv_w, conv_b, conv_b, conv_b, dt, cum, cumt, dskip, norm]
    o_spec = pl.BlockSpec((seq_len, GW_C), lambda s, g: (s, g))
    o_shape = jax.ShapeDtypeStruct((n_seq * seq_len, D_INNER), BF16)
    st_spec = pl.BlockSpec((None, 2, HG_C, P_C, N_C), lambda s, g: (s, 0, g, 0, 0))
    if has_state:
        in_specs.append(st_spec)
        args.append(state)
        out_specs, out_shape = o_spec, o_shape
    else:
        out_specs = [o_spec, st_spec]
        out_shape = [o_shape, jax.ShapeDtypeStruct((n_seq, 2, H_C, P_C, N_C), F32)]
    return pl.pallas_call(
        functools.partial(_ssd_kernel, n_chunks=n_chunks, period=period, has_state=has_state),
        grid=(n_seq, G_C),
        in_specs=in_specs, out_specs=out_specs, out_shape=out_shape,
        scratch_shapes=[pltpu.VMEM((seq_len, GW_C), F32)],
        compiler_params=_cparams("arbitrary", "arbitrary"),
        name="ssd_lat" if has_state else "ssd_ctx",
    )(*args)


def _dot3(a, b):
    a_hi = a.astype(BF16)
    a_lo = (a - a_hi.astype(F32)).astype(BF16)
    b_hi = b.astype(BF16)
    b_lo = (b - b_hi.astype(F32)).astype(BF16)
    d = lambda p, q: jnp.dot(p, q, preferred_element_type=F32)
    return d(a_hi, b_hi) + (d(a_hi, b_lo) + d(a_lo, b_hi))


def _resid_ln_kernel(*refs, n_y, gate, sh, sc, want_h, want_logits):
    x_ref = refs[0]
    y_refs = refs[1:1 + n_y]
    m_ref, mn_ref, g_ref, b_ref = refs[1 + n_y:5 + n_y]
    rest = list(refs[5 + n_y:])
    rw_ref = rest.pop(0) if want_logits else None
    xo_ref = rest.pop(0)
    y = y_refs[0][...]
    for r in y_refs[1:]:
        y = y + r[...]
    v = ALPHA * x_ref[...] + m_ref[gate:gate + 1, :] * y
    mu = jnp.mean(v, axis=1, keepdims=True)
    vc = v - mu
    var = jnp.mean(vc * vc, axis=1, keepdims=True)
    xn = vc * lax.rsqrt(var + LN_EPS) * g_ref[...] + b_ref[...]
    xo_ref[...] = xn
    if want_h:
        hm = xn * (1.0 + mn_ref[sc:sc + 1, :]) + mn_ref[sh:sh + 1, :]
        rest.pop(0)[...] = hm.astype(BF16)
        if want_logits:
            rest.pop(0)[...] = _dot3(hm, rw_ref[...])


def resid_ln(x, ys, mod, mod_next, ln_g, ln_b, router_w, *, gate, sh, sc, want_h, t_ctx, lat_len):
    t, d = x.shape
    tm = 256
    want_logits = router_w is not None
    grp = lambda i: (_group_of_block(i, tm, t_ctx, lat_len), 0, 0)
    row = pl.BlockSpec((tm, d), lambda i: (i, 0))
    vec = pl.BlockSpec((1, d), lambda i: (0, 0))
    in_specs = [row] * (1 + len(ys)) + [pl.BlockSpec((None, 6, d), grp), pl.BlockSpec((None, 6, d), grp), vec, vec]
    args = [x, *ys, mod, mod_next, ln_g.reshape(1, d), ln_b.reshape(1, d)]
    out_specs, out_shape = [row], [jax.ShapeDtypeStruct((t, d), F32)]
    if want_logits:
        in_specs.append(pl.BlockSpec((d, LANES), lambda i: (0, 0)))
        args.append(jnp.pad(router_w, ((0, 0), (0, LANES - router_w.shape[1]))))
    if want_h:
        out_specs.append(row)
        out_shape.append(jax.ShapeDtypeStruct((t, d), BF16))
    if want_logits:
        out_specs.append(pl.BlockSpec((tm, LANES), lambda i: (i, 0)))
        out_shape.append(jax.ShapeDtypeStruct((t, LANES), F32))
    return pl.pallas_call(
        functools.partial(_resid_ln_kernel, n_y=len(ys), gate=gate, sh=sh, sc=sc, want_h=want_h, want_logits=want_logits),
        grid=(t // tm,),
        in_specs=in_specs, out_specs=out_specs, out_shape=out_shape,
        compiler_params=_cparams("arbitrary"),
        name="resid_ln",
    )(*args)


def _ffn_kernel(be_ref, nu_ref, x_ref, wg_ref, wu_ref, wd_ref, o_ref, g_bf, u_bf, d_bf):
    b = pl.program_id(0)

    @pl.when(b < nu_ref[0])
    def _():
        prev = be_ref[jnp.maximum(b - 1, 0)]

        @pl.when((b == 0) | (be_ref[b] != prev))
        def _():
            g_bf[...] = wg_ref[...].astype(BF16)
            u_bf[...] = wu_ref[...].astype(BF16)
            d_bf[...] = wd_ref[...].astype(BF16)

        x = x_ref[...]
        hg = jnp.dot(x, g_bf[...], preferred_element_type=F32)
        hu = jnp.dot(x, u_bf[...], preferred_element_type=F32)
        a = (_silu(hg) * hu).astype(BF16)
        o_ref[...] = jnp.dot(a, d_bf[...], preferred_element_type=F32).astype(o_ref.dtype)


def expert_ffn(xs, blk_e, n_used, w_gate, w_up, w_down, *, tm, out_dtype=F32, name="expert_ffn"):
    r, d = xs.shape
    de = w_gate.shape[2]
    n_blk = r // tm
    wsel = lambda b, be, nu: (be[b], 0, 0)
    grid_spec = pltpu.PrefetchScalarGridSpec(
        num_scalar_prefetch=2,
        grid=(n_blk,),
        in_specs=[pl.BlockSpec((tm, d), lambda b, be, nu: (b, 0)),
                  pl.BlockSpec((None, d, de), wsel), pl.BlockSpec((None, d, de), wsel),
                  pl.BlockSpec((None, de, d), wsel)],
        out_specs=pl.BlockSpec((tm, d), lambda b, be, nu: (b, 0)),
        scratch_shapes=[pltpu.VMEM((d, de), BF16), pltpu.VMEM((d, de), BF16), pltpu.VMEM((de, d), BF16)],
    )
    return pl.pallas_call(
        _ffn_kernel,
        grid_spec=grid_spec,
        out_shape=jax.ShapeDtypeStruct((r, d), out_dtype),
        compiler_params=_cparams("arbitrary"),
        name=name,
    )(blk_e, n_used, xs, w_gate, w_up, w_down)


def _route(logits, router_bias):
    t = logits.shape[0]
    scores = jax.nn.sigmoid(logits)
    biased = scores + router_bias
    grp_score = lax.top_k(biased.reshape(t, N_GROUPS, N_EXP // N_GROUPS), 2)[0].sum(-1)
    _, top_g = lax.top_k(grp_score, TOPK_GROUPS)
    gmask = jax.nn.one_hot(top_g, N_GROUPS, dtype=F32).sum(1) > 0
    masked = jnp.where(jnp.repeat(gmask, N_EXP // N_GROUPS, axis=1), biased, -jnp.inf)
    _, top_e = lax.top_k(masked, TOP_K)
    sel = jnp.take_along_axis(scores, top_e, axis=1)
    wts = sel / sel.sum(-1, keepdims=True) * ROUTED_SCALE
    return top_e, wts


def moe(h, logits, router_bias, e_gate, e_up, e_down, s_gate, s_up, s_down):
    t, d = h.shape
    tm = 256
    top_e, wts = _route(logits, router_bias)
    n_assign = t * TOP_K
    flat_e = top_e.reshape(-1).astype(jnp.int32)
    order = jnp.argsort(flat_e).astype(jnp.int32)
    se = flat_e[order]
    counts = jnp.sum(jax.nn.one_hot(flat_e, N_EXP, dtype=jnp.int32), axis=0)
    padded = (counts + tm - 1) // tm * tm
    pad_end = jnp.cumsum(padded)
    pad_start = pad_end - padded
    grp_start = jnp.cumsum(counts) - counts
    dest_sorted = pad_start[se] + jnp.arange(n_assign, dtype=jnp.int32) - grp_start[se]
    n_blk = n_assign // tm + N_EXP
    blk_e = jnp.minimum(jnp.searchsorted(pad_end, jnp.arange(n_blk, dtype=jnp.int32) * tm, side='right'),
                        N_EXP - 1).astype(jnp.int32)
    n_used = (pad_end[-1] // tm).astype(jnp.int32).reshape(1)
    slot = jnp.arange(n_blk * tm, dtype=jnp.int32)
    slot_e = jnp.repeat(blk_e, tm)
    within = slot - pad_start[slot_e]
    valid = within < counts[slot_e]
    src = jnp.clip(grp_start[slot_e] + within, 0, n_assign - 1)
    slot_tok = jnp.where(valid, order[src] // TOP_K, 0)
    xs = jnp.take(h, slot_tok, axis=0)
    ys = expert_ffn(xs, blk_e, n_used, e_gate, e_up, e_down, tm=tm, name="routed_ffn")
    dest = jnp.zeros((n_assign,), jnp.int32).at[order].set(dest_sorted, unique_indices=True).reshape(t, TOP_K)
    routed = jnp.sum(jnp.take(ys, dest, axis=0) * wts[..., None], axis=1)
    n_sh = t // tm
    shared = expert_ffn(h, jnp.zeros((n_sh,), jnp.int32), jnp.full((1,), n_sh, jnp.int32),
                        s_gate[None], s_up[None], s_down[None], tm=tm, name="shared_ffn")
    return routed, shared


def kernel(x_prompt, x_sample, state_dn, state_ml_C, state_ml_n, state_ml_m, state_ssd, c, c_ctx,
           mod_w, mod_b, ln1_g, ln1_b, ln2_g, ln2_b, router_w, router_bias, exp_gate, exp_up, exp_down,
           sh_gate, sh_up, sh_down, ev_w_in, ev_conv_w, ev_conv_b, dn_A_log, dn_dt_bias, ml_b_i, ml_b_f,
           dn_norm, ml_norm, ev_w_out, od_w_in, od_conv_w, od_conv_b, ssd_A_log, ssd_dt_bias, ssd_D,
           ssd_norm, od_w_out):
    bp, sl, d = x_prompt.shape
    bl, ll, _ = x_sample.shape
    depth = mod_w.shape[0]
    t_ctx = bp * sl
    x = jnp.concatenate([x_prompt.reshape(t_ctx, d), x_sample.reshape(bl * ll, d)], axis=0)
    cvec = jnp.concatenate([c_ctx[None], c, jnp.zeros((8 - 1 - bl, d), F32)], axis=0)
    mods = compute_mods(cvec, mod_w, mod_b)[:, :1 + bl].reshape(depth, 1 + bl, 6, d)
    geo = dict(t_ctx=t_ctx, lat_len=ll)
    ctx = dict(row0=0, n_seq=bp, seq_len=sl)
    lat = dict(row0=t_ctx, n_seq=bl, seq_len=ll)
    rows = lambda a, b: jnp.concatenate([a, b], axis=0)

    h = modulate(x, mods[0], 0, 1, **geo)
    new_dn, new_c, new_n, new_m, new_ssd = [], [], [], [], []
    for l in range(depth):
        j = l // 2
        if l % 2 == 0:
            w_in = ev_w_in[j]
            proj = matmul(h, w_in, tm=1024, tn=512, n_out=EV_MAIN, name="ev_in_proj")
            graw = matmul(h, w_in, tm=1024, tn=LANES, n_out=LANES, col_block_off=EV_MAIN // LANES,
                          valid_cols=EV_GATES, name="ev_gate_proj")
            act, cum = gate_prep(graw, even_gate_params(dn_A_log[j], dn_dt_bias[j], ml_b_i[j], ml_b_f[j]), "even")
            actt, cumt = act.T, cum.T
            cw, cb, dnn, mln = ev_conv_w[j], ev_conv_b[j].reshape(1, -1), dn_norm[j].reshape(1, -1), ml_norm[j].reshape(1, -1)
            oa_c, s_dn = delta_mixer(proj, cw, cb, act, cum, cumt, dnn, None, period=sl, **ctx)
            oa_l = delta_mixer(proj, cw, cb, act, cum, cumt, dnn, state_dn[:, j], period=GRID_W, **lat)
            ob_c, s_c, s_nm = mlstm_mixer(proj, act, actt, cum, cumt, mln, None, **ctx)
            ob_l = mlstm_mixer(proj, act, actt, cum, cumt, mln,
                               (state_ml_C[:, j], state_ml_n[:, j], state_ml_m[:, j]), **lat)
            mixed = jnp.concatenate([rows(oa_c, oa_l), rows(ob_c, ob_l)], axis=1)
            y = matmul(mixed, ev_w_out[j], tm=1024, tn=512, n_out=d, name="ev_out_proj")
            new_dn.append(s_dn)
            new_c.append(s_c)
            new_n.append(s_nm[:, :, :, 0, :])
            new_m.append(s_nm[:, :, :, 1, 0])
        else:
            w_in = od_w_in[j]
            proj = matmul(h, w_in, tm=1024, tn=512, n_out=OD_MAIN, name="od_in_proj")
            draw = matmul(h, w_in, tm=1024, tn=LANES, n_out=LANES, col_block_off=OD_MAIN // LANES, name="od_dt_proj")
            dt, cum = gate_prep(draw, odd_gate_params(ssd_A_log[j], ssd_dt_bias[j]), "odd")
            cumt = cum.T
            cw, cb = od_conv_w[j], od_conv_b[j].reshape(1, -1)
            dsk, nrm = jnp.repeat(ssd_D[j], P_C).reshape(1, -1), ssd_norm[j].reshape(1, -1)
            o_c, s_ssd = ssd_mixer(proj, cw, cb, dt, cum, cumt, dsk, nrm, None, period=sl, **ctx)
            o_l = ssd_mixer(proj, cw, cb, dt, cum, cumt, dsk, nrm, state_ssd[:, j], period=GRID_W, **lat)
            y = matmul(rows(o_c, o_l), od_w_out[j], tm=512, tn=512, n_out=d, name="od_out_proj")
            new_ssd.append(s_ssd)
        x, h2, logits = resid_ln(x, [y], mods[l], mods[l], ln1_g[l], ln1_b[l], router_w[l],
                                 gate=2, sh=3, sc=4, want_h=True, **geo)
        routed, shared = moe(h2, logits[:, :N_EXP], router_bias[l], exp_gate[l], exp_up[l], exp_down[l],
                             sh_gate[l], sh_up[l], sh_down[l])
        last = l == depth - 1
        res = resid_ln(x, [routed, shared], mods[l], mods[min(l + 1, depth - 1)], ln2_g[l], ln2_b[l], None,
                       gate=5, sh=0, sc=1, want_h=not last, **geo)
        x = res[0]
        if not last:
            h = res[1]
    y_prompt = x[:t_ctx].reshape(bp, sl, d)
    y_sample = x[t_ctx:].reshape(bl, ll, d)
    return (y_prompt, y_sample, jnp.stack(new_dn, axis=1), jnp.stack(new_c, axis=1), jnp.stack(new_n, axis=1),
            jnp.stack(new_m, axis=1), jnp.stack(new_ssd, axis=1))
```

```python
import functools

import jax
import jax.numpy as jnp
from jax import lax
from jax.experimental import pallas as pl
from jax.experimental.pallas import tpu as pltpu

F32 = jnp.float32
BF16 = jnp.bfloat16

D_MODEL = 2048
DEPTH = 2
GRID_W = 64
ALPHA = (2 * DEPTH) ** 0.25
LN_EPS = 1e-5
RMS_EPS = 1e-6

H_A, DK_A, DV_A = 8, 128, 128
H_B, DK_B, DV_B = 4, 128, 256
CONV_A = 2 * H_A * DK_A + H_A * DV_A
EV_MAIN = CONV_A + H_A * DV_A + 2 * H_B * DK_B + 2 * H_B * DV_B
EV_GATES = 4 * H_A + 4 * H_B

D_INNER = 2 * D_MODEL
P_C, N_C, G_C = 64, 128, 8
H_C = D_INNER // P_C
HG_C = H_C // G_C
GW_C = D_INNER // G_C
OD_MAIN = 2 * D_INNER + 2 * G_C * N_C

N_EXP, TOP_K, N_GROUPS, TOPK_GROUPS = 64, 8, 8, 4
D_EXP = 512
ROUTED_SCALE = 2.5

CHUNK = 256
LANES = 128
VMEM_LIMIT = 56 * 1024 * 1024
NEG = -1e30


def _cparams(*sem):
    return pltpu.CompilerParams(dimension_semantics=sem, vmem_limit_bytes=VMEM_LIMIT)


def _bdot(a, b):
    return jnp.dot(a.astype(BF16), b.astype(BF16), preferred_element_type=F32)


def _bdot_nt(a, b):
    return lax.dot_general(a.astype(BF16), b.astype(BF16), (((1,), (1,)), ((), ())), preferred_element_type=F32)


def _bdot_tn(a, b):
    return lax.dot_general(a.astype(BF16), b.astype(BF16), (((0,), (0,)), ((), ())), preferred_element_type=F32)


def _split3(a):
    hi = a.astype(BF16)
    r = a - hi.astype(F32)
    mid = r.astype(BF16)
    lo = (r - mid.astype(F32)).astype(BF16)
    return hi, mid, lo


def _dot_exact_rhs(a, b_exact):
    hi, mid, lo = _split3(a)
    bb = b_exact.astype(BF16)
    d = lambda p: jnp.dot(p, bb, preferred_element_type=F32)
    return d(hi) + d(mid) + d(lo)


def _dot_exact_lhs(a_exact, b):
    hi, mid, lo = _split3(b)
    aa = a_exact.astype(BF16)
    d = lambda p: jnp.dot(aa, p, preferred_element_type=F32)
    return d(hi) + d(mid) + d(lo)


def _silu(x):
    return x * jax.nn.sigmoid(x)


def _softplus(x):
    return jnp.maximum(x, 0.0) + jnp.log(1.0 + jnp.exp(-jnp.abs(x)))


def _group_of_block(i, tm, t_ctx, lat_len):
    return jnp.maximum(i * tm - t_ctx, -1) // lat_len + 1


def _mod_kernel(c_ref, w_ref, b_ref, o_ref):
    c = c_ref[...]
    o_ref[...] = _bdot(_silu(c), w_ref[...]) + b_ref[...]


def compute_mods(cvec, mod_w, mod_b):
    depth, d, n = mod_w.shape
    tn = 512
    return pl.pallas_call(
        _mod_kernel,
        grid=(depth, n // tn),
        in_specs=[pl.BlockSpec((8, d), lambda l, j: (0, 0)),
                  pl.BlockSpec((None, d, tn), lambda l, j: (l, 0, j)),
                  pl.BlockSpec((None, 1, tn), lambda l, j: (l, 0, j))],
        out_specs=pl.BlockSpec((None, 8, tn), lambda l, j: (l, 0, j)),
        out_shape=jax.ShapeDtypeStruct((depth, 8, n), F32),
        compiler_params=_cparams("arbitrary", "arbitrary"),
        name="mod_vectors",
    )(cvec, mod_w, mod_b.reshape(depth, 1, n))


def _modulate_kernel(x_ref, m_ref, o_ref, *, sh, sc):
    o_ref[...] = (x_ref[...] * (1.0 + m_ref[sc:sc + 1, :]) + m_ref[sh:sh + 1, :]).astype(o_ref.dtype)


def modulate(x, mod, sh, sc, t_ctx, lat_len):
    t, d = x.shape
    tm = 512
    return pl.pallas_call(
        functools.partial(_modulate_kernel, sh=sh, sc=sc),
        grid=(t // tm,),
        in_specs=[pl.BlockSpec((tm, d), lambda i: (i, 0)),
                  pl.BlockSpec((None, 6, d), lambda i: (_group_of_block(i, tm, t_ctx, lat_len), 0, 0))],
        out_specs=pl.BlockSpec((tm, d), lambda i: (i, 0)),
        out_shape=jax.ShapeDtypeStruct((t, d), BF16),
        compiler_params=_cparams("arbitrary"),
        name="modulate",
    )(x, mod)


def _matmul_kernel(x_ref, w_ref, o_ref, wbf_ref, *, valid_cols):
    @pl.when(pl.program_id(1) == 0)
    def _():
        wbf_ref[...] = w_ref[...].astype(BF16)

    y = jnp.dot(x_ref[...], wbf_ref[...], preferred_element_type=F32)
    if valid_cols is not None:
        col = lax.broadcasted_iota(jnp.int32, y.shape, 1)
        y = jnp.where(col < valid_cols, y, 0.0)
    o_ref[...] = y.astype(o_ref.dtype)


def matmul(x, w, *, tm, tn, n_out, col_block_off=0, valid_cols=None, out_dtype=F32, name="matmul"):
    m, k = x.shape
    return pl.pallas_call(
        functools.partial(_matmul_kernel, valid_cols=valid_cols),
        grid=(n_out // tn, m // tm),
        in_specs=[pl.BlockSpec((tm, k), lambda j, i: (i, 0)),
                  pl.BlockSpec((k, tn), lambda j, i: (0, j + col_block_off))],
        out_specs=pl.BlockSpec((tm, tn), lambda j, i: (i, j)),
        out_shape=jax.ShapeDtypeStruct((m, n_out), out_dtype),
        scratch_shapes=[pltpu.VMEM((k, tn), BF16)],
        compiler_params=_cparams("arbitrary", "arbitrary"),
        name=name,
    )(x, w)


def _tri_masks(n):
    r = lax.broadcasted_iota(jnp.int32, (n, n), 0)
    c = lax.broadcasted_iota(jnp.int32, (n, n), 1)
    return r, c


def _gate_kernel(raw_ref, p_ref, act_ref, cum_ref, *, mode):
    x = raw_ref[...]
    coef, bias, rev = p_ref[0:1, :], p_ref[1:2, :], p_ref[2:3, :]
    col = lax.broadcasted_iota(jnp.int32, x.shape, 1)
    xb = x + bias
    if mode == "even":
        act = jnp.where(col < 2 * H_A, jax.nn.sigmoid(xb),
                        jnp.where(col < 4 * H_A, coef * _softplus(xb),
                                  jnp.where(col < 4 * H_A + 2 * H_B, xb,
                                            jnp.minimum(xb, 0.0) - jnp.log(1.0 + jnp.exp(-jnp.abs(xb))))))
        to_sum = act
    else:
        act = _softplus(xb)
        to_sum = act * coef
    r, c = _tri_masks(CHUNK)
    lower = jnp.where(c <= r, 1.0, 0.0)
    upper = jnp.where(c >= r, 1.0, 0.0)
    cum_f = _dot_exact_lhs(lower, to_sum)
    cum_r = _dot_exact_lhs(upper, to_sum)
    act_ref[...] = act
    cum_ref[...] = jnp.where(rev > 0.5, cum_r, cum_f)


def gate_prep(raw, params, mode):
    t = raw.shape[0]
    return pl.pallas_call(
        functools.partial(_gate_kernel, mode=mode),
        grid=(t // CHUNK,),
        in_specs=[pl.BlockSpec((CHUNK, LANES), lambda i: (i, 0)),
                  pl.BlockSpec((8, LANES), lambda i: (0, 0))],
        out_specs=[pl.BlockSpec((CHUNK, LANES), lambda i: (i, 0))] * 2,
        out_shape=[jax.ShapeDtypeStruct((t, LANES), F32)] * 2,
        compiler_params=_cparams("arbitrary"),
        name="gate_prep_" + mode,
    )(raw, params)


def _conv_silu(x, cw_ref, cb_ref, period):
    n = x.shape[0]
    row = lax.broadcasted_iota(jnp.int32, x.shape, 0) % period
    prev = jnp.where(row == 0, 0.0, pltpu.roll(x, 1, 0))
    nxt = jnp.where(row == period - 1, 0.0, pltpu.roll(x, n - 1, 0))
    y = cb_ref[...] + prev * cw_ref[0:1, :] + x * cw_ref[1:2, :] + nxt * cw_ref[2:3, :]
    return _silu(y)


def _pick_col(blk, idx):
    lane = lax.broadcasted_iota(jnp.int32, blk.shape, 1)
    return jnp.sum(jnp.where(lane == idx, blk, 0.0), axis=1, keepdims=True)


def _dir_masks(rev):
    r, c = _tri_masks(CHUNK)
    if rev:
        return c >= r, c > r
    return c <= r, c < r


def _tri_inverse(lmat, rev):
    r, c = _tri_masks(CHUNK)
    eye = jnp.where(r == c, 1.0, 0.0)

    def off_block(s):
        same = (r // (2 * s)) == (c // (2 * s))
        r_hi = (r // s) % 2
        c_hi = (c // s) % 2
        sel = (r_hi == 1) & (c_hi == 0) if not rev else (r_hi == 0) & (c_hi == 1)
        return jnp.where(same & sel, lmat, 0.0)

    t = eye - off_block(1)
    s = 2
    while s < CHUNK:
        t = t - _bdot(_bdot(t, off_block(s)), t)
        s *= 2
    return t


def _delta_kernel(*refs, n_chunks, period, has_state):
    (q_ref, k_ref, v_ref, z_ref, cwq, cwk, cwv, cbq, cbk, cbv, act_ref, cum_ref, cumt_ref, norm_ref) = refs[:14]
    rest = refs[14:]
    if has_state:
        s0_ref, o_ref, acc_ref = rest
        sout_ref = None
    else:
        o_ref, sout_ref, acc_ref = rest
    h = pl.program_id(1)

    q = _conv_silu(q_ref[...], cwq, cbq, period)
    k = _conv_silu(k_ref[...], cwk, cbk, period)
    v = _conv_silu(v_ref[...], cwv, cbv, period)
    q = q * lax.rsqrt(jnp.sum(q * q, axis=1, keepdims=True) + RMS_EPS) * (DK_A ** -0.5)
    k = k * lax.rsqrt(jnp.sum(k * k, axis=1, keepdims=True) + RMS_EPS)

    for d in (0, 1):
        rev = d == 1
        m_incl, m_strict = _dir_masks(rev)
        state = s0_ref[d] if has_state else None
        order = range(n_chunks - 1, -1, -1) if rev else range(n_chunks)
        for ci in order:
            sl = slice(ci * CHUNK, (ci + 1) * CHUNK)
            qc, kc, vc = q[sl], k[sl], v[sl]
            beta = _pick_col(act_ref[sl, :], d * H_A + h)
            gcol = _pick_col(cum_ref[sl, :], 2 * H_A + d * H_A + h)
            grow = cumt_ref[pl.ds(2 * H_A + d * H_A + h, 1), sl]
            decay = jnp.exp(jnp.where(m_incl, gcol - grow, NEG))
            kb = kc * beta
            lmat = _bdot_nt(kb, kc) * jnp.where(m_strict, decay, 0.0)
            tinv = _tri_inverse(lmat, rev)
            u = _bdot(tinv, vc * beta)
            if state is not None:
                w = _bdot(tinv, kb * jnp.exp(gcol))
                u = u - _bdot(w, state)
            attn = _bdot_nt(qc, kc) * decay
            o = _bdot(attn, u)
            if state is not None:
                o = o + _bdot(qc * jnp.exp(gcol), state)
            glast = gcol[0:1, :] if rev else gcol[CHUNK - 1:CHUNK, :]
            upd = _bdot_tn(kc * jnp.exp(glast - gcol), u)
            state = upd if state is None else state * jnp.exp(glast) + upd
            if rev:
                acc_ref[sl, :] = acc_ref[sl, :] + o
            else:
                acc_ref[sl, :] = o
        if sout_ref is not None:
            sout_ref[d] = state

    o = acc_ref[...]
    o = o * lax.rsqrt(jnp.mean(o * o, axis=1, keepdims=True) + RMS_EPS) * norm_ref[...]
    o_ref[...] = (o * _silu(z_ref[...])).astype(o_ref.dtype)


def delta_mixer(proj, conv_w, conv_b, act, cum, cumt, norm, state, *, row0, n_seq, seq_len, period):
    assert row0 % seq_len == 0
    n_chunks = seq_len // CHUNK
    rb0 = row0 // seq_len
    has_state = state is not None
    nq = H_A
    row_blk = lambda s, h: (rb0 + s)
    col = lambda off: (lambda s, h: (rb0 + s, off + h))
    cw = lambda off: (lambda s, h: (0, off + h))
    in_specs = [pl.BlockSpec((seq_len, LANES), col(0)), pl.BlockSpec((seq_len, LANES), col(nq)),
                pl.BlockSpec((seq_len, LANES), col(2 * nq)), pl.BlockSpec((seq_len, LANES), col(3 * nq)),
                pl.BlockSpec((3, LANES), cw(0)), pl.BlockSpec((3, LANES), cw(nq)), pl.BlockSpec((3, LANES), cw(2 * nq)),
                pl.BlockSpec((1, LANES), cw(0)), pl.BlockSpec((1, LANES), cw(nq)), pl.BlockSpec((1, LANES), cw(2 * nq)),
                pl.BlockSpec((seq_len, LANES), lambda s, h: (rb0 + s, 0)),
                pl.BlockSpec((seq_len, LANES), lambda s, h: (rb0 + s, 0)),
                pl.BlockSpec((LANES, seq_len), lambda s, h: (0, rb0 + s)),
                pl.BlockSpec((1, DV_A), lambda s, h: (0, 0))]
    args = [proj, proj, proj, proj, conv_w, conv_w, conv_w, conv_b, conv_b, conv_b, act, cum, cumt, norm]
    o_spec = pl.BlockSpec((seq_len, LANES), lambda s, h: (s, h))
    o_shape = jax.ShapeDtypeStruct((n_seq * seq_len, H_A * DV_A), BF16)
    if has_state:
        in_specs.append(pl.BlockSpec((None, 2, None, DK_A, DV_A), lambda s, h: (s, 0, h, 0, 0)))
        args.append(state)
        out_specs, out_shape = o_spec, o_shape
    else:
        out_specs = [o_spec, pl.BlockSpec((None, 2, None, DK_A, DV_A), lambda s, h: (s, 0, h, 0, 0))]
        out_shape = [o_shape, jax.ShapeDtypeStruct((n_seq, 2, H_A, DK_A, DV_A), F32)]
    return pl.pallas_call(
        functools.partial(_delta_kernel, n_chunks=n_chunks, period=period, has_state=has_state),
        grid=(n_seq, H_A),
        in_specs=in_specs, out_specs=out_specs, out_shape=out_shape,
        scratch_shapes=[pltpu.VMEM((seq_len, DV_A), F32)],
        compiler_params=_cparams("arbitrary", "arbitrary"),
        name="delta_lat" if has_state else "delta_ctx",
    )(*args)


def even_gate_params(a_log, dt_bias, b_i, b_f):
    zeros_a = jnp.zeros((2 * H_A,), F32)
    coef = jnp.concatenate([zeros_a, -jnp.exp(a_log.astype(F32)).reshape(-1), jnp.zeros((4 * H_B,), F32)])
    bias = jnp.concatenate([zeros_a, dt_bias.reshape(-1), b_i.reshape(-1), b_f.reshape(-1)]).astype(F32)
    rev = jnp.concatenate([jnp.repeat(jnp.arange(2, dtype=F32), H_A)] * 2 + [jnp.repeat(jnp.arange(2, dtype=F32), H_B)] * 2)
    p = jnp.stack([coef, bias, rev])
    return jnp.pad(p, ((0, 5), (0, LANES - EV_GATES)))


def _mlstm_kernel(*refs, n_chunks, has_state):
    (q_ref, k_ref, v_ref, og_ref, act_ref, actt_ref, cum_ref, cumt_ref, norm_ref) = refs[:9]
    rest = refs[9:]
    if has_state:
        c0_ref, n0_ref, m0_ref, o_ref, acc_ref = rest
    else:
        o_ref, cout_ref, nm_ref, acc_ref = rest
    h = pl.program_id(1)
    i_col0, f_col0 = 4 * H_A, 4 * H_A + 2 * H_B

    for d in (0, 1):
        rev = d == 1
        m_incl, _ = _dir_masks(rev)
        if has_state:
            cm, nv, m = c0_ref[d], n0_ref[d], m0_ref[d]
        else:
            cm, nv, m = None, None, jnp.zeros((1, 1), F32)
        order = range(n_chunks - 1, -1, -1) if rev else range(n_chunks)
        for ci in order:
            sl = slice(ci * CHUNK, (ci + 1) * CHUNK)
            qc = q_ref[sl, :] * (DK_B ** -0.5)
            kc = k_ref[sl, :]
            vc = v_ref[sl, :]
            li_col = _pick_col(act_ref[sl, :], i_col0 + d * H_B + h)
            li_row = actt_ref[pl.ds(i_col0 + d * H_B + h, 1), sl]
            b_col = _pick_col(cum_ref[sl, :], f_col0 + d * H_B + h)
            b_row = cumt_ref[pl.ds(f_col0 + d * H_B + h, 1), sl]
            dlog = jnp.where(m_incl, b_col - b_row + li_row, NEG)
            inter = b_col + m
            m_q = jnp.maximum(inter, jnp.max(dlog, axis=1, keepdims=True))
            s = _bdot_nt(qc, kc) * jnp.exp(dlog - m_q)
            num = _bdot(s, vc)
            den = jnp.sum(s, axis=1, keepdims=True)
            if cm is not None:
                w_inter = jnp.exp(inter - m_q)
                num = num + w_inter * _bdot(qc, cm)
                den = den + w_inter * jnp.sum(qc * nv, axis=1, keepdims=True)
            hout = num / jnp.maximum(jnp.abs(den), jnp.exp(-m_q))
            b_last = b_col[0:1, :] if rev else b_col[CHUNK - 1:CHUNK, :]
            wlog = b_last - b_col + li_col
            m_new = jnp.maximum(b_last + m, jnp.max(wlog, axis=0, keepdims=True))
            kw = kc * jnp.exp(wlog - m_new)
            c_upd = _bdot_tn(kw, vc)
            n_upd = jnp.sum(kw, axis=0, keepdims=True)
            if cm is not None:
                sc = jnp.exp(b_last + m - m_new)
                cm, nv = sc * cm + c_upd, sc * nv + n_upd
            else:
                cm, nv = c_upd, n_upd
            m = m_new
            if rev:
                acc_ref[sl, :] = acc_ref[sl, :] + hout
            else:
                acc_ref[sl, :] = hout
        if not has_state:
            cout_ref[d] = cm
            nm_ref[d, 0:1, :] = nv
            nm_ref[d, 1:2, :] = jnp.broadcast_to(m, (1, DK_B))
            nm_ref[d, 2:8, :] = jnp.zeros((6, DK_B), F32)

    o = acc_ref[...]
    o = o * lax.rsqrt(jnp.mean(o * o, axis=1, keepdims=True) + RMS_EPS) * norm_ref[...]
    o_ref[...] = (o * jax.nn.sigmoid(og_ref[...])).astype(o_ref.dtype)


def mlstm_mixer(proj, act, actt, cum, cumt, norm, state, *, row0, n_seq, seq_len):
    assert row0 % seq_len == 0
    n_chunks = seq_len // CHUNK
    rb0 = row0 // seq_len
    has_state = state is not None
    q0 = (CONV_A + H_A * DV_A) // LANES
    k0 = q0 + H_B
    v0 = (CONV_A + H_A * DV_A + 2 * H_B * DK_B) // DV_B
    o0 = v0 + H_B
    col = lambda off: (lambda s, h: (rb0 + s, off + h))
    in_specs = [pl.BlockSpec((seq_len, DK_B), col(q0)), pl.BlockSpec((seq_len, DK_B), col(k0)),
                pl.BlockSpec((seq_len, DV_B), col(v0)), pl.BlockSpec((seq_len, DV_B), col(o0)),
                pl.BlockSpec((seq_len, LANES), lambda s, h: (rb0 + s, 0)),
                pl.BlockSpec((LANES, seq_len), lambda s, h: (0, rb0 + s)),
                pl.BlockSpec((seq_len, LANES), lambda s, h: (rb0 + s, 0)),
                pl.BlockSpec((LANES, seq_len), lambda s, h: (0, rb0 + s)),
                pl.BlockSpec((1, DV_B), lambda s, h: (0, 0))]
    args = [proj, proj, proj, proj, act, actt, cum, cumt, norm]
    o_spec = pl.BlockSpec((seq_len, DV_B), lambda s, h: (s, h))
    o_shape = jax.ShapeDtypeStruct((n_seq * seq_len, H_B * DV_B), BF16)
    st_idx = lambda s, h: (s, 0, h, 0, 0)
    if has_state:
        c0, n0, m0 = state
        in_specs += [pl.BlockSpec((None, 2, None, DK_B, DV_B), st_idx),
                     pl.BlockSpec((None, 2, None, 1, DK_B), st_idx),
                     pl.BlockSpec((None, 2, None, 1, 1), st_idx)]
        args += [c0, n0.reshape(n_seq, 2, H_B, 1, DK_B), m0.reshape(n_seq, 2, H_B, 1, 1)]
        out_specs, out_shape = o_spec, o_shape
    else:
        out_specs = [o_spec, pl.BlockSpec((None, 2, None, DK_B, DV_B), st_idx),
                     pl.BlockSpec((None, 2, None, 8, DK_B), st_idx)]
        out_shape = [o_shape, jax.ShapeDtypeStruct((n_seq, 2, H_B, DK_B, DV_B), F32),
                     jax.ShapeDtypeStruct((n_seq, 2, H_B, 8, DK_B), F32)]
    return pl.pallas_call(
        functools.partial(_mlstm_kernel, n_chunks=n_chunks, has_state=has_state),
        grid=(n_seq, H_B),
        in_specs=in_specs, out_specs=out_specs, out_shape=out_shape,
        scratch_shapes=[pltpu.VMEM((seq_len, DV_B), F32)],
        compiler_params=_cparams("arbitrary", "arbitrary"),
        name="mlstm_lat" if has_state else "mlstm_ctx",
    )(*args)


def odd_gate_params(a_log, dt_bias):
    coef = -jnp.exp(a_log.astype(F32)).reshape(-1)
    bias = dt_bias.astype(F32).reshape(-1)
    rev = jnp.repeat(jnp.arange(2, dtype=F32), H_C)
    return jnp.pad(jnp.stack([coef, bias, rev]), ((0, 5), (0, 0)))


def _ssd_kernel(*refs, n_chunks, period, has_state):
    (z_ref, x_ref, b_ref, c_ref, cwx, cwb, cwc, cbx, cbb, cbc, dt_ref, cum_ref, cumt_ref, dskip_ref, norm_ref) = refs[:15]
    rest = refs[15:]
    if has_state:
        s0_ref, o_ref, acc_ref = rest
        sout_ref = None
    else:
        o_ref, sout_ref, acc_ref = rest
    g = pl.program_id(1)

    x = _conv_silu(x_ref[...], cwx, cbx, period)
    bm = _conv_silu(b_ref[...], cwb, cbb, period)
    cm = _conv_silu(c_ref[...], cwc, cbc, period)

    er = lax.broadcasted_iota(jnp.int32, (LANES, GW_C), 0)
    ec = lax.broadcasted_iota(jnp.int32, (LANES, GW_C), 1)
    tr = lax.broadcasted_iota(jnp.int32, (GW_C, LANES), 0)
    tc = lax.broadcasted_iota(jnp.int32, (GW_C, LANES), 1)

    for d in (0, 1):
        rev = d == 1
        m_incl, _ = _dir_masks(rev)
        col0 = d * H_C + g * HG_C
        expand = jnp.where(er == col0 + ec // P_C, 1.0, 0.0)
        expand_t = tc == col0 + tr // P_C
        state = s0_ref[d].reshape(GW_C, N_C) if has_state else None
        order = range(n_chunks - 1, -1, -1) if rev else range(n_chunks)
        for ci in order:
            sl = slice(ci * CHUNK, (ci + 1) * CHUNK)
            xc, bc, cc = x[sl], bm[sl], cm[sl]
            cum_blk = cum_ref[sl, :]
            cum_last = cum_blk[0:1, :] if rev else cum_blk[CHUNK - 1:CHUNK, :]
            xdt = xc * _dot_exact_rhs(dt_ref[sl, :], expand)
            scores = _bdot_nt(cc, bc)
            for hh in range(HG_C):
                ccol = _pick_col(cum_blk, col0 + hh)
                crow = cumt_ref[pl.ds(col0 + hh, 1), sl]
                seg = jnp.exp(jnp.where(m_incl, ccol - crow, NEG))
                y = _bdot(scores * seg, xdt[:, hh * P_C:(hh + 1) * P_C])
                cs = slice(hh * P_C, (hh + 1) * P_C)
                if rev:
                    acc_ref[sl, cs] = acc_ref[sl, cs] + y
                else:
                    acc_ref[sl, cs] = y
            if state is not None:
                y_in = _bdot_nt(cc, state) * _dot_exact_rhs(jnp.exp(cum_blk), expand)
                acc_ref[sl, :] = acc_ref[sl, :] + y_in
            dend = _dot_exact_rhs(jnp.exp(jnp.minimum(cum_last - cum_blk, 0.0)), expand)
            upd = _bdot_tn(xdt * dend, bc)
            if state is not None:
                tot = jnp.sum(jnp.where(expand_t, jnp.broadcast_to(cum_last, (GW_C, LANES)), 0.0), axis=1, keepdims=True)
                state = state * jnp.exp(tot) + upd
            else:
                state = upd
        if sout_ref is not None:
            sout_ref[d] = state.reshape(HG_C, P_C, N_C)

    y = acc_ref[...] + dskip_ref[...] * x
    y = y * _silu(z_ref[...])
    y = y * lax.rsqrt(jnp.mean(y * y, axis=1, keepdims=True) + RMS_EPS) * norm_ref[...]
    o_ref[...] = y.astype(o_ref.dtype)


def ssd_mixer(proj, conv_w, conv_b, dt, cum, cumt, dskip, norm, state, *, row0, n_seq, seq_len, period):
    assert row0 % seq_len == 0
    n_chunks = seq_len // CHUNK
    rb0 = row0 // seq_len
    has_state = state is not None
    xb0 = D_INNER // GW_C
    bb0 = 2 * D_INNER // N_C
    cb0 = bb0 + G_C
    wb0 = D_INNER // N_C
    wc0 = wb0 + G_C
    col = lambda off: (lambda s, g: (rb0 + s, off + g))
    cw = lambda off: (lambda s, g: (0, off + g))
    in_specs = [pl.BlockSpec((seq_len, GW_C), col(0)), pl.BlockSpec((seq_len, GW_C), col(xb0)),
                pl.BlockSpec((seq_len, N_C), col(bb0)), pl.BlockSpec((seq_len, N_C), col(cb0)),
                pl.BlockSpec((3, GW_C), cw(0)), pl.BlockSpec((3, N_C), cw(wb0)), pl.BlockSpec((3, N_C), cw(wc0)),
                pl.BlockSpec((1, GW_C), cw(0)), pl.BlockSpec((1, N_C), cw(wb0)), pl.BlockSpec((1, N_C), cw(wc0)),
                pl.BlockSpec((seq_len, LANES), lambda s, g: (rb0 + s, 0)),
                pl.BlockSpec((seq_len, LANES), lambda s, g: (rb0 + s, 0)),
                pl.BlockSpec((LANES, seq_len), lambda s, g: (0, rb0 + s)),
                pl.BlockSpec((1, GW_C), cw(0)), pl.BlockSpec((1, GW_C), cw(0))]
    args = [proj, proj, proj, proj, conv_w, conv_w, conv_w, conv_b, conv_b, conv_b, dt, cum, cumt, dskip, norm]
    o_spec = pl.BlockSpec((seq_len, GW_C), lambda s, g: (s, g))
    o_shape = jax.ShapeDtypeStruct((n_seq * seq_len, D_INNER), BF16)
    st_spec = pl.BlockSpec((None, 2, HG_C, P_C, N_C), lambda s, g: (s, 0, g, 0, 0))
    if has_state:
        in_specs.append(st_spec)
        args.append(state)
        out_specs, out_shape = o_spec, o_shape
    else:
        out_specs = [o_spec, st_spec]
        out_shape = [o_shape, jax.ShapeDtypeStruct((n_seq, 2, H_C, P_C, N_C), F32)]
    return pl.pallas_call(
        functools.partial(_ssd_kernel, n_chunks=n_chunks, period=period, has_state=has_state),
        grid=(n_seq, G_C),
        in_specs=in_specs, out_specs=out_specs, out_shape=out_shape,
        scratch_shapes=[pltpu.VMEM((seq_len, GW_C), F32)],
        compiler_params=_cparams("arbitrary", "arbitrary"),
        name="ssd_lat" if has_state else "ssd_ctx",
    )(*args)


def _dot3(a, b):
    a_hi = a.astype(BF16)
    a_lo = (a - a_hi.astype(F32)).astype(BF16)
    b_hi = b.astype(BF16)
    b_lo = (b - b_hi.astype(F32)).astype(BF16)
    d = lambda p, q: jnp.dot(p, q, preferred_element_type=F32)
    return d(a_hi, b_hi) + (d(a_hi, b_lo) + d(a_lo, b_hi))


def _dot3_nt(a, b):
    a_hi = a.astype(BF16)
    a_lo = (a - a_hi.astype(F32)).astype(BF16)
    b_hi = b.astype(BF16)
    b_lo = (b - b_hi.astype(F32)).astype(BF16)
    d = lambda p, q: lax.dot_general(p, q, (((1,), (1,)), ((), ())), preferred_element_type=F32)
    return d(a_hi, b_hi) + (d(a_hi, b_lo) + d(a_lo, b_hi))


def _resid_ln_kernel(*refs, n_y, n_gathered, gate, sh, sc, want_h, want_logits):
    x_ref = refs[0]
    y_refs = refs[1:1 + n_y]
    m_ref, mn_ref, g_ref, b_ref = refs[1 + n_y:5 + n_y]
    rest = list(refs[5 + n_y:])
    if n_gathered:
        gath_ref, gw_ref = rest.pop(0), rest.pop(0)
    rw_ref = rest.pop(0) if want_logits else None
    xo_ref = rest.pop(0)
    y = y_refs[0][...].astype(F32)
    for r in y_refs[1:]:
        y = y + r[...].astype(F32)
    for kk in range(n_gathered):
        y = y + gw_ref[:, kk:kk + 1] * gath_ref[kk].astype(F32)
    v = ALPHA * x_ref[...] + m_ref[gate:gate + 1, :] * y
    mu = jnp.mean(v, axis=1, keepdims=True)
    vc = v - mu
    var = jnp.mean(vc * vc, axis=1, keepdims=True)
    xn = vc * lax.rsqrt(var + LN_EPS) * g_ref[...] + b_ref[...]
    xo_ref[...] = xn
    if want_h:
        hm = xn * (1.0 + mn_ref[sc:sc + 1, :]) + mn_ref[sh:sh + 1, :]
        rest.pop(0)[...] = hm.astype(BF16)
        if want_logits:
            rest.pop(0)[...] = _dot3_nt(rw_ref[...], hm)


def resid_ln(x, ys, mod, mod_next, ln_g, ln_b, router_w, *, gate, sh, sc, want_h, t_ctx, lat_len, gathered=None):
    t, d = x.shape
    tm = 256
    want_logits = router_w is not None
    grp = lambda i: (_group_of_block(i, tm, t_ctx, lat_len), 0, 0)
    row = pl.BlockSpec((tm, d), lambda i: (i, 0))
    vec = pl.BlockSpec((1, d), lambda i: (0, 0))
    in_specs = [row] * (1 + len(ys)) + [pl.BlockSpec((None, 6, d), grp), pl.BlockSpec((None, 6, d), grp), vec, vec]
    args = [x, *ys, mod, mod_next, ln_g.reshape(1, d), ln_b.reshape(1, d)]
    n_gathered = 0
    if gathered is not None:
        n_gathered = gathered[0].shape[0]
        in_specs += [pl.BlockSpec((n_gathered, tm, d), lambda i: (0, i, 0)),
                     pl.BlockSpec((tm, n_gathered), lambda i: (i, 0))]
        args += list(gathered)
    out_specs, out_shape = [row], [jax.ShapeDtypeStruct((t, d), F32)]
    if want_logits:
        n_e = router_w.shape[1]
        in_specs.append(pl.BlockSpec((n_e, d), lambda i: (0, 0)))
        args.append(router_w.T)
    if want_h:
        out_specs.append(row)
        out_shape.append(jax.ShapeDtypeStruct((t, d), BF16))
    if want_logits:
        out_specs.append(pl.BlockSpec((n_e, tm), lambda i: (0, i)))
        out_shape.append(jax.ShapeDtypeStruct((n_e, t), F32))
    return pl.pallas_call(
        functools.partial(_resid_ln_kernel, n_y=len(ys), n_gathered=n_gathered, gate=gate, sh=sh, sc=sc, want_h=want_h,
                          want_logits=want_logits),
        grid=(t // tm,),
        in_specs=in_specs, out_specs=out_specs, out_shape=out_shape,
        compiler_params=_cparams("arbitrary"),
        name="resid_ln",
    )(*args)


def _ffn_kernel(be_ref, nu_ref, x_ref, wg_ref, wu_ref, wd_ref, o_ref, g_bf, u_bf, d_bf):
    b = pl.program_id(0)

    @pl.when(b < nu_ref[0])
    def _():
        prev = be_ref[jnp.maximum(b - 1, 0)]

        @pl.when((b == 0) | (be_ref[b] != prev))
        def _():
            g_bf[...] = wg_ref[...].astype(BF16)
            u_bf[...] = wu_ref[...].astype(BF16)
            d_bf[...] = wd_ref[...].astype(BF16)

        x = x_ref[...]
        hg = jnp.dot(x, g_bf[...], preferred_element_type=F32)
        hu = jnp.dot(x, u_bf[...], preferred_element_type=F32)
        a = (_silu(hg) * hu).astype(BF16)
        o_ref[...] = jnp.dot(a, d_bf[...], preferred_element_type=F32).astype(o_ref.dtype)


def expert_ffn(xs, blk_e, n_used, w_gate, w_up, w_down, layer, *, tm, out_dtype=BF16, name="expert_ffn"):
    r, d = xs.shape
    de = w_gate.shape[3]
    n_blk = r // tm
    wsel = lambda b, be, nu: (layer, be[b], 0, 0)
    grid_spec = pltpu.PrefetchScalarGridSpec(
        num_scalar_prefetch=2,
        grid=(n_blk,),
        in_specs=[pl.BlockSpec((tm, d), lambda b, be, nu: (b, 0)),
                  pl.BlockSpec((None, None, d, de), wsel), pl.BlockSpec((None, None, d, de), wsel),
                  pl.BlockSpec((None, None, de, d), wsel)],
        out_specs=pl.BlockSpec((tm, d), lambda b, be, nu: (b, 0)),
        scratch_shapes=[pltpu.VMEM((d, de), BF16), pltpu.VMEM((d, de), BF16), pltpu.VMEM((de, d), BF16)],
    )
    return pl.pallas_call(
        _ffn_kernel,
        grid_spec=grid_spec,
        out_shape=jax.ShapeDtypeStruct((r, d), out_dtype),
        compiler_params=_cparams("arbitrary"),
        name=name,
    )(blk_e, n_used, xs, w_gate, w_up, w_down)


ROUTE_TM = 512
GROUP_SIZE = N_EXP // N_GROUPS


def _first_argmax(v, idx, axis, sentinel):
    mx = jnp.max(v, axis=axis, keepdims=True)
    return mx, jnp.min(jnp.where(v == mx, idx, sentinel), axis=axis, keepdims=True)


def _route_kernel(lt_ref, bias_ref, idx_ref, w_ref, rank_ref, cnt_ref, carry_ref):
    i = pl.program_id(0)
    tm = lt_ref.shape[1]

    @pl.when(i == 0)
    def _():
        carry_ref[...] = jnp.zeros_like(carry_ref)

    scores = jax.nn.sigmoid(lt_ref[...])
    biased = scores + bias_ref[...]
    b3 = biased.reshape(N_GROUPS, GROUP_SIZE, tm)
    mem = lax.broadcasted_iota(jnp.int32, b3.shape, 1).astype(F32)
    m1, first = _first_argmax(b3, mem, 1, float(GROUP_SIZE))
    m2 = jnp.max(jnp.where(mem == first, -jnp.inf, b3), axis=1, keepdims=True)
    gs = (m1 + m2).reshape(N_GROUPS, tm)
    gi = lax.broadcasted_iota(jnp.int32, gs.shape, 0).astype(F32)
    gsel = jnp.zeros(gs.shape, F32)
    cur = gs
    for _ in range(TOPK_GROUPS):
        _, pick = _first_argmax(cur, gi, 0, float(N_GROUPS))
        hit = gi == pick
        gsel = jnp.where(hit, 1.0, gsel)
        cur = jnp.where(hit, -jnp.inf, cur)
    masked = jnp.where(gsel.reshape(N_GROUPS, 1, tm) > 0.5, b3, -jnp.inf).reshape(N_EXP, tm)

    ei = lax.broadcasted_iota(jnp.int32, masked.shape, 0).astype(F32)
    picks, sel_scores = [], []
    chosen = jnp.zeros(masked.shape, F32)
    cur = masked
    for _ in range(TOP_K):
        _, pick = _first_argmax(cur, ei, 0, float(N_EXP))
        hit = ei == pick
        picks.append(pick)
        sel_scores.append(jnp.sum(jnp.where(hit, scores, 0.0), axis=0, keepdims=True))
        chosen = jnp.where(hit, 1.0, chosen)
        cur = jnp.where(hit, -jnp.inf, cur)

    r, c = _tri_masks(tm)
    before = jnp.where(r < c, 1.0, 0.0).astype(BF16)
    rank = jnp.dot(chosen.astype(BF16), before, preferred_element_type=F32) + carry_ref[...]
    carry_ref[...] = carry_ref[...] + jnp.sum(chosen, axis=1, keepdims=True)
    cnt_ref[...] = carry_ref[...]

    total = sel_scores[0]
    for s in sel_scores[1:]:
        total = total + s
    for k in range(TOP_K):
        idx_ref[k:k + 1, :] = picks[k].astype(jnp.int32)
        w_ref[k:k + 1, :] = sel_scores[k] / total * ROUTED_SCALE
        rank_ref[k:k + 1, :] = jnp.sum(jnp.where(ei == picks[k], rank, 0.0), axis=0, keepdims=True).astype(jnp.int32)


def route(logits_t, router_bias):
    n_e, t = logits_t.shape
    tm = ROUTE_TM
    kt = pl.BlockSpec((TOP_K, tm), lambda i: (0, i))
    return pl.pallas_call(
        _route_kernel,
        grid=(t // tm,),
        in_specs=[pl.BlockSpec((n_e, tm), lambda i: (0, i)), pl.BlockSpec((n_e, 1), lambda i: (0, 0))],
        out_specs=[kt, kt, kt, pl.BlockSpec((n_e, 1), lambda i: (0, 0))],
        out_shape=[jax.ShapeDtypeStruct((TOP_K, t), jnp.int32), jax.ShapeDtypeStruct((TOP_K, t), F32),
                   jax.ShapeDtypeStruct((TOP_K, t), jnp.int32), jax.ShapeDtypeStruct((n_e, 1), F32)],
        scratch_shapes=[pltpu.VMEM((n_e, 1), F32)],
        compiler_params=_cparams("arbitrary"),
        name="route",
    )(logits_t, router_bias.reshape(n_e, 1))


def moe(h, logits_t, router_bias, e_gate, e_up, e_down, s_gate, s_up, s_down, layer):
    t, d = h.shape
    tm = 256
    top_e, wts, rank, counts = route(logits_t, router_bias)
    counts = counts.reshape(-1).astype(jnp.int32)
    n_assign = t * TOP_K
    padded = (counts + tm - 1) // tm * tm
    pad_end = jnp.cumsum(padded)
    pad_start = pad_end - padded
    n_blk = n_assign // tm + N_EXP
    blk_e = jnp.minimum(jnp.searchsorted(pad_end, jnp.arange(n_blk, dtype=jnp.int32) * tm, side='right'),
                        N_EXP - 1).astype(jnp.int32)
    n_used = (pad_end[-1] // tm).astype(jnp.int32).reshape(1)
    dest = pad_start[top_e] + rank
    tok = jnp.broadcast_to(jnp.arange(t, dtype=jnp.int32), (TOP_K, t))
    slot_tok = jnp.zeros((n_blk * tm,), jnp.int32).at[dest.reshape(-1)].set(tok.reshape(-1), unique_indices=True)
    xs = jnp.take(h, slot_tok, axis=0)
    ys = expert_ffn(xs, blk_e, n_used, e_gate, e_up, e_down, layer, tm=tm, name="routed_ffn")
    routed_rows = jnp.take(ys, dest, axis=0)
    n_sh = t // tm
    shared = expert_ffn(h, jnp.zeros((n_sh,), jnp.int32), jnp.full((1,), n_sh, jnp.int32),
                        s_gate[:, None], s_up[:, None], s_down[:, None], layer, tm=tm, name="shared_ffn")
    return shared, routed_rows, wts.T


def kernel(x_prompt, x_sample, state_dn, state_ml_C, state_ml_n, state_ml_m, state_ssd, c, c_ctx,
           mod_w, mod_b, ln1_g, ln1_b, ln2_g, ln2_b, router_w, router_bias, exp_gate, exp_up, exp_down,
           sh_gate, sh_up, sh_down, ev_w_in, ev_conv_w, ev_conv_b, dn_A_log, dn_dt_bias, ml_b_i, ml_b_f,
           dn_norm, ml_norm, ev_w_out, od_w_in, od_conv_w, od_conv_b, ssd_A_log, ssd_dt_bias, ssd_D,
           ssd_norm, od_w_out):
    bp, sl, d = x_prompt.shape
    bl, ll, _ = x_sample.shape
    depth = mod_w.shape[0]
    t_ctx = bp * sl
    x = jnp.concatenate([x_prompt.reshape(t_ctx, d), x_sample.reshape(bl * ll, d)], axis=0)
    cvec = jnp.concatenate([c_ctx[None], c, jnp.zeros((8 - 1 - bl, d), F32)], axis=0)
    mods = compute_mods(cvec, mod_w, mod_b)[:, :1 + bl].reshape(depth, 1 + bl, 6, d)
    geo = dict(t_ctx=t_ctx, lat_len=ll)
    ctx = dict(row0=0, n_seq=bp, seq_len=sl)
    lat = dict(row0=t_ctx, n_seq=bl, seq_len=ll)
    rows = lambda a, b: jnp.concatenate([a, b], axis=0)

    h = modulate(x, mods[0], 0, 1, **geo)
    new_dn, new_c, new_n, new_m, new_ssd = [], [], [], [], []
    for l in range(depth):
        j = l // 2
        if l % 2 == 0:
            w_in = ev_w_in[j]
            proj = matmul(h, w_in, tm=1024, tn=512, n_out=EV_MAIN, name="ev_in_proj")
            graw = matmul(h, w_in, tm=1024, tn=LANES, n_out=LANES, col_block_off=EV_MAIN // LANES,
                          valid_cols=EV_GATES, name="ev_gate_proj")
            act, cum = gate_prep(graw, even_gate_params(dn_A_log[j], dn_dt_bias[j], ml_b_i[j], ml_b_f[j]), "even")
            actt, cumt = act.T, cum.T
            cw, cb, dnn, mln = ev_conv_w[j], ev_conv_b[j].reshape(1, -1), dn_norm[j].reshape(1, -1), ml_norm[j].reshape(1, -1)
            oa_c, s_dn = delta_mixer(proj, cw, cb, act, cum, cumt, dnn, None, period=sl, **ctx)
            oa_l = delta_mixer(proj, cw, cb, act, cum, cumt, dnn, state_dn[:, j], period=GRID_W, **lat)
            ob_c, s_c, s_nm = mlstm_mixer(proj, act, actt, cum, cumt, mln, None, **ctx)
            ob_l = mlstm_mixer(proj, act, actt, cum, cumt, mln,
                               (state_ml_C[:, j], state_ml_n[:, j], state_ml_m[:, j]), **lat)
            mixed = jnp.concatenate([rows(oa_c, oa_l), rows(ob_c, ob_l)], axis=1)
            y = matmul(mixed, ev_w_out[j], tm=1024, tn=512, n_out=d, name="ev_out_proj")
            new_dn.append(s_dn)
            new_c.append(s_c)
            new_n.append(s_nm[:, :, :, 0, :])
            new_m.append(s_nm[:, :, :, 1, 0])
        else:
            w_in = od_w_in[j]
            proj = matmul(h, w_in, tm=1024, tn=512, n_out=OD_MAIN, name="od_in_proj")
            draw = matmul(h, w_in, tm=1024, tn=LANES, n_out=LANES, col_block_off=OD_MAIN // LANES, name="od_dt_proj")
            dt, cum = gate_prep(draw, odd_gate_params(ssd_A_log[j], ssd_dt_bias[j]), "odd")
            cumt = cum.T
            cw, cb = od_conv_w[j], od_conv_b[j].reshape(1, -1)
            dsk, nrm = jnp.repeat(ssd_D[j], P_C).reshape(1, -1), ssd_norm[j].reshape(1, -1)
            o_c, s_ssd = ssd_mixer(proj, cw, cb, dt, cum, cumt, dsk, nrm, None, period=sl, **ctx)
            o_l = ssd_mixer(proj, cw, cb, dt, cum, cumt, dsk, nrm, state_ssd[:, j], period=GRID_W, **lat)
            y = matmul(rows(o_c, o_l), od_w_out[j], tm=512, tn=512, n_out=d, name="od_out_proj")
            new_ssd.append(s_ssd)
        x, h2, logits_t = resid_ln(x, [y], mods[l], mods[l], ln1_g[l], ln1_b[l], router_w[l],
                                   gate=2, sh=3, sc=4, want_h=True, **geo)
        shared, routed_rows, wts = moe(h2, logits_t, router_bias[l], exp_gate, exp_up, exp_down,
                                       sh_gate, sh_up, sh_down, l)
        last = l == depth - 1
        res = resid_ln(x, [shared], mods[l], mods[min(l + 1, depth - 1)], ln2_g[l], ln2_b[l], None,
                       gate=5, sh=0, sc=1, want_h=not last, gathered=(routed_rows, wts), **geo)
        x = res[0]
        if not last:
            h = res[1]
    y_prompt = x[:t_ctx].reshape(bp, sl, d)
    y_sample = x[t_ctx:].reshape(bl, ll, d)
    return (y_prompt, y_sample, jnp.stack(new_dn, axis=1), jnp.stack(new_c, axis=1), jnp.stack(new_n, axis=1),
            jnp.stack(new_m, axis=1), jnp.stack(new_ssd, axis=1))
```

```python
import functools

import jax
import jax.numpy as jnp
from jax import lax
from jax.experimental import pallas as pl
from jax.experimental.pallas import tpu as pltpu

F32 = jnp.float32
BF16 = jnp.bfloat16

D_MODEL = 2048
DEPTH = 2
GRID_W = 64
ALPHA = (2 * DEPTH) ** 0.25
LN_EPS = 1e-5
RMS_EPS = 1e-6

H_A, DK_A, DV_A = 8, 128, 128
H_B, DK_B, DV_B = 4, 128, 256
CONV_A = 2 * H_A * DK_A + H_A * DV_A
EV_MAIN = CONV_A + H_A * DV_A + 2 * H_B * DK_B + 2 * H_B * DV_B
EV_GATES = 4 * H_A + 4 * H_B

D_INNER = 2 * D_MODEL
P_C, N_C, G_C = 64, 128, 8
H_C = D_INNER // P_C
HG_C = H_C // G_C
GW_C = D_INNER // G_C
OD_MAIN = 2 * D_INNER + 2 * G_C * N_C

N_EXP, TOP_K, N_GROUPS, TOPK_GROUPS = 64, 8, 8, 4
D_EXP = 512
ROUTED_SCALE = 2.5

GATHER_SRC_ROWS = 16384
CHUNK = 256
LANES = 128
VMEM_LIMIT = 56 * 1024 * 1024
NEG = -1e30


def _cparams(*sem):
    return pltpu.CompilerParams(dimension_semantics=sem, vmem_limit_bytes=VMEM_LIMIT)


def _bdot(a, b):
    return jnp.dot(a.astype(BF16), b.astype(BF16), preferred_element_type=F32)


def _bdot_nt(a, b):
    return lax.dot_general(a.astype(BF16), b.astype(BF16), (((1,), (1,)), ((), ())), preferred_element_type=F32)


def _bdot_tn(a, b):
    return lax.dot_general(a.astype(BF16), b.astype(BF16), (((0,), (0,)), ((), ())), preferred_element_type=F32)


def _split3(a):
    hi = a.astype(BF16)
    r = a - hi.astype(F32)
    mid = r.astype(BF16)
    lo = (r - mid.astype(F32)).astype(BF16)
    return hi, mid, lo


def _dot_exact_rhs(a, b_exact):
    hi, mid, lo = _split3(a)
    bb = b_exact.astype(BF16)
    d = lambda p: jnp.dot(p, bb, preferred_element_type=F32)
    return d(hi) + d(mid) + d(lo)


def _dot_exact_lhs(a_exact, b):
    hi, mid, lo = _split3(b)
    aa = a_exact.astype(BF16)
    d = lambda p: jnp.dot(aa, p, preferred_element_type=F32)
    return d(hi) + d(mid) + d(lo)


def _silu(x):
    return x * jax.nn.sigmoid(x)


def _softplus(x):
    return jnp.maximum(x, 0.0) + jnp.log(1.0 + jnp.exp(-jnp.abs(x)))


def _group_of_block(i, tm, t_ctx, lat_len):
    return jnp.maximum(i * tm - t_ctx, -1) // lat_len + 1


def _mod_kernel(c_ref, w_ref, b_ref, o_ref):
    c = c_ref[...]
    o_ref[...] = _bdot(_silu(c), w_ref[...]) + b_ref[...]


def compute_mods(cvec, mod_w, mod_b):
    depth, d, n = mod_w.shape
    tn = 512
    return pl.pallas_call(
        _mod_kernel,
        grid=(depth, n // tn),
        in_specs=[pl.BlockSpec((8, d), lambda l, j: (0, 0)),
                  pl.BlockSpec((None, d, tn), lambda l, j: (l, 0, j)),
                  pl.BlockSpec((None, 1, tn), lambda l, j: (l, 0, j))],
        out_specs=pl.BlockSpec((None, 8, tn), lambda l, j: (l, 0, j)),
        out_shape=jax.ShapeDtypeStruct((depth, 8, n), F32),
        compiler_params=_cparams("arbitrary", "arbitrary"),
        name="mod_vectors",
    )(cvec, mod_w, mod_b.reshape(depth, 1, n))


def _modulate_kernel(x_ref, m_ref, o_ref, *, sh, sc):
    o_ref[...] = (x_ref[...] * (1.0 + m_ref[sc:sc + 1, :]) + m_ref[sh:sh + 1, :]).astype(o_ref.dtype)


def modulate(x, mod, sh, sc, t_ctx, lat_len):
    t, d = x.shape
    tm = 512
    return pl.pallas_call(
        functools.partial(_modulate_kernel, sh=sh, sc=sc),
        grid=(t // tm,),
        in_specs=[pl.BlockSpec((tm, d), lambda i: (i, 0)),
                  pl.BlockSpec((None, 6, d), lambda i: (_group_of_block(i, tm, t_ctx, lat_len), 0, 0))],
        out_specs=pl.BlockSpec((tm, d), lambda i: (i, 0)),
        out_shape=jax.ShapeDtypeStruct((t, d), BF16),
        compiler_params=_cparams("arbitrary"),
        name="modulate",
    )(x, mod)


def _matmul_kernel(x_ref, w_ref, o_ref, wbf_ref, *, valid_cols):
    @pl.when(pl.program_id(1) == 0)
    def _():
        wbf_ref[...] = w_ref[...].astype(BF16)

    y = jnp.dot(x_ref[...], wbf_ref[...], preferred_element_type=F32)
    if valid_cols is not None:
        col = lax.broadcasted_iota(jnp.int32, y.shape, 1)
        y = jnp.where(col < valid_cols, y, 0.0)
    o_ref[...] = y.astype(o_ref.dtype)


def matmul(x, w, *, tm, tn, n_out, col_block_off=0, valid_cols=None, out_dtype=F32, name="matmul"):
    m, k = x.shape
    return pl.pallas_call(
        functools.partial(_matmul_kernel, valid_cols=valid_cols),
        grid=(n_out // tn, m // tm),
        in_specs=[pl.BlockSpec((tm, k), lambda j, i: (i, 0)),
                  pl.BlockSpec((k, tn), lambda j, i: (0, j + col_block_off))],
        out_specs=pl.BlockSpec((tm, tn), lambda j, i: (i, j)),
        out_shape=jax.ShapeDtypeStruct((m, n_out), out_dtype),
        scratch_shapes=[pltpu.VMEM((k, tn), BF16)],
        compiler_params=_cparams("arbitrary", "arbitrary"),
        name=name,
    )(x, w)


def _tri_masks(n):
    r = lax.broadcasted_iota(jnp.int32, (n, n), 0)
    c = lax.broadcasted_iota(jnp.int32, (n, n), 1)
    return r, c


def _gate_kernel(raw_ref, p_ref, act_ref, cum_ref, *, mode):
    x = raw_ref[...]
    coef, bias, rev = p_ref[0:1, :], p_ref[1:2, :], p_ref[2:3, :]
    col = lax.broadcasted_iota(jnp.int32, x.shape, 1)
    xb = x + bias
    if mode == "even":
        act = jnp.where(col < 2 * H_A, jax.nn.sigmoid(xb),
                        jnp.where(col < 4 * H_A, coef * _softplus(xb),
                                  jnp.where(col < 4 * H_A + 2 * H_B, xb,
                                            jnp.minimum(xb, 0.0) - jnp.log(1.0 + jnp.exp(-jnp.abs(xb))))))
        to_sum = act
    else:
        act = _softplus(xb)
        to_sum = act * coef
    r, c = _tri_masks(CHUNK)
    lower = jnp.where(c <= r, 1.0, 0.0)
    upper = jnp.where(c >= r, 1.0, 0.0)
    cum_f = _dot_exact_lhs(lower, to_sum)
    cum_r = _dot_exact_lhs(upper, to_sum)
    act_ref[...] = act
    cum_ref[...] = jnp.where(rev > 0.5, cum_r, cum_f)


def gate_prep(raw, params, mode):
    t = raw.shape[0]
    return pl.pallas_call(
        functools.partial(_gate_kernel, mode=mode),
        grid=(t // CHUNK,),
        in_specs=[pl.BlockSpec((CHUNK, LANES), lambda i: (i, 0)),
                  pl.BlockSpec((8, LANES), lambda i: (0, 0))],
        out_specs=[pl.BlockSpec((CHUNK, LANES), lambda i: (i, 0))] * 2,
        out_shape=[jax.ShapeDtypeStruct((t, LANES), F32)] * 2,
        compiler_params=_cparams("arbitrary"),
        name="gate_prep_" + mode,
    )(raw, params)


def _conv_silu(x, cw_ref, cb_ref, period):
    n = x.shape[0]
    row = lax.broadcasted_iota(jnp.int32, x.shape, 0) % period
    prev = jnp.where(row == 0, 0.0, pltpu.roll(x, 1, 0))
    nxt = jnp.where(row == period - 1, 0.0, pltpu.roll(x, n - 1, 0))
    y = cb_ref[...] + prev * cw_ref[0:1, :] + x * cw_ref[1:2, :] + nxt * cw_ref[2:3, :]
    return _silu(y)


def _pick_col(blk, idx):
    lane = lax.broadcasted_iota(jnp.int32, blk.shape, 1)
    return jnp.sum(jnp.where(lane == idx, blk, 0.0), axis=1, keepdims=True)


def _dir_masks(rev):
    r, c = _tri_masks(CHUNK)
    if rev:
        return c >= r, c > r
    return c <= r, c < r


def _tri_inverse(lmat, rev):
    return _tri_inverse_many([lmat], [rev])[0]


def _tri_inverse_many(lmats, revs):
    r, c = _tri_masks(CHUNK)
    eye = jnp.where(r == c, 1.0, 0.0)

    def off_mask(s, rev):
        same = (r // (2 * s)) == (c // (2 * s))
        r_hi = (r // s) % 2
        c_hi = (c // s) % 2
        return same & ((r_hi == 0) & (c_hi == 1) if rev else (r_hi == 1) & (c_hi == 0))

    masks = {rev: off_mask(1, rev) for rev in set(revs)}
    ts = [eye - jnp.where(masks[rev], lm, 0.0) for lm, rev in zip(lmats, revs)]
    s = 2
    while s < CHUNK:
        masks = {rev: off_mask(s, rev) for rev in set(revs)}
        ps = [_bdot(t, jnp.where(masks[rev], lm, 0.0)) for t, lm, rev in zip(ts, lmats, revs)]
        ts = [t - _bdot(p, t) for p, t in zip(ps, ts)]
        s *= 2
    return ts


def _delta_kernel(*refs, n_chunks, period, has_state, hb):
    (q_ref, k_ref, v_ref, z_ref, cwq, cwk, cwv, cbq, cbk, cbv, act_ref, cum_ref, cumt_ref, norm_ref) = refs[:14]
    rest = refs[14:]
    if has_state:
        s0_ref, o_ref, acc_ref = rest
        sout_ref = None
    else:
        o_ref, sout_ref, acc_ref = rest
    h0 = pl.program_id(1) * hb

    q = _conv_silu(q_ref[...], cwq, cbq, period)
    k = _conv_silu(k_ref[...], cwk, cbk, period)
    v = _conv_silu(v_ref[...], cwv, cbv, period)

    triples = [(hh, d, ci) for hh in range(hb) for d in (0, 1) for ci in range(n_chunks)]
    pre = {}
    for hh in range(hb):
        hs = slice(hh * DK_A, (hh + 1) * DK_A)
        qh, kh = q[:, hs], k[:, hs]
        qh = qh * lax.rsqrt(jnp.sum(qh * qh, axis=1, keepdims=True) + RMS_EPS) * (DK_A ** -0.5)
        kh = kh * lax.rsqrt(jnp.sum(kh * kh, axis=1, keepdims=True) + RMS_EPS)
        for d in (0, 1):
            m_incl, m_strict = _dir_masks(d == 1)
            for ci in range(n_chunks):
                sl = slice(ci * CHUNK, (ci + 1) * CHUNK)
                qc, kc, vc = qh[sl], kh[sl], v[sl, hs]
                beta = _pick_col(act_ref[sl, :], d * H_A + h0 + hh)
                gcol = _pick_col(cum_ref[sl, :], 2 * H_A + d * H_A + h0 + hh)
                grow = cumt_ref[pl.ds(2 * H_A + d * H_A + h0 + hh, 1), sl]
                decay = jnp.exp(jnp.where(m_incl, gcol - grow, NEG))
                kb = kc * beta
                pre[hh, d, ci] = dict(
                    qc=qc, kc=kc, kb=kb, vb=vc * beta, gcol=gcol, decay=decay,
                    lmat=_bdot_nt(kb, kc) * jnp.where(m_strict, decay, 0.0),
                    attn=_bdot_nt(qc, kc) * decay)
    tinvs = _tri_inverse_many([pre[t]["lmat"] for t in triples], [t[1] == 1 for t in triples])
    for t, tinv in zip(triples, tinvs):
        p = pre[t]
        p["u"] = _bdot(tinv, p["vb"])
        if has_state:
            p["w"] = _bdot(tinv, p["kb"] * jnp.exp(p["gcol"]))

    for hh in range(hb):
        hs = slice(hh * DK_A, (hh + 1) * DK_A)
        for d in (0, 1):
            rev = d == 1
            state = s0_ref[d, hh] if has_state else None
            order = range(n_chunks - 1, -1, -1) if rev else range(n_chunks)
            for ci in order:
                sl = slice(ci * CHUNK, (ci + 1) * CHUNK)
                p = pre[hh, d, ci]
                u, gcol = p["u"], p["gcol"]
                if state is not None:
                    u = u - _bdot(p["w"], state)
                o = _bdot(p["attn"], u)
                if state is not None:
                    o = o + _bdot(p["qc"] * jnp.exp(gcol), state)
                glast = gcol[0:1, :] if rev else gcol[CHUNK - 1:CHUNK, :]
                upd = _bdot_tn(p["kc"] * jnp.exp(glast - gcol), u)
                state = upd if state is None else state * jnp.exp(glast) + upd
                if rev:
                    acc_ref[sl, hs] = acc_ref[sl, hs] + o
                else:
                    acc_ref[sl, hs] = o
            if sout_ref is not None:
                sout_ref[d, hh] = state

    z = z_ref[...]
    for hh in range(hb):
        hs = slice(hh * DK_A, (hh + 1) * DK_A)
        o = acc_ref[:, hs]
        o = o * lax.rsqrt(jnp.mean(o * o, axis=1, keepdims=True) + RMS_EPS) * norm_ref[...]
        o_ref[:, hs] = (o * _silu(z[:, hs])).astype(o_ref.dtype)


def delta_mixer(proj, conv_w, conv_b, act, cum, cumt, norm, state, *, row0, n_seq, seq_len, period, hb):
    assert row0 % seq_len == 0 and H_A % hb == 0
    n_chunks = seq_len // CHUNK
    rb0 = row0 // seq_len
    has_state = state is not None
    w = hb * DK_A
    nq = H_A // hb
    col = lambda off: (lambda s, h: (rb0 + s, off + h))
    cw = lambda off: (lambda s, h: (0, off + h))
    in_specs = [pl.BlockSpec((seq_len, w), col(0)), pl.BlockSpec((seq_len, w), col(nq)),
                pl.BlockSpec((seq_len, w), col(2 * nq)), pl.BlockSpec((seq_len, w), col(3 * nq)),
                pl.BlockSpec((3, w), cw(0)), pl.BlockSpec((3, w), cw(nq)), pl.BlockSpec((3, w), cw(2 * nq)),
                pl.BlockSpec((1, w), cw(0)), pl.BlockSpec((1, w), cw(nq)), pl.BlockSpec((1, w), cw(2 * nq)),
                pl.BlockSpec((seq_len, LANES), lambda s, h: (rb0 + s, 0)),
                pl.BlockSpec((seq_len, LANES), lambda s, h: (rb0 + s, 0)),
                pl.BlockSpec((LANES, seq_len), lambda s, h: (0, rb0 + s)),
                pl.BlockSpec((1, DV_A), lambda s, h: (0, 0))]
    args = [proj, proj, proj, proj, conv_w, conv_w, conv_w, conv_b, conv_b, conv_b, act, cum, cumt, norm]
    o_spec = pl.BlockSpec((seq_len, w), lambda s, h: (s, h))
    o_shape = jax.ShapeDtypeStruct((n_seq * seq_len, H_A * DV_A), BF16)
    st_spec = pl.BlockSpec((None, 2, hb, DK_A, DV_A), lambda s, h: (s, 0, h, 0, 0))
    if has_state:
        in_specs.append(st_spec)
        args.append(state)
        out_specs, out_shape = o_spec, o_shape
    else:
        out_specs = [o_spec, st_spec]
        out_shape = [o_shape, jax.ShapeDtypeStruct((n_seq, 2, H_A, DK_A, DV_A), F32)]
    return pl.pallas_call(
        functools.partial(_delta_kernel, n_chunks=n_chunks, period=period, has_state=has_state, hb=hb),
        grid=(n_seq, H_A // hb),
        in_specs=in_specs, out_specs=out_specs, out_shape=out_shape,
        scratch_shapes=[pltpu.VMEM((seq_len, w), F32)],
        compiler_params=_cparams("arbitrary", "arbitrary"),
        name="delta_lat" if has_state else "delta_ctx",
    )(*args)


def even_gate_params(a_log, dt_bias, b_i, b_f):
    zeros_a = jnp.zeros((2 * H_A,), F32)
    coef = jnp.concatenate([zeros_a, -jnp.exp(a_log.astype(F32)).reshape(-1), jnp.zeros((4 * H_B,), F32)])
    bias = jnp.concatenate([zeros_a, dt_bias.reshape(-1), b_i.reshape(-1), b_f.reshape(-1)]).astype(F32)
    rev = jnp.concatenate([jnp.repeat(jnp.arange(2, dtype=F32), H_A)] * 2 + [jnp.repeat(jnp.arange(2, dtype=F32), H_B)] * 2)
    p = jnp.stack([coef, bias, rev])
    return jnp.pad(p, ((0, 5), (0, LANES - EV_GATES)))


def _mlstm_kernel(*refs, n_chunks, has_state):
    (q_ref, k_ref, v_ref, og_ref, act_ref, actt_ref, cum_ref, cumt_ref, norm_ref) = refs[:9]
    rest = refs[9:]
    if has_state:
        c0_ref, n0_ref, m0_ref, o_ref, acc_ref = rest
    else:
        o_ref, cout_ref, nm_ref, acc_ref = rest
    h = pl.program_id(1)
    i_col0, f_col0 = 4 * H_A, 4 * H_A + 2 * H_B

    for d in (0, 1):
        rev = d == 1
        m_incl, _ = _dir_masks(rev)
        if has_state:
            cm, nv, m = c0_ref[d], n0_ref[d], m0_ref[d]
        else:
            cm, nv, m = None, None, jnp.zeros((1, 1), F32)
        order = range(n_chunks - 1, -1, -1) if rev else range(n_chunks)
        for ci in order:
            sl = slice(ci * CHUNK, (ci + 1) * CHUNK)
            qc = q_ref[sl, :] * (DK_B ** -0.5)
            kc = k_ref[sl, :]
            vc = v_ref[sl, :]
            li_col = _pick_col(act_ref[sl, :], i_col0 + d * H_B + h)
            li_row = actt_ref[pl.ds(i_col0 + d * H_B + h, 1), sl]
            b_col = _pick_col(cum_ref[sl, :], f_col0 + d * H_B + h)
            b_row = cumt_ref[pl.ds(f_col0 + d * H_B + h, 1), sl]
            dlog = jnp.where(m_incl, b_col - b_row + li_row, NEG)
            inter = b_col + m
            m_q = jnp.maximum(inter, jnp.max(dlog, axis=1, keepdims=True))
            s = _bdot_nt(qc, kc) * jnp.exp(dlog - m_q)
            num = _bdot(s, vc)
            den = jnp.sum(s, axis=1, keepdims=True)
            if cm is not None:
                w_inter = jnp.exp(inter - m_q)
                num = num + w_inter * _bdot(qc, cm)
                den = den + w_inter * jnp.sum(qc * nv, axis=1, keepdims=True)
            hout = num / jnp.maximum(jnp.abs(den), jnp.exp(-m_q))
            b_last = b_col[0:1, :] if rev else b_col[CHUNK - 1:CHUNK, :]
            wlog = b_last - b_col + li_col
            m_new = jnp.maximum(b_last + m, jnp.max(wlog, axis=0, keepdims=True))
            kw = kc * jnp.exp(wlog - m_new)
            c_upd = _bdot_tn(kw, vc)
            n_upd = jnp.sum(kw, axis=0, keepdims=True)
            if cm is not None:
                sc = jnp.exp(b_last + m - m_new)
                cm, nv = sc * cm + c_upd, sc * nv + n_upd
            else:
                cm, nv = c_upd, n_upd
            m = m_new
            if rev:
                acc_ref[sl, :] = acc_ref[sl, :] + hout
            else:
                acc_ref[sl, :] = hout
        if not has_state:
            cout_ref[d] = cm
            nm_ref[d, 0:1, :] = nv
            nm_ref[d, 1:2, :] = jnp.broadcast_to(m, (1, DK_B))
            nm_ref[d, 2:8, :] = jnp.zeros((6, DK_B), F32)

    o = acc_ref[...]
    o = o * lax.rsqrt(jnp.mean(o * o, axis=1, keepdims=True) + RMS_EPS) * norm_ref[...]
    o_ref[...] = (o * jax.nn.sigmoid(og_ref[...])).astype(o_ref.dtype)


def mlstm_mixer(proj, act, actt, cum, cumt, norm, state, *, row0, n_seq, seq_len):
    assert row0 % seq_len == 0
    n_chunks = seq_len // CHUNK
    rb0 = row0 // seq_len
    has_state = state is not None
    q0 = (CONV_A + H_A * DV_A) // LANES
    k0 = q0 + H_B
    v0 = (CONV_A + H_A * DV_A + 2 * H_B * DK_B) // DV_B
    o0 = v0 + H_B
    col = lambda off: (lambda s, h: (rb0 + s, off + h))
    in_specs = [pl.BlockSpec((seq_len, DK_B), col(q0)), pl.BlockSpec((seq_len, DK_B), col(k0)),
                pl.BlockSpec((seq_len, DV_B), col(v0)), pl.BlockSpec((seq_len, DV_B), col(o0)),
                pl.BlockSpec((seq_len, LANES), lambda s, h: (rb0 + s, 0)),
                pl.BlockSpec((LANES, seq_len), lambda s, h: (0, rb0 + s)),
                pl.BlockSpec((seq_len, LANES), lambda s, h: (rb0 + s, 0)),
                pl.BlockSpec((LANES, seq_len), lambda s, h: (0, rb0 + s)),
                pl.BlockSpec((1, DV_B), lambda s, h: (0, 0))]
    args = [proj, proj, proj, proj, act, actt, cum, cumt, norm]
    o_spec = pl.BlockSpec((seq_len, DV_B), lambda s, h: (s, h))
    o_shape = jax.ShapeDtypeStruct((n_seq * seq_len, H_B * DV_B), BF16)
    st_idx = lambda s, h: (s, 0, h, 0, 0)
    if has_state:
        c0, n0, m0 = state
        in_specs += [pl.BlockSpec((None, 2, None, DK_B, DV_B), st_idx),
                     pl.BlockSpec((None, 2, None, 1, DK_B), st_idx),
                     pl.BlockSpec((None, 2, None, 1, 1), st_idx)]
        args += [c0, n0.reshape(n_seq, 2, H_B, 1, DK_B), m0.reshape(n_seq, 2, H_B, 1, 1)]
        out_specs, out_shape = o_spec, o_shape
    else:
        out_specs = [o_spec, pl.BlockSpec((None, 2, None, DK_B, DV_B), st_idx),
                     pl.BlockSpec((None, 2, None, 8, DK_B), st_idx)]
        out_shape = [o_shape, jax.ShapeDtypeStruct((n_seq, 2, H_B, DK_B, DV_B), F32),
                     jax.ShapeDtypeStruct((n_seq, 2, H_B, 8, DK_B), F32)]
    return pl.pallas_call(
        functools.partial(_mlstm_kernel, n_chunks=n_chunks, has_state=has_state),
        grid=(n_seq, H_B),
        in_specs=in_specs, out_specs=out_specs, out_shape=out_shape,
        scratch_shapes=[pltpu.VMEM((seq_len, DV_B), F32)],
        compiler_params=_cparams("arbitrary", "arbitrary"),
        name="mlstm_lat" if has_state else "mlstm_ctx",
    )(*args)


def odd_gate_params(a_log, dt_bias):
    coef = -jnp.exp(a_log.astype(F32)).reshape(-1)
    bias = dt_bias.astype(F32).reshape(-1)
    rev = jnp.repeat(jnp.arange(2, dtype=F32), H_C)
    return jnp.pad(jnp.stack([coef, bias, rev]), ((0, 5), (0, 0)))


def _ssd_kernel(*refs, n_chunks, period, has_state):
    (z_ref, x_ref, b_ref, c_ref, cwx, cwb, cwc, cbx, cbb, cbc, dt_ref, cum_ref, cumt_ref, dskip_ref, norm_ref) = refs[:15]
    rest = refs[15:]
    if has_state:
        s0_ref, o_ref, acc_ref = rest
        sout_ref = None
    else:
        o_ref, sout_ref, acc_ref = rest
    g = pl.program_id(1)

    x = _conv_silu(x_ref[...], cwx, cbx, period)
    bm = _conv_silu(b_ref[...], cwb, cbb, period)
    cm = _conv_silu(c_ref[...], cwc, cbc, period)

    er = lax.broadcasted_iota(jnp.int32, (LANES, GW_C), 0)
    ec = lax.broadcasted_iota(jnp.int32, (LANES, GW_C), 1)
    tr = lax.broadcasted_iota(jnp.int32, (GW_C, LANES), 0)
    tc = lax.broadcasted_iota(jnp.int32, (GW_C, LANES), 1)

    for d in (0, 1):
        rev = d == 1
        m_incl, _ = _dir_masks(rev)
        col0 = d * H_C + g * HG_C
        expand = jnp.where(er == col0 + ec // P_C, 1.0, 0.0)
        expand_t = tc == col0 + tr // P_C
        state = s0_ref[d].reshape(GW_C, N_C) if has_state else None
        order = range(n_chunks - 1, -1, -1) if rev else range(n_chunks)
        for ci in order:
            sl = slice(ci * CHUNK, (ci + 1) * CHUNK)
            xc, bc, cc = x[sl], bm[sl], cm[sl]
            cum_blk = cum_ref[sl, :]
            cum_last = cum_blk[0:1, :] if rev else cum_blk[CHUNK - 1:CHUNK, :]
            xdt = xc * _dot_exact_rhs(dt_ref[sl, :], expand)
            scores = _bdot_nt(cc, bc)
            for hh in range(HG_C):
                ccol = _pick_col(cum_blk, col0 + hh)
                crow = cumt_ref[pl.ds(col0 + hh, 1), sl]
                seg = jnp.exp(jnp.where(m_incl, ccol - crow, NEG))
                y = _bdot(scores * seg, xdt[:, hh * P_C:(hh + 1) * P_C])
                cs = slice(hh * P_C, (hh + 1) * P_C)
                if rev:
                    acc_ref[sl, cs] = acc_ref[sl, cs] + y
                else:
                    acc_ref[sl, cs] = y
            if state is not None:
                y_in = _bdot_nt(cc, state) * _dot_exact_rhs(jnp.exp(cum_blk), expand)
                acc_ref[sl, :] = acc_ref[sl, :] + y_in
            dend = _dot_exact_rhs(jnp.exp(jnp.minimum(cum_last - cum_blk, 0.0)), expand)
            upd = _bdot_tn(xdt * dend, bc)
            if state is not None:
                tot = jnp.sum(jnp.where(expand_t, jnp.broadcast_to(cum_last, (GW_C, LANES)), 0.0), axis=1, keepdims=True)
                state = state * jnp.exp(tot) + upd
            else:
                state = upd
        if sout_ref is not None:
            sout_ref[d] = state.reshape(HG_C, P_C, N_C)

    y = acc_ref[...] + dskip_ref[...] * x
    y = y * _silu(z_ref[...])
    y = y * lax.rsqrt(jnp.mean(y * y, axis=1, keepdims=True) + RMS_EPS) * norm_ref[...]
    o_ref[...] = y.astype(o_ref.dtype)


def ssd_mixer(proj, conv_w, conv_b, dt, cum, cumt, dskip, norm, state, *, row0, n_seq, seq_len, period):
    assert row0 % seq_len == 0
    n_chunks = seq_len // CHUNK
    rb0 = row0 // seq_len
    has_state = state is not None
    xb0 = D_INNER // GW_C
    bb0 = 2 * D_INNER // N_C
    cb0 = bb0 + G_C
    wb0 = D_INNER // N_C
    wc0 = wb0 + G_C
    col = lambda off: (lambda s, g: (rb0 + s, off + g))
    cw = lambda off: (lambda s, g: (0, off + g))
    in_specs = [pl.BlockSpec((seq_len, GW_C), col(0)), pl.BlockSpec((seq_len, GW_C), col(xb0)),
                pl.BlockSpec((seq_len, N_C), col(bb0)), pl.BlockSpec((seq_len, N_C), col(cb0)),
                pl.BlockSpec((3, GW_C), cw(0)), pl.BlockSpec((3, N_C), cw(wb0)), pl.BlockSpec((3, N_C), cw(wc0)),
                pl.BlockSpec((1, GW_C), cw(0)), pl.BlockSpec((1, N_C), cw(wb0)), pl.BlockSpec((1, N_C), cw(wc0)),
                pl.BlockSpec((seq_len, LANES), lambda s, g: (rb0 + s, 0)),
                pl.BlockSpec((seq_len, LANES), lambda s, g: (rb0 + s, 0)),
                pl.BlockSpec((LANES, seq_len), lambda s, g: (0, rb0 + s)),
                pl.BlockSpec((1, GW_C), cw(0)), pl.BlockSpec((1, GW_C), cw(0))]
    args = [proj, proj, proj, proj, conv_w, conv_w, conv_w, conv_b, conv_b, conv_b, dt, cum, cumt, dskip, norm]
    o_spec = pl.BlockSpec((seq_len, GW_C), lambda s, g: (s, g))
    o_shape = jax.ShapeDtypeStruct((n_seq * seq_len, D_INNER), BF16)
    st_spec = pl.BlockSpec((None, 2, HG_C, P_C, N_C), lambda s, g: (s, 0, g, 0, 0))
    if has_state:
        in_specs.append(st_spec)
        args.append(state)
        out_specs, out_shape = o_spec, o_shape
    else:
        out_specs = [o_spec, st_spec]
        out_shape = [o_shape, jax.ShapeDtypeStruct((n_seq, 2, H_C, P_C, N_C), F32)]
    return pl.pallas_call(
        functools.partial(_ssd_kernel, n_chunks=n_chunks, period=period, has_state=has_state),
        grid=(n_seq, G_C),
        in_specs=in_specs, out_specs=out_specs, out_shape=out_shape,
        scratch_shapes=[pltpu.VMEM((seq_len, GW_C), F32)],
        compiler_params=_cparams("arbitrary", "arbitrary"),
        name="ssd_lat" if has_state else "ssd_ctx",
    )(*args)


def _dot3(a, b):
    a_hi = a.astype(BF16)
    a_lo = (a - a_hi.astype(F32)).astype(BF16)
    b_hi = b.astype(BF16)
    b_lo = (b - b_hi.astype(F32)).astype(BF16)
    d = lambda p, q: jnp.dot(p, q, preferred_element_type=F32)
    return d(a_hi, b_hi) + (d(a_hi, b_lo) + d(a_lo, b_hi))


def _dot3_nt(a, b):
    a_hi = a.astype(BF16)
    a_lo = (a - a_hi.astype(F32)).astype(BF16)
    b_hi = b.astype(BF16)
    b_lo = (b - b_hi.astype(F32)).astype(BF16)
    d = lambda p, q: lax.dot_general(p, q, (((1,), (1,)), ((), ())), preferred_element_type=F32)
    return d(a_hi, b_hi) + (d(a_hi, b_lo) + d(a_lo, b_hi))


def _resid_ln_kernel(*refs, n_y, n_gathered, gate, sh, sc, want_h, want_logits):
    x_ref = refs[0]
    y_refs = refs[1:1 + n_y]
    m_ref, mn_ref, g_ref, b_ref = refs[1 + n_y:5 + n_y]
    rest = list(refs[5 + n_y:])
    if n_gathered:
        gath_ref, gw_ref = rest.pop(0), rest.pop(0)
    rw_ref = rest.pop(0) if want_logits else None
    xo_ref = rest.pop(0)
    y = y_refs[0][...].astype(F32)
    for r in y_refs[1:]:
        y = y + r[...].astype(F32)
    for kk in range(n_gathered):
        y = y + gw_ref[:, kk:kk + 1] * gath_ref[kk].astype(F32)
    v = ALPHA * x_ref[...] + m_ref[gate:gate + 1, :] * y
    mu = jnp.mean(v, axis=1, keepdims=True)
    vc = v - mu
    var = jnp.mean(vc * vc, axis=1, keepdims=True)
    xn = vc * lax.rsqrt(var + LN_EPS) * g_ref[...] + b_ref[...]
    xo_ref[...] = xn
    if want_h:
        hm = xn * (1.0 + mn_ref[sc:sc + 1, :]) + mn_ref[sh:sh + 1, :]
        rest.pop(0)[...] = hm.astype(BF16)
        if want_logits:
            rest.pop(0)[...] = _dot3_nt(rw_ref[...], hm)


def resid_ln(x, ys, mod, mod_next, ln_g, ln_b, router_w, *, gate, sh, sc, want_h, t_ctx, lat_len, gathered=None,
             h_rows=None):
    t, d = x.shape
    tm = 256
    want_logits = router_w is not None
    grp = lambda i: (_group_of_block(i, tm, t_ctx, lat_len), 0, 0)
    row = pl.BlockSpec((tm, d), lambda i: (i, 0))
    vec = pl.BlockSpec((1, d), lambda i: (0, 0))
    in_specs = [row] * (1 + len(ys)) + [pl.BlockSpec((None, 6, d), grp), pl.BlockSpec((None, 6, d), grp), vec, vec]
    args = [x, *ys, mod, mod_next, ln_g.reshape(1, d), ln_b.reshape(1, d)]
    n_gathered = 0
    if gathered is not None:
        n_gathered = gathered[0].shape[0]
        in_specs += [pl.BlockSpec((n_gathered, tm, d), lambda i: (0, i, 0)),
                     pl.BlockSpec((tm, n_gathered), lambda i: (i, 0))]
        args += list(gathered)
    out_specs, out_shape = [row], [jax.ShapeDtypeStruct((t, d), F32)]
    if want_logits:
        n_e = router_w.shape[1]
        in_specs.append(pl.BlockSpec((n_e, d), lambda i: (0, 0)))
        args.append(router_w.T)
    if want_h:
        out_specs.append(row)
        out_shape.append(jax.ShapeDtypeStruct((h_rows or t, d), BF16))
    if want_logits:
        out_specs.append(pl.BlockSpec((n_e, tm), lambda i: (0, i)))
        out_shape.append(jax.ShapeDtypeStruct((n_e, t), F32))
    return pl.pallas_call(
        functools.partial(_resid_ln_kernel, n_y=len(ys), n_gathered=n_gathered, gate=gate, sh=sh, sc=sc, want_h=want_h,
                          want_logits=want_logits),
        grid=(t // tm,),
        in_specs=in_specs, out_specs=out_specs, out_shape=out_shape,
        compiler_params=_cparams("arbitrary"),
        name="resid_ln",
    )(*args)


def _ffn_kernel(be_ref, nu_ref, x_ref, wg_ref, wu_ref, wd_ref, o_ref, g_bf, u_bf, d_bf):
    b = pl.program_id(0)

    @pl.when(b < nu_ref[0])
    def _():
        prev = be_ref[jnp.maximum(b - 1, 0)]

        @pl.when((b == 0) | (be_ref[b] != prev))
        def _():
            g_bf[...] = wg_ref[...].astype(BF16)
            u_bf[...] = wu_ref[...].astype(BF16)
            d_bf[...] = wd_ref[...].astype(BF16)

        x = x_ref[...]
        hg = jnp.dot(x, g_bf[...], preferred_element_type=F32)
        hu = jnp.dot(x, u_bf[...], preferred_element_type=F32)
        a = (_silu(hg) * hu).astype(BF16)
        o_ref[...] = jnp.dot(a, d_bf[...], preferred_element_type=F32).astype(o_ref.dtype)


def expert_ffn(xs, blk_e, n_used, w_gate, w_up, w_down, layer, *, tm, out_dtype=BF16, name="expert_ffn"):
    d = xs.shape[1]
    de = w_gate.shape[3]
    n_blk = blk_e.shape[0]
    r = n_blk * tm
    wsel = lambda b, be, nu: (layer, be[b], 0, 0)
    grid_spec = pltpu.PrefetchScalarGridSpec(
        num_scalar_prefetch=2,
        grid=(n_blk,),
        in_specs=[pl.BlockSpec((tm, d), lambda b, be, nu: (b, 0)),
                  pl.BlockSpec((None, None, d, de), wsel), pl.BlockSpec((None, None, d, de), wsel),
                  pl.BlockSpec((None, None, de, d), wsel)],
        out_specs=pl.BlockSpec((tm, d), lambda b, be, nu: (b, 0)),
        scratch_shapes=[pltpu.VMEM((d, de), BF16), pltpu.VMEM((d, de), BF16), pltpu.VMEM((de, d), BF16)],
    )
    return pl.pallas_call(
        _ffn_kernel,
        grid_spec=grid_spec,
        out_shape=jax.ShapeDtypeStruct((r, d), out_dtype),
        compiler_params=_cparams("arbitrary"),
        name=name,
    )(blk_e, n_used, xs, w_gate, w_up, w_down)


ROUTE_TM = 512
GROUP_SIZE = N_EXP // N_GROUPS


def _first_argmax(v, idx, axis, sentinel):
    mx = jnp.max(v, axis=axis, keepdims=True)
    return mx, jnp.min(jnp.where(v == mx, idx, sentinel), axis=axis, keepdims=True)


def _route_kernel(lt_ref, bias_ref, idx_ref, w_ref, rank_ref, cnt_ref, carry_ref):
    i = pl.program_id(0)
    tm = lt_ref.shape[1]

    @pl.when(i == 0)
    def _():
        carry_ref[...] = jnp.zeros_like(carry_ref)

    scores = jax.nn.sigmoid(lt_ref[...])
    biased = scores + bias_ref[...]
    b3 = biased.reshape(N_GROUPS, GROUP_SIZE, tm)
    mem = lax.broadcasted_iota(jnp.int32, b3.shape, 1).astype(F32)
    m1, first = _first_argmax(b3, mem, 1, float(GROUP_SIZE))
    m2 = jnp.max(jnp.where(mem == first, -jnp.inf, b3), axis=1, keepdims=True)
    gs = (m1 + m2).reshape(N_GROUPS, tm)
    gi = lax.broadcasted_iota(jnp.int32, gs.shape, 0).astype(F32)
    gsel = jnp.zeros(gs.shape, F32)
    cur = gs
    for _ in range(TOPK_GROUPS):
        _, pick = _first_argmax(cur, gi, 0, float(N_GROUPS))
        hit = gi == pick
        gsel = jnp.where(hit, 1.0, gsel)
        cur = jnp.where(hit, -jnp.inf, cur)
    masked = jnp.where(gsel.reshape(N_GROUPS, 1, tm) > 0.5, b3, -jnp.inf).reshape(N_EXP, tm)

    ei = lax.broadcasted_iota(jnp.int32, masked.shape, 0).astype(F32)
    picks, sel_scores = [], []
    chosen = jnp.zeros(masked.shape, F32)
    cur = masked
    for _ in range(TOP_K):
        _, pick = _first_argmax(cur, ei, 0, float(N_EXP))
        hit = ei == pick
        picks.append(pick)
        sel_scores.append(jnp.sum(jnp.where(hit, scores, 0.0), axis=0, keepdims=True))
        chosen = jnp.where(hit, 1.0, chosen)
        cur = jnp.where(hit, -jnp.inf, cur)

    r, c = _tri_masks(tm)
    before = jnp.where(r < c, 1.0, 0.0).astype(BF16)
    rank = jnp.dot(chosen.astype(BF16), before, preferred_element_type=F32) + carry_ref[...]
    carry_ref[...] = carry_ref[...] + jnp.sum(chosen, axis=1, keepdims=True)
    cnt_ref[...] = carry_ref[...]

    total = sel_scores[0]
    for s in sel_scores[1:]:
        total = total + s
    for k in range(TOP_K):
        idx_ref[k:k + 1, :] = picks[k].astype(jnp.int32)
        w_ref[k:k + 1, :] = sel_scores[k] / total * ROUTED_SCALE
        rank_ref[k:k + 1, :] = jnp.sum(jnp.where(ei == picks[k], rank, 0.0), axis=0, keepdims=True).astype(jnp.int32)


def route(logits_t, router_bias):
    n_e, t = logits_t.shape
    tm = ROUTE_TM
    kt = pl.BlockSpec((TOP_K, tm), lambda i: (0, i))
    return pl.pallas_call(
        _route_kernel,
        grid=(t // tm,),
        in_specs=[pl.BlockSpec((n_e, tm), lambda i: (0, i)), pl.BlockSpec((n_e, 1), lambda i: (0, 0))],
        out_specs=[kt, kt, kt, pl.BlockSpec((n_e, 1), lambda i: (0, 0))],
        out_shape=[jax.ShapeDtypeStruct((TOP_K, t), jnp.int32), jax.ShapeDtypeStruct((TOP_K, t), F32),
                   jax.ShapeDtypeStruct((TOP_K, t), jnp.int32), jax.ShapeDtypeStruct((n_e, 1), F32)],
        scratch_shapes=[pltpu.VMEM((n_e, 1), F32)],
        compiler_params=_cparams("arbitrary"),
        name="route",
    )(logits_t, router_bias.reshape(n_e, 1))


def moe(h, logits_t, router_bias, e_gate, e_up, e_down, s_gate, s_up, s_down, layer):
    d = h.shape[1]
    t = logits_t.shape[1]
    tm = 256
    top_e, wts, rank, counts = route(logits_t, router_bias)
    counts = counts.reshape(-1).astype(jnp.int32)
    n_assign = t * TOP_K
    padded = (counts + tm - 1) // tm * tm
    pad_end = jnp.cumsum(padded)
    pad_start = pad_end - padded
    n_blk = n_assign // tm + N_EXP
    blk_first = jnp.arange(n_blk, dtype=jnp.int32) * tm
    blk_e = jnp.minimum(jnp.sum((pad_end[None, :] <= blk_first[:, None]).astype(jnp.int32), axis=1), N_EXP - 1)
    n_used = (pad_end[-1] // tm).astype(jnp.int32).reshape(1)
    expert_ids = jnp.arange(N_EXP, dtype=jnp.int32)
    dest = jnp.sum(jnp.where(top_e[..., None] == expert_ids, pad_start, 0), axis=-1) + rank
    tok = jnp.broadcast_to(jnp.arange(t, dtype=jnp.int32), (TOP_K, t))
    slot_tok = jnp.zeros((n_blk * tm,), jnp.int32).at[dest.reshape(-1)].set(tok.reshape(-1), unique_indices=True)
    xs = h.at[slot_tok].get(mode="promise_in_bounds")
    ys = expert_ffn(xs, blk_e, n_used, e_gate, e_up, e_down, layer, tm=tm, name="routed_ffn")
    routed_rows = ys.at[dest.reshape(-1)].get(mode="promise_in_bounds").reshape(TOP_K, t, d)
    n_sh = t // tm
    shared = expert_ffn(h, jnp.zeros((n_sh,), jnp.int32), jnp.full((1,), n_sh, jnp.int32),
                        s_gate[:, None], s_up[:, None], s_down[:, None], layer, tm=tm, name="shared_ffn")
    return shared, routed_rows, wts.T


def kernel(x_prompt, x_sample, state_dn, state_ml_C, state_ml_n, state_ml_m, state_ssd, c, c_ctx,
           mod_w, mod_b, ln1_g, ln1_b, ln2_g, ln2_b, router_w, router_bias, exp_gate, exp_up, exp_down,
           sh_gate, sh_up, sh_down, ev_w_in, ev_conv_w, ev_conv_b, dn_A_log, dn_dt_bias, ml_b_i, ml_b_f,
           dn_norm, ml_norm, ev_w_out, od_w_in, od_conv_w, od_conv_b, ssd_A_log, ssd_dt_bias, ssd_D,
           ssd_norm, od_w_out):
    bp, sl, d = x_prompt.shape
    bl, ll, _ = x_sample.shape
    depth = mod_w.shape[0]
    t_ctx = bp * sl
    x = jnp.concatenate([x_prompt.reshape(t_ctx, d), x_sample.reshape(bl * ll, d)], axis=0)
    cvec = jnp.concatenate([c_ctx[None], c, jnp.zeros((8 - 1 - bl, d), F32)], axis=0)
    mods = compute_mods(cvec, mod_w, mod_b)[:, :1 + bl].reshape(depth, 1 + bl, 6, d)
    geo = dict(t_ctx=t_ctx, lat_len=ll)
    ctx = dict(row0=0, n_seq=bp, seq_len=sl)
    lat = dict(row0=t_ctx, n_seq=bl, seq_len=ll)
    rows = lambda a, b: jnp.concatenate([a, b], axis=0)

    h = modulate(x, mods[0], 0, 1, **geo)
    new_dn, new_c, new_n, new_m, new_ssd = [], [], [], [], []
    for l in range(depth):
        j = l // 2
        if l % 2 == 0:
            w_in = ev_w_in[j]
            proj = matmul(h, w_in, tm=1024, tn=512, n_out=EV_MAIN, name="ev_in_proj")
            graw = matmul(h, w_in, tm=1024, tn=LANES, n_out=LANES, col_block_off=EV_MAIN // LANES,
                          valid_cols=EV_GATES, name="ev_gate_proj")
            act, cum = gate_prep(graw, even_gate_params(dn_A_log[j], dn_dt_bias[j], ml_b_i[j], ml_b_f[j]), "even")
            actt, cumt = act.T, cum.T
            cw, cb, dnn, mln = ev_conv_w[j], ev_conv_b[j].reshape(1, -1), dn_norm[j].reshape(1, -1), ml_norm[j].reshape(1, -1)
            oa_c, s_dn = delta_mixer(proj, cw, cb, act, cum, cumt, dnn, None, period=sl, hb=4, **ctx)
            oa_l = delta_mixer(proj, cw, cb, act, cum, cumt, dnn, state_dn[:, j], period=GRID_W, hb=1, **lat)
            ob_c, s_c, s_nm = mlstm_mixer(proj, act, actt, cum, cumt, mln, None, **ctx)
            ob_l = mlstm_mixer(proj, act, actt, cum, cumt, mln,
                               (state_ml_C[:, j], state_ml_n[:, j], state_ml_m[:, j]), **lat)
            mixed = jnp.concatenate([rows(oa_c, oa_l), rows(ob_c, ob_l)], axis=1)
            y = matmul(mixed, ev_w_out[j], tm=1024, tn=512, n_out=d, name="ev_out_proj")
            new_dn.append(s_dn)
            new_c.append(s_c)
            new_n.append(s_nm[:, :, :, 0, :])
            new_m.append(s_nm[:, :, :, 1, 0])
        else:
            w_in = od_w_in[j]
            proj = matmul(h, w_in, tm=1024, tn=512, n_out=OD_MAIN, name="od_in_proj")
            draw = matmul(h, w_in, tm=1024, tn=LANES, n_out=LANES, col_block_off=OD_MAIN // LANES, name="od_dt_proj")
            dt, cum = gate_prep(draw, odd_gate_params(ssd_A_log[j], ssd_dt_bias[j]), "odd")
            cumt = cum.T
            cw, cb = od_conv_w[j], od_conv_b[j].reshape(1, -1)
            dsk, nrm = jnp.repeat(ssd_D[j], P_C).reshape(1, -1), ssd_norm[j].reshape(1, -1)
            o_c, s_ssd = ssd_mixer(proj, cw, cb, dt, cum, cumt, dsk, nrm, None, period=sl, **ctx)
            o_l = ssd_mixer(proj, cw, cb, dt, cum, cumt, dsk, nrm, state_ssd[:, j], period=GRID_W, **lat)
            y = matmul(rows(o_c, o_l), od_w_out[j], tm=512, tn=512, n_out=d, name="od_out_proj")
            new_ssd.append(s_ssd)
        x, h2, logits_t = resid_ln(x, [y], mods[l], mods[l], ln1_g[l], ln1_b[l], router_w[l],
                                   gate=2, sh=3, sc=4, want_h=True, h_rows=GATHER_SRC_ROWS, **geo)
        shared, routed_rows, wts = moe(h2, logits_t, router_bias[l], exp_gate, exp_up, exp_down,
                                       sh_gate, sh_up, sh_down, l)
        last = l == depth - 1
        res = resid_ln(x, [shared], mods[l], mods[min(l + 1, depth - 1)], ln2_g[l], ln2_b[l], None,
                       gate=5, sh=0, sc=1, want_h=not last, gathered=(routed_rows, wts), **geo)
        x = res[0]
        if not last:
            h = res[1]
    y_prompt = x[:t_ctx].reshape(bp, sl, d)
    y_sample = x[t_ctx:].reshape(bl, ll, d)
    return (y_prompt, y_sample, jnp.stack(new_dn, axis=1), jnp.stack(new_c, axis=1), jnp.stack(new_n, axis=1),
            jnp.stack(new_m, axis=1), jnp.stack(new_ssd, axis=1))
```

```python
import functools

import jax
import jax.numpy as jnp
from jax import lax
from jax.experimental import pallas as pl
from jax.experimental.pallas import tpu as pltpu

F32 = jnp.float32
BF16 = jnp.bfloat16

D_MODEL = 2048
DEPTH = 2
GRID_W = 64
ALPHA = (2 * DEPTH) ** 0.25
LN_EPS = 1e-5
RMS_EPS = 1e-6

H_A, DK_A, DV_A = 8, 128, 128
H_B, DK_B, DV_B = 4, 128, 256
CONV_A = 2 * H_A * DK_A + H_A * DV_A
EV_MAIN = CONV_A + H_A * DV_A + 2 * H_B * DK_B + 2 * H_B * DV_B
EV_GATES = 4 * H_A + 4 * H_B

D_INNER = 2 * D_MODEL
P_C, N_C, G_C = 64, 128, 8
H_C = D_INNER // P_C
HG_C = H_C // G_C
GW_C = D_INNER // G_C
OD_MAIN = 2 * D_INNER + 2 * G_C * N_C

N_EXP, TOP_K, N_GROUPS, TOPK_GROUPS = 64, 8, 8, 4
D_EXP = 512
ROUTED_SCALE = 2.5

GATHER_SRC_ROWS = 16384
CHUNK = 256
LANES = 128
VMEM_LIMIT = 56 * 1024 * 1024
NEG = -1e30


def _cparams(*sem):
    return pltpu.CompilerParams(dimension_semantics=sem, vmem_limit_bytes=VMEM_LIMIT)


def _bdot(a, b):
    return jnp.dot(a.astype(BF16), b.astype(BF16), preferred_element_type=F32)


def _bdot_nt(a, b):
    return lax.dot_general(a.astype(BF16), b.astype(BF16), (((1,), (1,)), ((), ())), preferred_element_type=F32)


def _bdot_tn(a, b):
    return lax.dot_general(a.astype(BF16), b.astype(BF16), (((0,), (0,)), ((), ())), preferred_element_type=F32)


def _split3(a):
    hi = a.astype(BF16)
    r = a - hi.astype(F32)
    mid = r.astype(BF16)
    lo = (r - mid.astype(F32)).astype(BF16)
    return hi, mid, lo


def _dot_exact_rhs(a, b_exact, passes=3):
    hi, mid, lo = _split3(a)
    bb = b_exact.astype(BF16)
    d = lambda p: jnp.dot(p, bb, preferred_element_type=F32)
    return d(hi) + d(mid) + d(lo) if passes == 3 else d(hi) + d(mid)


def _dot_exact_lhs(a_exact, b):
    hi, mid, lo = _split3(b)
    aa = a_exact.astype(BF16)
    d = lambda p: jnp.dot(aa, p, preferred_element_type=F32)
    return d(hi) + d(mid) + d(lo)


def _silu(x):
    return x * jax.nn.sigmoid(x)


def _softplus(x):
    return jnp.maximum(x, 0.0) + jnp.log(1.0 + jnp.exp(-jnp.abs(x)))


def _group_of_block(i, tm, t_ctx, lat_len):
    return jnp.maximum(i * tm - t_ctx, -1) // lat_len + 1


def _mod_kernel(c_ref, w_ref, b_ref, o_ref):
    c = c_ref[...]
    o_ref[...] = _bdot(_silu(c), w_ref[...]) + b_ref[...]


def compute_mods(cvec, mod_w, mod_b):
    depth, d, n = mod_w.shape
    tn = 512
    return pl.pallas_call(
        _mod_kernel,
        grid=(depth, n // tn),
        in_specs=[pl.BlockSpec((8, d), lambda l, j: (0, 0)),
                  pl.BlockSpec((None, d, tn), lambda l, j: (l, 0, j)),
                  pl.BlockSpec((None, 1, tn), lambda l, j: (l, 0, j))],
        out_specs=pl.BlockSpec((None, 8, tn), lambda l, j: (l, 0, j)),
        out_shape=jax.ShapeDtypeStruct((depth, 8, n), F32),
        compiler_params=_cparams("arbitrary", "arbitrary"),
        name="mod_vectors",
    )(cvec, mod_w, mod_b.reshape(depth, 1, n))


def _modulate_kernel(x_ref, m_ref, o_ref, *, sh, sc):
    o_ref[...] = (x_ref[...] * (1.0 + m_ref[sc:sc + 1, :]) + m_ref[sh:sh + 1, :]).astype(o_ref.dtype)


def modulate(x, mod, sh, sc, t_ctx, lat_len):
    t, d = x.shape
    tm = 512
    return pl.pallas_call(
        functools.partial(_modulate_kernel, sh=sh, sc=sc),
        grid=(t // tm,),
        in_specs=[pl.BlockSpec((tm, d), lambda i: (i, 0)),
                  pl.BlockSpec((None, 6, d), lambda i: (_group_of_block(i, tm, t_ctx, lat_len), 0, 0))],
        out_specs=pl.BlockSpec((tm, d), lambda i: (i, 0)),
        out_shape=jax.ShapeDtypeStruct((t, d), BF16),
        compiler_params=_cparams("arbitrary"),
        name="modulate",
    )(x, mod)


def _matmul_kernel(x_ref, w_ref, o_ref, wbf_ref, *, valid_cols):
    @pl.when(pl.program_id(1) == 0)
    def _():
        wbf_ref[...] = w_ref[...].astype(BF16)

    y = jnp.dot(x_ref[...], wbf_ref[...], preferred_element_type=F32)
    if valid_cols is not None:
        col = lax.broadcasted_iota(jnp.int32, y.shape, 1)
        y = jnp.where(col < valid_cols, y, 0.0)
    o_ref[...] = y.astype(o_ref.dtype)


def matmul(x, w, *, tm, tn, n_out, col_block_off=0, valid_cols=None, out_dtype=F32, name="matmul"):
    m, k = x.shape
    return pl.pallas_call(
        functools.partial(_matmul_kernel, valid_cols=valid_cols),
        grid=(n_out // tn, m // tm),
        in_specs=[pl.BlockSpec((tm, k), lambda j, i: (i, 0)),
                  pl.BlockSpec((k, tn), lambda j, i: (0, j + col_block_off))],
        out_specs=pl.BlockSpec((tm, tn), lambda j, i: (i, j)),
        out_shape=jax.ShapeDtypeStruct((m, n_out), out_dtype),
        scratch_shapes=[pltpu.VMEM((k, tn), BF16)],
        compiler_params=_cparams("arbitrary", "arbitrary"),
        name=name,
    )(x, w)


def _tri_masks(n):
    r = lax.broadcasted_iota(jnp.int32, (n, n), 0)
    c = lax.broadcasted_iota(jnp.int32, (n, n), 1)
    return r, c


def _gate_kernel(raw_ref, p_ref, act_ref, cum_ref, *, mode):
    x = raw_ref[...]
    coef, bias, rev = p_ref[0:1, :], p_ref[1:2, :], p_ref[2:3, :]
    col = lax.broadcasted_iota(jnp.int32, x.shape, 1)
    xb = x + bias
    if mode == "even":
        act = jnp.where(col < 2 * H_A, jax.nn.sigmoid(xb),
                        jnp.where(col < 4 * H_A, coef * _softplus(xb),
                                  jnp.where(col < 4 * H_A + 2 * H_B, xb,
                                            jnp.minimum(xb, 0.0) - jnp.log(1.0 + jnp.exp(-jnp.abs(xb))))))
        to_sum = act
    else:
        act = _softplus(xb)
        to_sum = act * coef
    r, c = _tri_masks(CHUNK)
    lower = jnp.where(c <= r, 1.0, 0.0)
    upper = jnp.where(c >= r, 1.0, 0.0)
    cum_f = _dot_exact_lhs(lower, to_sum)
    cum_r = _dot_exact_lhs(upper, to_sum)
    act_ref[...] = act
    cum_ref[...] = jnp.where(rev > 0.5, cum_r, cum_f)


def gate_prep(raw, params, mode):
    t = raw.shape[0]
    return pl.pallas_call(
        functools.partial(_gate_kernel, mode=mode),
        grid=(t // CHUNK,),
        in_specs=[pl.BlockSpec((CHUNK, LANES), lambda i: (i, 0)),
                  pl.BlockSpec((8, LANES), lambda i: (0, 0))],
        out_specs=[pl.BlockSpec((CHUNK, LANES), lambda i: (i, 0))] * 2,
        out_shape=[jax.ShapeDtypeStruct((t, LANES), F32)] * 2,
        compiler_params=_cparams("arbitrary"),
        name="gate_prep_" + mode,
    )(raw, params)


def _conv_silu(x, cw_ref, cb_ref, period):
    n = x.shape[0]
    row = lax.broadcasted_iota(jnp.int32, x.shape, 0) % period
    prev = jnp.where(row == 0, 0.0, pltpu.roll(x, 1, 0))
    nxt = jnp.where(row == period - 1, 0.0, pltpu.roll(x, n - 1, 0))
    y = cb_ref[...] + prev * cw_ref[0:1, :] + x * cw_ref[1:2, :] + nxt * cw_ref[2:3, :]
    return _silu(y)


def _pick_col(blk, idx):
    lane = lax.broadcasted_iota(jnp.int32, blk.shape, 1)
    return jnp.sum(jnp.where(lane == idx, blk, 0.0), axis=1, keepdims=True)


def _dir_masks(rev):
    r, c = _tri_masks(CHUNK)
    if rev:
        return c >= r, c > r
    return c <= r, c < r


def _tri_inverse(lmat, rev):
    return _tri_inverse_many([lmat], [rev])[0]


def _tri_inverse_many(lmats, revs):
    r, c = _tri_masks(CHUNK)
    eye = jnp.where(r == c, 1.0, 0.0)

    def off_mask(s, rev):
        same = (r // (2 * s)) == (c // (2 * s))
        r_hi = (r // s) % 2
        c_hi = (c // s) % 2
        return same & ((r_hi == 0) & (c_hi == 1) if rev else (r_hi == 1) & (c_hi == 0))

    masks = {rev: off_mask(1, rev) for rev in set(revs)}
    ts = [eye - jnp.where(masks[rev], lm, 0.0) for lm, rev in zip(lmats, revs)]
    s = 2
    while s < CHUNK:
        masks = {rev: off_mask(s, rev) for rev in set(revs)}
        ps = [_bdot(t, jnp.where(masks[rev], lm, 0.0)) for t, lm, rev in zip(ts, lmats, revs)]
        ts = [t - _bdot(p, t) for p, t in zip(ps, ts)]
        s *= 2
    return ts


def _delta_kernel(*refs, n_chunks, period, has_state, hb):
    (q_ref, k_ref, v_ref, z_ref, cwq, cwk, cwv, cbq, cbk, cbv, act_ref, cum_ref, cumt_ref, norm_ref) = refs[:14]
    rest = refs[14:]
    if has_state:
        s0_ref, o_ref, acc_ref = rest
        sout_ref = None
    else:
        o_ref, sout_ref, acc_ref = rest
    h0 = pl.program_id(1) * hb

    q = _conv_silu(q_ref[...], cwq, cbq, period)
    k = _conv_silu(k_ref[...], cwk, cbk, period)
    v = _conv_silu(v_ref[...], cwv, cbv, period)

    triples = [(hh, d, ci) for hh in range(hb) for d in (0, 1) for ci in range(n_chunks)]
    pre = {}
    for hh in range(hb):
        hs = slice(hh * DK_A, (hh + 1) * DK_A)
        qh, kh = q[:, hs], k[:, hs]
        qh = qh * lax.rsqrt(jnp.sum(qh * qh, axis=1, keepdims=True) + RMS_EPS) * (DK_A ** -0.5)
        kh = kh * lax.rsqrt(jnp.sum(kh * kh, axis=1, keepdims=True) + RMS_EPS)
        for d in (0, 1):
            m_incl, m_strict = _dir_masks(d == 1)
            for ci in range(n_chunks):
                sl = slice(ci * CHUNK, (ci + 1) * CHUNK)
                qc, kc, vc = qh[sl], kh[sl], v[sl, hs]
                beta = _pick_col(act_ref[sl, :], d * H_A + h0 + hh)
                gcol = _pick_col(cum_ref[sl, :], 2 * H_A + d * H_A + h0 + hh)
                grow = cumt_ref[pl.ds(2 * H_A + d * H_A + h0 + hh, 1), sl]
                decay = jnp.exp(jnp.where(m_incl, gcol - grow, NEG))
                kb = kc * beta
                pre[hh, d, ci] = dict(
                    qc=qc, kc=kc, kb=kb, vb=vc * beta, gcol=gcol, decay=decay,
                    lmat=_bdot_nt(kb, kc) * jnp.where(m_strict, decay, 0.0),
                    attn=_bdot_nt(qc, kc) * decay)
    tinvs = _tri_inverse_many([pre[t]["lmat"] for t in triples], [t[1] == 1 for t in triples])
    for t, tinv in zip(triples, tinvs):
        p = pre[t]
        p["u"] = _bdot(tinv, p["vb"])
        if has_state:
            p["w"] = _bdot(tinv, p["kb"] * jnp.exp(p["gcol"]))

    for hh in range(hb):
        hs = slice(hh * DK_A, (hh + 1) * DK_A)
        for d in (0, 1):
            rev = d == 1
            state = s0_ref[d, hh] if has_state else None
            order = range(n_chunks - 1, -1, -1) if rev else range(n_chunks)
            for ci in order:
                sl = slice(ci * CHUNK, (ci + 1) * CHUNK)
                p = pre[hh, d, ci]
                u, gcol = p["u"], p["gcol"]
                if state is not None:
                    u = u - _bdot(p["w"], state)
                o = _bdot(p["attn"], u)
                if state is not None:
                    o = o + _bdot(p["qc"] * jnp.exp(gcol), state)
                glast = gcol[0:1, :] if rev else gcol[CHUNK - 1:CHUNK, :]
                upd = _bdot_tn(p["kc"] * jnp.exp(glast - gcol), u)
                state = upd if state is None else state * jnp.exp(glast) + upd
                if rev:
                    acc_ref[sl, hs] = acc_ref[sl, hs] + o
                else:
                    acc_ref[sl, hs] = o
            if sout_ref is not None:
                sout_ref[d, hh] = state

    z = z_ref[...]
    for hh in range(hb):
        hs = slice(hh * DK_A, (hh + 1) * DK_A)
        o = acc_ref[:, hs]
        o = o * lax.rsqrt(jnp.mean(o * o, axis=1, keepdims=True) + RMS_EPS) * norm_ref[...]
        o_ref[:, hs] = (o * _silu(z[:, hs])).astype(o_ref.dtype)


def delta_mixer(proj, conv_w, conv_b, act, cum, cumt, norm, state, *, row0, n_seq, seq_len, period, hb):
    assert row0 % seq_len == 0 and H_A % hb == 0
    n_chunks = seq_len // CHUNK
    rb0 = row0 // seq_len
    has_state = state is not None
    w = hb * DK_A
    nq = H_A // hb
    col = lambda off: (lambda s, h: (rb0 + s, off + h))
    cw = lambda off: (lambda s, h: (0, off + h))
    in_specs = [pl.BlockSpec((seq_len, w), col(0)), pl.BlockSpec((seq_len, w), col(nq)),
                pl.BlockSpec((seq_len, w), col(2 * nq)), pl.BlockSpec((seq_len, w), col(3 * nq)),
                pl.BlockSpec((3, w), cw(0)), pl.BlockSpec((3, w), cw(nq)), pl.BlockSpec((3, w), cw(2 * nq)),
                pl.BlockSpec((1, w), cw(0)), pl.BlockSpec((1, w), cw(nq)), pl.BlockSpec((1, w), cw(2 * nq)),
                pl.BlockSpec((seq_len, LANES), lambda s, h: (rb0 + s, 0)),
                pl.BlockSpec((seq_len, LANES), lambda s, h: (rb0 + s, 0)),
                pl.BlockSpec((LANES, seq_len), lambda s, h: (0, rb0 + s)),
                pl.BlockSpec((1, DV_A), lambda s, h: (0, 0))]
    args = [proj, proj, proj, proj, conv_w, conv_w, conv_w, conv_b, conv_b, conv_b, act, cum, cumt, norm]
    o_spec = pl.BlockSpec((seq_len, w), lambda s, h: (s, h))
    o_shape = jax.ShapeDtypeStruct((n_seq * seq_len, H_A * DV_A), BF16)
    st_spec = pl.BlockSpec((None, 2, hb, DK_A, DV_A), lambda s, h: (s, 0, h, 0, 0))
    if has_state:
        in_specs.append(st_spec)
        args.append(state)
        out_specs, out_shape = o_spec, o_shape
    else:
        out_specs = [o_spec, st_spec]
        out_shape = [o_shape, jax.ShapeDtypeStruct((n_seq, 2, H_A, DK_A, DV_A), F32)]
    return pl.pallas_call(
        functools.partial(_delta_kernel, n_chunks=n_chunks, period=period, has_state=has_state, hb=hb),
        grid=(n_seq, H_A // hb),
        in_specs=in_specs, out_specs=out_specs, out_shape=out_shape,
        scratch_shapes=[pltpu.VMEM((seq_len, w), F32)],
        compiler_params=_cparams("arbitrary", "arbitrary"),
        name="delta_lat" if has_state else "delta_ctx",
    )(*args)


def even_gate_params(a_log, dt_bias, b_i, b_f):
    zeros_a = jnp.zeros((2 * H_A,), F32)
    coef = jnp.concatenate([zeros_a, -jnp.exp(a_log.astype(F32)).reshape(-1), jnp.zeros((4 * H_B,), F32)])
    bias = jnp.concatenate([zeros_a, dt_bias.reshape(-1), b_i.reshape(-1), b_f.reshape(-1)]).astype(F32)
    rev = jnp.concatenate([jnp.repeat(jnp.arange(2, dtype=F32), H_A)] * 2 + [jnp.repeat(jnp.arange(2, dtype=F32), H_B)] * 2)
    p = jnp.stack([coef, bias, rev])
    return jnp.pad(p, ((0, 5), (0, LANES - EV_GATES)))


def _mlstm_kernel(*refs, n_chunks, has_state):
    (q_ref, k_ref, v_ref, og_ref, act_ref, actt_ref, cum_ref, cumt_ref, norm_ref) = refs[:9]
    rest = refs[9:]
    if has_state:
        c0_ref, n0_ref, m0_ref, o_ref, acc_ref = rest
    else:
        o_ref, cout_ref, nm_ref, acc_ref = rest
    h = pl.program_id(1)
    i_col0, f_col0 = 4 * H_A, 4 * H_A + 2 * H_B

    for d in (0, 1):
        rev = d == 1
        m_incl, _ = _dir_masks(rev)
        if has_state:
            cm, nv, m = c0_ref[d], n0_ref[d], m0_ref[d]
        else:
            cm, nv, m = None, None, jnp.zeros((1, 1), F32)
        order = range(n_chunks - 1, -1, -1) if rev else range(n_chunks)
        for ci in order:
            sl = slice(ci * CHUNK, (ci + 1) * CHUNK)
            qc = q_ref[sl, :] * (DK_B ** -0.5)
            kc = k_ref[sl, :]
            vc = v_ref[sl, :]
            li_col = _pick_col(act_ref[sl, :], i_col0 + d * H_B + h)
            li_row = actt_ref[pl.ds(i_col0 + d * H_B + h, 1), sl]
            b_col = _pick_col(cum_ref[sl, :], f_col0 + d * H_B + h)
            b_row = cumt_ref[pl.ds(f_col0 + d * H_B + h, 1), sl]
            dlog = jnp.where(m_incl, b_col - b_row + li_row, NEG)
            inter = b_col + m
            m_q = jnp.maximum(inter, jnp.max(dlog, axis=1, keepdims=True))
            s = _bdot_nt(qc, kc) * jnp.exp(dlog - m_q)
            num = _bdot(s, vc)
            den = jnp.sum(s, axis=1, keepdims=True)
            if cm is not None:
                w_inter = jnp.exp(inter - m_q)
                num = num + w_inter * _bdot(qc, cm)
                den = den + w_inter * jnp.sum(qc * nv, axis=1, keepdims=True)
            hout = num / jnp.maximum(jnp.abs(den), jnp.exp(-m_q))
            b_last = b_col[0:1, :] if rev else b_col[CHUNK - 1:CHUNK, :]
            wlog = b_last - b_col + li_col
            m_new = jnp.maximum(b_last + m, jnp.max(wlog, axis=0, keepdims=True))
            kw = kc * jnp.exp(wlog - m_new)
            c_upd = _bdot_tn(kw, vc)
            n_upd = jnp.sum(kw, axis=0, keepdims=True)
            if cm is not None:
                sc = jnp.exp(b_last + m - m_new)
                cm, nv = sc * cm + c_upd, sc * nv + n_upd
            else:
                cm, nv = c_upd, n_upd
            m = m_new
            if rev:
                acc_ref[sl, :] = acc_ref[sl, :] + hout
            else:
                acc_ref[sl, :] = hout
        if not has_state:
            cout_ref[d] = cm
            nm_ref[d, 0:1, :] = nv
            nm_ref[d, 1:2, :] = jnp.broadcast_to(m, (1, DK_B))
            nm_ref[d, 2:8, :] = jnp.zeros((6, DK_B), F32)

    o = acc_ref[...]
    o = o * lax.rsqrt(jnp.mean(o * o, axis=1, keepdims=True) + RMS_EPS) * norm_ref[...]
    o_ref[...] = (o * jax.nn.sigmoid(og_ref[...])).astype(o_ref.dtype)


def mlstm_mixer(proj, act, actt, cum, cumt, norm, state, *, row0, n_seq, seq_len):
    assert row0 % seq_len == 0
    n_chunks = seq_len // CHUNK
    rb0 = row0 // seq_len
    has_state = state is not None
    q0 = (CONV_A + H_A * DV_A) // LANES
    k0 = q0 + H_B
    v0 = (CONV_A + H_A * DV_A + 2 * H_B * DK_B) // DV_B
    o0 = v0 + H_B
    col = lambda off: (lambda s, h: (rb0 + s, off + h))
    in_specs = [pl.BlockSpec((seq_len, DK_B), col(q0)), pl.BlockSpec((seq_len, DK_B), col(k0)),
                pl.BlockSpec((seq_len, DV_B), col(v0)), pl.BlockSpec((seq_len, DV_B), col(o0)),
                pl.BlockSpec((seq_len, LANES), lambda s, h: (rb0 + s, 0)),
                pl.BlockSpec((LANES, seq_len), lambda s, h: (0, rb0 + s)),
                pl.BlockSpec((seq_len, LANES), lambda s, h: (rb0 + s, 0)),
                pl.BlockSpec((LANES, seq_len), lambda s, h: (0, rb0 + s)),
                pl.BlockSpec((1, DV_B), lambda s, h: (0, 0))]
    args = [proj, proj, proj, proj, act, actt, cum, cumt, norm]
    o_spec = pl.BlockSpec((seq_len, DV_B), lambda s, h: (s, h))
    o_shape = jax.ShapeDtypeStruct((n_seq * seq_len, H_B * DV_B), BF16)
    st_idx = lambda s, h: (s, 0, h, 0, 0)
    if has_state:
        c0, n0, m0 = state
        in_specs += [pl.BlockSpec((None, 2, None, DK_B, DV_B), st_idx),
                     pl.BlockSpec((None, 2, None, 1, DK_B), st_idx),
                     pl.BlockSpec((None, 2, None, 1, 1), st_idx)]
        args += [c0, n0.reshape(n_seq, 2, H_B, 1, DK_B), m0.reshape(n_seq, 2, H_B, 1, 1)]
        out_specs, out_shape = o_spec, o_shape
    else:
        out_specs = [o_spec, pl.BlockSpec((None, 2, None, DK_B, DV_B), st_idx),
                     pl.BlockSpec((None, 2, None, 8, DK_B), st_idx)]
        out_shape = [o_shape, jax.ShapeDtypeStruct((n_seq, 2, H_B, DK_B, DV_B), F32),
                     jax.ShapeDtypeStruct((n_seq, 2, H_B, 8, DK_B), F32)]
    return pl.pallas_call(
        functools.partial(_mlstm_kernel, n_chunks=n_chunks, has_state=has_state),
        grid=(n_seq, H_B),
        in_specs=in_specs, out_specs=out_specs, out_shape=out_shape,
        scratch_shapes=[pltpu.VMEM((seq_len, DV_B), F32)],
        compiler_params=_cparams("arbitrary", "arbitrary"),
        name="mlstm_lat" if has_state else "mlstm_ctx",
    )(*args)


def odd_gate_params(a_log, dt_bias):
    coef = -jnp.exp(a_log.astype(F32)).reshape(-1)
    bias = dt_bias.astype(F32).reshape(-1)
    rev = jnp.repeat(jnp.arange(2, dtype=F32), H_C)
    return jnp.pad(jnp.stack([coef, bias, rev]), ((0, 5), (0, 0)))


def _ssd_kernel(*refs, n_chunks, period, has_state):
    (z_ref, x_ref, b_ref, c_ref, cwx, cwb, cwc, cbx, cbb, cbc, dt_ref, cum_ref, cumt_ref, dskip_ref, norm_ref) = refs[:15]
    rest = refs[15:]
    if has_state:
        s0_ref, o_ref, acc_ref = rest
        sout_ref = None
    else:
        o_ref, sout_ref, acc_ref = rest
    g = pl.program_id(1)

    x = _conv_silu(x_ref[...], cwx, cbx, period)
    bm = _conv_silu(b_ref[...], cwb, cbb, period)
    cm = _conv_silu(c_ref[...], cwc, cbc, period)

    er = lax.broadcasted_iota(jnp.int32, (LANES, GW_C), 0)
    ec = lax.broadcasted_iota(jnp.int32, (LANES, GW_C), 1)
    tr = lax.broadcasted_iota(jnp.int32, (GW_C, LANES), 0)
    tc = lax.broadcasted_iota(jnp.int32, (GW_C, LANES), 1)
    lane_in_tile = lax.broadcasted_iota(jnp.int32, (CHUNK, LANES), 1)

    for d in (0, 1):
        rev = d == 1
        m_incl, _ = _dir_masks(rev)
        col0 = d * H_C + g * HG_C
        expand = jnp.where(er == col0 + ec // P_C, 1.0, 0.0)
        expand_t = tc == col0 + tr // P_C
        state = s0_ref[d].reshape(GW_C, N_C) if has_state else None
        order = range(n_chunks - 1, -1, -1) if rev else range(n_chunks)
        for ci in order:
            sl = slice(ci * CHUNK, (ci + 1) * CHUNK)
            xc, bc, cc = x[sl], bm[sl], cm[sl]
            cum_blk = cum_ref[sl, :]
            cum_last = cum_blk[0:1, :] if rev else cum_blk[CHUNK - 1:CHUNK, :]
            xdt = xc * _dot_exact_rhs(dt_ref[sl, :], expand, passes=2)
            scores = _bdot_nt(cc, bc)
            for hp in range(HG_C // 2):
                ps = slice(hp * LANES, (hp + 1) * LANES)
                xpair = xdt[:, ps]
                ypair = None
                for sub in (0, 1):
                    hh = 2 * hp + sub
                    cb = _pick_col(cum_blk, col0 + hh)
                    crow = cumt_ref[pl.ds(col0 + hh, 1), sl]
                    seg = jnp.exp(jnp.where(m_incl, cb - crow, NEG))
                    mine = (lane_in_tile < P_C) if sub == 0 else (lane_in_tile >= P_C)
                    y = _bdot(scores * seg, jnp.where(mine, xpair, 0.0))
                    ypair = y if ypair is None else ypair + y
                if rev:
                    acc_ref[sl, ps] = acc_ref[sl, ps] + ypair
                else:
                    acc_ref[sl, ps] = ypair
            if state is not None:
                y_in = _bdot_nt(cc, state) * _dot_exact_rhs(jnp.exp(cum_blk), expand, passes=2)
                acc_ref[sl, :] = acc_ref[sl, :] + y_in
            dend = _dot_exact_rhs(jnp.exp(jnp.minimum(cum_last - cum_blk, 0.0)), expand, passes=2)
            upd = _bdot_tn(xdt * dend, bc)
            if state is not None:
                tot = jnp.sum(jnp.where(expand_t, jnp.broadcast_to(cum_last, (GW_C, LANES)), 0.0), axis=1, keepdims=True)
                state = state * jnp.exp(tot) + upd
            else:
                state = upd
        if sout_ref is not None:
            sout_ref[d] = state.reshape(HG_C, P_C, N_C)

    y = acc_ref[...] + dskip_ref[...] * x
    y = y * _silu(z_ref[...])
    y = y * lax.rsqrt(jnp.mean(y * y, axis=1, keepdims=True) + RMS_EPS) * norm_ref[...]
    o_ref[...] = y.astype(o_ref.dtype)


def ssd_mixer(proj, conv_w, conv_b, dt, cum, cumt, dskip, norm, state, *, row0, n_seq, seq_len, period):
    assert row0 % seq_len == 0
    n_chunks = seq_len // CHUNK
    rb0 = row0 // seq_len
    has_state = state is not None
    xb0 = D_INNER // GW_C
    bb0 = 2 * D_INNER // N_C
    cb0 = bb0 + G_C
    wb0 = D_INNER // N_C
    wc0 = wb0 + G_C
    col = lambda off: (lambda s, g: (rb0 + s, off + g))
    cw = lambda off: (lambda s, g: (0, off + g))
    in_specs = [pl.BlockSpec((seq_len, GW_C), col(0)), pl.BlockSpec((seq_len, GW_C), col(xb0)),
                pl.BlockSpec((seq_len, N_C), col(bb0)), pl.BlockSpec((seq_len, N_C), col(cb0)),
                pl.BlockSpec((3, GW_C), cw(0)), pl.BlockSpec((3, N_C), cw(wb0)), pl.BlockSpec((3, N_C), cw(wc0)),
                pl.BlockSpec((1, GW_C), cw(0)), pl.BlockSpec((1, N_C), cw(wb0)), pl.BlockSpec((1, N_C), cw(wc0)),
                pl.BlockSpec((seq_len, LANES), lambda s, g: (rb0 + s, 0)),
                pl.BlockSpec((seq_len, LANES), lambda s, g: (rb0 + s, 0)),
                pl.BlockSpec((LANES, seq_len), lambda s, g: (0, rb0 + s)),
                pl.BlockSpec((1, GW_C), cw(0)), pl.BlockSpec((1, GW_C), cw(0))]
    args = [proj, proj, proj, proj, conv_w, conv_w, conv_w, conv_b, conv_b, conv_b, dt, cum, cumt, dskip, norm]
    o_spec = pl.BlockSpec((seq_len, GW_C), lambda s, g: (s, g))
    o_shape = jax.ShapeDtypeStruct((n_seq * seq_len, D_INNER), BF16)
    st_spec = pl.BlockSpec((None, 2, HG_C, P_C, N_C), lambda s, g: (s, 0, g, 0, 0))
    if has_state:
        in_specs.append(st_spec)
        args.append(state)
        out_specs, out_shape = o_spec, o_shape
    else:
        out_specs = [o_spec, st_spec]
        out_shape = [o_shape, jax.ShapeDtypeStruct((n_seq, 2, H_C, P_C, N_C), F32)]
    return pl.pallas_call(
        functools.partial(_ssd_kernel, n_chunks=n_chunks, period=period, has_state=has_state),
        grid=(n_seq, G_C),
        in_specs=in_specs, out_specs=out_specs, out_shape=out_shape,
        scratch_shapes=[pltpu.VMEM((seq_len, GW_C), F32)],
        compiler_params=_cparams("arbitrary", "arbitrary"),
        name="ssd_lat" if has_state else "ssd_ctx",
    )(*args)


def _dot3(a, b):
    a_hi = a.astype(BF16)
    a_lo = (a - a_hi.astype(F32)).astype(BF16)
    b_hi = b.astype(BF16)
    b_lo = (b - b_hi.astype(F32)).astype(BF16)
    d = lambda p, q: jnp.dot(p, q, preferred_element_type=F32)
    return d(a_hi, b_hi) + (d(a_hi, b_lo) + d(a_lo, b_hi))


def _dot3_nt(a, b):
    a_hi = a.astype(BF16)
    a_lo = (a - a_hi.astype(F32)).astype(BF16)
    b_hi = b.astype(BF16)
    b_lo = (b - b_hi.astype(F32)).astype(BF16)
    d = lambda p, q: lax.dot_general(p, q, (((1,), (1,)), ((), ())), preferred_element_type=F32)
    return d(a_hi, b_hi) + (d(a_hi, b_lo) + d(a_lo, b_hi))


def _resid_ln_kernel(*refs, n_y, n_gathered, gate, sh, sc, want_h, want_logits):
    x_ref = refs[0]
    y_refs = refs[1:1 + n_y]
    m_ref, mn_ref, g_ref, b_ref = refs[1 + n_y:5 + n_y]
    rest = list(refs[5 + n_y:])
    if n_gathered:
        gath_ref, gw_ref = rest.pop(0), rest.pop(0)
    rw_ref = rest.pop(0) if want_logits else None
    xo_ref = rest.pop(0)
    y = y_refs[0][...].astype(F32)
    for r in y_refs[1:]:
        y = y + r[...].astype(F32)
    for kk in range(n_gathered):
        y = y + gw_ref[:, kk:kk + 1] * gath_ref[kk].astype(F32)
    v = ALPHA * x_ref[...] + m_ref[gate:gate + 1, :] * y
    mu = jnp.mean(v, axis=1, keepdims=True)
    vc = v - mu
    var = jnp.mean(vc * vc, axis=1, keepdims=True)
    xn = vc * lax.rsqrt(var + LN_EPS) * g_ref[...] + b_ref[...]
    xo_ref[...] = xn
    if want_h:
        hm = xn * (1.0 + mn_ref[sc:sc + 1, :]) + mn_ref[sh:sh + 1, :]
        rest.pop(0)[...] = hm.astype(BF16)
        if want_logits:
            rest.pop(0)[...] = _dot3_nt(rw_ref[...], hm)


def resid_ln(x, ys, mod, mod_next, ln_g, ln_b, router_w, *, gate, sh, sc, want_h, t_ctx, lat_len, gathered=None,
             h_rows=None):
    t, d = x.shape
    tm = 256
    want_logits = router_w is not None
    grp = lambda i: (_group_of_block(i, tm, t_ctx, lat_len), 0, 0)
    row = pl.BlockSpec((tm, d), lambda i: (i, 0))
    vec = pl.BlockSpec((1, d), lambda i: (0, 0))
    in_specs = [row] * (1 + len(ys)) + [pl.BlockSpec((None, 6, d), grp), pl.BlockSpec((None, 6, d), grp), vec, vec]
    args = [x, *ys, mod, mod_next, ln_g.reshape(1, d), ln_b.reshape(1, d)]
    n_gathered = 0
    if gathered is not None:
        n_gathered = gathered[0].shape[0]
        in_specs += [pl.BlockSpec((n_gathered, tm, d), lambda i: (0, i, 0)),
                     pl.BlockSpec((tm, n_gathered), lambda i: (i, 0))]
        args += list(gathered)
    out_specs, out_shape = [row], [jax.ShapeDtypeStruct((t, d), F32)]
    if want_logits:
        n_e = router_w.shape[1]
        in_specs.append(pl.BlockSpec((n_e, d), lambda i: (0, 0)))
        args.append(router_w.T)
    if want_h:
        out_specs.append(row)
        out_shape.append(jax.ShapeDtypeStruct((h_rows or t, d), BF16))
    if want_logits:
        out_specs.append(pl.BlockSpec((n_e, tm), lambda i: (0, i)))
        out_shape.append(jax.ShapeDtypeStruct((n_e, t), F32))
    return pl.pallas_call(
        functools.partial(_resid_ln_kernel, n_y=len(ys), n_gathered=n_gathered, gate=gate, sh=sh, sc=sc, want_h=want_h,
                          want_logits=want_logits),
        grid=(t // tm,),
        in_specs=in_specs, out_specs=out_specs, out_shape=out_shape,
        compiler_params=_cparams("arbitrary"),
        name="resid_ln",
    )(*args)


def _ffn_kernel(be_ref, nu_ref, x_ref, wg_ref, wu_ref, wd_ref, o_ref, g_bf, u_bf, d_bf):
    b = pl.program_id(0)

    @pl.when(b < nu_ref[0])
    def _():
        prev = be_ref[jnp.maximum(b - 1, 0)]

        @pl.when((b == 0) | (be_ref[b] != prev))
        def _():
            g_bf[...] = wg_ref[...].astype(BF16)
            u_bf[...] = wu_ref[...].astype(BF16)
            d_bf[...] = wd_ref[...].astype(BF16)

        x = x_ref[...]
        hg = jnp.dot(x, g_bf[...], preferred_element_type=F32)
        hu = jnp.dot(x, u_bf[...], preferred_element_type=F32)
        a = (_silu(hg) * hu).astype(BF16)
        o_ref[...] = jnp.dot(a, d_bf[...], preferred_element_type=F32).astype(o_ref.dtype)


def expert_ffn(xs, blk_e, n_used, w_gate, w_up, w_down, layer, *, tm, out_dtype=BF16, name="expert_ffn"):
    d = xs.shape[1]
    de = w_gate.shape[3]
    n_blk = blk_e.shape[0]
    r = n_blk * tm
    wsel = lambda b, be, nu: (layer, be[b], 0, 0)
    grid_spec = pltpu.PrefetchScalarGridSpec(
        num_scalar_prefetch=2,
        grid=(n_blk,),
        in_specs=[pl.BlockSpec((tm, d), lambda b, be, nu: (b, 0)),
                  pl.BlockSpec((None, None, d, de), wsel), pl.BlockSpec((None, None, d, de), wsel),
                  pl.BlockSpec((None, None, de, d), wsel)],
        out_specs=pl.BlockSpec((tm, d), lambda b, be, nu: (b, 0)),
        scratch_shapes=[pltpu.VMEM((d, de), BF16), pltpu.VMEM((d, de), BF16), pltpu.VMEM((de, d), BF16)],
    )
    return pl.pallas_call(
        _ffn_kernel,
        grid_spec=grid_spec,
        out_shape=jax.ShapeDtypeStruct((r, d), out_dtype),
        compiler_params=_cparams("arbitrary"),
        name=name,
    )(blk_e, n_used, xs, w_gate, w_up, w_down)


ROUTE_TM = 512
GROUP_SIZE = N_EXP // N_GROUPS


def _first_argmax(v, idx, axis, sentinel):
    mx = jnp.max(v, axis=axis, keepdims=True)
    return mx, jnp.min(jnp.where(v == mx, idx, sentinel), axis=axis, keepdims=True)


def _route_kernel(lt_ref, bias_ref, idx_ref, w_ref, rank_ref, cnt_ref, carry_ref):
    i = pl.program_id(0)
    tm = lt_ref.shape[1]

    @pl.when(i == 0)
    def _():
        carry_ref[...] = jnp.zeros_like(carry_ref)

    scores = jax.nn.sigmoid(lt_ref[...])
    biased = scores + bias_ref[...]
    b3 = biased.reshape(N_GROUPS, GROUP_SIZE, tm)
    mem = lax.broadcasted_iota(jnp.int32, b3.shape, 1).astype(F32)
    m1, first = _first_argmax(b3, mem, 1, float(GROUP_SIZE))
    m2 = jnp.max(jnp.where(mem == first, -jnp.inf, b3), axis=1, keepdims=True)
    gs = (m1 + m2).reshape(N_GROUPS, tm)
    gi = lax.broadcasted_iota(jnp.int32, gs.shape, 0).astype(F32)
    gsel = jnp.zeros(gs.shape, F32)
    cur = gs
    for _ in range(TOPK_GROUPS):
        _, pick = _first_argmax(cur, gi, 0, float(N_GROUPS))
        hit = gi == pick
        gsel = jnp.where(hit, 1.0, gsel)
        cur = jnp.where(hit, -jnp.inf, cur)
    masked = jnp.where(gsel.reshape(N_GROUPS, 1, tm) > 0.5, b3, -jnp.inf).reshape(N_EXP, tm)

    ei = lax.broadcasted_iota(jnp.int32, masked.shape, 0).astype(F32)
    picks, sel_scores = [], []
    chosen = jnp.zeros(masked.shape, F32)
    cur = masked
    for _ in range(TOP_K):
        _, pick = _first_argmax(cur, ei, 0, float(N_EXP))
        hit = ei == pick
        picks.append(pick)
        sel_scores.append(jnp.sum(jnp.where(hit, scores, 0.0), axis=0, keepdims=True))
        chosen = jnp.where(hit, 1.0, chosen)
        cur = jnp.where(hit, -jnp.inf, cur)

    r, c = _tri_masks(tm)
    before = jnp.where(r < c, 1.0, 0.0).astype(BF16)
    rank = jnp.dot(chosen.astype(BF16), before, preferred_element_type=F32) + carry_ref[...]
    carry_ref[...] = carry_ref[...] + jnp.sum(chosen, axis=1, keepdims=True)
    cnt_ref[...] = carry_ref[...]

    total = sel_scores[0]
    for s in sel_scores[1:]:
        total = total + s
    for k in range(TOP_K):
        idx_ref[k:k + 1, :] = picks[k].astype(jnp.int32)
        w_ref[k:k + 1, :] = sel_scores[k] / total * ROUTED_SCALE
        rank_ref[k:k + 1, :] = jnp.sum(jnp.where(ei == picks[k], rank, 0.0), axis=0, keepdims=True).astype(jnp.int32)


def route(logits_t, router_bias):
    n_e, t = logits_t.shape
    tm = ROUTE_TM
    kt = pl.BlockSpec((TOP_K, tm), lambda i: (0, i))
    return pl.pallas_call(
        _route_kernel,
        grid=(t // tm,),
        in_specs=[pl.BlockSpec((n_e, tm), lambda i: (0, i)), pl.BlockSpec((n_e, 1), lambda i: (0, 0))],
        out_specs=[kt, kt, kt, pl.BlockSpec((n_e, 1), lambda i: (0, 0))],
        out_shape=[jax.ShapeDtypeStruct((TOP_K, t), jnp.int32), jax.ShapeDtypeStruct((TOP_K, t), F32),
                   jax.ShapeDtypeStruct((TOP_K, t), jnp.int32), jax.ShapeDtypeStruct((n_e, 1), F32)],
        scratch_shapes=[pltpu.VMEM((n_e, 1), F32)],
        compiler_params=_cparams("arbitrary"),
        name="route",
    )(logits_t, router_bias.reshape(n_e, 1))


def moe(h, logits_t, router_bias, e_gate, e_up, e_down, s_gate, s_up, s_down, layer):
    d = h.shape[1]
    t = logits_t.shape[1]
    tm = 256
    top_e, wts, rank, counts = route(logits_t, router_bias)
    counts = counts.reshape(-1).astype(jnp.int32)
    n_assign = t * TOP_K
    padded = (counts + tm - 1) // tm * tm
    pad_end = jnp.cumsum(padded)
    pad_start = pad_end - padded
    n_blk = n_assign // tm + N_EXP
    blk_first = jnp.arange(n_blk, dtype=jnp.int32) * tm
    blk_e = jnp.minimum(jnp.sum((pad_end[None, :] <= blk_first[:, None]).astype(jnp.int32), axis=1), N_EXP - 1)
    n_used = (pad_end[-1] // tm).astype(jnp.int32).reshape(1)
    expert_ids = jnp.arange(N_EXP, dtype=jnp.int32)
    dest = jnp.sum(jnp.where(top_e[..., None] == expert_ids, pad_start, 0), axis=-1) + rank
    tok = jnp.broadcast_to(jnp.arange(t, dtype=jnp.int32), (TOP_K, t))
    filler = jnp.arange(n_blk * tm, dtype=jnp.int32) % t
    slot_tok = filler.at[dest.reshape(-1)].set(tok.reshape(-1), unique_indices=True)
    xs = h.at[slot_tok].get(mode="promise_in_bounds")
    ys = expert_ffn(xs, blk_e, n_used, e_gate, e_up, e_down, layer, tm=tm, name="routed_ffn")
    routed_rows = ys.at[dest.reshape(-1)].get(mode="promise_in_bounds").reshape(TOP_K, t, d)
    n_sh = t // tm
    shared = expert_ffn(h, jnp.zeros((n_sh,), jnp.int32), jnp.full((1,), n_sh, jnp.int32),
                        s_gate[:, None], s_up[:, None], s_down[:, None], layer, tm=tm, name="shared_ffn")
    return shared, routed_rows, wts.T


def kernel(x_prompt, x_sample, state_dn, state_ml_C, state_ml_n, state_ml_m, state_ssd, c, c_ctx,
           mod_w, mod_b, ln1_g, ln1_b, ln2_g, ln2_b, router_w, router_bias, exp_gate, exp_up, exp_down,
           sh_gate, sh_up, sh_down, ev_w_in, ev_conv_w, ev_conv_b, dn_A_log, dn_dt_bias, ml_b_i, ml_b_f,
           dn_norm, ml_norm, ev_w_out, od_w_in, od_conv_w, od_conv_b, ssd_A_log, ssd_dt_bias, ssd_D,
           ssd_norm, od_w_out):
    bp, sl, d = x_prompt.shape
    bl, ll, _ = x_sample.shape
    depth = mod_w.shape[0]
    t_ctx = bp * sl
    x = jnp.concatenate([x_prompt.reshape(t_ctx, d), x_sample.reshape(bl * ll, d)], axis=0)
    cvec = jnp.concatenate([c_ctx[None], c, jnp.zeros((8 - 1 - bl, d), F32)], axis=0)
    mods = compute_mods(cvec, mod_w, mod_b)[:, :1 + bl].reshape(depth, 1 + bl, 6, d)
    geo = dict(t_ctx=t_ctx, lat_len=ll)
    ctx = dict(row0=0, n_seq=bp, seq_len=sl)
    lat = dict(row0=t_ctx, n_seq=bl, seq_len=ll)
    rows = lambda a, b: jnp.concatenate([a, b], axis=0)

    h = modulate(x, mods[0], 0, 1, **geo)
    new_dn, new_c, new_n, new_m, new_ssd = [], [], [], [], []
    for l in range(depth):
        j = l // 2
        if l % 2 == 0:
            w_in = ev_w_in[j]
            proj = matmul(h, w_in, tm=1024, tn=512, n_out=EV_MAIN, name="ev_in_proj")
            graw = matmul(h, w_in, tm=1024, tn=LANES, n_out=LANES, col_block_off=EV_MAIN // LANES,
                          valid_cols=EV_GATES, name="ev_gate_proj")
            act, cum = gate_prep(graw, even_gate_params(dn_A_log[j], dn_dt_bias[j], ml_b_i[j], ml_b_f[j]), "even")
            actt, cumt = act.T, cum.T
            cw, cb, dnn, mln = ev_conv_w[j], ev_conv_b[j].reshape(1, -1), dn_norm[j].reshape(1, -1), ml_norm[j].reshape(1, -1)
            oa_c, s_dn = delta_mixer(proj, cw, cb, act, cum, cumt, dnn, None, period=sl, hb=4, **ctx)
            oa_l = delta_mixer(proj, cw, cb, act, cum, cumt, dnn, state_dn[:, j], period=GRID_W, hb=1, **lat)
            ob_c, s_c, s_nm = mlstm_mixer(proj, act, actt, cum, cumt, mln, None, **ctx)
            ob_l = mlstm_mixer(proj, act, actt, cum, cumt, mln,
                               (state_ml_C[:, j], state_ml_n[:, j], state_ml_m[:, j]), **lat)
            mixed = jnp.concatenate([rows(oa_c, oa_l), rows(ob_c, ob_l)], axis=1)
            y = matmul(mixed, ev_w_out[j], tm=1024, tn=512, n_out=d, name="ev_out_proj")
            new_dn.append(s_dn)
            new_c.append(s_c)
            new_n.append(s_nm[:, :, :, 0, :])
            new_m.append(s_nm[:, :, :, 1, 0])
        else:
            w_in = od_w_in[j]
            proj = matmul(h, w_in, tm=1024, tn=512, n_out=OD_MAIN, name="od_in_proj")
            draw = matmul(h, w_in, tm=1024, tn=LANES, n_out=LANES, col_block_off=OD_MAIN // LANES, name="od_dt_proj")
            dt, cum = gate_prep(draw, odd_gate_params(ssd_A_log[j], ssd_dt_bias[j]), "odd")
            cumt = cum.T
            cw, cb = od_conv_w[j], od_conv_b[j].reshape(1, -1)
            dsk, nrm = jnp.repeat(ssd_D[j], P_C).reshape(1, -1), ssd_norm[j].reshape(1, -1)
            o_c, s_ssd = ssd_mixer(proj, cw, cb, dt, cum, cumt, dsk, nrm, None, period=sl, **ctx)
            o_l = ssd_mixer(proj, cw, cb, dt, cum, cumt, dsk, nrm, state_ssd[:, j], period=GRID_W, **lat)
            y = matmul(rows(o_c, o_l), od_w_out[j], tm=512, tn=512, n_out=d, name="od_out_proj")
            new_ssd.append(s_ssd)
        x, h2, logits_t = resid_ln(x, [y], mods[l], mods[l], ln1_g[l], ln1_b[l], router_w[l],
                                   gate=2, sh=3, sc=4, want_h=True, h_rows=GATHER_SRC_ROWS, **geo)
        shared, routed_rows, wts = moe(h2, logits_t, router_bias[l], exp_gate, exp_up, exp_down,
                                       sh_gate, sh_up, sh_down, l)
        last = l == depth - 1
        res = resid_ln(x, [shared], mods[l], mods[min(l + 1, depth - 1)], ln2_g[l], ln2_b[l], None,
                       gate=5, sh=0, sc=1, want_h=not last, gathered=(routed_rows, wts), **geo)
        x = res[0]
        if not last:
            h = res[1]
    y_prompt = x[:t_ctx].reshape(bp, sl, d)
    y_sample = x[t_ctx:].reshape(bl, ll, d)
    return (y_prompt, y_sample, jnp.stack(new_dn, axis=1), jnp.stack(new_c, axis=1), jnp.stack(new_n, axis=1),
            jnp.stack(new_m, axis=1), jnp.stack(new_ssd, axis=1))
```

```python
import functools

import jax
import jax.numpy as jnp
from jax import lax
from jax.experimental import pallas as pl
from jax.experimental.pallas import tpu as pltpu

F32 = jnp.float32
BF16 = jnp.bfloat16

D_MODEL = 2048
DEPTH = 2
GRID_W = 64
ALPHA = (2 * DEPTH) ** 0.25
LN_EPS = 1e-5
RMS_EPS = 1e-6

H_A, DK_A, DV_A = 8, 128, 128
H_B, DK_B, DV_B = 4, 128, 256
CONV_A = 2 * H_A * DK_A + H_A * DV_A
EV_MAIN = CONV_A + H_A * DV_A + 2 * H_B * DK_B + 2 * H_B * DV_B
EV_GATES = 4 * H_A + 4 * H_B

D_INNER = 2 * D_MODEL
P_C, N_C, G_C = 64, 128, 8
H_C = D_INNER // P_C
HG_C = H_C // G_C
GW_C = D_INNER // G_C
OD_MAIN = 2 * D_INNER + 2 * G_C * N_C

N_EXP, TOP_K, N_GROUPS, TOPK_GROUPS = 64, 8, 8, 4
D_EXP = 512
ROUTED_SCALE = 2.5

GATHER_SRC_ROWS = 16384
CHUNK = 256
LANES = 128
VMEM_LIMIT = 56 * 1024 * 1024
NEG = -1e30


def _cparams(*sem):
    return pltpu.CompilerParams(dimension_semantics=sem, vmem_limit_bytes=VMEM_LIMIT)


def _bdot(a, b):
    return jnp.dot(a.astype(BF16), b.astype(BF16), preferred_element_type=F32)


def _bdot_nt(a, b):
    return lax.dot_general(a.astype(BF16), b.astype(BF16), (((1,), (1,)), ((), ())), preferred_element_type=F32)


def _bdot_tn(a, b):
    return lax.dot_general(a.astype(BF16), b.astype(BF16), (((0,), (0,)), ((), ())), preferred_element_type=F32)


def _split3(a):
    hi = a.astype(BF16)
    r = a - hi.astype(F32)
    mid = r.astype(BF16)
    lo = (r - mid.astype(F32)).astype(BF16)
    return hi, mid, lo


def _dot_exact_rhs(a, b_exact, passes=3):
    hi, mid, lo = _split3(a)
    bb = b_exact.astype(BF16)
    d = lambda p: jnp.dot(p, bb, preferred_element_type=F32)
    return d(hi) + d(mid) + d(lo) if passes == 3 else d(hi) + d(mid)


def _dot_exact_lhs(a_exact, b):
    hi, mid, lo = _split3(b)
    aa = a_exact.astype(BF16)
    d = lambda p: jnp.dot(aa, p, preferred_element_type=F32)
    return d(hi) + d(mid) + d(lo)


def _silu(x):
    return x * jax.nn.sigmoid(x)


def _softplus(x):
    return jnp.maximum(x, 0.0) + jnp.log(1.0 + jnp.exp(-jnp.abs(x)))


def _group_of_block(i, tm, t_ctx, lat_len):
    return jnp.maximum(i * tm - t_ctx, -1) // lat_len + 1


def _mod_kernel(c_ref, w_ref, b_ref, o_ref):
    c = c_ref[...]
    o_ref[...] = _bdot(_silu(c), w_ref[...]) + b_ref[...]


def compute_mods(cvec, mod_w, mod_b):
    depth, d, n = mod_w.shape
    tn = 512
    return pl.pallas_call(
        _mod_kernel,
        grid=(depth, n // tn),
        in_specs=[pl.BlockSpec((8, d), lambda l, j: (0, 0)),
                  pl.BlockSpec((None, d, tn), lambda l, j: (l, 0, j)),
                  pl.BlockSpec((None, 1, tn), lambda l, j: (l, 0, j))],
        out_specs=pl.BlockSpec((None, 8, tn), lambda l, j: (l, 0, j)),
        out_shape=jax.ShapeDtypeStruct((depth, 8, n), F32),
        compiler_params=_cparams("arbitrary", "arbitrary"),
        name="mod_vectors",
    )(cvec, mod_w, mod_b.reshape(depth, 1, n))


def _modulate_kernel(x_ref, m_ref, o_ref, *, sh, sc):
    o_ref[...] = (x_ref[...] * (1.0 + m_ref[sc:sc + 1, :]) + m_ref[sh:sh + 1, :]).astype(o_ref.dtype)


def modulate(x, mod, sh, sc, t_ctx, lat_len):
    t, d = x.shape
    tm = 512
    return pl.pallas_call(
        functools.partial(_modulate_kernel, sh=sh, sc=sc),
        grid=(t // tm,),
        in_specs=[pl.BlockSpec((tm, d), lambda i: (i, 0)),
                  pl.BlockSpec((None, 6, d), lambda i: (_group_of_block(i, tm, t_ctx, lat_len), 0, 0))],
        out_specs=pl.BlockSpec((tm, d), lambda i: (i, 0)),
        out_shape=jax.ShapeDtypeStruct((t, d), BF16),
        compiler_params=_cparams("arbitrary"),
        name="modulate",
    )(x, mod)


def _matmul_kernel(x_ref, w_ref, o_ref, wbf_ref, *, valid_cols):
    @pl.when(pl.program_id(1) == 0)
    def _():
        wbf_ref[...] = w_ref[...].astype(BF16)

    y = jnp.dot(x_ref[...], wbf_ref[...], preferred_element_type=F32)
    if valid_cols is not None:
        col = lax.broadcasted_iota(jnp.int32, y.shape, 1)
        y = jnp.where(col < valid_cols, y, 0.0)
    o_ref[...] = y.astype(o_ref.dtype)


def matmul(x, w, *, tm, tn, n_out, col_block_off=0, valid_cols=None, out_dtype=F32, name="matmul"):
    m, k = x.shape
    return pl.pallas_call(
        functools.partial(_matmul_kernel, valid_cols=valid_cols),
        grid=(n_out // tn, m // tm),
        in_specs=[pl.BlockSpec((tm, k), lambda j, i: (i, 0)),
                  pl.BlockSpec((k, tn), lambda j, i: (0, j + col_block_off))],
        out_specs=pl.BlockSpec((tm, tn), lambda j, i: (i, j)),
        out_shape=jax.ShapeDtypeStruct((m, n_out), out_dtype),
        scratch_shapes=[pltpu.VMEM((k, tn), BF16)],
        compiler_params=_cparams("arbitrary", "arbitrary"),
        name=name,
    )(x, w)


def _tri_masks(n):
    r = lax.broadcasted_iota(jnp.int32, (n, n), 0)
    c = lax.broadcasted_iota(jnp.int32, (n, n), 1)
    return r, c


def _gate_kernel(raw_ref, p_ref, act_ref, cum_ref, *, mode):
    x = raw_ref[...]
    coef, bias, rev = p_ref[0:1, :], p_ref[1:2, :], p_ref[2:3, :]
    col = lax.broadcasted_iota(jnp.int32, x.shape, 1)
    xb = x + bias
    if mode == "even":
        act = jnp.where(col < 2 * H_A, jax.nn.sigmoid(xb),
                        jnp.where(col < 4 * H_A, coef * _softplus(xb),
                                  jnp.where(col < 4 * H_A + 2 * H_B, xb,
                                            jnp.minimum(xb, 0.0) - jnp.log(1.0 + jnp.exp(-jnp.abs(xb))))))
        to_sum = act
    else:
        act = _softplus(xb)
        to_sum = act * coef
    r, c = _tri_masks(CHUNK)
    lower = jnp.where(c <= r, 1.0, 0.0)
    upper = jnp.where(c >= r, 1.0, 0.0)
    cum_f = _dot_exact_lhs(lower, to_sum)
    cum_r = _dot_exact_lhs(upper, to_sum)
    act_ref[...] = act
    cum_ref[...] = jnp.where(rev > 0.5, cum_r, cum_f)


def gate_prep(raw, params, mode):
    t = raw.shape[0]
    return pl.pallas_call(
        functools.partial(_gate_kernel, mode=mode),
        grid=(t // CHUNK,),
        in_specs=[pl.BlockSpec((CHUNK, LANES), lambda i: (i, 0)),
                  pl.BlockSpec((8, LANES), lambda i: (0, 0))],
        out_specs=[pl.BlockSpec((CHUNK, LANES), lambda i: (i, 0))] * 2,
        out_shape=[jax.ShapeDtypeStruct((t, LANES), F32)] * 2,
        compiler_params=_cparams("arbitrary"),
        name="gate_prep_" + mode,
    )(raw, params)


def _conv_silu(x, cw_ref, cb_ref, period):
    n = x.shape[0]
    row = lax.broadcasted_iota(jnp.int32, x.shape, 0) % period
    prev = jnp.where(row == 0, 0.0, pltpu.roll(x, 1, 0))
    nxt = jnp.where(row == period - 1, 0.0, pltpu.roll(x, n - 1, 0))
    y = cb_ref[...] + prev * cw_ref[0:1, :] + x * cw_ref[1:2, :] + nxt * cw_ref[2:3, :]
    return _silu(y)


def _pick_col(blk, idx):
    lane = lax.broadcasted_iota(jnp.int32, blk.shape, 1)
    return jnp.sum(jnp.where(lane == idx, blk, 0.0), axis=1, keepdims=True)


def _dir_masks(rev):
    r, c = _tri_masks(CHUNK)
    if rev:
        return c >= r, c > r
    return c <= r, c < r


def _tri_inverse(lmat, rev):
    return _tri_inverse_many([lmat], [rev])[0]


def _tri_inverse_many(lmats, revs):
    r, c = _tri_masks(CHUNK)
    eye = jnp.where(r == c, 1.0, 0.0)

    def off_mask(s, rev):
        same = (r // (2 * s)) == (c // (2 * s))
        r_hi = (r // s) % 2
        c_hi = (c // s) % 2
        return same & ((r_hi == 0) & (c_hi == 1) if rev else (r_hi == 1) & (c_hi == 0))

    masks = {rev: off_mask(1, rev) for rev in set(revs)}
    ts = [eye - jnp.where(masks[rev], lm, 0.0) for lm, rev in zip(lmats, revs)]
    s = 2
    while s < CHUNK:
        masks = {rev: off_mask(s, rev) for rev in set(revs)}
        ps = [_bdot(t, jnp.where(masks[rev], lm, 0.0)) for t, lm, rev in zip(ts, lmats, revs)]
        ts = [t - _bdot(p, t) for p, t in zip(ps, ts)]
        s *= 2
    return ts


def _delta_kernel(*refs, n_chunks, period, has_state, hb):
    (q_ref, k_ref, v_ref, z_ref, cwq, cwk, cwv, cbq, cbk, cbv, act_ref, cum_ref, cumt_ref, norm_ref) = refs[:14]
    rest = refs[14:]
    if has_state:
        s0_ref, *_aliased_out, o_ref, acc_ref = rest
        sout_ref = None
    else:
        o_ref, sout_ref, acc_ref = rest
    h0 = pl.program_id(1) * hb

    q = _conv_silu(q_ref[...].astype(F32), cwq, cbq, period)
    k = _conv_silu(k_ref[...].astype(F32), cwk, cbk, period)
    v = _conv_silu(v_ref[...].astype(F32), cwv, cbv, period)

    triples = [(hh, d, ci) for hh in range(hb) for d in (0, 1) for ci in range(n_chunks)]
    pre = {}
    for hh in range(hb):
        hs = slice(hh * DK_A, (hh + 1) * DK_A)
        qh, kh = q[:, hs], k[:, hs]
        qh = qh * lax.rsqrt(jnp.sum(qh * qh, axis=1, keepdims=True) + RMS_EPS) * (DK_A ** -0.5)
        kh = kh * lax.rsqrt(jnp.sum(kh * kh, axis=1, keepdims=True) + RMS_EPS)
        for d in (0, 1):
            m_incl, m_strict = _dir_masks(d == 1)
            for ci in range(n_chunks):
                sl = slice(ci * CHUNK, (ci + 1) * CHUNK)
                qc, kc, vc = qh[sl], kh[sl], v[sl, hs]
                beta = _pick_col(act_ref[sl, :], d * H_A + h0 + hh)
                gcol = _pick_col(cum_ref[sl, :], 2 * H_A + d * H_A + h0 + hh)
                grow = cumt_ref[pl.ds(2 * H_A + d * H_A + h0 + hh, 1), sl]
                decay = jnp.exp(jnp.where(m_incl, gcol - grow, NEG))
                kb = kc * beta
                pre[hh, d, ci] = dict(
                    qc=qc, kc=kc, kb=kb, vb=vc * beta, gcol=gcol, decay=decay,
                    lmat=_bdot_nt(kb, kc) * jnp.where(m_strict, decay, 0.0),
                    attn=_bdot_nt(qc, kc) * decay)
    tinvs = _tri_inverse_many([pre[t]["lmat"] for t in triples], [t[1] == 1 for t in triples])
    for t, tinv in zip(triples, tinvs):
        p = pre[t]
        p["u"] = _bdot(tinv, p["vb"])
        if has_state:
            p["w"] = _bdot(tinv, p["kb"] * jnp.exp(p["gcol"]))

    for hh in range(hb):
        hs = slice(hh * DK_A, (hh + 1) * DK_A)
        for d in (0, 1):
            rev = d == 1
            state = s0_ref[d, hh] if has_state else None
            order = range(n_chunks - 1, -1, -1) if rev else range(n_chunks)
            for ci in order:
                sl = slice(ci * CHUNK, (ci + 1) * CHUNK)
                p = pre[hh, d, ci]
                u, gcol = p["u"], p["gcol"]
                if state is not None:
                    u = u - _bdot(p["w"], state)
                o = _bdot(p["attn"], u)
                if state is not None:
                    o = o + _bdot(p["qc"] * jnp.exp(gcol), state)
                glast = gcol[0:1, :] if rev else gcol[CHUNK - 1:CHUNK, :]
                upd = _bdot_tn(p["kc"] * jnp.exp(glast - gcol), u)
                state = upd if state is None else state * jnp.exp(glast) + upd
                if rev:
                    acc_ref[sl, hs] = acc_ref[sl, hs] + o
                else:
                    acc_ref[sl, hs] = o
            if sout_ref is not None:
                sout_ref[d, hh] = state

    z = z_ref[...].astype(F32)
    for hh in range(hb):
        hs = slice(hh * DK_A, (hh + 1) * DK_A)
        o = acc_ref[:, hs]
        o = o * lax.rsqrt(jnp.mean(o * o, axis=1, keepdims=True) + RMS_EPS) * norm_ref[...]
        o_ref[:, hs] = (o * _silu(z[:, hs])).astype(o_ref.dtype)


def _alias_out(in_specs, args, out_buf):
    if out_buf is None:
        return {}
    in_specs.append(pl.BlockSpec(memory_space=pl.ANY))
    args.append(out_buf)
    return {len(args) - 1: 0}


def delta_mixer(proj, conv_w, conv_b, act, cum, cumt, norm, state, *, row0, n_seq, seq_len, period, hb,
                out_buf=None):
    assert row0 % seq_len == 0 and H_A % hb == 0
    n_chunks = seq_len // CHUNK
    rb0 = row0 // seq_len
    has_state = state is not None
    w = hb * DK_A
    nq = H_A // hb
    col = lambda off: (lambda s, h: (rb0 + s, off + h))
    cw = lambda off: (lambda s, h: (0, off + h))
    in_specs = [pl.BlockSpec((seq_len, w), col(0)), pl.BlockSpec((seq_len, w), col(nq)),
                pl.BlockSpec((seq_len, w), col(2 * nq)), pl.BlockSpec((seq_len, w), col(3 * nq)),
                pl.BlockSpec((3, w), cw(0)), pl.BlockSpec((3, w), cw(nq)), pl.BlockSpec((3, w), cw(2 * nq)),
                pl.BlockSpec((1, w), cw(0)), pl.BlockSpec((1, w), cw(nq)), pl.BlockSpec((1, w), cw(2 * nq)),
                pl.BlockSpec((seq_len, LANES), lambda s, h: (rb0 + s, 0)),
                pl.BlockSpec((seq_len, LANES), lambda s, h: (rb0 + s, 0)),
                pl.BlockSpec((LANES, seq_len), lambda s, h: (0, rb0 + s)),
                pl.BlockSpec((1, DV_A), lambda s, h: (0, 0))]
    args = [proj, proj, proj, proj, conv_w, conv_w, conv_w, conv_b, conv_b, conv_b, act, cum, cumt, norm]
    o_spec = pl.BlockSpec((seq_len, w), lambda s, h: (rb0 + s, h))
    o_shape = jax.ShapeDtypeStruct((proj.shape[0], H_A * DV_A), BF16)
    st_spec = pl.BlockSpec((None, 2, hb, DK_A, DV_A), lambda s, h: (s, 0, h, 0, 0))
    if has_state:
        in_specs.append(st_spec)
        args.append(state)
        out_specs, out_shape = o_spec, o_shape
    else:
        out_specs = [o_spec, st_spec]
        out_shape = [o_shape, jax.ShapeDtypeStruct((n_seq, 2, H_A, DK_A, DV_A), F32)]
    aliases = _alias_out(in_specs, args, out_buf)
    return pl.pallas_call(
        functools.partial(_delta_kernel, n_chunks=n_chunks, period=period, has_state=has_state, hb=hb),
        grid=(n_seq, H_A // hb),
        in_specs=in_specs, out_specs=out_specs, out_shape=out_shape, input_output_aliases=aliases,
        scratch_shapes=[pltpu.VMEM((seq_len, w), F32)],
        compiler_params=_cparams("arbitrary", "arbitrary"),
        name="delta_lat" if has_state else "delta_ctx",
    )(*args)


def even_gate_params(a_log, dt_bias, b_i, b_f):
    zeros_a = jnp.zeros((2 * H_A,), F32)
    coef = jnp.concatenate([zeros_a, -jnp.exp(a_log.astype(F32)).reshape(-1), jnp.zeros((4 * H_B,), F32)])
    bias = jnp.concatenate([zeros_a, dt_bias.reshape(-1), b_i.reshape(-1), b_f.reshape(-1)]).astype(F32)
    rev = jnp.concatenate([jnp.repeat(jnp.arange(2, dtype=F32), H_A)] * 2 + [jnp.repeat(jnp.arange(2, dtype=F32), H_B)] * 2)
    p = jnp.stack([coef, bias, rev])
    return jnp.pad(p, ((0, 5), (0, LANES - EV_GATES)))


def _mlstm_kernel(*refs, n_chunks, has_state):
    (q_ref, k_ref, v_ref, og_ref, act_ref, actt_ref, cum_ref, cumt_ref, norm_ref) = refs[:9]
    rest = refs[9:]
    if has_state:
        c0_ref, n0_ref, m0_ref, *_aliased_out, o_ref, acc_ref = rest
    else:
        o_ref, cout_ref, nm_ref, acc_ref = rest
    h = pl.program_id(1)
    i_col0, f_col0 = 4 * H_A, 4 * H_A + 2 * H_B

    for d in (0, 1):
        rev = d == 1
        m_incl, _ = _dir_masks(rev)
        if has_state:
            cm, nv, m = c0_ref[d], n0_ref[d], m0_ref[d]
        else:
            cm, nv, m = None, None, jnp.zeros((1, 1), F32)
        order = range(n_chunks - 1, -1, -1) if rev else range(n_chunks)
        for ci in order:
            sl = slice(ci * CHUNK, (ci + 1) * CHUNK)
            qc = q_ref[sl, :].astype(F32) * (DK_B ** -0.5)
            kc = k_ref[sl, :].astype(F32)
            vc = v_ref[sl, :].astype(F32)
            li_col = _pick_col(act_ref[sl, :], i_col0 + d * H_B + h)
            li_row = actt_ref[pl.ds(i_col0 + d * H_B + h, 1), sl]
            b_col = _pick_col(cum_ref[sl, :], f_col0 + d * H_B + h)
            b_row = cumt_ref[pl.ds(f_col0 + d * H_B + h, 1), sl]
            dlog = jnp.where(m_incl, b_col - b_row + li_row, NEG)
            inter = b_col + m
            m_q = jnp.maximum(inter, jnp.max(dlog, axis=1, keepdims=True))
            s = _bdot_nt(qc, kc) * jnp.exp(dlog - m_q)
            num = _bdot(s, vc)
            den = jnp.sum(s, axis=1, keepdims=True)
            if cm is not None:
                w_inter = jnp.exp(inter - m_q)
                num = num + w_inter * _bdot(qc, cm)
                den = den + w_inter * jnp.sum(qc * nv, axis=1, keepdims=True)
            hout = num / jnp.maximum(jnp.abs(den), jnp.exp(-m_q))
            b_last = b_col[0:1, :] if rev else b_col[CHUNK - 1:CHUNK, :]
            wlog = b_last - b_col + li_col
            m_new = jnp.maximum(b_last + m, jnp.max(wlog, axis=0, keepdims=True))
            kw = kc * jnp.exp(wlog - m_new)
            c_upd = _bdot_tn(kw, vc)
            n_upd = jnp.sum(kw, axis=0, keepdims=True)
            if cm is not None:
                sc = jnp.exp(b_last + m - m_new)
                cm, nv = sc * cm + c_upd, sc * nv + n_upd
            else:
                cm, nv = c_upd, n_upd
            m = m_new
            if rev:
                acc_ref[sl, :] = acc_ref[sl, :] + hout
            else:
                acc_ref[sl, :] = hout
        if not has_state:
            cout_ref[d] = cm
            nm_ref[d, 0:1, :] = nv
            nm_ref[d, 1:2, :] = jnp.broadcast_to(m, (1, DK_B))
            nm_ref[d, 2:8, :] = jnp.zeros((6, DK_B), F32)

    o = acc_ref[...]
    o = o * lax.rsqrt(jnp.mean(o * o, axis=1, keepdims=True) + RMS_EPS) * norm_ref[...]
    o_ref[...] = (o * jax.nn.sigmoid(og_ref[...].astype(F32))).astype(o_ref.dtype)


def mlstm_mixer(proj, act, actt, cum, cumt, norm, state, *, row0, n_seq, seq_len, out_buf=None):
    assert row0 % seq_len == 0
    n_chunks = seq_len // CHUNK
    rb0 = row0 // seq_len
    has_state = state is not None
    q0 = (CONV_A + H_A * DV_A) // LANES
    k0 = q0 + H_B
    v0 = (CONV_A + H_A * DV_A + 2 * H_B * DK_B) // DV_B
    o0 = v0 + H_B
    col = lambda off: (lambda s, h: (rb0 + s, off + h))
    in_specs = [pl.BlockSpec((seq_len, DK_B), col(q0)), pl.BlockSpec((seq_len, DK_B), col(k0)),
                pl.BlockSpec((seq_len, DV_B), col(v0)), pl.BlockSpec((seq_len, DV_B), col(o0)),
                pl.BlockSpec((seq_len, LANES), lambda s, h: (rb0 + s, 0)),
                pl.BlockSpec((LANES, seq_len), lambda s, h: (0, rb0 + s)),
                pl.BlockSpec((seq_len, LANES), lambda s, h: (rb0 + s, 0)),
                pl.BlockSpec((LANES, seq_len), lambda s, h: (0, rb0 + s)),
                pl.BlockSpec((1, DV_B), lambda s, h: (0, 0))]
    args = [proj, proj, proj, proj, act, actt, cum, cumt, norm]
    o_spec = pl.BlockSpec((seq_len, DV_B), lambda s, h: (rb0 + s, h))
    o_shape = jax.ShapeDtypeStruct((proj.shape[0], H_B * DV_B), BF16)
    st_idx = lambda s, h: (s, 0, h, 0, 0)
    if has_state:
        c0, n0, m0 = state
        in_specs += [pl.BlockSpec((None, 2, None, DK_B, DV_B), st_idx),
                     pl.BlockSpec((None, 2, None, 1, DK_B), st_idx),
                     pl.BlockSpec((None, 2, None, 1, 1), st_idx)]
        args += [c0, n0.reshape(n_seq, 2, H_B, 1, DK_B), m0.reshape(n_seq, 2, H_B, 1, 1)]
        out_specs, out_shape = o_spec, o_shape
    else:
        out_specs = [o_spec, pl.BlockSpec((None, 2, None, DK_B, DV_B), st_idx),
                     pl.BlockSpec((None, 2, None, 8, DK_B), st_idx)]
        out_shape = [o_shape, jax.ShapeDtypeStruct((n_seq, 2, H_B, DK_B, DV_B), F32),
                     jax.ShapeDtypeStruct((n_seq, 2, H_B, 8, DK_B), F32)]
    aliases = _alias_out(in_specs, args, out_buf)
    return pl.pallas_call(
        functools.partial(_mlstm_kernel, n_chunks=n_chunks, has_state=has_state),
        grid=(n_seq, H_B),
        in_specs=in_specs, out_specs=out_specs, out_shape=out_shape, input_output_aliases=aliases,
        scratch_shapes=[pltpu.VMEM((seq_len, DV_B), F32)],
        compiler_params=_cparams("arbitrary", "arbitrary"),
        name="mlstm_lat" if has_state else "mlstm_ctx",
    )(*args)


def odd_gate_params(a_log, dt_bias):
    coef = -jnp.exp(a_log.astype(F32)).reshape(-1)
    bias = dt_bias.astype(F32).reshape(-1)
    rev = jnp.repeat(jnp.arange(2, dtype=F32), H_C)
    return jnp.pad(jnp.stack([coef, bias, rev]), ((0, 5), (0, 0)))


def _ssd_kernel(*refs, n_chunks, period, has_state):
    (z_ref, x_ref, b_ref, c_ref, cwx, cwb, cwc, cbx, cbb, cbc, dt_ref, cum_ref, cumt_ref, dskip_ref, norm_ref) = refs[:15]
    rest = refs[15:]
    if has_state:
        s0_ref, *_aliased_out, o_ref, acc_ref = rest
        sout_ref = None
    else:
        o_ref, sout_ref, acc_ref = rest
    g = pl.program_id(1)

    x = _conv_silu(x_ref[...].astype(F32), cwx, cbx, period)
    bm = _conv_silu(b_ref[...].astype(F32), cwb, cbb, period)
    cm = _conv_silu(c_ref[...].astype(F32), cwc, cbc, period)

    er = lax.broadcasted_iota(jnp.int32, (LANES, GW_C), 0)
    ec = lax.broadcasted_iota(jnp.int32, (LANES, GW_C), 1)
    tr = lax.broadcasted_iota(jnp.int32, (GW_C, LANES), 0)
    tc = lax.broadcasted_iota(jnp.int32, (GW_C, LANES), 1)
    lane_in_tile = lax.broadcasted_iota(jnp.int32, (CHUNK, LANES), 1)

    for d in (0, 1):
        rev = d == 1
        m_incl, _ = _dir_masks(rev)
        col0 = d * H_C + g * HG_C
        expand = jnp.where(er == col0 + ec // P_C, 1.0, 0.0)
        expand_t = tc == col0 + tr // P_C
        state = s0_ref[d].reshape(GW_C, N_C) if has_state else None
        order = range(n_chunks - 1, -1, -1) if rev else range(n_chunks)
        for ci in order:
            sl = slice(ci * CHUNK, (ci + 1) * CHUNK)
            xc, bc, cc = x[sl], bm[sl], cm[sl]
            cum_blk = cum_ref[sl, :]
            cum_last = cum_blk[0:1, :] if rev else cum_blk[CHUNK - 1:CHUNK, :]
            xdt = xc * _dot_exact_rhs(dt_ref[sl, :], expand, passes=2)
            scores = _bdot_nt(cc, bc)
            for hp in range(HG_C // 2):
                ps = slice(hp * LANES, (hp + 1) * LANES)
                xpair = xdt[:, ps]
                ypair = None
                for sub in (0, 1):
                    hh = 2 * hp + sub
                    cb = _pick_col(cum_blk, col0 + hh)
                    crow = cumt_ref[pl.ds(col0 + hh, 1), sl]
                    seg = jnp.exp(jnp.where(m_incl, cb - crow, NEG))
                    mine = (lane_in_tile < P_C) if sub == 0 else (lane_in_tile >= P_C)
                    y = _bdot(scores * seg, jnp.where(mine, xpair, 0.0))
                    ypair = y if ypair is None else ypair + y
                if rev:
                    acc_ref[sl, ps] = acc_ref[sl, ps] + ypair
                else:
                    acc_ref[sl, ps] = ypair
            if state is not None:
                y_in = _bdot_nt(cc, state) * _dot_exact_rhs(jnp.exp(cum_blk), expand, passes=2)
                acc_ref[sl, :] = acc_ref[sl, :] + y_in
            dend = _dot_exact_rhs(jnp.exp(jnp.minimum(cum_last - cum_blk, 0.0)), expand, passes=2)
            upd = _bdot_tn(xdt * dend, bc)
            if state is not None:
                tot = jnp.sum(jnp.where(expand_t, jnp.broadcast_to(cum_last, (GW_C, LANES)), 0.0), axis=1, keepdims=True)
                state = state * jnp.exp(tot) + upd
            else:
                state = upd
        if sout_ref is not None:
            sout_ref[d] = state.reshape(HG_C, P_C, N_C)

    y = acc_ref[...] + dskip_ref[...] * x
    y = y * _silu(z_ref[...].astype(F32))
    y = y * lax.rsqrt(jnp.mean(y * y, axis=1, keepdims=True) + RMS_EPS) * norm_ref[...]
    o_ref[...] = y.astype(o_ref.dtype)


def ssd_mixer(proj, conv_w, conv_b, dt, cum, cumt, dskip, norm, state, *, row0, n_seq, seq_len, period, out_buf=None):
    assert row0 % seq_len == 0
    n_chunks = seq_len // CHUNK
    rb0 = row0 // seq_len
    has_state = state is not None
    xb0 = D_INNER // GW_C
    bb0 = 2 * D_INNER // N_C
    cb0 = bb0 + G_C
    wb0 = D_INNER // N_C
    wc0 = wb0 + G_C
    col = lambda off: (lambda s, g: (rb0 + s, off + g))
    cw = lambda off: (lambda s, g: (0, off + g))
    in_specs = [pl.BlockSpec((seq_len, GW_C), col(0)), pl.BlockSpec((seq_len, GW_C), col(xb0)),
                pl.BlockSpec((seq_len, N_C), col(bb0)), pl.BlockSpec((seq_len, N_C), col(cb0)),
                pl.BlockSpec((3, GW_C), cw(0)), pl.BlockSpec((3, N_C), cw(wb0)), pl.BlockSpec((3, N_C), cw(wc0)),
                pl.BlockSpec((1, GW_C), cw(0)), pl.BlockSpec((1, N_C), cw(wb0)), pl.BlockSpec((1, N_C), cw(wc0)),
                pl.BlockSpec((seq_len, LANES), lambda s, g: (rb0 + s, 0)),
                pl.BlockSpec((seq_len, LANES), lambda s, g: (rb0 + s, 0)),
                pl.BlockSpec((LANES, seq_len), lambda s, g: (0, rb0 + s)),
                pl.BlockSpec((1, GW_C), cw(0)), pl.BlockSpec((1, GW_C), cw(0))]
    args = [proj, proj, proj, proj, conv_w, conv_w, conv_w, conv_b, conv_b, conv_b, dt, cum, cumt, dskip, norm]
    o_spec = pl.BlockSpec((seq_len, GW_C), lambda s, g: (rb0 + s, g))
    o_shape = jax.ShapeDtypeStruct((proj.shape[0], D_INNER), BF16)
    st_spec = pl.BlockSpec((None, 2, HG_C, P_C, N_C), lambda s, g: (s, 0, g, 0, 0))
    if has_state:
        in_specs.append(st_spec)
        args.append(state)
        out_specs, out_shape = o_spec, o_shape
    else:
        out_specs = [o_spec, st_spec]
        out_shape = [o_shape, jax.ShapeDtypeStruct((n_seq, 2, H_C, P_C, N_C), F32)]
    aliases = _alias_out(in_specs, args, out_buf)
    return pl.pallas_call(
        functools.partial(_ssd_kernel, n_chunks=n_chunks, period=period, has_state=has_state),
        grid=(n_seq, G_C),
        in_specs=in_specs, out_specs=out_specs, out_shape=out_shape, input_output_aliases=aliases,
        scratch_shapes=[pltpu.VMEM((seq_len, GW_C), F32)],
        compiler_params=_cparams("arbitrary", "arbitrary"),
        name="ssd_lat" if has_state else "ssd_ctx",
    )(*args)


def _dot3(a, b):
    a_hi = a.astype(BF16)
    a_lo = (a - a_hi.astype(F32)).astype(BF16)
    b_hi = b.astype(BF16)
    b_lo = (b - b_hi.astype(F32)).astype(BF16)
    d = lambda p, q: jnp.dot(p, q, preferred_element_type=F32)
    return d(a_hi, b_hi) + (d(a_hi, b_lo) + d(a_lo, b_hi))


def _dot3_nt(a, b):
    a_hi = a.astype(BF16)
    a_lo = (a - a_hi.astype(F32)).astype(BF16)
    b_hi = b.astype(BF16)
    b_lo = (b - b_hi.astype(F32)).astype(BF16)
    d = lambda p, q: lax.dot_general(p, q, (((1,), (1,)), ((), ())), preferred_element_type=F32)
    return d(a_hi, b_hi) + (d(a_hi, b_lo) + d(a_lo, b_hi))


def _resid_ln_kernel(*refs, n_y, lhs_widths, n_gathered, gate, sh, sc, want_h, want_logits):
    x_ref = refs[0]
    y_refs = refs[1:1 + n_y]
    rest = list(refs[1 + n_y:])
    lhs_refs = [rest.pop(0) for _ in lhs_widths]
    pw_ref = rest.pop(0) if lhs_widths else None
    m_ref, mn_ref, g_ref, b_ref = (rest.pop(0) for _ in range(4))
    if n_gathered:
        gath_ref, gw_ref = rest.pop(0), rest.pop(0)
    rw_ref = rest.pop(0) if want_logits else None
    xo_ref = rest.pop(0)
    y = None
    for r in y_refs:
        y = r[...].astype(F32) if y is None else y + r[...].astype(F32)
    off = 0
    for lhs_ref, width in zip(lhs_refs, lhs_widths):
        part = jnp.dot(lhs_ref[...], pw_ref[off:off + width, :], preferred_element_type=F32)
        y = part if y is None else y + part
        off += width
    for kk in range(n_gathered):
        y = y + gw_ref[:, kk:kk + 1] * gath_ref[kk].astype(F32)
    v = ALPHA * x_ref[...] + m_ref[gate:gate + 1, :] * y
    mu = jnp.mean(v, axis=1, keepdims=True)
    vc = v - mu
    var = jnp.mean(vc * vc, axis=1, keepdims=True)
    xn = vc * lax.rsqrt(var + LN_EPS) * g_ref[...] + b_ref[...]
    xo_ref[...] = xn
    if want_h:
        hm = xn * (1.0 + mn_ref[sc:sc + 1, :]) + mn_ref[sh:sh + 1, :]
        rest.pop(0)[...] = hm.astype(BF16)
        if want_logits:
            rest.pop(0)[...] = _dot3_nt(rw_ref[...], hm)


def resid_ln(x, ys, mod, mod_next, ln_g, ln_b, router_w, *, gate, sh, sc, want_h, t_ctx, lat_len, gathered=None,
             h_rows=None, proj=None, tm=256):
    t, d = x.shape
    want_logits = router_w is not None
    grp = lambda i: (_group_of_block(i, tm, t_ctx, lat_len), 0, 0)
    row = pl.BlockSpec((tm, d), lambda i: (i, 0))
    vec = pl.BlockSpec((1, d), lambda i: (0, 0))
    in_specs = [row] * (1 + len(ys))
    args = [x, *ys]
    lhs_widths = ()
    if proj is not None:
        lhs_list, pw = proj
        lhs_widths = tuple(a.shape[1] for a in lhs_list)
        in_specs += [pl.BlockSpec((tm, wd), lambda i: (i, 0)) for wd in lhs_widths]
        in_specs.append(pl.BlockSpec(pw.shape, lambda i: (0, 0), pipeline_mode=pl.Buffered(1)))
        args += [*lhs_list, pw]
    in_specs += [pl.BlockSpec((None, 6, d), grp), pl.BlockSpec((None, 6, d), grp), vec, vec]
    args += [mod, mod_next, ln_g.reshape(1, d), ln_b.reshape(1, d)]
    n_gathered = 0
    if gathered is not None:
        n_gathered = gathered[0].shape[0]
        in_specs += [pl.BlockSpec((n_gathered, tm, d), lambda i: (0, i, 0)),
                     pl.BlockSpec((tm, n_gathered), lambda i: (i, 0))]
        args += list(gathered)
    out_specs, out_shape = [row], [jax.ShapeDtypeStruct((t, d), F32)]
    if want_logits:
        n_e = router_w.shape[1]
        in_specs.append(pl.BlockSpec((n_e, d), lambda i: (0, 0)))
        args.append(router_w.T)
    if want_h:
        out_specs.append(row)
        out_shape.append(jax.ShapeDtypeStruct((h_rows or t, d), BF16))
    if want_logits:
        out_specs.append(pl.BlockSpec((n_e, tm), lambda i: (0, i)))
        out_shape.append(jax.ShapeDtypeStruct((n_e, t), F32))
    return pl.pallas_call(
        functools.partial(_resid_ln_kernel, n_y=len(ys), lhs_widths=lhs_widths, n_gathered=n_gathered, gate=gate, sh=sh,
                          sc=sc, want_h=want_h, want_logits=want_logits),
        grid=(t // tm,),
        in_specs=in_specs, out_specs=out_specs, out_shape=out_shape,
        compiler_params=_cparams("arbitrary"),
        name="resid_ln",
    )(*args)


def _ffn_kernel(be_ref, nu_ref, x_ref, wg_ref, wu_ref, wd_ref, o_ref, g_bf, u_bf, d_bf):
    b = pl.program_id(0)

    @pl.when(b < nu_ref[0])
    def _():
        prev = be_ref[jnp.maximum(b - 1, 0)]

        @pl.when((b == 0) | (be_ref[b] != prev))
        def _():
            g_bf[...] = wg_ref[...].astype(BF16)
            u_bf[...] = wu_ref[...].astype(BF16)
            d_bf[...] = wd_ref[...].astype(BF16)

        x = x_ref[...]
        hg = jnp.dot(x, g_bf[...], preferred_element_type=F32)
        hu = jnp.dot(x, u_bf[...], preferred_element_type=F32)
        a = (_silu(hg) * hu).astype(BF16)
        o_ref[...] = jnp.dot(a, d_bf[...], preferred_element_type=F32).astype(o_ref.dtype)


def expert_ffn(xs, blk_e, n_used, w_gate, w_up, w_down, layer, *, tm, weight_buffers=2, out_dtype=BF16,
               name="expert_ffn"):
    d = xs.shape[1]
    de = w_gate.shape[3]
    n_blk = blk_e.shape[0]
    r = n_blk * tm
    wsel = lambda b, be, nu: (layer, be[b], 0, 0)
    wmode = dict(pipeline_mode=pl.Buffered(weight_buffers)) if weight_buffers != 2 else {}
    grid_spec = pltpu.PrefetchScalarGridSpec(
        num_scalar_prefetch=2,
        grid=(n_blk,),
        in_specs=[pl.BlockSpec((tm, d), lambda b, be, nu: (b, 0)),
                  pl.BlockSpec((None, None, d, de), wsel, **wmode), pl.BlockSpec((None, None, d, de), wsel, **wmode),
                  pl.BlockSpec((None, None, de, d), wsel, **wmode)],
        out_specs=pl.BlockSpec((tm, d), lambda b, be, nu: (b, 0)),
        scratch_shapes=[pltpu.VMEM((d, de), BF16), pltpu.VMEM((d, de), BF16), pltpu.VMEM((de, d), BF16)],
    )
    return pl.pallas_call(
        _ffn_kernel,
        grid_spec=grid_spec,
        out_shape=jax.ShapeDtypeStruct((r, d), out_dtype),
        compiler_params=_cparams("arbitrary"),
        name=name,
    )(blk_e, n_used, xs, w_gate, w_up, w_down)


ROUTE_TM = 512
GROUP_SIZE = N_EXP // N_GROUPS


def _first_argmax(v, idx, axis, sentinel):
    mx = jnp.max(v, axis=axis, keepdims=True)
    return mx, jnp.min(jnp.where(v == mx, idx, sentinel), axis=axis, keepdims=True)


def _route_kernel(lt_ref, bias_ref, idx_ref, w_ref, rank_ref, cnt_ref, carry_ref):
    i = pl.program_id(0)
    tm = lt_ref.shape[1]

    @pl.when(i == 0)
    def _():
        carry_ref[...] = jnp.zeros_like(carry_ref)

    scores = jax.nn.sigmoid(lt_ref[...])
    biased = scores + bias_ref[...]
    b3 = biased.reshape(N_GROUPS, GROUP_SIZE, tm)
    mem = lax.broadcasted_iota(jnp.int32, b3.shape, 1).astype(F32)
    m1, first = _first_argmax(b3, mem, 1, float(GROUP_SIZE))
    m2 = jnp.max(jnp.where(mem == first, -jnp.inf, b3), axis=1, keepdims=True)
    gs = (m1 + m2).reshape(N_GROUPS, tm)
    gi = lax.broadcasted_iota(jnp.int32, gs.shape, 0).astype(F32)
    gsel = jnp.zeros(gs.shape, F32)
    cur = gs
    for _ in range(TOPK_GROUPS):
        _, pick = _first_argmax(cur, gi, 0, float(N_GROUPS))
        hit = gi == pick
        gsel = jnp.where(hit, 1.0, gsel)
        cur = jnp.where(hit, -jnp.inf, cur)
    masked = jnp.where(gsel.reshape(N_GROUPS, 1, tm) > 0.5, b3, -jnp.inf).reshape(N_EXP, tm)

    ei = lax.broadcasted_iota(jnp.int32, masked.shape, 0).astype(F32)
    picks, sel_scores = [], []
    chosen = jnp.zeros(masked.shape, F32)
    cur = masked
    for _ in range(TOP_K):
        _, pick = _first_argmax(cur, ei, 0, float(N_EXP))
        hit = ei == pick
        picks.append(pick)
        sel_scores.append(jnp.sum(jnp.where(hit, scores, 0.0), axis=0, keepdims=True))
        chosen = jnp.where(hit, 1.0, chosen)
        cur = jnp.where(hit, -jnp.inf, cur)

    r, c = _tri_masks(tm)
    before = jnp.where(r < c, 1.0, 0.0).astype(BF16)
    rank = jnp.dot(chosen.astype(BF16), before, preferred_element_type=F32) + carry_ref[...]
    carry_ref[...] = carry_ref[...] + jnp.sum(chosen, axis=1, keepdims=True)
    cnt_ref[...] = carry_ref[...]

    total = sel_scores[0]
    for s in sel_scores[1:]:
        total = total + s
    for k in range(TOP_K):
        idx_ref[k:k + 1, :] = picks[k].astype(jnp.int32)
        w_ref[k:k + 1, :] = sel_scores[k] / total * ROUTED_SCALE
        rank_ref[k:k + 1, :] = jnp.sum(jnp.where(ei == picks[k], rank, 0.0), axis=0, keepdims=True).astype(jnp.int32)


def route(logits_t, router_bias):
    n_e, t = logits_t.shape
    tm = ROUTE_TM
    kt = pl.BlockSpec((TOP_K, tm), lambda i: (0, i))
    return pl.pallas_call(
        _route_kernel,
        grid=(t // tm,),
        in_specs=[pl.BlockSpec((n_e, tm), lambda i: (0, i)), pl.BlockSpec((n_e, 1), lambda i: (0, 0))],
        out_specs=[kt, kt, kt, pl.BlockSpec((n_e, 1), lambda i: (0, 0))],
        out_shape=[jax.ShapeDtypeStruct((TOP_K, t), jnp.int32), jax.ShapeDtypeStruct((TOP_K, t), F32),
                   jax.ShapeDtypeStruct((TOP_K, t), jnp.int32), jax.ShapeDtypeStruct((n_e, 1), F32)],
        scratch_shapes=[pltpu.VMEM((n_e, 1), F32)],
        compiler_params=_cparams("arbitrary"),
        name="route",
    )(logits_t, router_bias.reshape(n_e, 1))


def moe(h, logits_t, router_bias, e_gate, e_up, e_down, s_gate, s_up, s_down, layer):
    d = h.shape[1]
    t = logits_t.shape[1]
    tm = 256
    top_e, wts, rank, counts = route(logits_t, router_bias)
    counts = counts.reshape(-1).astype(jnp.int32)
    n_assign = t * TOP_K
    padded = (counts + tm - 1) // tm * tm
    pad_end = jnp.cumsum(padded)
    pad_start = pad_end - padded
    n_blk = n_assign // tm + N_EXP
    blk_first = jnp.arange(n_blk, dtype=jnp.int32) * tm
    blk_e = jnp.minimum(jnp.sum((pad_end[None, :] <= blk_first[:, None]).astype(jnp.int32), axis=1), N_EXP - 1)
    n_used = (pad_end[-1] // tm).astype(jnp.int32).reshape(1)
    expert_ids = jnp.arange(N_EXP, dtype=jnp.int32)
    dest = jnp.sum(jnp.where(top_e[..., None] == expert_ids, pad_start, 0), axis=-1) + rank
    tok = jnp.broadcast_to(jnp.arange(t, dtype=jnp.int32), (TOP_K, t))
    filler = jnp.arange(n_blk * tm, dtype=jnp.int32) % t
    slot_tok = filler.at[dest.reshape(-1)].set(tok.reshape(-1), unique_indices=True)
    xs = h.at[slot_tok].get(mode="promise_in_bounds")
    ys = expert_ffn(xs, blk_e, n_used, e_gate, e_up, e_down, layer, tm=tm, name="routed_ffn")
    routed_rows = ys.at[dest.reshape(-1)].get(mode="promise_in_bounds").reshape(TOP_K, t, d)
    tm_sh = 512
    n_sh = t // tm_sh
    shared = expert_ffn(h, jnp.zeros((n_sh,), jnp.int32), jnp.full((1,), n_sh, jnp.int32),
                        s_gate[:, None], s_up[:, None], s_down[:, None], layer, tm=tm_sh, weight_buffers=1,
                        name="shared_ffn")
    return shared, routed_rows, wts.T


def kernel(x_prompt, x_sample, state_dn, state_ml_C, state_ml_n, state_ml_m, state_ssd, c, c_ctx,
           mod_w, mod_b, ln1_g, ln1_b, ln2_g, ln2_b, router_w, router_bias, exp_gate, exp_up, exp_down,
           sh_gate, sh_up, sh_down, ev_w_in, ev_conv_w, ev_conv_b, dn_A_log, dn_dt_bias, ml_b_i, ml_b_f,
           dn_norm, ml_norm, ev_w_out, od_w_in, od_conv_w, od_conv_b, ssd_A_log, ssd_dt_bias, ssd_D,
           ssd_norm, od_w_out):
    bp, sl, d = x_prompt.shape
    bl, ll, _ = x_sample.shape
    depth = mod_w.shape[0]
    t_ctx = bp * sl
    x = jnp.concatenate([x_prompt.reshape(t_ctx, d), x_sample.reshape(bl * ll, d)], axis=0)
    cvec = jnp.concatenate([c_ctx[None], c, jnp.zeros((8 - 1 - bl, d), F32)], axis=0)
    mods = compute_mods(cvec, mod_w, mod_b)[:, :1 + bl].reshape(depth, 1 + bl, 6, d)
    geo = dict(t_ctx=t_ctx, lat_len=ll)
    ctx = dict(row0=0, n_seq=bp, seq_len=sl)
    lat = dict(row0=t_ctx, n_seq=bl, seq_len=ll)

    h = modulate(x, mods[0], 0, 1, **geo)
    new_dn, new_c, new_n, new_m, new_ssd = [], [], [], [], []
    for l in range(depth):
        j = l // 2
        if l % 2 == 0:
            w_in = ev_w_in[j]
            proj = matmul(h, w_in, tm=1024, tn=1024, n_out=EV_MAIN, out_dtype=BF16, name="ev_in_proj")
            graw = matmul(h, w_in, tm=1024, tn=LANES, n_out=LANES, col_block_off=EV_MAIN // LANES,
                          valid_cols=EV_GATES, name="ev_gate_proj")
            act, cum = gate_prep(graw, even_gate_params(dn_A_log[j], dn_dt_bias[j], ml_b_i[j], ml_b_f[j]), "even")
            actt, cumt = act.T, cum.T
            cw, cb, dnn, mln = ev_conv_w[j], ev_conv_b[j].reshape(1, -1), dn_norm[j].reshape(1, -1), ml_norm[j].reshape(1, -1)
            oa, s_dn = delta_mixer(proj, cw, cb, act, cum, cumt, dnn, None, period=sl, hb=4, **ctx)
            oa = delta_mixer(proj, cw, cb, act, cum, cumt, dnn, state_dn[:, j], period=GRID_W, hb=1, out_buf=oa, **lat)
            ob, s_c, s_nm = mlstm_mixer(proj, act, actt, cum, cumt, mln, None, **ctx)
            ob = mlstm_mixer(proj, act, actt, cum, cumt, mln,
                             (state_ml_C[:, j], state_ml_n[:, j], state_ml_m[:, j]), out_buf=ob, **lat)
            out_proj = ([oa, ob], ev_w_out[j].astype(BF16))
            new_dn.append(s_dn)
            new_c.append(s_c)
            new_n.append(s_nm[:, :, :, 0, :])
            new_m.append(s_nm[:, :, :, 1, 0])
        else:
            w_in = od_w_in[j]
            proj = matmul(h, w_in, tm=1024, tn=1024, n_out=OD_MAIN, out_dtype=BF16, name="od_in_proj")
            draw = matmul(h, w_in, tm=1024, tn=LANES, n_out=LANES, col_block_off=OD_MAIN // LANES, name="od_dt_proj")
            dt, cum = gate_prep(draw, odd_gate_params(ssd_A_log[j], ssd_dt_bias[j]), "odd")
            cumt = cum.T
            cw, cb = od_conv_w[j], od_conv_b[j].reshape(1, -1)
            dsk, nrm = jnp.repeat(ssd_D[j], P_C).reshape(1, -1), ssd_norm[j].reshape(1, -1)
            oc, s_ssd = ssd_mixer(proj, cw, cb, dt, cum, cumt, dsk, nrm, None, period=sl, **ctx)
            oc = ssd_mixer(proj, cw, cb, dt, cum, cumt, dsk, nrm, state_ssd[:, j], period=GRID_W, out_buf=oc, **lat)
            out_proj = ([oc], od_w_out[j].astype(BF16))
            new_ssd.append(s_ssd)
        x, h2, logits_t = resid_ln(x, [], mods[l], mods[l], ln1_g[l], ln1_b[l], router_w[l], proj=out_proj, tm=512,
                                   gate=2, sh=3, sc=4, want_h=True, h_rows=GATHER_SRC_ROWS, **geo)
        shared, routed_rows, wts = moe(h2, logits_t, router_bias[l], exp_gate, exp_up, exp_down,
                                       sh_gate, sh_up, sh_down, l)
        last = l == depth - 1
        res = resid_ln(x, [shared], mods[l], mods[min(l + 1, depth - 1)], ln2_g[l], ln2_b[l], None,
                       gate=5, sh=0, sc=1, want_h=not last, gathered=(routed_rows, wts), **geo)
        x = res[0]
        if not last:
            h = res[1]
    y_prompt = x[:t_ctx].reshape(bp, sl, d)
    y_sample = x[t_ctx:].reshape(bl, ll, d)
    return (y_prompt, y_sample, jnp.stack(new_dn, axis=1), jnp.stack(new_c, axis=1), jnp.stack(new_n, axis=1),
            jnp.stack(new_m, axis=1), jnp.stack(new_ssd, axis=1))
```

```python
import functools

import jax
import jax.numpy as jnp
from jax import lax
from jax.experimental import pallas as pl
from jax.experimental.pallas import tpu as pltpu

F32 = jnp.float32
BF16 = jnp.bfloat16

D_MODEL = 2048
DEPTH = 2
GRID_W = 64
ALPHA = (2 * DEPTH) ** 0.25
LN_EPS = 1e-5
RMS_EPS = 1e-6

H_A, DK_A, DV_A = 8, 128, 128
H_B, DK_B, DV_B = 4, 128, 256
CONV_A = 2 * H_A * DK_A + H_A * DV_A
EV_MAIN = CONV_A + H_A * DV_A + 2 * H_B * DK_B + 2 * H_B * DV_B
EV_GATES = 4 * H_A + 4 * H_B

D_INNER = 2 * D_MODEL
P_C, N_C, G_C = 64, 128, 8
H_C = D_INNER // P_C
HG_C = H_C // G_C
GW_C = D_INNER // G_C
OD_MAIN = 2 * D_INNER + 2 * G_C * N_C

N_EXP, TOP_K, N_GROUPS, TOPK_GROUPS = 64, 8, 8, 4
D_EXP = 512
ROUTED_SCALE = 2.5

MOE_CHUNKS = 3
GATHER_SRC_ROWS = 16384
CHUNK = 256
LANES = 128
VMEM_LIMIT = 56 * 1024 * 1024
NEG = -1e30


def _cparams(*sem):
    return pltpu.CompilerParams(dimension_semantics=sem, vmem_limit_bytes=VMEM_LIMIT)


def _bdot(a, b):
    return jnp.dot(a.astype(BF16), b.astype(BF16), preferred_element_type=F32)


def _bdot_nt(a, b):
    return lax.dot_general(a.astype(BF16), b.astype(BF16), (((1,), (1,)), ((), ())), preferred_element_type=F32)


def _bdot_tn(a, b):
    return lax.dot_general(a.astype(BF16), b.astype(BF16), (((0,), (0,)), ((), ())), preferred_element_type=F32)


def _split3(a):
    hi = a.astype(BF16)
    r = a - hi.astype(F32)
    mid = r.astype(BF16)
    lo = (r - mid.astype(F32)).astype(BF16)
    return hi, mid, lo


def _dot_exact_rhs(a, b_exact, passes=3):
    hi, mid, lo = _split3(a)
    bb = b_exact.astype(BF16)
    d = lambda p: jnp.dot(p, bb, preferred_element_type=F32)
    return d(hi) + d(mid) + d(lo) if passes == 3 else d(hi) + d(mid)


def _dot_exact_lhs(a_exact, b):
    hi, mid, lo = _split3(b)
    aa = a_exact.astype(BF16)
    d = lambda p: jnp.dot(aa, p, preferred_element_type=F32)
    return d(hi) + d(mid) + d(lo)


def _silu(x):
    return x * jax.nn.sigmoid(x)


def _softplus(x):
    return jnp.maximum(x, 0.0) + jnp.log(1.0 + jnp.exp(-jnp.abs(x)))


def _group_of_block(i, tm, t_ctx, lat_len):
    return jnp.maximum(i * tm - t_ctx, -1) // lat_len + 1


def _mod_kernel(c_ref, w_ref, b_ref, o_ref):
    c = c_ref[...]
    o_ref[...] = _bdot(_silu(c), w_ref[...]) + b_ref[...]


def compute_mods(cvec, mod_w, mod_b):
    depth, d, n = mod_w.shape
    tn = 512
    return pl.pallas_call(
        _mod_kernel,
        grid=(depth, n // tn),
        in_specs=[pl.BlockSpec((8, d), lambda l, j: (0, 0)),
                  pl.BlockSpec((None, d, tn), lambda l, j: (l, 0, j)),
                  pl.BlockSpec((None, 1, tn), lambda l, j: (l, 0, j))],
        out_specs=pl.BlockSpec((None, 8, tn), lambda l, j: (l, 0, j)),
        out_shape=jax.ShapeDtypeStruct((depth, 8, n), F32),
        compiler_params=_cparams("arbitrary", "arbitrary"),
        name="mod_vectors",
    )(cvec, mod_w, mod_b.reshape(depth, 1, n))


def _modulate_kernel(x_ref, m_ref, o_ref, *, sh, sc):
    o_ref[...] = (x_ref[...] * (1.0 + m_ref[sc:sc + 1, :]) + m_ref[sh:sh + 1, :]).astype(o_ref.dtype)


def modulate(x, mod, sh, sc, t_ctx, lat_len):
    t, d = x.shape
    tm = 512
    return pl.pallas_call(
        functools.partial(_modulate_kernel, sh=sh, sc=sc),
        grid=(t // tm,),
        in_specs=[pl.BlockSpec((tm, d), lambda i: (i, 0)),
                  pl.BlockSpec((None, 6, d), lambda i: (_group_of_block(i, tm, t_ctx, lat_len), 0, 0))],
        out_specs=pl.BlockSpec((tm, d), lambda i: (i, 0)),
        out_shape=jax.ShapeDtypeStruct((t, d), BF16),
        compiler_params=_cparams("arbitrary"),
        name="modulate",
    )(x, mod)


def _matmul_kernel(x_ref, w_ref, o_ref, wbf_ref, *, valid_cols):
    @pl.when(pl.program_id(1) == 0)
    def _():
        wbf_ref[...] = w_ref[...].astype(BF16)

    y = jnp.dot(x_ref[...], wbf_ref[...], preferred_element_type=F32)
    if valid_cols is not None:
        col = lax.broadcasted_iota(jnp.int32, y.shape, 1)
        y = jnp.where(col < valid_cols, y, 0.0)
    o_ref[...] = y.astype(o_ref.dtype)


def matmul(x, w, *, tm, tn, n_out, col_block_off=0, valid_cols=None, out_dtype=F32, name="matmul"):
    m, k = x.shape
    return pl.pallas_call(
        functools.partial(_matmul_kernel, valid_cols=valid_cols),
        grid=(n_out // tn, m // tm),
        in_specs=[pl.BlockSpec((tm, k), lambda j, i: (i, 0)),
                  pl.BlockSpec((k, tn), lambda j, i: (0, j + col_block_off))],
        out_specs=pl.BlockSpec((tm, tn), lambda j, i: (i, j)),
        out_shape=jax.ShapeDtypeStruct((m, n_out), out_dtype),
        scratch_shapes=[pltpu.VMEM((k, tn), BF16)],
        compiler_params=_cparams("arbitrary", "arbitrary"),
        name=name,
    )(x, w)


def _tri_masks(n):
    r = lax.broadcasted_iota(jnp.int32, (n, n), 0)
    c = lax.broadcasted_iota(jnp.int32, (n, n), 1)
    return r, c


def _gate_kernel(raw_ref, p_ref, act_ref, cum_ref, *, mode):
    x = raw_ref[...]
    coef, bias, rev = p_ref[0:1, :], p_ref[1:2, :], p_ref[2:3, :]
    col = lax.broadcasted_iota(jnp.int32, x.shape, 1)
    xb = x + bias
    if mode == "even":
        act = jnp.where(col < 2 * H_A, jax.nn.sigmoid(xb),
                        jnp.where(col < 4 * H_A, coef * _softplus(xb),
                                  jnp.where(col < 4 * H_A + 2 * H_B, xb,
                                            jnp.minimum(xb, 0.0) - jnp.log(1.0 + jnp.exp(-jnp.abs(xb))))))
        to_sum = act
    else:
        act = _softplus(xb)
        to_sum = act * coef
    r, c = _tri_masks(CHUNK)
    lower = jnp.where(c <= r, 1.0, 0.0)
    upper = jnp.where(c >= r, 1.0, 0.0)
    cum_f = _dot_exact_lhs(lower, to_sum)
    cum_r = _dot_exact_lhs(upper, to_sum)
    act_ref[...] = act
    cum_ref[...] = jnp.where(rev > 0.5, cum_r, cum_f)


def gate_prep(raw, params, mode):
    t = raw.shape[0]
    return pl.pallas_call(
        functools.partial(_gate_kernel, mode=mode),
        grid=(t // CHUNK,),
        in_specs=[pl.BlockSpec((CHUNK, LANES), lambda i: (i, 0)),
                  pl.BlockSpec((8, LANES), lambda i: (0, 0))],
        out_specs=[pl.BlockSpec((CHUNK, LANES), lambda i: (i, 0))] * 2,
        out_shape=[jax.ShapeDtypeStruct((t, LANES), F32)] * 2,
        compiler_params=_cparams("arbitrary"),
        name="gate_prep_" + mode,
    )(raw, params)


def _conv_silu(x, cw_ref, cb_ref, period):
    n = x.shape[0]
    row = lax.broadcasted_iota(jnp.int32, x.shape, 0) % period
    prev = jnp.where(row == 0, 0.0, pltpu.roll(x, 1, 0))
    nxt = jnp.where(row == period - 1, 0.0, pltpu.roll(x, n - 1, 0))
    y = cb_ref[...] + prev * cw_ref[0:1, :] + x * cw_ref[1:2, :] + nxt * cw_ref[2:3, :]
    return _silu(y)


def _pick_col(blk, idx):
    lane = lax.broadcasted_iota(jnp.int32, blk.shape, 1)
    return jnp.sum(jnp.where(lane == idx, blk, 0.0), axis=1, keepdims=True)


def _dir_masks(rev):
    r, c = _tri_masks(CHUNK)
    if rev:
        return c >= r, c > r
    return c <= r, c < r


def _tri_inverse(lmat, rev):
    return _tri_inverse_many([lmat], [rev])[0]


def _tri_inverse_many(lmats, revs):
    r, c = _tri_masks(CHUNK)
    eye = jnp.where(r == c, 1.0, 0.0)

    def off_mask(s, rev):
        same = (r // (2 * s)) == (c // (2 * s))
        r_hi = (r // s) % 2
        c_hi = (c // s) % 2
        return same & ((r_hi == 0) & (c_hi == 1) if rev else (r_hi == 1) & (c_hi == 0))

    masks = {rev: off_mask(1, rev) for rev in set(revs)}
    ts = [eye - jnp.where(masks[rev], lm, 0.0) for lm, rev in zip(lmats, revs)]
    s = 2
    while s < CHUNK:
        masks = {rev: off_mask(s, rev) for rev in set(revs)}
        ps = [_bdot(t, jnp.where(masks[rev], lm, 0.0)) for t, lm, rev in zip(ts, lmats, revs)]
        ts = [t - _bdot(p, t) for p, t in zip(ps, ts)]
        s *= 2
    return ts


def _delta_kernel(*refs, n_chunks, period, has_state, hb):
    (q_ref, k_ref, v_ref, z_ref, cwq, cwk, cwv, cbq, cbk, cbv, act_ref, cum_ref, cumt_ref, norm_ref) = refs[:14]
    rest = refs[14:]
    if has_state:
        s0_ref, *_aliased_out, o_ref, acc_ref = rest
        sout_ref = None
    else:
        o_ref, sout_ref, acc_ref = rest
    h0 = pl.program_id(1) * hb

    q = _conv_silu(q_ref[...].astype(F32), cwq, cbq, period)
    k = _conv_silu(k_ref[...].astype(F32), cwk, cbk, period)
    v = _conv_silu(v_ref[...].astype(F32), cwv, cbv, period)

    triples = [(hh, d, ci) for hh in range(hb) for d in (0, 1) for ci in range(n_chunks)]
    pre = {}
    for hh in range(hb):
        hs = slice(hh * DK_A, (hh + 1) * DK_A)
        qh, kh = q[:, hs], k[:, hs]
        qh = qh * lax.rsqrt(jnp.sum(qh * qh, axis=1, keepdims=True) + RMS_EPS) * (DK_A ** -0.5)
        kh = kh * lax.rsqrt(jnp.sum(kh * kh, axis=1, keepdims=True) + RMS_EPS)
        for d in (0, 1):
            m_incl, m_strict = _dir_masks(d == 1)
            for ci in range(n_chunks):
                sl = slice(ci * CHUNK, (ci + 1) * CHUNK)
                qc, kc, vc = qh[sl], kh[sl], v[sl, hs]
                beta = _pick_col(act_ref[sl, :], d * H_A + h0 + hh)
                gcol = _pick_col(cum_ref[sl, :], 2 * H_A + d * H_A + h0 + hh)
                grow = cumt_ref[pl.ds(2 * H_A + d * H_A + h0 + hh, 1), sl]
                decay = jnp.exp(jnp.where(m_incl, gcol - grow, NEG))
                kb = kc * beta
                pre[hh, d, ci] = dict(
                    qc=qc, kc=kc, kb=kb, vb=vc * beta, gcol=gcol, decay=decay,
                    lmat=_bdot_nt(kb, kc) * jnp.where(m_strict, decay, 0.0),
                    attn=_bdot_nt(qc, kc) * decay)
    tinvs = _tri_inverse_many([pre[t]["lmat"] for t in triples], [t[1] == 1 for t in triples])
    for t, tinv in zip(triples, tinvs):
        p = pre[t]
        p["u"] = _bdot(tinv, p["vb"])
        if has_state:
            p["w"] = _bdot(tinv, p["kb"] * jnp.exp(p["gcol"]))

    for hh in range(hb):
        hs = slice(hh * DK_A, (hh + 1) * DK_A)
        for d in (0, 1):
            rev = d == 1
            state = s0_ref[d, hh] if has_state else None
            order = range(n_chunks - 1, -1, -1) if rev else range(n_chunks)
            for ci in order:
                sl = slice(ci * CHUNK, (ci + 1) * CHUNK)
                p = pre[hh, d, ci]
                u, gcol = p["u"], p["gcol"]
                if state is not None:
                    u = u - _bdot(p["w"], state)
                o = _bdot(p["attn"], u)
                if state is not None:
                    o = o + _bdot(p["qc"] * jnp.exp(gcol), state)
                glast = gcol[0:1, :] if rev else gcol[CHUNK - 1:CHUNK, :]
                upd = _bdot_tn(p["kc"] * jnp.exp(glast - gcol), u)
                state = upd if state is None else state * jnp.exp(glast) + upd
                if rev:
                    acc_ref[sl, hs] = acc_ref[sl, hs] + o
                else:
                    acc_ref[sl, hs] = o
            if sout_ref is not None:
                sout_ref[d, hh] = state

    z = z_ref[...].astype(F32)
    for hh in range(hb):
        hs = slice(hh * DK_A, (hh + 1) * DK_A)
        o = acc_ref[:, hs]
        o = o * lax.rsqrt(jnp.mean(o * o, axis=1, keepdims=True) + RMS_EPS) * norm_ref[...]
        o_ref[:, hs] = (o * _silu(z[:, hs])).astype(o_ref.dtype)


def _alias_out(in_specs, args, out_buf):
    if out_buf is None:
        return {}
    in_specs.append(pl.BlockSpec(memory_space=pl.ANY))
    args.append(out_buf)
    return {len(args) - 1: 0}


def delta_mixer(proj, conv_w, conv_b, act, cum, cumt, norm, state, *, row0, n_seq, seq_len, period, hb,
                out_buf=None):
    assert row0 % seq_len == 0 and H_A % hb == 0
    n_chunks = seq_len // CHUNK
    rb0 = row0 // seq_len
    has_state = state is not None
    w = hb * DK_A
    nq = H_A // hb
    col = lambda off: (lambda s, h: (rb0 + s, off + h))
    cw = lambda off: (lambda s, h: (0, off + h))
    in_specs = [pl.BlockSpec((seq_len, w), col(0)), pl.BlockSpec((seq_len, w), col(nq)),
                pl.BlockSpec((seq_len, w), col(2 * nq)), pl.BlockSpec((seq_len, w), col(3 * nq)),
                pl.BlockSpec((3, w), cw(0)), pl.BlockSpec((3, w), cw(nq)), pl.BlockSpec((3, w), cw(2 * nq)),
                pl.BlockSpec((1, w), cw(0)), pl.BlockSpec((1, w), cw(nq)), pl.BlockSpec((1, w), cw(2 * nq)),
                pl.BlockSpec((seq_len, LANES), lambda s, h: (rb0 + s, 0)),
                pl.BlockSpec((seq_len, LANES), lambda s, h: (rb0 + s, 0)),
                pl.BlockSpec((LANES, seq_len), lambda s, h: (0, rb0 + s)),
                pl.BlockSpec((1, DV_A), lambda s, h: (0, 0))]
    args = [proj, proj, proj, proj, conv_w, conv_w, conv_w, conv_b, conv_b, conv_b, act, cum, cumt, norm]
    o_spec = pl.BlockSpec((seq_len, w), lambda s, h: (rb0 + s, h))
    o_shape = jax.ShapeDtypeStruct((proj.shape[0], H_A * DV_A), BF16)
    st_spec = pl.BlockSpec((None, 2, hb, DK_A, DV_A), lambda s, h: (s, 0, h, 0, 0))
    if has_state:
        in_specs.append(st_spec)
        args.append(state)
        out_specs, out_shape = o_spec, o_shape
    else:
        out_specs = [o_spec, st_spec]
        out_shape = [o_shape, jax.ShapeDtypeStruct((n_seq, 2, H_A, DK_A, DV_A), F32)]
    aliases = _alias_out(in_specs, args, out_buf)
    return pl.pallas_call(
        functools.partial(_delta_kernel, n_chunks=n_chunks, period=period, has_state=has_state, hb=hb),
        grid=(n_seq, H_A // hb),
        in_specs=in_specs, out_specs=out_specs, out_shape=out_shape, input_output_aliases=aliases,
        scratch_shapes=[pltpu.VMEM((seq_len, w), F32)],
        compiler_params=_cparams("arbitrary", "arbitrary"),
        name="delta_lat" if has_state else "delta_ctx",
    )(*args)


def even_gate_params(a_log, dt_bias, b_i, b_f):
    zeros_a = jnp.zeros((2 * H_A,), F32)
    coef = jnp.concatenate([zeros_a, -jnp.exp(a_log.astype(F32)).reshape(-1), jnp.zeros((4 * H_B,), F32)])
    bias = jnp.concatenate([zeros_a, dt_bias.reshape(-1), b_i.reshape(-1), b_f.reshape(-1)]).astype(F32)
    rev = jnp.concatenate([jnp.repeat(jnp.arange(2, dtype=F32), H_A)] * 2 + [jnp.repeat(jnp.arange(2, dtype=F32), H_B)] * 2)
    p = jnp.stack([coef, bias, rev])
    return jnp.pad(p, ((0, 5), (0, LANES - EV_GATES)))


def _mlstm_kernel(*refs, n_chunks, has_state):
    (q_ref, k_ref, v_ref, og_ref, act_ref, actt_ref, cum_ref, cumt_ref, norm_ref) = refs[:9]
    rest = refs[9:]
    if has_state:
        c0_ref, n0_ref, m0_ref, *_aliased_out, o_ref, acc_ref = rest
    else:
        o_ref, cout_ref, nm_ref, acc_ref = rest
    h = pl.program_id(1)
    i_col0, f_col0 = 4 * H_A, 4 * H_A + 2 * H_B

    for d in (0, 1):
        rev = d == 1
        m_incl, _ = _dir_masks(rev)
        if has_state:
            cm, nv, m = c0_ref[d], n0_ref[d], m0_ref[d]
        else:
            cm, nv, m = None, None, jnp.zeros((1, 1), F32)
        order = range(n_chunks - 1, -1, -1) if rev else range(n_chunks)
        for ci in order:
            sl = slice(ci * CHUNK, (ci + 1) * CHUNK)
            qc = q_ref[sl, :].astype(F32) * (DK_B ** -0.5)
            kc = k_ref[sl, :].astype(F32)
            vc = v_ref[sl, :].astype(F32)
            li_col = _pick_col(act_ref[sl, :], i_col0 + d * H_B + h)
            li_row = actt_ref[pl.ds(i_col0 + d * H_B + h, 1), sl]
            b_col = _pick_col(cum_ref[sl, :], f_col0 + d * H_B + h)
            b_row = cumt_ref[pl.ds(f_col0 + d * H_B + h, 1), sl]
            dlog = jnp.where(m_incl, b_col - b_row + li_row, NEG)
            inter = b_col + m
            m_q = jnp.maximum(inter, jnp.max(dlog, axis=1, keepdims=True))
            s = _bdot_nt(qc, kc) * jnp.exp(dlog - m_q)
            num = _bdot(s, vc)
            den = jnp.sum(s, axis=1, keepdims=True)
            if cm is not None:
                w_inter = jnp.exp(inter - m_q)
                num = num + w_inter * _bdot(qc, cm)
                den = den + w_inter * jnp.sum(qc * nv, axis=1, keepdims=True)
            hout = num / jnp.maximum(jnp.abs(den), jnp.exp(-m_q))
            b_last = b_col[0:1, :] if rev else b_col[CHUNK - 1:CHUNK, :]
            wlog = b_last - b_col + li_col
            m_new = jnp.maximum(b_last + m, jnp.max(wlog, axis=0, keepdims=True))
            kw = kc * jnp.exp(wlog - m_new)
            c_upd = _bdot_tn(kw, vc)
            n_upd = jnp.sum(kw, axis=0, keepdims=True)
            if cm is not None:
                sc = jnp.exp(b_last + m - m_new)
                cm, nv = sc * cm + c_upd, sc * nv + n_upd
            else:
                cm, nv = c_upd, n_upd
            m = m_new
            if rev:
                acc_ref[sl, :] = acc_ref[sl, :] + hout
            else:
                acc_ref[sl, :] = hout
        if not has_state:
            cout_ref[d] = cm
            nm_ref[d, 0:1, :] = nv
            nm_ref[d, 1:2, :] = jnp.broadcast_to(m, (1, DK_B))
            nm_ref[d, 2:8, :] = jnp.zeros((6, DK_B), F32)

    o = acc_ref[...]
    o = o * lax.rsqrt(jnp.mean(o * o, axis=1, keepdims=True) + RMS_EPS) * norm_ref[...]
    o_ref[...] = (o * jax.nn.sigmoid(og_ref[...].astype(F32))).astype(o_ref.dtype)


def mlstm_mixer(proj, act, actt, cum, cumt, norm, state, *, row0, n_seq, seq_len, out_buf=None):
    assert row0 % seq_len == 0
    n_chunks = seq_len // CHUNK
    rb0 = row0 // seq_len
    has_state = state is not None
    q0 = (CONV_A + H_A * DV_A) // LANES
    k0 = q0 + H_B
    v0 = (CONV_A + H_A * DV_A + 2 * H_B * DK_B) // DV_B
    o0 = v0 + H_B
    col = lambda off: (lambda s, h: (rb0 + s, off + h))
    in_specs = [pl.BlockSpec((seq_len, DK_B), col(q0)), pl.BlockSpec((seq_len, DK_B), col(k0)),
                pl.BlockSpec((seq_len, DV_B), col(v0)), pl.BlockSpec((seq_len, DV_B), col(o0)),
                pl.BlockSpec((seq_len, LANES), lambda s, h: (rb0 + s, 0)),
                pl.BlockSpec((LANES, seq_len), lambda s, h: (0, rb0 + s)),
                pl.BlockSpec((seq_len, LANES), lambda s, h: (rb0 + s, 0)),
                pl.BlockSpec((LANES, seq_len), lambda s, h: (0, rb0 + s)),
                pl.BlockSpec((1, DV_B), lambda s, h: (0, 0))]
    args = [proj, proj, proj, proj, act, actt, cum, cumt, norm]
    o_spec = pl.BlockSpec((seq_len, DV_B), lambda s, h: (rb0 + s, h))
    o_shape = jax.ShapeDtypeStruct((proj.shape[0], H_B * DV_B), BF16)
    st_idx = lambda s, h: (s, 0, h, 0, 0)
    if has_state:
        c0, n0, m0 = state
        in_specs += [pl.BlockSpec((None, 2, None, DK_B, DV_B), st_idx),
                     pl.BlockSpec((None, 2, None, 1, DK_B), st_idx),
                     pl.BlockSpec((None, 2, None, 1, 1), st_idx)]
        args += [c0, n0.reshape(n_seq, 2, H_B, 1, DK_B), m0.reshape(n_seq, 2, H_B, 1, 1)]
        out_specs, out_shape = o_spec, o_shape
    else:
        out_specs = [o_spec, pl.BlockSpec((None, 2, None, DK_B, DV_B), st_idx),
                     pl.BlockSpec((None, 2, None, 8, DK_B), st_idx)]
        out_shape = [o_shape, jax.ShapeDtypeStruct((n_seq, 2, H_B, DK_B, DV_B), F32),
                     jax.ShapeDtypeStruct((n_seq, 2, H_B, 8, DK_B), F32)]
    aliases = _alias_out(in_specs, args, out_buf)
    return pl.pallas_call(
        functools.partial(_mlstm_kernel, n_chunks=n_chunks, has_state=has_state),
        grid=(n_seq, H_B),
        in_specs=in_specs, out_specs=out_specs, out_shape=out_shape, input_output_aliases=aliases,
        scratch_shapes=[pltpu.VMEM((seq_len, DV_B), F32)],
        compiler_params=_cparams("arbitrary", "arbitrary"),
        name="mlstm_lat" if has_state else "mlstm_ctx",
    )(*args)


def odd_gate_params(a_log, dt_bias):
    coef = -jnp.exp(a_log.astype(F32)).reshape(-1)
    bias = dt_bias.astype(F32).reshape(-1)
    rev = jnp.repeat(jnp.arange(2, dtype=F32), H_C)
    return jnp.pad(jnp.stack([coef, bias, rev]), ((0, 5), (0, 0)))


def _ssd_kernel(*refs, n_chunks, period, has_state):
    (z_ref, x_ref, b_ref, c_ref, cwx, cwb, cwc, cbx, cbb, cbc, dt_ref, cum_ref, cumt_ref, dskip_ref, norm_ref) = refs[:15]
    rest = refs[15:]
    if has_state:
        s0_ref, *_aliased_out, o_ref, acc_ref = rest
        sout_ref = None
    else:
        o_ref, sout_ref, acc_ref = rest
    g = pl.program_id(1)

    x = _conv_silu(x_ref[...].astype(F32), cwx, cbx, period)
    bm = _conv_silu(b_ref[...].astype(F32), cwb, cbb, period)
    cm = _conv_silu(c_ref[...].astype(F32), cwc, cbc, period)

    er = lax.broadcasted_iota(jnp.int32, (LANES, GW_C), 0)
    ec = lax.broadcasted_iota(jnp.int32, (LANES, GW_C), 1)
    tr = lax.broadcasted_iota(jnp.int32, (GW_C, LANES), 0)
    tc = lax.broadcasted_iota(jnp.int32, (GW_C, LANES), 1)
    lane_in_tile = lax.broadcasted_iota(jnp.int32, (CHUNK, LANES), 1)

    for d in (0, 1):
        rev = d == 1
        m_incl, _ = _dir_masks(rev)
        col0 = d * H_C + g * HG_C
        expand = jnp.where(er == col0 + ec // P_C, 1.0, 0.0)
        expand_t = tc == col0 + tr // P_C
        state = s0_ref[d].reshape(GW_C, N_C) if has_state else None
        order = range(n_chunks - 1, -1, -1) if rev else range(n_chunks)
        for ci in order:
            sl = slice(ci * CHUNK, (ci + 1) * CHUNK)
            xc, bc, cc = x[sl], bm[sl], cm[sl]
            cum_blk = cum_ref[sl, :]
            cum_last = cum_blk[0:1, :] if rev else cum_blk[CHUNK - 1:CHUNK, :]
            xdt = xc * _dot_exact_rhs(dt_ref[sl, :], expand, passes=2)
            scores = _bdot_nt(cc, bc)
            for hp in range(HG_C // 2):
                ps = slice(hp * LANES, (hp + 1) * LANES)
                xpair = xdt[:, ps]
                ypair = None
                for sub in (0, 1):
                    hh = 2 * hp + sub
                    cb = _pick_col(cum_blk, col0 + hh)
                    crow = cumt_ref[pl.ds(col0 + hh, 1), sl]
                    seg = jnp.exp(jnp.where(m_incl, cb - crow, NEG))
                    mine = (lane_in_tile < P_C) if sub == 0 else (lane_in_tile >= P_C)
                    y = _bdot(scores * seg, jnp.where(mine, xpair, 0.0))
                    ypair = y if ypair is None else ypair + y
                if rev:
                    acc_ref[sl, ps] = acc_ref[sl, ps] + ypair
                else:
                    acc_ref[sl, ps] = ypair
            if state is not None:
                y_in = _bdot_nt(cc, state) * _dot_exact_rhs(jnp.exp(cum_blk), expand, passes=2)
                acc_ref[sl, :] = acc_ref[sl, :] + y_in
            dend = _dot_exact_rhs(jnp.exp(jnp.minimum(cum_last - cum_blk, 0.0)), expand, passes=2)
            upd = _bdot_tn(xdt * dend, bc)
            if state is not None:
                tot = jnp.sum(jnp.where(expand_t, jnp.broadcast_to(cum_last, (GW_C, LANES)), 0.0), axis=1, keepdims=True)
                state = state * jnp.exp(tot) + upd
            else:
                state = upd
        if sout_ref is not None:
            sout_ref[d] = state.reshape(HG_C, P_C, N_C)

    y = acc_ref[...] + dskip_ref[...] * x
    y = y * _silu(z_ref[...].astype(F32))
    y = y * lax.rsqrt(jnp.mean(y * y, axis=1, keepdims=True) + RMS_EPS) * norm_ref[...]
    o_ref[...] = y.astype(o_ref.dtype)


def ssd_mixer(proj, conv_w, conv_b, dt, cum, cumt, dskip, norm, state, *, row0, n_seq, seq_len, period, out_buf=None):
    assert row0 % seq_len == 0
    n_chunks = seq_len // CHUNK
    rb0 = row0 // seq_len
    has_state = state is not None
    xb0 = D_INNER // GW_C
    bb0 = 2 * D_INNER // N_C
    cb0 = bb0 + G_C
    wb0 = D_INNER // N_C
    wc0 = wb0 + G_C
    col = lambda off: (lambda s, g: (rb0 + s, off + g))
    cw = lambda off: (lambda s, g: (0, off + g))
    in_specs = [pl.BlockSpec((seq_len, GW_C), col(0)), pl.BlockSpec((seq_len, GW_C), col(xb0)),
                pl.BlockSpec((seq_len, N_C), col(bb0)), pl.BlockSpec((seq_len, N_C), col(cb0)),
                pl.BlockSpec((3, GW_C), cw(0)), pl.BlockSpec((3, N_C), cw(wb0)), pl.BlockSpec((3, N_C), cw(wc0)),
                pl.BlockSpec((1, GW_C), cw(0)), pl.BlockSpec((1, N_C), cw(wb0)), pl.BlockSpec((1, N_C), cw(wc0)),
                pl.BlockSpec((seq_len, LANES), lambda s, g: (rb0 + s, 0)),
                pl.BlockSpec((seq_len, LANES), lambda s, g: (rb0 + s, 0)),
                pl.BlockSpec((LANES, seq_len), lambda s, g: (0, rb0 + s)),
                pl.BlockSpec((1, GW_C), cw(0)), pl.BlockSpec((1, GW_C), cw(0))]
    args = [proj, proj, proj, proj, conv_w, conv_w, conv_w, conv_b, conv_b, conv_b, dt, cum, cumt, dskip, norm]
    o_spec = pl.BlockSpec((seq_len, GW_C), lambda s, g: (rb0 + s, g))
    o_shape = jax.ShapeDtypeStruct((proj.shape[0], D_INNER), BF16)
    st_spec = pl.BlockSpec((None, 2, HG_C, P_C, N_C), lambda s, g: (s, 0, g, 0, 0))
    if has_state:
        in_specs.append(st_spec)
        args.append(state)
        out_specs, out_shape = o_spec, o_shape
    else:
        out_specs = [o_spec, st_spec]
        out_shape = [o_shape, jax.ShapeDtypeStruct((n_seq, 2, H_C, P_C, N_C), F32)]
    aliases = _alias_out(in_specs, args, out_buf)
    return pl.pallas_call(
        functools.partial(_ssd_kernel, n_chunks=n_chunks, period=period, has_state=has_state),
        grid=(n_seq, G_C),
        in_specs=in_specs, out_specs=out_specs, out_shape=out_shape, input_output_aliases=aliases,
        scratch_shapes=[pltpu.VMEM((seq_len, GW_C), F32)],
        compiler_params=_cparams("arbitrary", "arbitrary"),
        name="ssd_lat" if has_state else "ssd_ctx",
    )(*args)


def _dot3(a, b):
    a_hi = a.astype(BF16)
    a_lo = (a - a_hi.astype(F32)).astype(BF16)
    b_hi = b.astype(BF16)
    b_lo = (b - b_hi.astype(F32)).astype(BF16)
    d = lambda p, q: jnp.dot(p, q, preferred_element_type=F32)
    return d(a_hi, b_hi) + (d(a_hi, b_lo) + d(a_lo, b_hi))


def _dot3_nt(a, b):
    a_hi = a.astype(BF16)
    a_lo = (a - a_hi.astype(F32)).astype(BF16)
    b_hi = b.astype(BF16)
    b_lo = (b - b_hi.astype(F32)).astype(BF16)
    d = lambda p, q: lax.dot_general(p, q, (((1,), (1,)), ((), ())), preferred_element_type=F32)
    return d(a_hi, b_hi) + (d(a_hi, b_lo) + d(a_lo, b_hi))


def _resid_ln_kernel(*refs, n_y, lhs_widths, n_gathered, gate, sh, sc, want_h, want_logits):
    x_ref = refs[0]
    y_refs = refs[1:1 + n_y]
    rest = list(refs[1 + n_y:])
    lhs_refs = [rest.pop(0) for _ in lhs_widths]
    pw_ref = rest.pop(0) if lhs_widths else None
    m_ref, mn_ref, g_ref, b_ref = (rest.pop(0) for _ in range(4))
    if n_gathered:
        gath_ref, gw_ref = rest.pop(0), rest.pop(0)
    rw_ref = rest.pop(0) if want_logits else None
    xo_ref = rest.pop(0)
    y = None
    for r in y_refs:
        y = r[...].astype(F32) if y is None else y + r[...].astype(F32)
    off = 0
    for lhs_ref, width in zip(lhs_refs, lhs_widths):
        part = jnp.dot(lhs_ref[...], pw_ref[off:off + width, :], preferred_element_type=F32)
        y = part if y is None else y + part
        off += width
    for kk in range(n_gathered):
        y = y + gw_ref[:, kk:kk + 1] * gath_ref[kk].astype(F32)
    v = ALPHA * x_ref[...] + m_ref[gate:gate + 1, :] * y
    mu = jnp.mean(v, axis=1, keepdims=True)
    vc = v - mu
    var = jnp.mean(vc * vc, axis=1, keepdims=True)
    xn = vc * lax.rsqrt(var + LN_EPS) * g_ref[...] + b_ref[...]
    xo_ref[...] = xn
    if want_h:
        hm = xn * (1.0 + mn_ref[sc:sc + 1, :]) + mn_ref[sh:sh + 1, :]
        rest.pop(0)[...] = hm.astype(BF16)
        if want_logits:
            rest.pop(0)[...] = _dot3_nt(rw_ref[...], hm)


def resid_ln(x, ys, mod, mod_next, ln_g, ln_b, router_w, *, gate, sh, sc, want_h, t_ctx, lat_len, gathered=None,
             h_rows=None, proj=None, tm=256):
    t, d = x.shape
    want_logits = router_w is not None
    grp = lambda i: (_group_of_block(i, tm, t_ctx, lat_len), 0, 0)
    row = pl.BlockSpec((tm, d), lambda i: (i, 0))
    vec = pl.BlockSpec((1, d), lambda i: (0, 0))
    in_specs = [row] * (1 + len(ys))
    args = [x, *ys]
    lhs_widths = ()
    if proj is not None:
        lhs_list, pw = proj
        lhs_widths = tuple(a.shape[1] for a in lhs_list)
        in_specs += [pl.BlockSpec((tm, wd), lambda i: (i, 0)) for wd in lhs_widths]
        in_specs.append(pl.BlockSpec(pw.shape, lambda i: (0, 0), pipeline_mode=pl.Buffered(1)))
        args += [*lhs_list, pw]
    in_specs += [pl.BlockSpec((None, 6, d), grp), pl.BlockSpec((None, 6, d), grp), vec, vec]
    args += [mod, mod_next, ln_g.reshape(1, d), ln_b.reshape(1, d)]
    n_gathered = 0
    if gathered is not None:
        n_gathered = gathered[0].shape[0]
        in_specs += [pl.BlockSpec((n_gathered, tm, d), lambda i: (0, i, 0)),
                     pl.BlockSpec((tm, n_gathered), lambda i: (i, 0))]
        args += list(gathered)
    out_specs, out_shape = [row], [jax.ShapeDtypeStruct((t, d), F32)]
    if want_logits:
        n_e = router_w.shape[1]
        in_specs.append(pl.BlockSpec((n_e, d), lambda i: (0, 0)))
        args.append(router_w.T)
    if want_h:
        out_specs.append(row)
        out_shape.append(jax.ShapeDtypeStruct((h_rows or t, d), BF16))
    if want_logits:
        out_specs.append(pl.BlockSpec((n_e, tm), lambda i: (0, i)))
        out_shape.append(jax.ShapeDtypeStruct((n_e, t), F32))
    return pl.pallas_call(
        functools.partial(_resid_ln_kernel, n_y=len(ys), lhs_widths=lhs_widths, n_gathered=n_gathered, gate=gate, sh=sh,
                          sc=sc, want_h=want_h, want_logits=want_logits),
        grid=(t // tm,),
        in_specs=in_specs, out_specs=out_specs, out_shape=out_shape,
        compiler_params=_cparams("arbitrary"),
        name="resid_ln",
    )(*args)


def _ffn_kernel(be_ref, nx_ref, nu_ref, x_ref, wg_hbm, wu_hbm, wd_hbm, *rest, layer):
    *_aliased_out, o_ref, g_f32, u_f32, d_f32, g_bf, u_bf, d_bf, sem = rest
    b = pl.program_id(0)

    def copies(e):
        return (pltpu.make_async_copy(wg_hbm.at[layer, e], g_f32, sem.at[0]),
                pltpu.make_async_copy(wu_hbm.at[layer, e], u_f32, sem.at[1]),
                pltpu.make_async_copy(wd_hbm.at[layer, e], d_f32, sem.at[2]))

    @pl.when(b < nu_ref[0])
    def _():
        e = be_ref[b]

        @pl.when(b == 0)
        def _():
            for cp in copies(e):
                cp.start()

        @pl.when((b == 0) | (e != be_ref[jnp.maximum(b - 1, 0)]))
        def _():
            for cp in copies(e):
                cp.wait()
            g_bf[...] = g_f32[...].astype(BF16)
            u_bf[...] = u_f32[...].astype(BF16)
            d_bf[...] = d_f32[...].astype(BF16)
            nxt = nx_ref[b]

            @pl.when(nxt >= 0)
            def _():
                for cp in copies(nxt):
                    cp.start()

        x = x_ref[...]
        hg = jnp.dot(x, g_bf[...], preferred_element_type=F32)
        hu = jnp.dot(x, u_bf[...], preferred_element_type=F32)
        a = (_silu(hg) * hu).astype(BF16)
        o_ref[...] = jnp.dot(a, d_bf[...], preferred_element_type=F32).astype(o_ref.dtype)


def expert_ffn(xs, blk_e, next_e, n_used, w_gate, w_up, w_down, layer, *, tm, out_rows=None, out_block0=0,
               out_buf=None, out_dtype=BF16, name="expert_ffn"):
    d = xs.shape[1]
    de = w_gate.shape[3]
    n_blk = blk_e.shape[0]
    in_specs = [pl.BlockSpec((tm, d), lambda b, be, nx, nu: (b, 0))] + [pl.BlockSpec(memory_space=pl.ANY)] * 3
    args = [blk_e, next_e, n_used, xs, w_gate, w_up, w_down]
    aliases = _alias_out(in_specs, args, out_buf)
    grid_spec = pltpu.PrefetchScalarGridSpec(
        num_scalar_prefetch=3,
        grid=(n_blk,),
        in_specs=in_specs,
        out_specs=pl.BlockSpec((tm, d), lambda b, be, nx, nu: (out_block0 + b, 0)),
        scratch_shapes=[pltpu.VMEM((d, de), F32), pltpu.VMEM((d, de), F32), pltpu.VMEM((de, d), F32),
                        pltpu.VMEM((d, de), BF16), pltpu.VMEM((d, de), BF16), pltpu.VMEM((de, d), BF16),
                        pltpu.SemaphoreType.DMA((3,))],
    )
    return pl.pallas_call(
        functools.partial(_ffn_kernel, layer=layer),
        grid_spec=grid_spec,
        out_shape=jax.ShapeDtypeStruct((out_rows or n_blk * tm, d), out_dtype),
        input_output_aliases=aliases,
        compiler_params=_cparams("arbitrary"),
        name=name,
    )(*args)


ROUTE_TM = 512
GROUP_SIZE = N_EXP // N_GROUPS


def _first_argmax(v, idx, axis, sentinel):
    mx = jnp.max(v, axis=axis, keepdims=True)
    return mx, jnp.min(jnp.where(v == mx, idx, sentinel), axis=axis, keepdims=True)


def _route_kernel(lt_ref, bias_ref, idx_ref, w_ref, rank_ref, cnt_ref, carry_ref):
    i = pl.program_id(0)
    tm = lt_ref.shape[1]

    @pl.when(i == 0)
    def _():
        carry_ref[...] = jnp.zeros_like(carry_ref)

    scores = jax.nn.sigmoid(lt_ref[...])
    biased = scores + bias_ref[...]
    b3 = biased.reshape(N_GROUPS, GROUP_SIZE, tm)
    mem = lax.broadcasted_iota(jnp.int32, b3.shape, 1).astype(F32)
    m1, first = _first_argmax(b3, mem, 1, float(GROUP_SIZE))
    m2 = jnp.max(jnp.where(mem == first, -jnp.inf, b3), axis=1, keepdims=True)
    gs = (m1 + m2).reshape(N_GROUPS, tm)
    gi = lax.broadcasted_iota(jnp.int32, gs.shape, 0).astype(F32)
    gsel = jnp.zeros(gs.shape, F32)
    cur = gs
    for _ in range(TOPK_GROUPS):
        _, pick = _first_argmax(cur, gi, 0, float(N_GROUPS))
        hit = gi == pick
        gsel = jnp.where(hit, 1.0, gsel)
        cur = jnp.where(hit, -jnp.inf, cur)
    masked = jnp.where(gsel.reshape(N_GROUPS, 1, tm) > 0.5, b3, -jnp.inf).reshape(N_EXP, tm)

    ei = lax.broadcasted_iota(jnp.int32, masked.shape, 0).astype(F32)
    picks, sel_scores = [], []
    chosen = jnp.zeros(masked.shape, F32)
    cur = masked
    for _ in range(TOP_K):
        _, pick = _first_argmax(cur, ei, 0, float(N_EXP))
        hit = ei == pick
        picks.append(pick)
        sel_scores.append(jnp.sum(jnp.where(hit, scores, 0.0), axis=0, keepdims=True))
        chosen = jnp.where(hit, 1.0, chosen)
        cur = jnp.where(hit, -jnp.inf, cur)

    r, c = _tri_masks(tm)
    before = jnp.where(r < c, 1.0, 0.0).astype(BF16)
    rank = jnp.dot(chosen.astype(BF16), before, preferred_element_type=F32) + carry_ref[...]
    carry_ref[...] = carry_ref[...] + jnp.sum(chosen, axis=1, keepdims=True)
    cnt_ref[...] = carry_ref[...]

    total = sel_scores[0]
    for s in sel_scores[1:]:
        total = total + s
    for k in range(TOP_K):
        idx_ref[k:k + 1, :] = picks[k].astype(jnp.int32)
        w_ref[k:k + 1, :] = sel_scores[k] / total * ROUTED_SCALE
        rank_ref[k:k + 1, :] = jnp.sum(jnp.where(ei == picks[k], rank, 0.0), axis=0, keepdims=True).astype(jnp.int32)


def route(logits_t, router_bias):
    n_e, t = logits_t.shape
    tm = ROUTE_TM
    kt = pl.BlockSpec((TOP_K, tm), lambda i: (0, i))
    return pl.pallas_call(
        _route_kernel,
        grid=(t // tm,),
        in_specs=[pl.BlockSpec((n_e, tm), lambda i: (0, i)), pl.BlockSpec((n_e, 1), lambda i: (0, 0))],
        out_specs=[kt, kt, kt, pl.BlockSpec((n_e, 1), lambda i: (0, 0))],
        out_shape=[jax.ShapeDtypeStruct((TOP_K, t), jnp.int32), jax.ShapeDtypeStruct((TOP_K, t), F32),
                   jax.ShapeDtypeStruct((TOP_K, t), jnp.int32), jax.ShapeDtypeStruct((n_e, 1), F32)],
        scratch_shapes=[pltpu.VMEM((n_e, 1), F32)],
        compiler_params=_cparams("arbitrary"),
        name="route",
    )(logits_t, router_bias.reshape(n_e, 1))


def moe(h, logits_t, router_bias, e_gate, e_up, e_down, s_gate, s_up, s_down, layer):
    d = h.shape[1]
    t = logits_t.shape[1]
    tm = 256
    top_e, wts, rank, counts = route(logits_t, router_bias)
    counts = counts.reshape(-1).astype(jnp.int32)
    n_assign = t * TOP_K
    padded = (counts + tm - 1) // tm * tm
    pad_end = jnp.cumsum(padded)
    pad_start = pad_end - padded
    n_blk = n_assign // tm + N_EXP
    blk_first = jnp.arange(n_blk, dtype=jnp.int32) * tm
    blk_e = jnp.minimum(jnp.sum((pad_end[None, :] <= blk_first[:, None]).astype(jnp.int32), axis=1), N_EXP - 1)
    n_used = (pad_end[-1] // tm).astype(jnp.int32).reshape(1)
    expert_ids = jnp.arange(N_EXP, dtype=jnp.int32)
    dest = jnp.sum(jnp.where(top_e[..., None] == expert_ids, pad_start, 0), axis=-1) + rank
    tok = jnp.broadcast_to(jnp.arange(t, dtype=jnp.int32), (TOP_K, t))
    filler = jnp.arange(n_blk * tm, dtype=jnp.int32) % t
    slot_tok = filler.at[dest.reshape(-1)].set(tok.reshape(-1), unique_indices=True)
    blk_ids = jnp.arange(n_blk, dtype=jnp.int32)
    run_end = jnp.sum(jnp.where(blk_e[:, None] >= expert_ids, padded, 0), axis=1) // tm
    run_end_e = jnp.sum(jnp.where(run_end[:, None] == blk_ids, blk_e, 0), axis=1)
    cb = n_blk // MOE_CHUNKS
    ys = None
    for ci in range(MOE_CHUNKS):
        b0, b1 = ci * cb, (ci + 1) * cb
        xs = h.at[slot_tok[b0 * tm:b1 * tm]].get(mode="promise_in_bounds")
        last_blk = jnp.minimum(n_used[0], b1)
        next_e = jnp.where(run_end[b0:b1] < last_blk, run_end_e[b0:b1], -1).astype(jnp.int32)
        ys = expert_ffn(xs, blk_e[b0:b1], next_e, jnp.clip(n_used - b0, 0, cb), e_gate, e_up, e_down, layer, tm=tm,
                        out_rows=n_blk * tm, out_block0=b0, out_buf=ys, name="routed_ffn")
    routed_rows = ys.at[dest.reshape(-1)].get(mode="promise_in_bounds").reshape(TOP_K, t, d)
    tm_sh = 512
    n_sh = t // tm_sh
    shared = expert_ffn(h, jnp.zeros((n_sh,), jnp.int32), jnp.full((n_sh,), -1, jnp.int32),
                        jnp.full((1,), n_sh, jnp.int32), s_gate[:, None], s_up[:, None], s_down[:, None], layer,
                        tm=tm_sh, name="shared_ffn")
    return shared, routed_rows, wts.T


def kernel(x_prompt, x_sample, state_dn, state_ml_C, state_ml_n, state_ml_m, state_ssd, c, c_ctx,
           mod_w, mod_b, ln1_g, ln1_b, ln2_g, ln2_b, router_w, router_bias, exp_gate, exp_up, exp_down,
           sh_gate, sh_up, sh_down, ev_w_in, ev_conv_w, ev_conv_b, dn_A_log, dn_dt_bias, ml_b_i, ml_b_f,
           dn_norm, ml_norm, ev_w_out, od_w_in, od_conv_w, od_conv_b, ssd_A_log, ssd_dt_bias, ssd_D,
           ssd_norm, od_w_out):
    bp, sl, d = x_prompt.shape
    bl, ll, _ = x_sample.shape
    depth = mod_w.shape[0]
    t_ctx = bp * sl
    x = jnp.concatenate([x_prompt.reshape(t_ctx, d), x_sample.reshape(bl * ll, d)], axis=0)
    cvec = jnp.concatenate([c_ctx[None], c, jnp.zeros((8 - 1 - bl, d), F32)], axis=0)
    mods = compute_mods(cvec, mod_w, mod_b)[:, :1 + bl].reshape(depth, 1 + bl, 6, d)
    geo = dict(t_ctx=t_ctx, lat_len=ll)
    ctx = dict(row0=0, n_seq=bp, seq_len=sl)
    lat = dict(row0=t_ctx, n_seq=bl, seq_len=ll)

    h = modulate(x, mods[0], 0, 1, **geo)
    new_dn, new_c, new_n, new_m, new_ssd = [], [], [], [], []
    for l in range(depth):
        j = l // 2
        if l % 2 == 0:
            w_in = ev_w_in[j]
            proj = matmul(h, w_in, tm=1024, tn=1024, n_out=EV_MAIN, out_dtype=BF16, name="ev_in_proj")
            graw = matmul(h, w_in, tm=1024, tn=LANES, n_out=LANES, col_block_off=EV_MAIN // LANES,
                          valid_cols=EV_GATES, name="ev_gate_proj")
            act, cum = gate_prep(graw, even_gate_params(dn_A_log[j], dn_dt_bias[j], ml_b_i[j], ml_b_f[j]), "even")
            actt, cumt = act.T, cum.T
            cw, cb, dnn, mln = ev_conv_w[j], ev_conv_b[j].reshape(1, -1), dn_norm[j].reshape(1, -1), ml_norm[j].reshape(1, -1)
            oa, s_dn = delta_mixer(proj, cw, cb, act, cum, cumt, dnn, None, period=sl, hb=4, **ctx)
            oa = delta_mixer(proj, cw, cb, act, cum, cumt, dnn, state_dn[:, j], period=GRID_W, hb=1, out_buf=oa, **lat)
            ob, s_c, s_nm = mlstm_mixer(proj, act, actt, cum, cumt, mln, None, **ctx)
            ob = mlstm_mixer(proj, act, actt, cum, cumt, mln,
                             (state_ml_C[:, j], state_ml_n[:, j], state_ml_m[:, j]), out_buf=ob, **lat)
            out_proj = ([oa, ob], ev_w_out[j].astype(BF16))
            new_dn.append(s_dn)
            new_c.append(s_c)
            new_n.append(s_nm[:, :, :, 0, :])
            new_m.append(s_nm[:, :, :, 1, 0])
        else:
            w_in = od_w_in[j]
            proj = matmul(h, w_in, tm=1024, tn=1024, n_out=OD_MAIN, out_dtype=BF16, name="od_in_proj")
            draw = matmul(h, w_in, tm=1024, tn=LANES, n_out=LANES, col_block_off=OD_MAIN // LANES, name="od_dt_proj")
            dt, cum = gate_prep(draw, odd_gate_params(ssd_A_log[j], ssd_dt_bias[j]), "odd")
            cumt = cum.T
            cw, cb = od_conv_w[j], od_conv_b[j].reshape(1, -1)
            dsk, nrm = jnp.repeat(ssd_D[j], P_C).reshape(1, -1), ssd_norm[j].reshape(1, -1)
            oc, s_ssd = ssd_mixer(proj, cw, cb, dt, cum, cumt, dsk, nrm, None, period=sl, **ctx)
            oc = ssd_mixer(proj, cw, cb, dt, cum, cumt, dsk, nrm, state_ssd[:, j], period=GRID_W, out_buf=oc, **lat)
            out_proj = ([oc], od_w_out[j].astype(BF16))
            new_ssd.append(s_ssd)
        x, h2, logits_t = resid_ln(x, [], mods[l], mods[l], ln1_g[l], ln1_b[l], router_w[l], proj=out_proj, tm=512,
                                   gate=2, sh=3, sc=4, want_h=True, h_rows=GATHER_SRC_ROWS, **geo)
        shared, routed_rows, wts = moe(h2, logits_t, router_bias[l], exp_gate, exp_up, exp_down,
                                       sh_gate, sh_up, sh_down, l)
        last = l == depth - 1
        res = resid_ln(x, [shared], mods[l], mods[min(l + 1, depth - 1)], ln2_g[l], ln2_b[l], None,
                       gate=5, sh=0, sc=1, want_h=not last, gathered=(routed_rows, wts), **geo)
        x = res[0]
        if not last:
            h = res[1]
    y_prompt = x[:t_ctx].reshape(bp, sl, d)
    y_sample = x[t_ctx:].reshape(bl, ll, d)
    return (y_prompt, y_sample, jnp.stack(new_dn, axis=1), jnp.stack(new_c, axis=1), jnp.stack(new_n, axis=1),
            jnp.stack(new_m, axis=1), jnp.stack(new_ssd, axis=1))
```

```python
import functools

import jax
import jax.numpy as jnp
from jax import lax
from jax.experimental import pallas as pl
from jax.experimental.pallas import tpu as pltpu

F32 = jnp.float32
BF16 = jnp.bfloat16

D_MODEL = 2048
DEPTH = 2
GRID_W = 64
ALPHA = (2 * DEPTH) ** 0.25
LN_EPS = 1e-5
RMS_EPS = 1e-6

H_A, DK_A, DV_A = 8, 128, 128
H_B, DK_B, DV_B = 4, 128, 256
CONV_A = 2 * H_A * DK_A + H_A * DV_A
EV_MAIN = CONV_A + H_A * DV_A + 2 * H_B * DK_B + 2 * H_B * DV_B
EV_GATES = 4 * H_A + 4 * H_B

D_INNER = 2 * D_MODEL
P_C, N_C, G_C = 64, 128, 8
H_C = D_INNER // P_C
HG_C = H_C // G_C
GW_C = D_INNER // G_C
OD_MAIN = 2 * D_INNER + 2 * G_C * N_C

N_EXP, TOP_K, N_GROUPS, TOPK_GROUPS = 64, 8, 8, 4
D_EXP = 512
ROUTED_SCALE = 2.5

MOE_CHUNKS = 3
GATHER_SRC_ROWS = 16384
CHUNK = 256
LANES = 128
VMEM_LIMIT = 56 * 1024 * 1024
NEG = -1e30
LOG2E = 1.4426950408889634


def _cparams(*sem):
    return pltpu.CompilerParams(dimension_semantics=sem, vmem_limit_bytes=VMEM_LIMIT)


def _bdot(a, b):
    return jnp.dot(a.astype(BF16), b.astype(BF16), preferred_element_type=F32)


def _bdot_nt(a, b):
    return lax.dot_general(a.astype(BF16), b.astype(BF16), (((1,), (1,)), ((), ())), preferred_element_type=F32)


def _bdot_tn(a, b):
    return lax.dot_general(a.astype(BF16), b.astype(BF16), (((0,), (0,)), ((), ())), preferred_element_type=F32)


def _split3(a):
    hi = a.astype(BF16)
    r = a - hi.astype(F32)
    mid = r.astype(BF16)
    lo = (r - mid.astype(F32)).astype(BF16)
    return hi, mid, lo


def _dot_exact_rhs(a, b_exact, passes=3):
    hi, mid, lo = _split3(a)
    bb = b_exact.astype(BF16)
    d = lambda p: jnp.dot(p, bb, preferred_element_type=F32)
    return d(hi) + d(mid) + d(lo) if passes == 3 else d(hi) + d(mid)


def _dot_exact_lhs(a_exact, b):
    hi, mid, lo = _split3(b)
    aa = a_exact.astype(BF16)
    d = lambda p: jnp.dot(aa, p, preferred_element_type=F32)
    return d(hi) + d(mid) + d(lo)


def _silu(x):
    return x * jax.nn.sigmoid(x)


def _softplus(x):
    return jnp.maximum(x, 0.0) + jnp.log(1.0 + jnp.exp(-jnp.abs(x)))


def _group_of_block(i, tm, t_ctx, lat_len):
    return jnp.maximum(i * tm - t_ctx, -1) // lat_len + 1


def _mod_kernel(c_ref, w_ref, b_ref, o_ref):
    c = c_ref[...]
    o_ref[...] = _bdot(_silu(c), w_ref[...]) + b_ref[...]


def compute_mods(cvec, mod_w, mod_b):
    depth, d, n = mod_w.shape
    tn = 512
    return pl.pallas_call(
        _mod_kernel,
        grid=(depth, n // tn),
        in_specs=[pl.BlockSpec((8, d), lambda l, j: (0, 0)),
                  pl.BlockSpec((None, d, tn), lambda l, j: (l, 0, j)),
                  pl.BlockSpec((None, 1, tn), lambda l, j: (l, 0, j))],
        out_specs=pl.BlockSpec((None, 8, tn), lambda l, j: (l, 0, j)),
        out_shape=jax.ShapeDtypeStruct((depth, 8, n), F32),
        compiler_params=_cparams("arbitrary", "arbitrary"),
        name="mod_vectors",
    )(cvec, mod_w, mod_b.reshape(depth, 1, n))


def _modulate_kernel(x_ref, m_ref, *rest, sh, sc):
    o_ref = rest[-1]
    o_ref[...] = (x_ref[...] * (1.0 + m_ref[sc:sc + 1, :]) + m_ref[sh:sh + 1, :]).astype(o_ref.dtype)


def modulate(x_part, mod, sh, sc, *, row0, t_total, t_ctx, lat_len, out_buf=None):
    tp, d = x_part.shape
    tm = 512
    blk0 = row0 // tm
    in_specs = [pl.BlockSpec((tm, d), lambda i: (i, 0)),
                pl.BlockSpec((None, 6, d), lambda i: (_group_of_block(blk0 + i, tm, t_ctx, lat_len), 0, 0))]
    args = [x_part, mod]
    aliases = _alias_out(in_specs, args, out_buf)
    return pl.pallas_call(
        functools.partial(_modulate_kernel, sh=sh, sc=sc),
        grid=(tp // tm,),
        in_specs=in_specs,
        out_specs=pl.BlockSpec((tm, d), lambda i: (blk0 + i, 0)),
        out_shape=jax.ShapeDtypeStruct((t_total, d), BF16),
        input_output_aliases=aliases,
        compiler_params=_cparams("arbitrary"),
        name="modulate",
    )(*args)


def _matmul_kernel(x_ref, w_ref, o_ref, wbf_ref, *, valid_cols):
    @pl.when(pl.program_id(1) == 0)
    def _():
        wbf_ref[...] = w_ref[...].astype(BF16)

    y = jnp.dot(x_ref[...], wbf_ref[...], preferred_element_type=F32)
    if valid_cols is not None:
        col = lax.broadcasted_iota(jnp.int32, y.shape, 1)
        y = jnp.where(col < valid_cols, y, 0.0)
    o_ref[...] = y.astype(o_ref.dtype)


def matmul(x, w, *, tm, tn, n_out, col_block_off=0, valid_cols=None, out_dtype=F32, name="matmul"):
    m, k = x.shape
    return pl.pallas_call(
        functools.partial(_matmul_kernel, valid_cols=valid_cols),
        grid=(n_out // tn, m // tm),
        in_specs=[pl.BlockSpec((tm, k), lambda j, i: (i, 0)),
                  pl.BlockSpec((k, tn), lambda j, i: (0, j + col_block_off))],
        out_specs=pl.BlockSpec((tm, tn), lambda j, i: (i, j)),
        out_shape=jax.ShapeDtypeStruct((m, n_out), out_dtype),
        scratch_shapes=[pltpu.VMEM((k, tn), BF16)],
        compiler_params=_cparams("arbitrary", "arbitrary"),
        name=name,
    )(x, w)


def _tri_masks(n):
    r = lax.broadcasted_iota(jnp.int32, (n, n), 0)
    c = lax.broadcasted_iota(jnp.int32, (n, n), 1)
    return r, c


def _gate_kernel(raw_ref, p_ref, act_ref, cum_ref, *, mode):
    x = raw_ref[...]
    coef, bias, rev = p_ref[0:1, :], p_ref[1:2, :], p_ref[2:3, :]
    col = lax.broadcasted_iota(jnp.int32, x.shape, 1)
    xb = x + bias
    if mode == "even":
        act = jnp.where(col < 2 * H_A, jax.nn.sigmoid(xb),
                        jnp.where(col < 4 * H_A, coef * _softplus(xb),
                                  jnp.where(col < 4 * H_A + 2 * H_B, xb,
                                            jnp.minimum(xb, 0.0) - jnp.log(1.0 + jnp.exp(-jnp.abs(xb))))))
        to_sum = act
    else:
        act = _softplus(xb)
        to_sum = act * (coef * LOG2E)
    r, c = _tri_masks(CHUNK)
    lower = jnp.where(c <= r, 1.0, 0.0)
    upper = jnp.where(c >= r, 1.0, 0.0)
    cum_f = _dot_exact_lhs(lower, to_sum)
    cum_r = _dot_exact_lhs(upper, to_sum)
    act_ref[...] = act
    cum_ref[...] = jnp.where(rev > 0.5, cum_r, cum_f)


def gate_prep(raw, params, mode):
    t = raw.shape[0]
    return pl.pallas_call(
        functools.partial(_gate_kernel, mode=mode),
        grid=(t // CHUNK,),
        in_specs=[pl.BlockSpec((CHUNK, LANES), lambda i: (i, 0)),
                  pl.BlockSpec((8, LANES), lambda i: (0, 0))],
        out_specs=[pl.BlockSpec((CHUNK, LANES), lambda i: (i, 0))] * 2,
        out_shape=[jax.ShapeDtypeStruct((t, LANES), F32)] * 2,
        compiler_params=_cparams("arbitrary"),
        name="gate_prep_" + mode,
    )(raw, params)


def _conv_silu(x, cw_ref, cb_ref, period):
    n = x.shape[0]
    row = lax.broadcasted_iota(jnp.int32, x.shape, 0) % period
    prev = jnp.where(row == 0, 0.0, pltpu.roll(x, 1, 0))
    nxt = jnp.where(row == period - 1, 0.0, pltpu.roll(x, n - 1, 0))
    y = cb_ref[...] + prev * cw_ref[0:1, :] + x * cw_ref[1:2, :] + nxt * cw_ref[2:3, :]
    return _silu(y)


def _pick_col(blk, idx):
    lane = lax.broadcasted_iota(jnp.int32, blk.shape, 1)
    return jnp.sum(jnp.where(lane == idx, blk, 0.0), axis=1, keepdims=True)


def _dir_masks(rev):
    r, c = _tri_masks(CHUNK)
    if rev:
        return c >= r, c > r
    return c <= r, c < r


def _tri_inverse(lmat, rev):
    return _tri_inverse_many([lmat], [rev])[0]


def _tri_inverse_many(lmats, revs):
    r, c = _tri_masks(CHUNK)
    eye = jnp.where(r == c, 1.0, 0.0)

    def off_mask(s, rev):
        same = (r // (2 * s)) == (c // (2 * s))
        r_hi = (r // s) % 2
        c_hi = (c // s) % 2
        return same & ((r_hi == 0) & (c_hi == 1) if rev else (r_hi == 1) & (c_hi == 0))

    masks = {rev: off_mask(1, rev) for rev in set(revs)}
    ts = [eye - jnp.where(masks[rev], lm, 0.0) for lm, rev in zip(lmats, revs)]
    s = 2
    while s < CHUNK:
        masks = {rev: off_mask(s, rev) for rev in set(revs)}
        ps = [_bdot(t, jnp.where(masks[rev], lm, 0.0)) for t, lm, rev in zip(ts, lmats, revs)]
        ts = [t - _bdot(p, t) for p, t in zip(ps, ts)]
        s *= 2
    return ts


def _delta_kernel(*refs, n_chunks, period, has_state, hb):
    (q_ref, k_ref, v_ref, z_ref, cwq, cwk, cwv, cbq, cbk, cbv, act_ref, cum_ref, cumt_ref, norm_ref) = refs[:14]
    rest = refs[14:]
    if has_state:
        s0_ref, *_aliased_out, o_ref, acc_ref = rest
        sout_ref = None
    else:
        o_ref, sout_ref, acc_ref = rest
    h0 = pl.program_id(1) * hb

    q = _conv_silu(q_ref[...].astype(F32), cwq, cbq, period)
    k = _conv_silu(k_ref[...].astype(F32), cwk, cbk, period)
    v = _conv_silu(v_ref[...].astype(F32), cwv, cbv, period)

    triples = [(hh, d, ci) for hh in range(hb) for d in (0, 1) for ci in range(n_chunks)]
    pre = {}
    for hh in range(hb):
        hs = slice(hh * DK_A, (hh + 1) * DK_A)
        qh, kh = q[:, hs], k[:, hs]
        qh = qh * lax.rsqrt(jnp.sum(qh * qh, axis=1, keepdims=True) + RMS_EPS) * (DK_A ** -0.5)
        kh = kh * lax.rsqrt(jnp.sum(kh * kh, axis=1, keepdims=True) + RMS_EPS)
        for d in (0, 1):
            m_incl, m_strict = _dir_masks(d == 1)
            for ci in range(n_chunks):
                sl = slice(ci * CHUNK, (ci + 1) * CHUNK)
                qc, kc, vc = qh[sl], kh[sl], v[sl, hs]
                beta = _pick_col(act_ref[sl, :], d * H_A + h0 + hh)
                gcol = _pick_col(cum_ref[sl, :], 2 * H_A + d * H_A + h0 + hh)
                grow = cumt_ref[pl.ds(2 * H_A + d * H_A + h0 + hh, 1), sl]
                decay = jnp.exp(jnp.where(m_incl, gcol - grow, NEG))
                kb = kc * beta
                pre[hh, d, ci] = dict(
                    qc=qc, kc=kc, kb=kb, vb=vc * beta, gcol=gcol, decay=decay,
                    lmat=_bdot_nt(kb, kc) * jnp.where(m_strict, decay, 0.0),
                    attn=_bdot_nt(qc, kc) * decay)
    tinvs = _tri_inverse_many([pre[t]["lmat"] for t in triples], [t[1] == 1 for t in triples])
    for t, tinv in zip(triples, tinvs):
        p = pre[t]
        p["u"] = _bdot(tinv, p["vb"])
        if has_state:
            p["w"] = _bdot(tinv, p["kb"] * jnp.exp(p["gcol"]))

    for hh in range(hb):
        hs = slice(hh * DK_A, (hh + 1) * DK_A)
        for d in (0, 1):
            rev = d == 1
            state = s0_ref[d, hh] if has_state else None
            order = range(n_chunks - 1, -1, -1) if rev else range(n_chunks)
            for ci in order:
                sl = slice(ci * CHUNK, (ci + 1) * CHUNK)
                p = pre[hh, d, ci]
                u, gcol = p["u"], p["gcol"]
                if state is not None:
                    u = u - _bdot(p["w"], state)
                o = _bdot(p["attn"], u)
                if state is not None:
                    o = o + _bdot(p["qc"] * jnp.exp(gcol), state)
                glast = gcol[0:1, :] if rev else gcol[CHUNK - 1:CHUNK, :]
                upd = _bdot_tn(p["kc"] * jnp.exp(glast - gcol), u)
                state = upd if state is None else state * jnp.exp(glast) + upd
                if rev:
                    acc_ref[sl, hs] = acc_ref[sl, hs] + o
                else:
                    acc_ref[sl, hs] = o
            if sout_ref is not None:
                sout_ref[d, hh] = state

    z = z_ref[...].astype(F32)
    for hh in range(hb):
        hs = slice(hh * DK_A, (hh + 1) * DK_A)
        o = acc_ref[:, hs]
        o = o * lax.rsqrt(jnp.mean(o * o, axis=1, keepdims=True) + RMS_EPS) * norm_ref[...]
        o_ref[:, hs] = (o * _silu(z[:, hs])).astype(o_ref.dtype)


def _alias_out(in_specs, args, out_buf):
    if out_buf is None:
        return {}
    in_specs.append(pl.BlockSpec(memory_space=pl.ANY))
    args.append(out_buf)
    return {len(args) - 1: 0}


def delta_mixer(proj, conv_w, conv_b, act, cum, cumt, norm, state, *, row0, n_seq, seq_len, period, hb,
                out_buf=None):
    assert row0 % seq_len == 0 and H_A % hb == 0
    n_chunks = seq_len // CHUNK
    rb0 = row0 // seq_len
    has_state = state is not None
    w = hb * DK_A
    nq = H_A // hb
    col = lambda off: (lambda s, h: (rb0 + s, off + h))
    cw = lambda off: (lambda s, h: (0, off + h))
    in_specs = [pl.BlockSpec((seq_len, w), col(0)), pl.BlockSpec((seq_len, w), col(nq)),
                pl.BlockSpec((seq_len, w), col(2 * nq)), pl.BlockSpec((seq_len, w), col(3 * nq)),
                pl.BlockSpec((3, w), cw(0)), pl.BlockSpec((3, w), cw(nq)), pl.BlockSpec((3, w), cw(2 * nq)),
                pl.BlockSpec((1, w), cw(0)), pl.BlockSpec((1, w), cw(nq)), pl.BlockSpec((1, w), cw(2 * nq)),
                pl.BlockSpec((seq_len, LANES), lambda s, h: (rb0 + s, 0)),
                pl.BlockSpec((seq_len, LANES), lambda s, h: (rb0 + s, 0)),
                pl.BlockSpec((LANES, seq_len), lambda s, h: (0, rb0 + s)),
                pl.BlockSpec((1, DV_A), lambda s, h: (0, 0))]
    args = [proj, proj, proj, proj, conv_w, conv_w, conv_w, conv_b, conv_b, conv_b, act, cum, cumt, norm]
    o_spec = pl.BlockSpec((seq_len, w), lambda s, h: (rb0 + s, h))
    o_shape = jax.ShapeDtypeStruct((proj.shape[0], H_A * DV_A), BF16)
    st_spec = pl.BlockSpec((None, 2, hb, DK_A, DV_A), lambda s, h: (s, 0, h, 0, 0))
    if has_state:
        in_specs.append(st_spec)
        args.append(state)
        out_specs, out_shape = o_spec, o_shape
    else:
        out_specs = [o_spec, st_spec]
        out_shape = [o_shape, jax.ShapeDtypeStruct((n_seq, 2, H_A, DK_A, DV_A), F32)]
    aliases = _alias_out(in_specs, args, out_buf)
    return pl.pallas_call(
        functools.partial(_delta_kernel, n_chunks=n_chunks, period=period, has_state=has_state, hb=hb),
        grid=(n_seq, H_A // hb),
        in_specs=in_specs, out_specs=out_specs, out_shape=out_shape, input_output_aliases=aliases,
        scratch_shapes=[pltpu.VMEM((seq_len, w), F32)],
        compiler_params=_cparams("arbitrary", "arbitrary"),
        name="delta_lat" if has_state else "delta_ctx",
    )(*args)


def even_gate_params(a_log, dt_bias, b_i, b_f):
    zeros_a = jnp.zeros((2 * H_A,), F32)
    coef = jnp.concatenate([zeros_a, -jnp.exp(a_log.astype(F32)).reshape(-1), jnp.zeros((4 * H_B,), F32)])
    bias = jnp.concatenate([zeros_a, dt_bias.reshape(-1), b_i.reshape(-1), b_f.reshape(-1)]).astype(F32)
    rev = jnp.concatenate([jnp.repeat(jnp.arange(2, dtype=F32), H_A)] * 2 + [jnp.repeat(jnp.arange(2, dtype=F32), H_B)] * 2)
    p = jnp.stack([coef, bias, rev])
    return jnp.pad(p, ((0, 5), (0, LANES - EV_GATES)))


def _mlstm_kernel(*refs, n_chunks, has_state):
    (q_ref, k_ref, v_ref, og_ref, act_ref, actt_ref, cum_ref, cumt_ref, norm_ref) = refs[:9]
    rest = refs[9:]
    if has_state:
        c0_ref, n0_ref, m0_ref, *_aliased_out, o_ref, acc_ref = rest
    else:
        o_ref, cout_ref, nm_ref, acc_ref = rest
    h = pl.program_id(1)
    i_col0, f_col0 = 4 * H_A, 4 * H_A + 2 * H_B

    for d in (0, 1):
        rev = d == 1
        m_incl, _ = _dir_masks(rev)
        if has_state:
            cm, nv, m = c0_ref[d], n0_ref[d], m0_ref[d]
        else:
            cm, nv, m = None, None, jnp.zeros((1, 1), F32)
        order = range(n_chunks - 1, -1, -1) if rev else range(n_chunks)
        for ci in order:
            sl = slice(ci * CHUNK, (ci + 1) * CHUNK)
            qc = q_ref[sl, :].astype(F32) * (DK_B ** -0.5)
            kc = k_ref[sl, :].astype(F32)
            vc = v_ref[sl, :].astype(F32)
            li_col = _pick_col(act_ref[sl, :], i_col0 + d * H_B + h)
            li_row = actt_ref[pl.ds(i_col0 + d * H_B + h, 1), sl]
            b_col = _pick_col(cum_ref[sl, :], f_col0 + d * H_B + h)
            b_row = cumt_ref[pl.ds(f_col0 + d * H_B + h, 1), sl]
            dlog = jnp.where(m_incl, b_col - b_row + li_row, NEG)
            inter = b_col + m
            m_q = jnp.maximum(inter, jnp.max(dlog, axis=1, keepdims=True))
            s = _bdot_nt(qc, kc) * jnp.exp(dlog - m_q)
            num = _bdot(s, vc)
            den = jnp.sum(s, axis=1, keepdims=True)
            if cm is not None:
                w_inter = jnp.exp(inter - m_q)
                num = num + w_inter * _bdot(qc, cm)
                den = den + w_inter * jnp.sum(qc * nv, axis=1, keepdims=True)
            hout = num / jnp.maximum(jnp.abs(den), jnp.exp(-m_q))
            b_last = b_col[0:1, :] if rev else b_col[CHUNK - 1:CHUNK, :]
            wlog = b_last - b_col + li_col
            m_new = jnp.maximum(b_last + m, jnp.max(wlog, axis=0, keepdims=True))
            kw = kc * jnp.exp(wlog - m_new)
            c_upd = _bdot_tn(kw, vc)
            n_upd = jnp.sum(kw, axis=0, keepdims=True)
            if cm is not None:
                sc = jnp.exp(b_last + m - m_new)
                cm, nv = sc * cm + c_upd, sc * nv + n_upd
            else:
                cm, nv = c_upd, n_upd
            m = m_new
            if rev:
                acc_ref[sl, :] = acc_ref[sl, :] + hout
            else:
                acc_ref[sl, :] = hout
        if not has_state:
            cout_ref[d] = cm
            nm_ref[d, 0:1, :] = nv
            nm_ref[d, 1:2, :] = jnp.broadcast_to(m, (1, DK_B))
            nm_ref[d, 2:8, :] = jnp.zeros((6, DK_B), F32)

    o = acc_ref[...]
    o = o * lax.rsqrt(jnp.mean(o * o, axis=1, keepdims=True) + RMS_EPS) * norm_ref[...]
    o_ref[...] = (o * jax.nn.sigmoid(og_ref[...].astype(F32))).astype(o_ref.dtype)


def mlstm_mixer(proj, act, actt, cum, cumt, norm, state, *, row0, n_seq, seq_len, out_buf=None):
    assert row0 % seq_len == 0
    n_chunks = seq_len // CHUNK
    rb0 = row0 // seq_len
    has_state = state is not None
    q0 = (CONV_A + H_A * DV_A) // LANES
    k0 = q0 + H_B
    v0 = (CONV_A + H_A * DV_A + 2 * H_B * DK_B) // DV_B
    o0 = v0 + H_B
    col = lambda off: (lambda s, h: (rb0 + s, off + h))
    in_specs = [pl.BlockSpec((seq_len, DK_B), col(q0)), pl.BlockSpec((seq_len, DK_B), col(k0)),
                pl.BlockSpec((seq_len, DV_B), col(v0)), pl.BlockSpec((seq_len, DV_B), col(o0)),
                pl.BlockSpec((seq_len, LANES), lambda s, h: (rb0 + s, 0)),
                pl.BlockSpec((LANES, seq_len), lambda s, h: (0, rb0 + s)),
                pl.BlockSpec((seq_len, LANES), lambda s, h: (rb0 + s, 0)),
                pl.BlockSpec((LANES, seq_len), lambda s, h: (0, rb0 + s)),
                pl.BlockSpec((1, DV_B), lambda s, h: (0, 0))]
    args = [proj, proj, proj, proj, act, actt, cum, cumt, norm]
    o_spec = pl.BlockSpec((seq_len, DV_B), lambda s, h: (rb0 + s, h))
    o_shape = jax.ShapeDtypeStruct((proj.shape[0], H_B * DV_B), BF16)
    st_idx = lambda s, h: (s, 0, h, 0, 0)
    if has_state:
        c0, n0, m0 = state
        in_specs += [pl.BlockSpec((None, 2, None, DK_B, DV_B), st_idx),
                     pl.BlockSpec((None, 2, None, 1, DK_B), st_idx),
                     pl.BlockSpec((None, 2, None, 1, 1), st_idx)]
        args += [c0, n0.reshape(n_seq, 2, H_B, 1, DK_B), m0.reshape(n_seq, 2, H_B, 1, 1)]
        out_specs, out_shape = o_spec, o_shape
    else:
        out_specs = [o_spec, pl.BlockSpec((None, 2, None, DK_B, DV_B), st_idx),
                     pl.BlockSpec((None, 2, None, 8, DK_B), st_idx)]
        out_shape = [o_shape, jax.ShapeDtypeStruct((n_seq, 2, H_B, DK_B, DV_B), F32),
                     jax.ShapeDtypeStruct((n_seq, 2, H_B, 8, DK_B), F32)]
    aliases = _alias_out(in_specs, args, out_buf)
    return pl.pallas_call(
        functools.partial(_mlstm_kernel, n_chunks=n_chunks, has_state=has_state),
        grid=(n_seq, H_B),
        in_specs=in_specs, out_specs=out_specs, out_shape=out_shape, input_output_aliases=aliases,
        scratch_shapes=[pltpu.VMEM((seq_len, DV_B), F32)],
        compiler_params=_cparams("arbitrary", "arbitrary"),
        name="mlstm_lat" if has_state else "mlstm_ctx",
    )(*args)


def odd_gate_params(a_log, dt_bias):
    coef = -jnp.exp(a_log.astype(F32)).reshape(-1)
    bias = dt_bias.astype(F32).reshape(-1)
    rev = jnp.repeat(jnp.arange(2, dtype=F32), H_C)
    return jnp.pad(jnp.stack([coef, bias, rev]), ((0, 5), (0, 0)))


def _ssd_kernel(*refs, n_chunks, period, has_state):
    (z_ref, x_ref, b_ref, c_ref, cwx, cwb, cwc, cbx, cbb, cbc, dt_ref, cum_ref, cumt_ref, dskip_ref, norm_ref) = refs[:15]
    rest = refs[15:]
    if has_state:
        s0_ref, *_aliased_out, o_ref, acc_ref = rest
        sout_ref = None
    else:
        o_ref, sout_ref, acc_ref = rest
    g = pl.program_id(1)

    x = _conv_silu(x_ref[...].astype(F32), cwx, cbx, period)
    bm = _conv_silu(b_ref[...].astype(F32), cwb, cbb, period)
    cm = _conv_silu(c_ref[...].astype(F32), cwc, cbc, period)

    er = lax.broadcasted_iota(jnp.int32, (LANES, GW_C), 0)
    ec = lax.broadcasted_iota(jnp.int32, (LANES, GW_C), 1)
    tr = lax.broadcasted_iota(jnp.int32, (GW_C, LANES), 0)
    tc = lax.broadcasted_iota(jnp.int32, (GW_C, LANES), 1)
    lane_in_tile = lax.broadcasted_iota(jnp.int32, (CHUNK, LANES), 1)

    for d in (0, 1):
        rev = d == 1
        m_incl, _ = _dir_masks(rev)
        lo, hi, full = slice(0, CHUNK // 2), slice(CHUNK // 2, CHUNK), slice(0, CHUNK)
        half_blocks = ((lo, full), (hi, hi)) if rev else ((lo, lo), (hi, full))
        col0 = d * H_C + g * HG_C
        expand = jnp.where(er == col0 + ec // P_C, 1.0, 0.0)
        expand_t = tc == col0 + tr // P_C
        state = s0_ref[d].reshape(GW_C, N_C) if has_state else None
        order = range(n_chunks - 1, -1, -1) if rev else range(n_chunks)
        for ci in order:
            sl = slice(ci * CHUNK, (ci + 1) * CHUNK)
            xc, bc, cc = x[sl], bm[sl], cm[sl]
            cum_blk = cum_ref[sl, :]
            cum_last = cum_blk[0:1, :] if rev else cum_blk[CHUNK - 1:CHUNK, :]
            xdt = xc * _dot_exact_rhs(dt_ref[sl, :], expand, passes=2)
            scores = _bdot_nt(cc, bc)
            for hp in range(HG_C // 2):
                ps = slice(hp * LANES, (hp + 1) * LANES)
                xpair = xdt[:, ps]
                ypair = None
                for sub in (0, 1):
                    hh = 2 * hp + sub
                    cb = _pick_col(cum_blk, col0 + hh)
                    crow = cumt_ref[pl.ds(col0 + hh, 1), sl]
                    mine = (lane_in_tile < P_C) if sub == 0 else (lane_in_tile >= P_C)
                    rhs = jnp.where(mine, xpair, 0.0)
                    parts = []
                    for rows, cols in half_blocks:
                        seg = jnp.exp2(jnp.where(m_incl[rows, cols], cb[rows] - crow[:, cols], NEG))
                        parts.append(_bdot(scores[rows, cols] * seg, rhs[cols]))
                    y = jnp.concatenate(parts, axis=0)
                    ypair = y if ypair is None else ypair + y
                if rev:
                    acc_ref[sl, ps] = acc_ref[sl, ps] + ypair
                else:
                    acc_ref[sl, ps] = ypair
            if state is not None:
                y_in = _bdot_nt(cc, state) * _dot_exact_rhs(jnp.exp2(cum_blk), expand, passes=2)
                acc_ref[sl, :] = acc_ref[sl, :] + y_in
            dend = _dot_exact_rhs(jnp.exp2(jnp.minimum(cum_last - cum_blk, 0.0)), expand, passes=2)
            upd = _bdot_tn(xdt * dend, bc)
            if state is not None:
                tot = jnp.sum(jnp.where(expand_t, jnp.broadcast_to(cum_last, (GW_C, LANES)), 0.0), axis=1, keepdims=True)
                state = state * jnp.exp2(tot) + upd
            else:
                state = upd
        if sout_ref is not None:
            sout_ref[d] = state.reshape(HG_C, P_C, N_C)

    y = acc_ref[...] + dskip_ref[...] * x
    y = y * _silu(z_ref[...].astype(F32))
    y = y * lax.rsqrt(jnp.mean(y * y, axis=1, keepdims=True) + RMS_EPS) * norm_ref[...]
    o_ref[...] = y.astype(o_ref.dtype)


def ssd_mixer(proj, conv_w, conv_b, dt, cum, cumt, dskip, norm, state, *, row0, n_seq, seq_len, period, out_buf=None):
    assert row0 % seq_len == 0
    n_chunks = seq_len // CHUNK
    rb0 = row0 // seq_len
    has_state = state is not None
    xb0 = D_INNER // GW_C
    bb0 = 2 * D_INNER // N_C
    cb0 = bb0 + G_C
    wb0 = D_INNER // N_C
    wc0 = wb0 + G_C
    col = lambda off: (lambda s, g: (rb0 + s, off + g))
    cw = lambda off: (lambda s, g: (0, off + g))
    in_specs = [pl.BlockSpec((seq_len, GW_C), col(0)), pl.BlockSpec((seq_len, GW_C), col(xb0)),
                pl.BlockSpec((seq_len, N_C), col(bb0)), pl.BlockSpec((seq_len, N_C), col(cb0)),
                pl.BlockSpec((3, GW_C), cw(0)), pl.BlockSpec((3, N_C), cw(wb0)), pl.BlockSpec((3, N_C), cw(wc0)),
                pl.BlockSpec((1, GW_C), cw(0)), pl.BlockSpec((1, N_C), cw(wb0)), pl.BlockSpec((1, N_C), cw(wc0)),
                pl.BlockSpec((seq_len, LANES), lambda s, g: (rb0 + s, 0)),
                pl.BlockSpec((seq_len, LANES), lambda s, g: (rb0 + s, 0)),
                pl.BlockSpec((LANES, seq_len), lambda s, g: (0, rb0 + s)),
                pl.BlockSpec((1, GW_C), cw(0)), pl.BlockSpec((1, GW_C), cw(0))]
    args = [proj, proj, proj, proj, conv_w, conv_w, conv_w, conv_b, conv_b, conv_b, dt, cum, cumt, dskip, norm]
    o_spec = pl.BlockSpec((seq_len, GW_C), lambda s, g: (rb0 + s, g))
    o_shape = jax.ShapeDtypeStruct((proj.shape[0], D_INNER), BF16)
    st_spec = pl.BlockSpec((None, 2, HG_C, P_C, N_C), lambda s, g: (s, 0, g, 0, 0))
    if has_state:
        in_specs.append(st_spec)
        args.append(state)
        out_specs, out_shape = o_spec, o_shape
    else:
        out_specs = [o_spec, st_spec]
        out_shape = [o_shape, jax.ShapeDtypeStruct((n_seq, 2, H_C, P_C, N_C), F32)]
    aliases = _alias_out(in_specs, args, out_buf)
    return pl.pallas_call(
        functools.partial(_ssd_kernel, n_chunks=n_chunks, period=period, has_state=has_state),
        grid=(n_seq, G_C),
        in_specs=in_specs, out_specs=out_specs, out_shape=out_shape, input_output_aliases=aliases,
        scratch_shapes=[pltpu.VMEM((seq_len, GW_C), F32)],
        compiler_params=_cparams("arbitrary", "arbitrary"),
        name="ssd_lat" if has_state else "ssd_ctx",
    )(*args)


def _dot3(a, b):
    a_hi = a.astype(BF16)
    a_lo = (a - a_hi.astype(F32)).astype(BF16)
    b_hi = b.astype(BF16)
    b_lo = (b - b_hi.astype(F32)).astype(BF16)
    d = lambda p, q: jnp.dot(p, q, preferred_element_type=F32)
    return d(a_hi, b_hi) + (d(a_hi, b_lo) + d(a_lo, b_hi))


def _dot3_nt(a, b):
    a_hi = a.astype(BF16)
    a_lo = (a - a_hi.astype(F32)).astype(BF16)
    b_hi = b.astype(BF16)
    b_lo = (b - b_hi.astype(F32)).astype(BF16)
    d = lambda p, q: lax.dot_general(p, q, (((1,), (1,)), ((), ())), preferred_element_type=F32)
    return d(a_hi, b_hi) + (d(a_hi, b_lo) + d(a_lo, b_hi))


def _resid_ln_kernel(*refs, n_y, lhs_widths, n_gathered, gate, sh, sc, want_h, want_logits, split_in, split_out,
                     n_ctx_blocks):
    in_ctx = pl.program_id(0) < n_ctx_blocks
    if split_in:
        x_in = jnp.where(in_ctx, refs[0][...], refs[1][...])
        refs = refs[1:]
    else:
        x_in = refs[0][...]
    y_refs = refs[1:1 + n_y]
    rest = list(refs[1 + n_y:])
    lhs_refs = [rest.pop(0) for _ in lhs_widths]
    pw_ref = rest.pop(0) if lhs_widths else None
    m_ref, mn_ref, g_ref, b_ref = (rest.pop(0) for _ in range(4))
    if n_gathered:
        gath_ref, gw_ref = rest.pop(0), rest.pop(0)
    rw_ref = rest.pop(0) if want_logits else None
    xo_refs = [rest.pop(0) for _ in range(2 if split_out else 1)]
    y = None
    for r in y_refs:
        y = r[...].astype(F32) if y is None else y + r[...].astype(F32)
    off = 0
    for lhs_ref, width in zip(lhs_refs, lhs_widths):
        part = jnp.dot(lhs_ref[...], pw_ref[off:off + width, :], preferred_element_type=F32)
        y = part if y is None else y + part
        off += width
    for kk in range(n_gathered):
        y = y + gw_ref[:, kk:kk + 1] * gath_ref[kk].astype(F32)
    v = ALPHA * x_in + m_ref[gate:gate + 1, :] * y
    mu = jnp.mean(v, axis=1, keepdims=True)
    vc = v - mu
    var = jnp.mean(vc * vc, axis=1, keepdims=True)
    xn = vc * lax.rsqrt(var + LN_EPS) * g_ref[...] + b_ref[...]
    if split_out:
        @pl.when(in_ctx)
        def _():
            xo_refs[0][...] = xn

        @pl.when(jnp.logical_not(in_ctx))
        def _():
            xo_refs[1][...] = xn
    else:
        xo_refs[0][...] = xn
    if want_h:
        hm = xn * (1.0 + mn_ref[sc:sc + 1, :]) + mn_ref[sh:sh + 1, :]
        rest.pop(0)[...] = hm.astype(BF16)
        if want_logits:
            rest.pop(0)[...] = _dot3_nt(rw_ref[...], hm)


def resid_ln(x, ys, mod, mod_next, ln_g, ln_b, router_w, *, gate, sh, sc, want_h, t_ctx, lat_len, gathered=None,
             h_rows=None, proj=None, tm=256, split_out=False):
    split_in = isinstance(x, tuple)
    d = x[0].shape[1] if split_in else x.shape[1]
    t = x[0].shape[0] + x[1].shape[0] if split_in else x.shape[0]
    n_ctx_blocks = t_ctx // tm
    ctx_blk = lambda i: (jnp.minimum(i, n_ctx_blocks - 1), 0)
    lat_blk = lambda i: (jnp.maximum(i - n_ctx_blocks, 0), 0)
    want_logits = router_w is not None
    grp = lambda i: (_group_of_block(i, tm, t_ctx, lat_len), 0, 0)
    row = pl.BlockSpec((tm, d), lambda i: (i, 0))
    vec = pl.BlockSpec((1, d), lambda i: (0, 0))
    if split_in:
        in_specs = [pl.BlockSpec((tm, d), ctx_blk), pl.BlockSpec((tm, d), lat_blk)] + [row] * len(ys)
        args = [*x, *ys]
    else:
        in_specs = [row] * (1 + len(ys))
        args = [x, *ys]
    lhs_widths = ()
    if proj is not None:
        lhs_list, pw = proj
        lhs_widths = tuple(a.shape[1] for a in lhs_list)
        in_specs += [pl.BlockSpec((tm, wd), lambda i: (i, 0)) for wd in lhs_widths]
        in_specs.append(pl.BlockSpec(pw.shape, lambda i: (0, 0), pipeline_mode=pl.Buffered(1)))
        args += [*lhs_list, pw]
    in_specs += [pl.BlockSpec((None, 6, d), grp), pl.BlockSpec((None, 6, d), grp), vec, vec]
    args += [mod, mod_next, ln_g.reshape(1, d), ln_b.reshape(1, d)]
    n_gathered = 0
    if gathered is not None:
        n_gathered = gathered[0].shape[0]
        in_specs += [pl.BlockSpec((n_gathered, tm, d), lambda i: (0, i, 0)),
                     pl.BlockSpec((tm, n_gathered), lambda i: (i, 0))]
        args += list(gathered)
    if split_out:
        out_specs = [pl.BlockSpec((tm, d), ctx_blk), pl.BlockSpec((tm, d), lat_blk)]
        out_shape = [jax.ShapeDtypeStruct((t_ctx, d), F32), jax.ShapeDtypeStruct((t - t_ctx, d), F32)]
    else:
        out_specs, out_shape = [row], [jax.ShapeDtypeStruct((t, d), F32)]
    if want_logits:
        n_e = router_w.shape[1]
        in_specs.append(pl.BlockSpec((n_e, d), lambda i: (0, 0)))
        args.append(router_w.T)
    if want_h:
        out_specs.append(row)
        out_shape.append(jax.ShapeDtypeStruct((h_rows or t, d), BF16))
    if want_logits:
        out_specs.append(pl.BlockSpec((n_e, tm), lambda i: (0, i)))
        out_shape.append(jax.ShapeDtypeStruct((n_e, t), F32))
    return pl.pallas_call(
        functools.partial(_resid_ln_kernel, n_y=len(ys), lhs_widths=lhs_widths, n_gathered=n_gathered, gate=gate, sh=sh,
                          sc=sc, want_h=want_h, want_logits=want_logits, split_in=split_in, split_out=split_out,
                          n_ctx_blocks=n_ctx_blocks),
        grid=(t // tm,),
        in_specs=in_specs, out_specs=out_specs, out_shape=out_shape,
        compiler_params=_cparams("arbitrary"),
        name="resid_ln",
    )(*args)


def _ffn_kernel(be_ref, nx_ref, nu_ref, x_ref, wg_hbm, wu_hbm, wd_hbm, *rest, layer):
    *_aliased_out, o_ref, g_f32, u_f32, d_f32, g_bf, u_bf, d_bf, sem = rest
    b = pl.program_id(0)

    def copies(e):
        return (pltpu.make_async_copy(wg_hbm.at[layer, e], g_f32, sem.at[0]),
                pltpu.make_async_copy(wu_hbm.at[layer, e], u_f32, sem.at[1]),
                pltpu.make_async_copy(wd_hbm.at[layer, e], d_f32, sem.at[2]))

    @pl.when(b < nu_ref[0])
    def _():
        e = be_ref[b]

        @pl.when(b == 0)
        def _():
            for cp in copies(e):
                cp.start()

        @pl.when((b == 0) | (e != be_ref[jnp.maximum(b - 1, 0)]))
        def _():
            for cp in copies(e):
                cp.wait()
            g_bf[...] = g_f32[...].astype(BF16)
            u_bf[...] = u_f32[...].astype(BF16)
            d_bf[...] = d_f32[...].astype(BF16)
            nxt = nx_ref[b]

            @pl.when(nxt >= 0)
            def _():
                for cp in copies(nxt):
                    cp.start()

        x = x_ref[...]
        hg = jnp.dot(x, g_bf[...], preferred_element_type=F32)
        hu = jnp.dot(x, u_bf[...], preferred_element_type=F32)
        a = (_silu(hg) * hu).astype(BF16)
        o_ref[...] = jnp.dot(a, d_bf[...], preferred_element_type=F32).astype(o_ref.dtype)


def expert_ffn(xs, blk_e, next_e, n_used, w_gate, w_up, w_down, layer, *, tm, out_rows=None, out_block0=0,
               out_buf=None, out_dtype=BF16, name="expert_ffn"):
    d = xs.shape[1]
    de = w_gate.shape[3]
    n_blk = blk_e.shape[0]
    in_specs = [pl.BlockSpec((tm, d), lambda b, be, nx, nu: (b, 0))] + [pl.BlockSpec(memory_space=pl.ANY)] * 3
    args = [blk_e, next_e, n_used, xs, w_gate, w_up, w_down]
    aliases = _alias_out(in_specs, args, out_buf)
    grid_spec = pltpu.PrefetchScalarGridSpec(
        num_scalar_prefetch=3,
        grid=(n_blk,),
        in_specs=in_specs,
        out_specs=pl.BlockSpec((tm, d), lambda b, be, nx, nu: (out_block0 + b, 0)),
        scratch_shapes=[pltpu.VMEM((d, de), F32), pltpu.VMEM((d, de), F32), pltpu.VMEM((de, d), F32),
                        pltpu.VMEM((d, de), BF16), pltpu.VMEM((d, de), BF16), pltpu.VMEM((de, d), BF16),
                        pltpu.SemaphoreType.DMA((3,))],
    )
    return pl.pallas_call(
        functools.partial(_ffn_kernel, layer=layer),
        grid_spec=grid_spec,
        out_shape=jax.ShapeDtypeStruct((out_rows or n_blk * tm, d), out_dtype),
        input_output_aliases=aliases,
        compiler_params=_cparams("arbitrary"),
        name=name,
    )(*args)


ROUTE_TM = 512
GROUP_SIZE = N_EXP // N_GROUPS


def _first_argmax(v, idx, axis, sentinel):
    mx = jnp.max(v, axis=axis, keepdims=True)
    return mx, jnp.min(jnp.where(v == mx, idx, sentinel), axis=axis, keepdims=True)


def _route_kernel(lt_ref, bias_ref, idx_ref, w_ref, rank_ref, cnt_ref, carry_ref):
    i = pl.program_id(0)
    tm = lt_ref.shape[1]

    @pl.when(i == 0)
    def _():
        carry_ref[...] = jnp.zeros_like(carry_ref)

    scores = jax.nn.sigmoid(lt_ref[...])
    biased = scores + bias_ref[...]
    b3 = biased.reshape(N_GROUPS, GROUP_SIZE, tm)
    mem = lax.broadcasted_iota(jnp.int32, b3.shape, 1).astype(F32)
    m1, first = _first_argmax(b3, mem, 1, float(GROUP_SIZE))
    m2 = jnp.max(jnp.where(mem == first, -jnp.inf, b3), axis=1, keepdims=True)
    gs = (m1 + m2).reshape(N_GROUPS, tm)
    gi = lax.broadcasted_iota(jnp.int32, gs.shape, 0).astype(F32)
    gsel = jnp.zeros(gs.shape, F32)
    cur = gs
    for _ in range(TOPK_GROUPS):
        _, pick = _first_argmax(cur, gi, 0, float(N_GROUPS))
        hit = gi == pick
        gsel = jnp.where(hit, 1.0, gsel)
        cur = jnp.where(hit, -jnp.inf, cur)
    masked = jnp.where(gsel.reshape(N_GROUPS, 1, tm) > 0.5, b3, -jnp.inf).reshape(N_EXP, tm)

    ei = lax.broadcasted_iota(jnp.int32, masked.shape, 0).astype(F32)
    picks, sel_scores = [], []
    chosen = jnp.zeros(masked.shape, F32)
    cur = masked
    for _ in range(TOP_K):
        _, pick = _first_argmax(cur, ei, 0, float(N_EXP))
        hit = ei == pick
        picks.append(pick)
        sel_scores.append(jnp.sum(jnp.where(hit, scores, 0.0), axis=0, keepdims=True))
        chosen = jnp.where(hit, 1.0, chosen)
        cur = jnp.where(hit, -jnp.inf, cur)

    r, c = _tri_masks(tm)
    before = jnp.where(r < c, 1.0, 0.0).astype(BF16)
    rank = jnp.dot(chosen.astype(BF16), before, preferred_element_type=F32) + carry_ref[...]
    carry_ref[...] = carry_ref[...] + jnp.sum(chosen, axis=1, keepdims=True)
    cnt_ref[...] = carry_ref[...]

    total = sel_scores[0]
    for s in sel_scores[1:]:
        total = total + s
    for k in range(TOP_K):
        idx_ref[k:k + 1, :] = picks[k].astype(jnp.int32)
        w_ref[k:k + 1, :] = sel_scores[k] / total * ROUTED_SCALE
        rank_ref[k:k + 1, :] = jnp.sum(jnp.where(ei == picks[k], rank, 0.0), axis=0, keepdims=True).astype(jnp.int32)


def route(logits_t, router_bias):
    n_e, t = logits_t.shape
    tm = ROUTE_TM
    kt = pl.BlockSpec((TOP_K, tm), lambda i: (0, i))
    return pl.pallas_call(
        _route_kernel,
        grid=(t // tm,),
        in_specs=[pl.BlockSpec((n_e, tm), lambda i: (0, i)), pl.BlockSpec((n_e, 1), lambda i: (0, 0))],
        out_specs=[kt, kt, kt, pl.BlockSpec((n_e, 1), lambda i: (0, 0))],
        out_shape=[jax.ShapeDtypeStruct((TOP_K, t), jnp.int32), jax.ShapeDtypeStruct((TOP_K, t), F32),
                   jax.ShapeDtypeStruct((TOP_K, t), jnp.int32), jax.ShapeDtypeStruct((n_e, 1), F32)],
        scratch_shapes=[pltpu.VMEM((n_e, 1), F32)],
        compiler_params=_cparams("arbitrary"),
        name="route",
    )(logits_t, router_bias.reshape(n_e, 1))


def moe(h, logits_t, router_bias, e_gate, e_up, e_down, s_gate, s_up, s_down, layer):
    d = h.shape[1]
    t = logits_t.shape[1]
    tm = 256
    top_e, wts, rank, counts = route(logits_t, router_bias)
    counts = counts.reshape(-1).astype(jnp.int32)
    n_assign = t * TOP_K
    padded = (counts + tm - 1) // tm * tm
    pad_end = jnp.cumsum(padded)
    pad_start = pad_end - padded
    n_blk = n_assign // tm + N_EXP
    blk_first = jnp.arange(n_blk, dtype=jnp.int32) * tm
    blk_e = jnp.minimum(jnp.sum((pad_end[None, :] <= blk_first[:, None]).astype(jnp.int32), axis=1), N_EXP - 1)
    n_used = (pad_end[-1] // tm).astype(jnp.int32).reshape(1)
    expert_ids = jnp.arange(N_EXP, dtype=jnp.int32)
    dest = jnp.sum(jnp.where(top_e[..., None] == expert_ids, pad_start, 0), axis=-1) + rank
    tok = jnp.broadcast_to(jnp.arange(t, dtype=jnp.int32), (TOP_K, t))
    filler = jnp.arange(n_blk * tm, dtype=jnp.int32) % t
    slot_tok = filler.at[dest.reshape(-1)].set(tok.reshape(-1), unique_indices=True)
    blk_ids = jnp.arange(n_blk, dtype=jnp.int32)
    run_end = jnp.sum(jnp.where(blk_e[:, None] >= expert_ids, padded, 0), axis=1) // tm
    run_end_e = jnp.sum(jnp.where(run_end[:, None] == blk_ids, blk_e, 0), axis=1)
    cb = n_blk // MOE_CHUNKS
    ys = None
    for ci in range(MOE_CHUNKS):
        b0, b1 = ci * cb, (ci + 1) * cb
        xs = h.at[slot_tok[b0 * tm:b1 * tm]].get(mode="promise_in_bounds")
        last_blk = jnp.minimum(n_used[0], b1)
        next_e = jnp.where(run_end[b0:b1] < last_blk, run_end_e[b0:b1], -1).astype(jnp.int32)
        ys = expert_ffn(xs, blk_e[b0:b1], next_e, jnp.clip(n_used - b0, 0, cb), e_gate, e_up, e_down, layer, tm=tm,
                        out_rows=n_blk * tm, out_block0=b0, out_buf=ys, name="routed_ffn")
    routed_rows = ys.at[dest.reshape(-1)].get(mode="promise_in_bounds").reshape(TOP_K, t, d)
    tm_sh = 512
    n_sh = t // tm_sh
    shared = expert_ffn(h, jnp.zeros((n_sh,), jnp.int32), jnp.full((n_sh,), -1, jnp.int32),
                        jnp.full((1,), n_sh, jnp.int32), s_gate[:, None], s_up[:, None], s_down[:, None], layer,
                        tm=tm_sh, name="shared_ffn")
    return shared, routed_rows, wts.T


def kernel(x_prompt, x_sample, state_dn, state_ml_C, state_ml_n, state_ml_m, state_ssd, c, c_ctx,
           mod_w, mod_b, ln1_g, ln1_b, ln2_g, ln2_b, router_w, router_bias, exp_gate, exp_up, exp_down,
           sh_gate, sh_up, sh_down, ev_w_in, ev_conv_w, ev_conv_b, dn_A_log, dn_dt_bias, ml_b_i, ml_b_f,
           dn_norm, ml_norm, ev_w_out, od_w_in, od_conv_w, od_conv_b, ssd_A_log, ssd_dt_bias, ssd_D,
           ssd_norm, od_w_out):
    bp, sl, d = x_prompt.shape
    bl, ll, _ = x_sample.shape
    depth = mod_w.shape[0]
    t_ctx = bp * sl
    t_all = t_ctx + bl * ll
    x = (x_prompt.reshape(t_ctx, d), x_sample.reshape(bl * ll, d))
    cvec = jnp.concatenate([c_ctx[None], c, jnp.zeros((8 - 1 - bl, d), F32)], axis=0)
    mods = compute_mods(cvec, mod_w, mod_b)[:, :1 + bl].reshape(depth, 1 + bl, 6, d)
    geo = dict(t_ctx=t_ctx, lat_len=ll)
    ctx = dict(row0=0, n_seq=bp, seq_len=sl)
    lat = dict(row0=t_ctx, n_seq=bl, seq_len=ll)

    h = modulate(x[0], mods[0], 0, 1, row0=0, t_total=t_all, **geo)
    h = modulate(x[1], mods[0], 0, 1, row0=t_ctx, t_total=t_all, out_buf=h, **geo)
    new_dn, new_c, new_n, new_m, new_ssd = [], [], [], [], []
    for l in range(depth):
        j = l // 2
        if l % 2 == 0:
            w_in = ev_w_in[j]
            proj = matmul(h, w_in, tm=1024, tn=1024, n_out=EV_MAIN, out_dtype=BF16, name="ev_in_proj")
            graw = matmul(h, w_in, tm=1024, tn=LANES, n_out=LANES, col_block_off=EV_MAIN // LANES,
                          valid_cols=EV_GATES, name="ev_gate_proj")
            act, cum = gate_prep(graw, even_gate_params(dn_A_log[j], dn_dt_bias[j], ml_b_i[j], ml_b_f[j]), "even")
            actt, cumt = act.T, cum.T
            cw, cb, dnn, mln = ev_conv_w[j], ev_conv_b[j].reshape(1, -1), dn_norm[j].reshape(1, -1), ml_norm[j].reshape(1, -1)
            oa, s_dn = delta_mixer(proj, cw, cb, act, cum, cumt, dnn, None, period=sl, hb=4, **ctx)
            oa = delta_mixer(proj, cw, cb, act, cum, cumt, dnn, state_dn[:, j], period=GRID_W, hb=1, out_buf=oa, **lat)
            ob, s_c, s_nm = mlstm_mixer(proj, act, actt, cum, cumt, mln, None, **ctx)
            ob = mlstm_mixer(proj, act, actt, cum, cumt, mln,
                             (state_ml_C[:, j], state_ml_n[:, j], state_ml_m[:, j]), out_buf=ob, **lat)
            out_proj = ([oa, ob], ev_w_out[j].astype(BF16))
            new_dn.append(s_dn)
            new_c.append(s_c)
            new_n.append(s_nm[:, :, :, 0, :])
            new_m.append(s_nm[:, :, :, 1, 0])
        else:
            w_in = od_w_in[j]
            proj = matmul(h, w_in, tm=1024, tn=1024, n_out=OD_MAIN, out_dtype=BF16, name="od_in_proj")
            draw = matmul(h, w_in, tm=1024, tn=LANES, n_out=LANES, col_block_off=OD_MAIN // LANES, name="od_dt_proj")
            dt, cum = gate_prep(draw, odd_gate_params(ssd_A_log[j], ssd_dt_bias[j]), "odd")
            cumt = cum.T
            cw, cb = od_conv_w[j], od_conv_b[j].reshape(1, -1)
            dsk, nrm = jnp.repeat(ssd_D[j], P_C).reshape(1, -1), ssd_norm[j].reshape(1, -1)
            oc, s_ssd = ssd_mixer(proj, cw, cb, dt, cum, cumt, dsk, nrm, None, period=sl, **ctx)
            oc = ssd_mixer(proj, cw, cb, dt, cum, cumt, dsk, nrm, state_ssd[:, j], period=GRID_W, out_buf=oc, **lat)
            out_proj = ([oc], od_w_out[j].astype(BF16))
            new_ssd.append(s_ssd)
        x, h2, logits_t = resid_ln(x, [], mods[l], mods[l], ln1_g[l], ln1_b[l], router_w[l], proj=out_proj, tm=512,
                                   gate=2, sh=3, sc=4, want_h=True, h_rows=GATHER_SRC_ROWS, **geo)
        shared, routed_rows, wts = moe(h2, logits_t, router_bias[l], exp_gate, exp_up, exp_down,
                                       sh_gate, sh_up, sh_down, l)
        last = l == depth - 1
        res = resid_ln(x, [shared], mods[l], mods[min(l + 1, depth - 1)], ln2_g[l], ln2_b[l], None,
                       gate=5, sh=0, sc=1, want_h=not last, gathered=(routed_rows, wts), split_out=last, **geo)
        if last:
            x = (res[0], res[1])
        else:
            x, h = res[0], res[1]
    y_prompt = x[0].reshape(bp, sl, d)
    y_sample = x[1].reshape(bl, ll, d)
    return (y_prompt, y_sample, jnp.stack(new_dn, axis=1), jnp.stack(new_c, axis=1), jnp.stack(new_n, axis=1),
            jnp.stack(new_m, axis=1), jnp.stack(new_ssd, axis=1))
```

```python
import functools

import jax
import jax.numpy as jnp
from jax import lax
from jax.experimental import pallas as pl
from jax.experimental.pallas import tpu as pltpu

F32 = jnp.float32
BF16 = jnp.bfloat16

D_MODEL = 2048
DEPTH = 2
GRID_W = 64
ALPHA = (2 * DEPTH) ** 0.25
LN_EPS = 1e-5
RMS_EPS = 1e-6

H_A, DK_A, DV_A = 8, 128, 128
H_B, DK_B, DV_B = 4, 128, 256
CONV_A = 2 * H_A * DK_A + H_A * DV_A
EV_MAIN = CONV_A + H_A * DV_A + 2 * H_B * DK_B + 2 * H_B * DV_B
EV_GATES = 4 * H_A + 4 * H_B

D_INNER = 2 * D_MODEL
P_C, N_C, G_C = 64, 128, 8
H_C = D_INNER // P_C
HG_C = H_C // G_C
GW_C = D_INNER // G_C
OD_MAIN = 2 * D_INNER + 2 * G_C * N_C

N_EXP, TOP_K, N_GROUPS, TOPK_GROUPS = 64, 8, 8, 4
D_EXP = 512
ROUTED_SCALE = 2.5

MOE_CHUNKS = 3
GATHER_SRC_ROWS = 16384
CHUNK = 256
LANES = 128
VMEM_LIMIT = 56 * 1024 * 1024
NEG = -1e30
LOG2E = 1.4426950408889634


def _cparams(*sem):
    return pltpu.CompilerParams(dimension_semantics=sem, vmem_limit_bytes=VMEM_LIMIT)


def _bdot(a, b):
    return jnp.dot(a.astype(BF16), b.astype(BF16), preferred_element_type=F32)


def _bdot_nt(a, b):
    return lax.dot_general(a.astype(BF16), b.astype(BF16), (((1,), (1,)), ((), ())), preferred_element_type=F32)


def _bdot_tn(a, b):
    return lax.dot_general(a.astype(BF16), b.astype(BF16), (((0,), (0,)), ((), ())), preferred_element_type=F32)


def _split3(a):
    hi = a.astype(BF16)
    r = a - hi.astype(F32)
    mid = r.astype(BF16)
    lo = (r - mid.astype(F32)).astype(BF16)
    return hi, mid, lo


def _dot_exact_rhs(a, b_exact, passes=3):
    hi, mid, lo = _split3(a)
    bb = b_exact.astype(BF16)
    d = lambda p: jnp.dot(p, bb, preferred_element_type=F32)
    return d(hi) + d(mid) + d(lo) if passes == 3 else d(hi) + d(mid)


def _dot_exact_lhs(a_exact, b):
    hi, mid, lo = _split3(b)
    aa = a_exact.astype(BF16)
    d = lambda p: jnp.dot(aa, p, preferred_element_type=F32)
    return d(hi) + d(mid) + d(lo)


def _silu(x):
    return x * jax.nn.sigmoid(x)


def _softplus(x):
    return jnp.maximum(x, 0.0) + jnp.log(1.0 + jnp.exp(-jnp.abs(x)))


def _group_of_block(i, tm, t_ctx, lat_len):
    return jnp.maximum(i * tm - t_ctx, -1) // lat_len + 1


def _mod_kernel(c_ref, w_ref, b_ref, o_ref):
    c = c_ref[...]
    o_ref[...] = _bdot(_silu(c), w_ref[...]) + b_ref[...]


def compute_mods(cvec, mod_w, mod_b):
    depth, d, n = mod_w.shape
    tn = 512
    return pl.pallas_call(
        _mod_kernel,
        grid=(depth, n // tn),
        in_specs=[pl.BlockSpec((8, d), lambda l, j: (0, 0)),
                  pl.BlockSpec((None, d, tn), lambda l, j: (l, 0, j)),
                  pl.BlockSpec((None, 1, tn), lambda l, j: (l, 0, j))],
        out_specs=pl.BlockSpec((None, 8, tn), lambda l, j: (l, 0, j)),
        out_shape=jax.ShapeDtypeStruct((depth, 8, n), F32),
        compiler_params=_cparams("arbitrary", "arbitrary"),
        name="mod_vectors",
    )(cvec, mod_w, mod_b.reshape(depth, 1, n))


def _modulate_kernel(x_ref, m_ref, *rest, sh, sc):
    o_ref = rest[-1]
    o_ref[...] = (x_ref[...] * (1.0 + m_ref[sc:sc + 1, :]) + m_ref[sh:sh + 1, :]).astype(o_ref.dtype)


def modulate(x_part, mod, sh, sc, *, row0, t_total, t_ctx, lat_len, out_buf=None):
    tp, d = x_part.shape
    tm = 512
    blk0 = row0 // tm
    in_specs = [pl.BlockSpec((tm, d), lambda i: (i, 0)),
                pl.BlockSpec((None, 6, d), lambda i: (_group_of_block(blk0 + i, tm, t_ctx, lat_len), 0, 0))]
    args = [x_part, mod]
    aliases = _alias_out(in_specs, args, out_buf)
    return pl.pallas_call(
        functools.partial(_modulate_kernel, sh=sh, sc=sc),
        grid=(tp // tm,),
        in_specs=in_specs,
        out_specs=pl.BlockSpec((tm, d), lambda i: (blk0 + i, 0)),
        out_shape=jax.ShapeDtypeStruct((t_total, d), BF16),
        input_output_aliases=aliases,
        compiler_params=_cparams("arbitrary"),
        name="modulate",
    )(*args)


def _matmul_kernel(x_ref, w_ref, o_ref, wbf_ref, *, valid_cols):
    @pl.when(pl.program_id(1) == 0)
    def _():
        wbf_ref[...] = w_ref[...].astype(BF16)

    y = jnp.dot(x_ref[...], wbf_ref[...], preferred_element_type=F32)
    if valid_cols is not None:
        col = lax.broadcasted_iota(jnp.int32, y.shape, 1)
        y = jnp.where(col < valid_cols, y, 0.0)
    o_ref[...] = y.astype(o_ref.dtype)


def matmul(x, w, *, tm, tn, n_out, col_block_off=0, valid_cols=None, out_dtype=F32, name="matmul"):
    m, k = x.shape
    return pl.pallas_call(
        functools.partial(_matmul_kernel, valid_cols=valid_cols),
        grid=(n_out // tn, m // tm),
        in_specs=[pl.BlockSpec((tm, k), lambda j, i: (i, 0)),
                  pl.BlockSpec((k, tn), lambda j, i: (0, j + col_block_off))],
        out_specs=pl.BlockSpec((tm, tn), lambda j, i: (i, j)),
        out_shape=jax.ShapeDtypeStruct((m, n_out), out_dtype),
        scratch_shapes=[pltpu.VMEM((k, tn), BF16)],
        compiler_params=_cparams("arbitrary", "arbitrary"),
        name=name,
    )(x, w)


def _tri_masks(n):
    r = lax.broadcasted_iota(jnp.int32, (n, n), 0)
    c = lax.broadcasted_iota(jnp.int32, (n, n), 1)
    return r, c


def _gate_kernel(raw_ref, p_ref, act_ref, cum_ref, *, mode):
    x = raw_ref[...]
    coef, bias, rev = p_ref[0:1, :], p_ref[1:2, :], p_ref[2:3, :]
    col = lax.broadcasted_iota(jnp.int32, x.shape, 1)
    xb = x + bias
    if mode == "even":
        act = jnp.where(col < 2 * H_A, jax.nn.sigmoid(xb),
                        jnp.where(col < 4 * H_A, coef * _softplus(xb),
                                  jnp.where(col < 4 * H_A + 2 * H_B, xb,
                                            jnp.minimum(xb, 0.0) - jnp.log(1.0 + jnp.exp(-jnp.abs(xb))))))
        to_sum = act
    else:
        act = _softplus(xb)
        to_sum = act * (coef * LOG2E)
    r, c = _tri_masks(CHUNK)
    lower = jnp.where(c <= r, 1.0, 0.0)
    upper = jnp.where(c >= r, 1.0, 0.0)
    cum_f = _dot_exact_lhs(lower, to_sum)
    cum_r = _dot_exact_lhs(upper, to_sum)
    act_ref[...] = act
    cum_ref[...] = jnp.where(rev > 0.5, cum_r, cum_f)


def gate_prep(raw, params, mode):
    t = raw.shape[0]
    return pl.pallas_call(
        functools.partial(_gate_kernel, mode=mode),
        grid=(t // CHUNK,),
        in_specs=[pl.BlockSpec((CHUNK, LANES), lambda i: (i, 0)),
                  pl.BlockSpec((8, LANES), lambda i: (0, 0))],
        out_specs=[pl.BlockSpec((CHUNK, LANES), lambda i: (i, 0))] * 2,
        out_shape=[jax.ShapeDtypeStruct((t, LANES), F32)] * 2,
        compiler_params=_cparams("arbitrary"),
        name="gate_prep_" + mode,
    )(raw, params)


def _conv_silu(x, cw_ref, cb_ref, period):
    n = x.shape[0]
    row = lax.broadcasted_iota(jnp.int32, x.shape, 0) % period
    prev = jnp.where(row == 0, 0.0, pltpu.roll(x, 1, 0))
    nxt = jnp.where(row == period - 1, 0.0, pltpu.roll(x, n - 1, 0))
    y = cb_ref[...] + prev * cw_ref[0:1, :] + x * cw_ref[1:2, :] + nxt * cw_ref[2:3, :]
    return _silu(y)


def _pick_col(blk, idx):
    lane = lax.broadcasted_iota(jnp.int32, blk.shape, 1)
    return jnp.sum(jnp.where(lane == idx, blk, 0.0), axis=1, keepdims=True)


def _dir_masks(rev):
    r, c = _tri_masks(CHUNK)
    if rev:
        return c >= r, c > r
    return c <= r, c < r


def _tri_inverse(lmat, rev):
    return _tri_inverse_many([lmat], [rev])[0]


def _tri_inverse_many(lmats, revs):
    r, c = _tri_masks(CHUNK)
    eye = jnp.where(r == c, 1.0, 0.0)

    def off_mask(s, rev):
        same = (r // (2 * s)) == (c // (2 * s))
        r_hi = (r // s) % 2
        c_hi = (c // s) % 2
        return same & ((r_hi == 0) & (c_hi == 1) if rev else (r_hi == 1) & (c_hi == 0))

    masks = {rev: off_mask(1, rev) for rev in set(revs)}
    ts = [eye - jnp.where(masks[rev], lm, 0.0) for lm, rev in zip(lmats, revs)]
    s = 2
    while s < CHUNK:
        masks = {rev: off_mask(s, rev) for rev in set(revs)}
        ps = [_bdot(t, jnp.where(masks[rev], lm, 0.0)) for t, lm, rev in zip(ts, lmats, revs)]
        ts = [t - _bdot(p, t) for p, t in zip(ps, ts)]
        s *= 2
    return ts


def _delta_kernel(*refs, n_chunks, period, has_state, hb):
    (q_ref, k_ref, v_ref, z_ref, cwq, cwk, cwv, cbq, cbk, cbv, act_ref, cum_ref, cumt_ref, norm_ref) = refs[:14]
    rest = refs[14:]
    if has_state:
        s0_ref, *_aliased_out, o_ref, acc_ref = rest
        sout_ref = None
    else:
        o_ref, sout_ref, acc_ref = rest
    h0 = pl.program_id(1) * hb

    q = _conv_silu(q_ref[...].astype(F32), cwq, cbq, period)
    k = _conv_silu(k_ref[...].astype(F32), cwk, cbk, period)
    v = _conv_silu(v_ref[...].astype(F32), cwv, cbv, period)

    triples = [(hh, d, ci) for hh in range(hb) for d in (0, 1) for ci in range(n_chunks)]
    pre = {}
    for hh in range(hb):
        hs = slice(hh * DK_A, (hh + 1) * DK_A)
        qh, kh = q[:, hs], k[:, hs]
        qh = qh * lax.rsqrt(jnp.sum(qh * qh, axis=1, keepdims=True) + RMS_EPS) * (DK_A ** -0.5)
        kh = kh * lax.rsqrt(jnp.sum(kh * kh, axis=1, keepdims=True) + RMS_EPS)
        for d in (0, 1):
            m_incl, m_strict = _dir_masks(d == 1)
            for ci in range(n_chunks):
                sl = slice(ci * CHUNK, (ci + 1) * CHUNK)
                qc, kc, vc = qh[sl], kh[sl], v[sl, hs]
                beta = _pick_col(act_ref[sl, :], d * H_A + h0 + hh)
                gcol = _pick_col(cum_ref[sl, :], 2 * H_A + d * H_A + h0 + hh)
                grow = cumt_ref[pl.ds(2 * H_A + d * H_A + h0 + hh, 1), sl]
                decay = jnp.exp(jnp.where(m_incl, gcol - grow, NEG))
                kb = kc * beta
                pre[hh, d, ci] = dict(
                    qc=qc, kc=kc, kb=kb, vb=vc * beta, gcol=gcol, decay=decay,
                    lmat=_bdot_nt(kb, kc) * jnp.where(m_strict, decay, 0.0),
                    attn=_bdot_nt(qc, kc) * decay)
    tinvs = _tri_inverse_many([pre[t]["lmat"] for t in triples], [t[1] == 1 for t in triples])
    for t, tinv in zip(triples, tinvs):
        p = pre[t]
        p["u"] = _bdot(tinv, p["vb"])
        if has_state:
            p["w"] = _bdot(tinv, p["kb"] * jnp.exp(p["gcol"]))

    for hh in range(hb):
        hs = slice(hh * DK_A, (hh + 1) * DK_A)
        for d in (0, 1):
            rev = d == 1
            state = s0_ref[d, hh] if has_state else None
            order = range(n_chunks - 1, -1, -1) if rev else range(n_chunks)
            for ci in order:
                sl = slice(ci * CHUNK, (ci + 1) * CHUNK)
                p = pre[hh, d, ci]
                u, gcol = p["u"], p["gcol"]
                if state is not None:
                    u = u - _bdot(p["w"], state)
                o = _bdot(p["attn"], u)
                if state is not None:
                    o = o + _bdot(p["qc"] * jnp.exp(gcol), state)
                glast = gcol[0:1, :] if rev else gcol[CHUNK - 1:CHUNK, :]
                upd = _bdot_tn(p["kc"] * jnp.exp(glast - gcol), u)
                state = upd if state is None else state * jnp.exp(glast) + upd
                if rev:
                    acc_ref[sl, hs] = acc_ref[sl, hs] + o
                else:
                    acc_ref[sl, hs] = o
            if sout_ref is not None:
                sout_ref[d, hh] = state

    z = z_ref[...].astype(F32)
    for hh in range(hb):
        hs = slice(hh * DK_A, (hh + 1) * DK_A)
        o = acc_ref[:, hs]
        o = o * lax.rsqrt(jnp.mean(o * o, axis=1, keepdims=True) + RMS_EPS) * norm_ref[...]
        o_ref[:, hs] = (o * _silu(z[:, hs])).astype(o_ref.dtype)


def _alias_out(in_specs, args, out_buf):
    if out_buf is None:
        return {}
    in_specs.append(pl.BlockSpec(memory_space=pl.ANY))
    args.append(out_buf)
    return {len(args) - 1: 0}


def delta_mixer(proj, conv_w, conv_b, act, cum, cumt, norm, state, *, row0, n_seq, seq_len, period, hb,
                out_buf=None):
    assert row0 % seq_len == 0 and H_A % hb == 0
    n_chunks = seq_len // CHUNK
    rb0 = row0 // seq_len
    has_state = state is not None
    w = hb * DK_A
    nq = H_A // hb
    col = lambda off: (lambda s, h: (rb0 + s, off + h))
    cw = lambda off: (lambda s, h: (0, off + h))
    in_specs = [pl.BlockSpec((seq_len, w), col(0)), pl.BlockSpec((seq_len, w), col(nq)),
                pl.BlockSpec((seq_len, w), col(2 * nq)), pl.BlockSpec((seq_len, w), col(3 * nq)),
                pl.BlockSpec((3, w), cw(0)), pl.BlockSpec((3, w), cw(nq)), pl.BlockSpec((3, w), cw(2 * nq)),
                pl.BlockSpec((1, w), cw(0)), pl.BlockSpec((1, w), cw(nq)), pl.BlockSpec((1, w), cw(2 * nq)),
                pl.BlockSpec((seq_len, LANES), lambda s, h: (rb0 + s, 0)),
                pl.BlockSpec((seq_len, LANES), lambda s, h: (rb0 + s, 0)),
                pl.BlockSpec((LANES, seq_len), lambda s, h: (0, rb0 + s)),
                pl.BlockSpec((1, DV_A), lambda s, h: (0, 0))]
    args = [proj, proj, proj, proj, conv_w, conv_w, conv_w, conv_b, conv_b, conv_b, act, cum, cumt, norm]
    o_spec = pl.BlockSpec((seq_len, w), lambda s, h: (rb0 + s, h))
    o_shape = jax.ShapeDtypeStruct((proj.shape[0], H_A * DV_A), BF16)
    st_spec = pl.BlockSpec((None, 2, hb, DK_A, DV_A), lambda s, h: (s, 0, h, 0, 0))
    if has_state:
        in_specs.append(st_spec)
        args.append(state)
        out_specs, out_shape = o_spec, o_shape
    else:
        out_specs = [o_spec, st_spec]
        out_shape = [o_shape, jax.ShapeDtypeStruct((n_seq, 2, H_A, DK_A, DV_A), F32)]
    aliases = _alias_out(in_specs, args, out_buf)
    return pl.pallas_call(
        functools.partial(_delta_kernel, n_chunks=n_chunks, period=period, has_state=has_state, hb=hb),
        grid=(n_seq, H_A // hb),
        in_specs=in_specs, out_specs=out_specs, out_shape=out_shape, input_output_aliases=aliases,
        scratch_shapes=[pltpu.VMEM((seq_len, w), F32)],
        compiler_params=_cparams("arbitrary", "arbitrary"),
        name="delta_lat" if has_state else "delta_ctx",
    )(*args)


def even_gate_params(a_log, dt_bias, b_i, b_f):
    zeros_a = jnp.zeros((2 * H_A,), F32)
    coef = jnp.concatenate([zeros_a, -jnp.exp(a_log.astype(F32)).reshape(-1), jnp.zeros((4 * H_B,), F32)])
    bias = jnp.concatenate([zeros_a, dt_bias.reshape(-1), b_i.reshape(-1), b_f.reshape(-1)]).astype(F32)
    rev = jnp.concatenate([jnp.repeat(jnp.arange(2, dtype=F32), H_A)] * 2 + [jnp.repeat(jnp.arange(2, dtype=F32), H_B)] * 2)
    p = jnp.stack([coef, bias, rev])
    return jnp.pad(p, ((0, 5), (0, LANES - EV_GATES)))


def _mlstm_kernel(*refs, n_chunks, has_state):
    (q_ref, k_ref, v_ref, og_ref, act_ref, actt_ref, cum_ref, cumt_ref, norm_ref) = refs[:9]
    rest = refs[9:]
    if has_state:
        c0_ref, n0_ref, m0_ref, *_aliased_out, o_ref, acc_ref = rest
    else:
        o_ref, cout_ref, nm_ref, acc_ref = rest
    h = pl.program_id(1)
    i_col0, f_col0 = 4 * H_A, 4 * H_A + 2 * H_B

    for d in (0, 1):
        rev = d == 1
        m_incl, _ = _dir_masks(rev)
        if has_state:
            cm, nv, m = c0_ref[d], n0_ref[d], m0_ref[d]
        else:
            cm, nv, m = None, None, jnp.zeros((1, 1), F32)
        order = range(n_chunks - 1, -1, -1) if rev else range(n_chunks)
        for ci in order:
            sl = slice(ci * CHUNK, (ci + 1) * CHUNK)
            qc = q_ref[sl, :].astype(F32) * (DK_B ** -0.5)
            kc = k_ref[sl, :].astype(F32)
            vc = v_ref[sl, :].astype(F32)
            li_col = _pick_col(act_ref[sl, :], i_col0 + d * H_B + h)
            li_row = actt_ref[pl.ds(i_col0 + d * H_B + h, 1), sl]
            b_col = _pick_col(cum_ref[sl, :], f_col0 + d * H_B + h)
            b_row = cumt_ref[pl.ds(f_col0 + d * H_B + h, 1), sl]
            dlog = jnp.where(m_incl, b_col - b_row + li_row, NEG)
            inter = b_col + m
            m_q = jnp.maximum(inter, jnp.max(dlog, axis=1, keepdims=True))
            s = _bdot_nt(qc, kc) * jnp.exp(dlog - m_q)
            num = _bdot(s, vc)
            den = jnp.sum(s, axis=1, keepdims=True)
            if cm is not None:
                w_inter = jnp.exp(inter - m_q)
                num = num + w_inter * _bdot(qc, cm)
                den = den + w_inter * jnp.sum(qc * nv, axis=1, keepdims=True)
            hout = num / jnp.maximum(jnp.abs(den), jnp.exp(-m_q))
            b_last = b_col[0:1, :] if rev else b_col[CHUNK - 1:CHUNK, :]
            wlog = b_last - b_col + li_col
            m_new = jnp.maximum(b_last + m, jnp.max(wlog, axis=0, keepdims=True))
            kw = kc * jnp.exp(wlog - m_new)
            c_upd = _bdot_tn(kw, vc)
            n_upd = jnp.sum(kw, axis=0, keepdims=True)
            if cm is not None:
                sc = jnp.exp(b_last + m - m_new)
                cm, nv = sc * cm + c_upd, sc * nv + n_upd
            else:
                cm, nv = c_upd, n_upd
            m = m_new
            if rev:
                acc_ref[sl, :] = acc_ref[sl, :] + hout
            else:
                acc_ref[sl, :] = hout
        if not has_state:
            cout_ref[d] = cm
            nm_ref[d, 0:1, :] = nv
            nm_ref[d, 1:2, :] = jnp.broadcast_to(m, (1, DK_B))
            nm_ref[d, 2:8, :] = jnp.zeros((6, DK_B), F32)

    o = acc_ref[...]
    o = o * lax.rsqrt(jnp.mean(o * o, axis=1, keepdims=True) + RMS_EPS) * norm_ref[...]
    o_ref[...] = (o * jax.nn.sigmoid(og_ref[...].astype(F32))).astype(o_ref.dtype)


def mlstm_mixer(proj, act, actt, cum, cumt, norm, state, *, row0, n_seq, seq_len, out_buf=None):
    assert row0 % seq_len == 0
    n_chunks = seq_len // CHUNK
    rb0 = row0 // seq_len
    has_state = state is not None
    q0 = (CONV_A + H_A * DV_A) // LANES
    k0 = q0 + H_B
    v0 = (CONV_A + H_A * DV_A + 2 * H_B * DK_B) // DV_B
    o0 = v0 + H_B
    col = lambda off: (lambda s, h: (rb0 + s, off + h))
    in_specs = [pl.BlockSpec((seq_len, DK_B), col(q0)), pl.BlockSpec((seq_len, DK_B), col(k0)),
                pl.BlockSpec((seq_len, DV_B), col(v0)), pl.BlockSpec((seq_len, DV_B), col(o0)),
                pl.BlockSpec((seq_len, LANES), lambda s, h: (rb0 + s, 0)),
                pl.BlockSpec((LANES, seq_len), lambda s, h: (0, rb0 + s)),
                pl.BlockSpec((seq_len, LANES), lambda s, h: (rb0 + s, 0)),
                pl.BlockSpec((LANES, seq_len), lambda s, h: (0, rb0 + s)),
                pl.BlockSpec((1, DV_B), lambda s, h: (0, 0))]
    args = [proj, proj, proj, proj, act, actt, cum, cumt, norm]
    o_spec = pl.BlockSpec((seq_len, DV_B), lambda s, h: (rb0 + s, h))
    o_shape = jax.ShapeDtypeStruct((proj.shape[0], H_B * DV_B), BF16)
    st_idx = lambda s, h: (s, 0, h, 0, 0)
    if has_state:
        c0, n0, m0 = state
        in_specs += [pl.BlockSpec((None, 2, None, DK_B, DV_B), st_idx),
                     pl.BlockSpec((None, 2, None, 1, DK_B), st_idx),
                     pl.BlockSpec((None, 2, None, 1, 1), st_idx)]
        args += [c0, n0.reshape(n_seq, 2, H_B, 1, DK_B), m0.reshape(n_seq, 2, H_B, 1, 1)]
        out_specs, out_shape = o_spec, o_shape
    else:
        out_specs = [o_spec, pl.BlockSpec((None, 2, None, DK_B, DV_B), st_idx),
                     pl.BlockSpec((None, 2, None, 8, DK_B), st_idx)]
        out_shape = [o_shape, jax.ShapeDtypeStruct((n_seq, 2, H_B, DK_B, DV_B), F32),
                     jax.ShapeDtypeStruct((n_seq, 2, H_B, 8, DK_B), F32)]
    aliases = _alias_out(in_specs, args, out_buf)
    return pl.pallas_call(
        functools.partial(_mlstm_kernel, n_chunks=n_chunks, has_state=has_state),
        grid=(n_seq, H_B),
        in_specs=in_specs, out_specs=out_specs, out_shape=out_shape, input_output_aliases=aliases,
        scratch_shapes=[pltpu.VMEM((seq_len, DV_B), F32)],
        compiler_params=_cparams("arbitrary", "arbitrary"),
        name="mlstm_lat" if has_state else "mlstm_ctx",
    )(*args)


def odd_gate_params(a_log, dt_bias):
    coef = -jnp.exp(a_log.astype(F32)).reshape(-1)
    bias = dt_bias.astype(F32).reshape(-1)
    rev = jnp.repeat(jnp.arange(2, dtype=F32), H_C)
    return jnp.pad(jnp.stack([coef, bias, rev]), ((0, 5), (0, 0)))


def _ssd_kernel(*refs, n_chunks, period, has_state):
    (z_ref, x_ref, b_ref, c_ref, cwx, cwb, cwc, cbx, cbb, cbc, dt_ref, cum_ref, cumt_ref, dskip_ref, norm_ref) = refs[:15]
    rest = refs[15:]
    if has_state:
        s0_ref, *_aliased_out, o_ref, acc_ref = rest
        sout_ref = None
    else:
        o_ref, sout_ref, acc_ref = rest
    g = pl.program_id(1)

    x = _conv_silu(x_ref[...].astype(F32), cwx, cbx, period)
    bm = _conv_silu(b_ref[...].astype(F32), cwb, cbb, period)
    cm = _conv_silu(c_ref[...].astype(F32), cwc, cbc, period)

    er = lax.broadcasted_iota(jnp.int32, (LANES, GW_C), 0)
    ec = lax.broadcasted_iota(jnp.int32, (LANES, GW_C), 1)
    tr = lax.broadcasted_iota(jnp.int32, (GW_C, LANES), 0)
    tc = lax.broadcasted_iota(jnp.int32, (GW_C, LANES), 1)
    lane_in_tile = lax.broadcasted_iota(jnp.int32, (CHUNK, LANES), 1)

    for d in (0, 1):
        rev = d == 1
        m_incl, _ = _dir_masks(rev)
        lo, hi, full = slice(0, CHUNK // 2), slice(CHUNK // 2, CHUNK), slice(0, CHUNK)
        half_blocks = ((lo, full), (hi, hi)) if rev else ((lo, lo), (hi, full))
        col0 = d * H_C + g * HG_C
        expand = jnp.where(er == col0 + ec // P_C, 1.0, 0.0)
        expand_t = tc == col0 + tr // P_C
        state = s0_ref[d].reshape(GW_C, N_C) if has_state else None
        order = range(n_chunks - 1, -1, -1) if rev else range(n_chunks)
        for ci in order:
            sl = slice(ci * CHUNK, (ci + 1) * CHUNK)
            xc, bc, cc = x[sl], bm[sl], cm[sl]
            cum_blk = cum_ref[sl, :]
            cum_last = cum_blk[0:1, :] if rev else cum_blk[CHUNK - 1:CHUNK, :]
            xdt = xc * _dot_exact_rhs(dt_ref[sl, :], expand, passes=2)
            scores_bf = _bdot_nt(cc, bc).astype(BF16)
            for hp in range(HG_C // 2):
                ps = slice(hp * LANES, (hp + 1) * LANES)
                xpair = xdt[:, ps]
                ypair = None
                for sub in (0, 1):
                    hh = 2 * hp + sub
                    cb = _pick_col(cum_blk, col0 + hh)
                    crow = cumt_ref[pl.ds(col0 + hh, 1), sl]
                    mine = (lane_in_tile < P_C) if sub == 0 else (lane_in_tile >= P_C)
                    rhs = jnp.where(mine, xpair, 0.0)
                    parts = []
                    for rows, cols in half_blocks:
                        seg = jnp.exp2(jnp.where(m_incl[rows, cols], cb[rows] - crow[:, cols], NEG))
                        parts.append(_bdot(scores_bf[rows, cols] * seg.astype(BF16), rhs[cols]))
                    y = jnp.concatenate(parts, axis=0)
                    ypair = y if ypair is None else ypair + y
                if rev:
                    acc_ref[sl, ps] = acc_ref[sl, ps] + ypair
                else:
                    acc_ref[sl, ps] = ypair
            if state is not None:
                y_in = _bdot_nt(cc, state) * _dot_exact_rhs(jnp.exp2(cum_blk), expand, passes=2)
                acc_ref[sl, :] = acc_ref[sl, :] + y_in
            dend = _dot_exact_rhs(jnp.exp2(jnp.minimum(cum_last - cum_blk, 0.0)), expand, passes=2)
            upd = _bdot_tn(xdt * dend, bc)
            if state is not None:
                tot = jnp.sum(jnp.where(expand_t, jnp.broadcast_to(cum_last, (GW_C, LANES)), 0.0), axis=1, keepdims=True)
                state = state * jnp.exp2(tot) + upd
            else:
                state = upd
        if sout_ref is not None:
            sout_ref[d] = state.reshape(HG_C, P_C, N_C)

    y = acc_ref[...] + dskip_ref[...] * x
    y = y * _silu(z_ref[...].astype(F32))
    y = y * lax.rsqrt(jnp.mean(y * y, axis=1, keepdims=True) + RMS_EPS) * norm_ref[...]
    o_ref[...] = y.astype(o_ref.dtype)


def ssd_mixer(proj, conv_w, conv_b, dt, cum, cumt, dskip, norm, state, *, row0, n_seq, seq_len, period, out_buf=None):
    assert row0 % seq_len == 0
    n_chunks = seq_len // CHUNK
    rb0 = row0 // seq_len
    has_state = state is not None
    xb0 = D_INNER // GW_C
    bb0 = 2 * D_INNER // N_C
    cb0 = bb0 + G_C
    wb0 = D_INNER // N_C
    wc0 = wb0 + G_C
    col = lambda off: (lambda s, g: (rb0 + s, off + g))
    cw = lambda off: (lambda s, g: (0, off + g))
    in_specs = [pl.BlockSpec((seq_len, GW_C), col(0)), pl.BlockSpec((seq_len, GW_C), col(xb0)),
                pl.BlockSpec((seq_len, N_C), col(bb0)), pl.BlockSpec((seq_len, N_C), col(cb0)),
                pl.BlockSpec((3, GW_C), cw(0)), pl.BlockSpec((3, N_C), cw(wb0)), pl.BlockSpec((3, N_C), cw(wc0)),
                pl.BlockSpec((1, GW_C), cw(0)), pl.BlockSpec((1, N_C), cw(wb0)), pl.BlockSpec((1, N_C), cw(wc0)),
                pl.BlockSpec((seq_len, LANES), lambda s, g: (rb0 + s, 0)),
                pl.BlockSpec((seq_len, LANES), lambda s, g: (rb0 + s, 0)),
                pl.BlockSpec((LANES, seq_len), lambda s, g: (0, rb0 + s)),
                pl.BlockSpec((1, GW_C), cw(0)), pl.BlockSpec((1, GW_C), cw(0))]
    args = [proj, proj, proj, proj, conv_w, conv_w, conv_w, conv_b, conv_b, conv_b, dt, cum, cumt, dskip, norm]
    o_spec = pl.BlockSpec((seq_len, GW_C), lambda s, g: (rb0 + s, g))
    o_shape = jax.ShapeDtypeStruct((proj.shape[0], D_INNER), BF16)
    st_spec = pl.BlockSpec((None, 2, HG_C, P_C, N_C), lambda s, g: (s, 0, g, 0, 0))
    if has_state:
        in_specs.append(st_spec)
        args.append(state)
        out_specs, out_shape = o_spec, o_shape
    else:
        out_specs = [o_spec, st_spec]
        out_shape = [o_shape, jax.ShapeDtypeStruct((n_seq, 2, H_C, P_C, N_C), F32)]
    aliases = _alias_out(in_specs, args, out_buf)
    return pl.pallas_call(
        functools.partial(_ssd_kernel, n_chunks=n_chunks, period=period, has_state=has_state),
        grid=(n_seq, G_C),
        in_specs=in_specs, out_specs=out_specs, out_shape=out_shape, input_output_aliases=aliases,
        scratch_shapes=[pltpu.VMEM((seq_len, GW_C), F32)],
        compiler_params=_cparams("arbitrary", "arbitrary"),
        name="ssd_lat" if has_state else "ssd_ctx",
    )(*args)


def _dot3(a, b):
    a_hi = a.astype(BF16)
    a_lo = (a - a_hi.astype(F32)).astype(BF16)
    b_hi = b.astype(BF16)
    b_lo = (b - b_hi.astype(F32)).astype(BF16)
    d = lambda p, q: jnp.dot(p, q, preferred_element_type=F32)
    return d(a_hi, b_hi) + (d(a_hi, b_lo) + d(a_lo, b_hi))


def _dot3_nt(a, b):
    a_hi = a.astype(BF16)
    a_lo = (a - a_hi.astype(F32)).astype(BF16)
    b_hi = b.astype(BF16)
    b_lo = (b - b_hi.astype(F32)).astype(BF16)
    d = lambda p, q: lax.dot_general(p, q, (((1,), (1,)), ((), ())), preferred_element_type=F32)
    return d(a_hi, b_hi) + (d(a_hi, b_lo) + d(a_lo, b_hi))


def _resid_ln_kernel(*refs, n_y, lhs_widths, n_gathered, gate, sh, sc, want_h, want_logits, split_in, split_out,
                     n_ctx_blocks):
    in_ctx = pl.program_id(0) < n_ctx_blocks
    if split_in:
        x_in = jnp.where(in_ctx, refs[0][...], refs[1][...])
        refs = refs[1:]
    else:
        x_in = refs[0][...]
    y_refs = refs[1:1 + n_y]
    rest = list(refs[1 + n_y:])
    lhs_refs = [rest.pop(0) for _ in lhs_widths]
    pw_ref = rest.pop(0) if lhs_widths else None
    m_ref, mn_ref, g_ref, b_ref = (rest.pop(0) for _ in range(4))
    if n_gathered:
        gath_ref, gw_ref = rest.pop(0), rest.pop(0)
    rw_ref = rest.pop(0) if want_logits else None
    xo_refs = [rest.pop(0) for _ in range(2 if split_out else 1)]
    y = None
    for r in y_refs:
        y = r[...].astype(F32) if y is None else y + r[...].astype(F32)
    off = 0
    for lhs_ref, width in zip(lhs_refs, lhs_widths):
        part = jnp.dot(lhs_ref[...], pw_ref[off:off + width, :], preferred_element_type=F32)
        y = part if y is None else y + part
        off += width
    for kk in range(n_gathered):
        y = y + gw_ref[:, kk:kk + 1] * gath_ref[kk].astype(F32)
    v = ALPHA * x_in + m_ref[gate:gate + 1, :] * y
    mu = jnp.mean(v, axis=1, keepdims=True)
    vc = v - mu
    var = jnp.mean(vc * vc, axis=1, keepdims=True)
    xn = vc * lax.rsqrt(var + LN_EPS) * g_ref[...] + b_ref[...]
    if split_out:
        @pl.when(in_ctx)
        def _():
            xo_refs[0][...] = xn

        @pl.when(jnp.logical_not(in_ctx))
        def _():
            xo_refs[1][...] = xn
    else:
        xo_refs[0][...] = xn
    if want_h:
        hm = xn * (1.0 + mn_ref[sc:sc + 1, :]) + mn_ref[sh:sh + 1, :]
        rest.pop(0)[...] = hm.astype(BF16)
        if want_logits:
            rest.pop(0)[...] = _dot3_nt(rw_ref[...], hm)


def resid_ln(x, ys, mod, mod_next, ln_g, ln_b, router_w, *, gate, sh, sc, want_h, t_ctx, lat_len, gathered=None,
             h_rows=None, proj=None, tm=256, split_out=False):
    split_in = isinstance(x, tuple)
    d = x[0].shape[1] if split_in else x.shape[1]
    t = x[0].shape[0] + x[1].shape[0] if split_in else x.shape[0]
    n_ctx_blocks = t_ctx // tm
    ctx_blk = lambda i: (jnp.minimum(i, n_ctx_blocks - 1), 0)
    lat_blk = lambda i: (jnp.maximum(i - n_ctx_blocks, 0), 0)
    want_logits = router_w is not None
    grp = lambda i: (_group_of_block(i, tm, t_ctx, lat_len), 0, 0)
    row = pl.BlockSpec((tm, d), lambda i: (i, 0))
    vec = pl.BlockSpec((1, d), lambda i: (0, 0))
    if split_in:
        in_specs = [pl.BlockSpec((tm, d), ctx_blk), pl.BlockSpec((tm, d), lat_blk)] + [row] * len(ys)
        args = [*x, *ys]
    else:
        in_specs = [row] * (1 + len(ys))
        args = [x, *ys]
    lhs_widths = ()
    if proj is not None:
        lhs_list, pw = proj
        lhs_widths = tuple(a.shape[1] for a in lhs_list)
        in_specs += [pl.BlockSpec((tm, wd), lambda i: (i, 0)) for wd in lhs_widths]
        in_specs.append(pl.BlockSpec(pw.shape, lambda i: (0, 0), pipeline_mode=pl.Buffered(1)))
        args += [*lhs_list, pw]
    in_specs += [pl.BlockSpec((None, 6, d), grp), pl.BlockSpec((None, 6, d), grp), vec, vec]
    args += [mod, mod_next, ln_g.reshape(1, d), ln_b.reshape(1, d)]
    n_gathered = 0
    if gathered is not None:
        n_gathered = gathered[0].shape[0]
        in_specs += [pl.BlockSpec((n_gathered, tm, d), lambda i: (0, i, 0)),
                     pl.BlockSpec((tm, n_gathered), lambda i: (i, 0))]
        args += list(gathered)
    if split_out:
        out_specs = [pl.BlockSpec((tm, d), ctx_blk), pl.BlockSpec((tm, d), lat_blk)]
        out_shape = [jax.ShapeDtypeStruct((t_ctx, d), F32), jax.ShapeDtypeStruct((t - t_ctx, d), F32)]
    else:
        out_specs, out_shape = [row], [jax.ShapeDtypeStruct((t, d), F32)]
    if want_logits:
        n_e = router_w.shape[1]
        in_specs.append(pl.BlockSpec((n_e, d), lambda i: (0, 0)))
        args.append(router_w.T)
    if want_h:
        out_specs.append(row)
        out_shape.append(jax.ShapeDtypeStruct((h_rows or t, d), BF16))
    if want_logits:
        out_specs.append(pl.BlockSpec((n_e, tm), lambda i: (0, i)))
        out_shape.append(jax.ShapeDtypeStruct((n_e, t), F32))
    return pl.pallas_call(
        functools.partial(_resid_ln_kernel, n_y=len(ys), lhs_widths=lhs_widths, n_gathered=n_gathered, gate=gate, sh=sh,
                          sc=sc, want_h=want_h, want_logits=want_logits, split_in=split_in, split_out=split_out,
                          n_ctx_blocks=n_ctx_blocks),
        grid=(t // tm,),
        in_specs=in_specs, out_specs=out_specs, out_shape=out_shape,
        compiler_params=_cparams("arbitrary"),
        name="resid_ln",
    )(*args)


def _ffn_kernel(be_ref, nx_ref, nu_ref, x_ref, wg_hbm, wu_hbm, wd_hbm, *rest, layer):
    *_aliased_out, o_ref, g_f32, u_f32, d_f32, g_bf, u_bf, d_bf, sem = rest
    b = pl.program_id(0)

    def copies(e):
        return (pltpu.make_async_copy(wg_hbm.at[layer, e], g_f32, sem.at[0]),
                pltpu.make_async_copy(wu_hbm.at[layer, e], u_f32, sem.at[1]),
                pltpu.make_async_copy(wd_hbm.at[layer, e], d_f32, sem.at[2]))

    @pl.when(b < nu_ref[0])
    def _():
        e = be_ref[b]

        @pl.when(b == 0)
        def _():
            for cp in copies(e):
                cp.start()

        @pl.when((b == 0) | (e != be_ref[jnp.maximum(b - 1, 0)]))
        def _():
            for cp in copies(e):
                cp.wait()
            g_bf[...] = g_f32[...].astype(BF16)
            u_bf[...] = u_f32[...].astype(BF16)
            d_bf[...] = d_f32[...].astype(BF16)
            nxt = nx_ref[b]

            @pl.when(nxt >= 0)
            def _():
                for cp in copies(nxt):
                    cp.start()

        x = x_ref[...]
        hg = jnp.dot(x, g_bf[...], preferred_element_type=F32)
        hu = jnp.dot(x, u_bf[...], preferred_element_type=F32)
        a = (_silu(hg) * hu).astype(BF16)
        o_ref[...] = jnp.dot(a, d_bf[...], preferred_element_type=F32).astype(o_ref.dtype)


def expert_ffn(xs, blk_e, next_e, n_used, w_gate, w_up, w_down, layer, *, tm, out_rows=None, out_block0=0,
               out_buf=None, out_dtype=BF16, name="expert_ffn"):
    d = xs.shape[1]
    de = w_gate.shape[3]
    n_blk = blk_e.shape[0]
    in_specs = [pl.BlockSpec((tm, d), lambda b, be, nx, nu: (b, 0))] + [pl.BlockSpec(memory_space=pl.ANY)] * 3
    args = [blk_e, next_e, n_used, xs, w_gate, w_up, w_down]
    aliases = _alias_out(in_specs, args, out_buf)
    grid_spec = pltpu.PrefetchScalarGridSpec(
        num_scalar_prefetch=3,
        grid=(n_blk,),
        in_specs=in_specs,
        out_specs=pl.BlockSpec((tm, d), lambda b, be, nx, nu: (out_block0 + b, 0)),
        scratch_shapes=[pltpu.VMEM((d, de), F32), pltpu.VMEM((d, de), F32), pltpu.VMEM((de, d), F32),
                        pltpu.VMEM((d, de), BF16), pltpu.VMEM((d, de), BF16), pltpu.VMEM((de, d), BF16),
                        pltpu.SemaphoreType.DMA((3,))],
    )
    return pl.pallas_call(
        functools.partial(_ffn_kernel, layer=layer),
        grid_spec=grid_spec,
        out_shape=jax.ShapeDtypeStruct((out_rows or n_blk * tm, d), out_dtype),
        input_output_aliases=aliases,
        compiler_params=_cparams("arbitrary"),
        name=name,
    )(*args)


ROUTE_TM = 512
GROUP_SIZE = N_EXP // N_GROUPS


def _first_argmax(v, idx, axis, sentinel):
    mx = jnp.max(v, axis=axis, keepdims=True)
    return mx, jnp.min(jnp.where(v == mx, idx, sentinel), axis=axis, keepdims=True)


def _route_kernel(lt_ref, bias_ref, idx_ref, w_ref, rank_ref, cnt_ref, carry_ref):
    i = pl.program_id(0)
    tm = lt_ref.shape[1]

    @pl.when(i == 0)
    def _():
        carry_ref[...] = jnp.zeros_like(carry_ref)

    scores = jax.nn.sigmoid(lt_ref[...])
    biased = scores + bias_ref[...]
    b3 = biased.reshape(N_GROUPS, GROUP_SIZE, tm)
    mem = lax.broadcasted_iota(jnp.int32, b3.shape, 1).astype(F32)
    m1, first = _first_argmax(b3, mem, 1, float(GROUP_SIZE))
    m2 = jnp.max(jnp.where(mem == first, -jnp.inf, b3), axis=1, keepdims=True)
    gs = (m1 + m2).reshape(N_GROUPS, tm)
    gi = lax.broadcasted_iota(jnp.int32, gs.shape, 0).astype(F32)
    gsel = jnp.zeros(gs.shape, F32)
    cur = gs
    for _ in range(TOPK_GROUPS):
        _, pick = _first_argmax(cur, gi, 0, float(N_GROUPS))
        hit = gi == pick
        gsel = jnp.where(hit, 1.0, gsel)
        cur = jnp.where(hit, -jnp.inf, cur)
    masked = jnp.where(gsel.reshape(N_GROUPS, 1, tm) > 0.5, b3, -jnp.inf).reshape(N_EXP, tm)

    ei = lax.broadcasted_iota(jnp.int32, masked.shape, 0).astype(F32)
    picks, sel_scores = [], []
    chosen = jnp.zeros(masked.shape, F32)
    cur = masked
    for _ in range(TOP_K):
        _, pick = _first_argmax(cur, ei, 0, float(N_EXP))
        hit = ei == pick
        picks.append(pick)
        sel_scores.append(jnp.sum(jnp.where(hit, scores, 0.0), axis=0, keepdims=True))
        chosen = jnp.where(hit, 1.0, chosen)
        cur = jnp.where(hit, -jnp.inf, cur)

    r, c = _tri_masks(tm)
    before = jnp.where(r < c, 1.0, 0.0).astype(BF16)
    rank = jnp.dot(chosen.astype(BF16), before, preferred_element_type=F32) + carry_ref[...]
    carry_ref[...] = carry_ref[...] + jnp.sum(chosen, axis=1, keepdims=True)
    cnt_ref[...] = carry_ref[...]

    total = sel_scores[0]
    for s in sel_scores[1:]:
        total = total + s
    for k in range(TOP_K):
        idx_ref[k:k + 1, :] = picks[k].astype(jnp.int32)
        w_ref[k:k + 1, :] = sel_scores[k] / total * ROUTED_SCALE
        rank_ref[k:k + 1, :] = jnp.sum(jnp.where(ei == picks[k], rank, 0.0), axis=0, keepdims=True).astype(jnp.int32)


def route(logits_t, router_bias):
    n_e, t = logits_t.shape
    tm = ROUTE_TM
    kt = pl.BlockSpec((TOP_K, tm), lambda i: (0, i))
    return pl.pallas_call(
        _route_kernel,
        grid=(t // tm,),
        in_specs=[pl.BlockSpec((n_e, tm), lambda i: (0, i)), pl.BlockSpec((n_e, 1), lambda i: (0, 0))],
        out_specs=[kt, kt, kt, pl.BlockSpec((n_e, 1), lambda i: (0, 0))],
        out_shape=[jax.ShapeDtypeStruct((TOP_K, t), jnp.int32), jax.ShapeDtypeStruct((TOP_K, t), F32),
                   jax.ShapeDtypeStruct((TOP_K, t), jnp.int32), jax.ShapeDtypeStruct((n_e, 1), F32)],
        scratch_shapes=[pltpu.VMEM((n_e, 1), F32)],
        compiler_params=_cparams("arbitrary"),
        name="route",
    )(logits_t, router_bias.reshape(n_e, 1))


def moe(h, logits_t, router_bias, e_gate, e_up, e_down, s_gate, s_up, s_down, layer):
    d = h.shape[1]
    t = logits_t.shape[1]
    tm = 256
    top_e, wts, rank, counts = route(logits_t, router_bias)
    counts = counts.reshape(-1).astype(jnp.int32)
    n_assign = t * TOP_K
    padded = (counts + tm - 1) // tm * tm
    pad_end = jnp.cumsum(padded)
    pad_start = pad_end - padded
    n_blk = n_assign // tm + N_EXP
    blk_first = jnp.arange(n_blk, dtype=jnp.int32) * tm
    blk_e = jnp.minimum(jnp.sum((pad_end[None, :] <= blk_first[:, None]).astype(jnp.int32), axis=1), N_EXP - 1)
    n_used = (pad_end[-1] // tm).astype(jnp.int32).reshape(1)
    expert_ids = jnp.arange(N_EXP, dtype=jnp.int32)
    dest = jnp.sum(jnp.where(top_e[..., None] == expert_ids, pad_start, 0), axis=-1) + rank
    tok = jnp.broadcast_to(jnp.arange(t, dtype=jnp.int32), (TOP_K, t))
    filler = jnp.arange(n_blk * tm, dtype=jnp.int32) % t
    slot_tok = filler.at[dest.reshape(-1)].set(tok.reshape(-1), unique_indices=True, mode="promise_in_bounds")
    blk_ids = jnp.arange(n_blk, dtype=jnp.int32)
    run_end = jnp.sum(jnp.where(blk_e[:, None] >= expert_ids, padded, 0), axis=1) // tm
    run_end_e = jnp.sum(jnp.where(run_end[:, None] == blk_ids, blk_e, 0), axis=1)
    cb = n_blk // MOE_CHUNKS
    ys = None
    for ci in range(MOE_CHUNKS):
        b0, b1 = ci * cb, (ci + 1) * cb
        xs = h.at[slot_tok[b0 * tm:b1 * tm]].get(mode="promise_in_bounds")
        last_blk = jnp.minimum(n_used[0], b1)
        next_e = jnp.where(run_end[b0:b1] < last_blk, run_end_e[b0:b1], -1).astype(jnp.int32)
        ys = expert_ffn(xs, blk_e[b0:b1], next_e, jnp.clip(n_used - b0, 0, cb), e_gate, e_up, e_down, layer, tm=tm,
                        out_rows=n_blk * tm, out_block0=b0, out_buf=ys, name="routed_ffn")
    routed_rows = ys.at[dest.reshape(-1)].get(mode="promise_in_bounds").reshape(TOP_K, t, d)
    tm_sh = 1024
    n_sh = t // tm_sh
    shared = expert_ffn(h, jnp.zeros((n_sh,), jnp.int32), jnp.full((n_sh,), -1, jnp.int32),
                        jnp.full((1,), n_sh, jnp.int32), s_gate[:, None], s_up[:, None], s_down[:, None], layer,
                        tm=tm_sh, name="shared_ffn")
    return shared, routed_rows, wts.T


def kernel(x_prompt, x_sample, state_dn, state_ml_C, state_ml_n, state_ml_m, state_ssd, c, c_ctx,
           mod_w, mod_b, ln1_g, ln1_b, ln2_g, ln2_b, router_w, router_bias, exp_gate, exp_up, exp_down,
           sh_gate, sh_up, sh_down, ev_w_in, ev_conv_w, ev_conv_b, dn_A_log, dn_dt_bias, ml_b_i, ml_b_f,
           dn_norm, ml_norm, ev_w_out, od_w_in, od_conv_w, od_conv_b, ssd_A_log, ssd_dt_bias, ssd_D,
           ssd_norm, od_w_out):
    bp, sl, d = x_prompt.shape
    bl, ll, _ = x_sample.shape
    depth = mod_w.shape[0]
    t_ctx = bp * sl
    t_all = t_ctx + bl * ll
    x = (x_prompt.reshape(t_ctx, d), x_sample.reshape(bl * ll, d))
    cvec = jnp.concatenate([c_ctx[None], c, jnp.zeros((8 - 1 - bl, d), F32)], axis=0)
    mods = compute_mods(cvec, mod_w, mod_b)[:, :1 + bl].reshape(depth, 1 + bl, 6, d)
    geo = dict(t_ctx=t_ctx, lat_len=ll)
    ctx = dict(row0=0, n_seq=bp, seq_len=sl)
    lat = dict(row0=t_ctx, n_seq=bl, seq_len=ll)

    h = modulate(x[0], mods[0], 0, 1, row0=0, t_total=t_all, **geo)
    h = modulate(x[1], mods[0], 0, 1, row0=t_ctx, t_total=t_all, out_buf=h, **geo)
    new_dn, new_c, new_n, new_m, new_ssd = [], [], [], [], []
    for l in range(depth):
        j = l // 2
        if l % 2 == 0:
            w_in = ev_w_in[j]
            proj = matmul(h, w_in, tm=1024, tn=1024, n_out=EV_MAIN, out_dtype=BF16, name="ev_in_proj")
            graw = matmul(h, w_in, tm=1024, tn=LANES, n_out=LANES, col_block_off=EV_MAIN // LANES,
                          valid_cols=EV_GATES, name="ev_gate_proj")
            act, cum = gate_prep(graw, even_gate_params(dn_A_log[j], dn_dt_bias[j], ml_b_i[j], ml_b_f[j]), "even")
            actt, cumt = act.T, cum.T
            cw, cb, dnn, mln = ev_conv_w[j], ev_conv_b[j].reshape(1, -1), dn_norm[j].reshape(1, -1), ml_norm[j].reshape(1, -1)
            oa, s_dn = delta_mixer(proj, cw, cb, act, cum, cumt, dnn, None, period=sl, hb=4, **ctx)
            oa = delta_mixer(proj, cw, cb, act, cum, cumt, dnn, state_dn[:, j], period=GRID_W, hb=1, out_buf=oa, **lat)
            ob, s_c, s_nm = mlstm_mixer(proj, act, actt, cum, cumt, mln, None, **ctx)
            ob = mlstm_mixer(proj, act, actt, cum, cumt, mln,
                             (state_ml_C[:, j], state_ml_n[:, j], state_ml_m[:, j]), out_buf=ob, **lat)
            out_proj = ([oa, ob], ev_w_out[j].astype(BF16))
            new_dn.append(s_dn)
            new_c.append(s_c)
            new_n.append(s_nm[:, :, :, 0, :])
            new_m.append(s_nm[:, :, :, 1, 0])
        else:
            w_in = od_w_in[j]
            proj = matmul(h, w_in, tm=1024, tn=1024, n_out=OD_MAIN, out_dtype=BF16, name="od_in_proj")
            draw = matmul(h, w_in, tm=1024, tn=LANES, n_out=LANES, col_block_off=OD_MAIN // LANES, name="od_dt_proj")
            dt, cum = gate_prep(draw, odd_gate_params(ssd_A_log[j], ssd_dt_bias[j]), "odd")
            cumt = cum.T
            cw, cb = od_conv_w[j], od_conv_b[j].reshape(1, -1)
            dsk, nrm = jnp.repeat(ssd_D[j], P_C).reshape(1, -1), ssd_norm[j].reshape(1, -1)
            oc, s_ssd = ssd_mixer(proj, cw, cb, dt, cum, cumt, dsk, nrm, None, period=sl, **ctx)
            oc = ssd_mixer(proj, cw, cb, dt, cum, cumt, dsk, nrm, state_ssd[:, j], period=GRID_W, out_buf=oc, **lat)
            out_proj = ([oc], od_w_out[j].astype(BF16))
            new_ssd.append(s_ssd)
        x, h2, logits_t = resid_ln(x, [], mods[l], mods[l], ln1_g[l], ln1_b[l], router_w[l], proj=out_proj, tm=512,
                                   gate=2, sh=3, sc=4, want_h=True, h_rows=GATHER_SRC_ROWS, **geo)
        shared, routed_rows, wts = moe(h2, logits_t, router_bias[l], exp_gate, exp_up, exp_down,
                                       sh_gate, sh_up, sh_down, l)
        last = l == depth - 1
        res = resid_ln(x, [shared], mods[l], mods[min(l + 1, depth - 1)], ln2_g[l], ln2_b[l], None,
                       gate=5, sh=0, sc=1, want_h=not last, gathered=(routed_rows, wts), split_out=last, **geo)
        if last:
            x = (res[0], res[1])
        else:
            x, h = res[0], res[1]
    y_prompt = x[0].reshape(bp, sl, d)
    y_sample = x[1].reshape(bl, ll, d)
    return (y_prompt, y_sample, jnp.stack(new_dn, axis=1), jnp.stack(new_c, axis=1), jnp.stack(new_n, axis=1),
            jnp.stack(new_m, axis=1), jnp.stack(new_ssd, axis=1))
```

```python
import functools

import jax
import jax.numpy as jnp
from jax import lax
from jax.experimental import pallas as pl
from jax.experimental.pallas import tpu as pltpu

F32 = jnp.float32
BF16 = jnp.bfloat16

D_MODEL = 2048
DEPTH = 2
GRID_W = 64
ALPHA = (2 * DEPTH) ** 0.25
LN_EPS = 1e-5
RMS_EPS = 1e-6

H_A, DK_A, DV_A = 8, 128, 128
H_B, DK_B, DV_B = 4, 128, 256
CONV_A = 2 * H_A * DK_A + H_A * DV_A
EV_MAIN = CONV_A + H_A * DV_A + 2 * H_B * DK_B + 2 * H_B * DV_B
EV_GATES = 4 * H_A + 4 * H_B

D_INNER = 2 * D_MODEL
P_C, N_C, G_C = 64, 128, 8
H_C = D_INNER // P_C
HG_C = H_C // G_C
GW_C = D_INNER // G_C
OD_MAIN = 2 * D_INNER + 2 * G_C * N_C

N_EXP, TOP_K, N_GROUPS, TOPK_GROUPS = 64, 8, 8, 4
D_EXP = 512
ROUTED_SCALE = 2.5

MOE_CHUNKS = 3
GATHER_SRC_ROWS = 16384
CHUNK = 256
DELTA_CHUNK = 128
SSD_CHUNK = 256
LANES = 128
VMEM_LIMIT = 56 * 1024 * 1024
NEG = -1e30
LOG2E = 1.4426950408889634


def _cparams(*sem):
    return pltpu.CompilerParams(dimension_semantics=sem, vmem_limit_bytes=VMEM_LIMIT)


def _bdot(a, b):
    return jnp.dot(a.astype(BF16), b.astype(BF16), preferred_element_type=F32)


def _bdot_nt(a, b):
    return lax.dot_general(a.astype(BF16), b.astype(BF16), (((1,), (1,)), ((), ())), preferred_element_type=F32)


def _bdot_tn(a, b):
    return lax.dot_general(a.astype(BF16), b.astype(BF16), (((0,), (0,)), ((), ())), preferred_element_type=F32)


def _split3(a):
    hi = a.astype(BF16)
    r = a - hi.astype(F32)
    mid = r.astype(BF16)
    lo = (r - mid.astype(F32)).astype(BF16)
    return hi, mid, lo


def _dot_exact_rhs(a, b_exact, passes=3):
    hi, mid, lo = _split3(a)
    bb = b_exact.astype(BF16)
    d = lambda p: jnp.dot(p, bb, preferred_element_type=F32)
    return d(hi) + d(mid) + d(lo) if passes == 3 else d(hi) + d(mid)


def _dot_exact_lhs(a_exact, b):
    hi, mid, lo = _split3(b)
    aa = a_exact.astype(BF16)
    d = lambda p: jnp.dot(aa, p, preferred_element_type=F32)
    return d(hi) + d(mid) + d(lo)


def _silu(x):
    return x * jax.nn.sigmoid(x)


def _softplus(x):
    return jnp.maximum(x, 0.0) + jnp.log(1.0 + jnp.exp(-jnp.abs(x)))


def _group_of_block(i, tm, t_ctx, lat_len):
    return jnp.maximum(i * tm - t_ctx, -1) // lat_len + 1


def _mod_kernel(c_ref, w_ref, b_ref, o_ref):
    c = c_ref[...]
    o_ref[...] = _bdot(_silu(c), w_ref[...]) + b_ref[...]


def compute_mods(cvec, mod_w, mod_b):
    depth, d, n = mod_w.shape
    tn = 512
    return pl.pallas_call(
        _mod_kernel,
        grid=(depth, n // tn),
        in_specs=[pl.BlockSpec((8, d), lambda l, j: (0, 0)),
                  pl.BlockSpec((None, d, tn), lambda l, j: (l, 0, j)),
                  pl.BlockSpec((None, 1, tn), lambda l, j: (l, 0, j))],
        out_specs=pl.BlockSpec((None, 8, tn), lambda l, j: (l, 0, j)),
        out_shape=jax.ShapeDtypeStruct((depth, 8, n), F32),
        compiler_params=_cparams("arbitrary", "arbitrary"),
        name="mod_vectors",
    )(cvec, mod_w, mod_b.reshape(depth, 1, n))


def _modulate_kernel(x_ref, m_ref, *rest, sh, sc):
    o_ref = rest[-1]
    o_ref[...] = (x_ref[...] * (1.0 + m_ref[sc:sc + 1, :]) + m_ref[sh:sh + 1, :]).astype(o_ref.dtype)


def modulate(x_part, mod, sh, sc, *, row0, t_total, t_ctx, lat_len, out_buf=None):
    tp, d = x_part.shape
    tm = 512
    blk0 = row0 // tm
    in_specs = [pl.BlockSpec((tm, d), lambda i: (i, 0)),
                pl.BlockSpec((None, 6, d), lambda i: (_group_of_block(blk0 + i, tm, t_ctx, lat_len), 0, 0))]
    args = [x_part, mod]
    aliases = _alias_out(in_specs, args, out_buf)
    return pl.pallas_call(
        functools.partial(_modulate_kernel, sh=sh, sc=sc),
        grid=(tp // tm,),
        in_specs=in_specs,
        out_specs=pl.BlockSpec((tm, d), lambda i: (blk0 + i, 0)),
        out_shape=jax.ShapeDtypeStruct((t_total, d), BF16),
        input_output_aliases=aliases,
        compiler_params=_cparams("arbitrary"),
        name="modulate",
    )(*args)


def _matmul_kernel(x_ref, w_ref, o_ref, wbf_ref, *, valid_cols):
    @pl.when(pl.program_id(1) == 0)
    def _():
        wbf_ref[...] = w_ref[...].astype(BF16)

    y = jnp.dot(x_ref[...], wbf_ref[...], preferred_element_type=F32)
    if valid_cols is not None:
        col = lax.broadcasted_iota(jnp.int32, y.shape, 1)
        y = jnp.where(col < valid_cols, y, 0.0)
    o_ref[...] = y.astype(o_ref.dtype)


def matmul(x, w, *, tm, tn, n_out, col_block_off=0, valid_cols=None, out_dtype=F32, name="matmul"):
    m, k = x.shape
    return pl.pallas_call(
        functools.partial(_matmul_kernel, valid_cols=valid_cols),
        grid=(n_out // tn, m // tm),
        in_specs=[pl.BlockSpec((tm, k), lambda j, i: (i, 0)),
                  pl.BlockSpec((k, tn), lambda j, i: (0, j + col_block_off))],
        out_specs=pl.BlockSpec((tm, tn), lambda j, i: (i, j)),
        out_shape=jax.ShapeDtypeStruct((m, n_out), out_dtype),
        scratch_shapes=[pltpu.VMEM((k, tn), BF16)],
        compiler_params=_cparams("arbitrary", "arbitrary"),
        name=name,
    )(x, w)


def _tri_masks(n):
    r = lax.broadcasted_iota(jnp.int32, (n, n), 0)
    c = lax.broadcasted_iota(jnp.int32, (n, n), 1)
    return r, c


def _gate_kernel(raw_ref, p_ref, act_ref, cum_ref, *, mode):
    x = raw_ref[...]
    coef, bias, rev = p_ref[0:1, :], p_ref[1:2, :], p_ref[2:3, :]
    col = lax.broadcasted_iota(jnp.int32, x.shape, 1)
    xb = x + bias
    if mode == "even":
        act = jnp.where(col < 2 * H_A, jax.nn.sigmoid(xb),
                        jnp.where(col < 4 * H_A, coef * _softplus(xb),
                                  jnp.where(col < 4 * H_A + 2 * H_B, xb,
                                            jnp.minimum(xb, 0.0) - jnp.log(1.0 + jnp.exp(-jnp.abs(xb))))))
        to_sum = act
    else:
        act = _softplus(xb)
        to_sum = act * (coef * LOG2E)
    r, c = _tri_masks(CHUNK)
    lower = jnp.where(c <= r, 1.0, 0.0)
    upper = jnp.where(c >= r, 1.0, 0.0)
    if mode == "odd" and SSD_CHUNK != CHUNK:
        same = (r // SSD_CHUNK) == (c // SSD_CHUNK)
        lower, upper = jnp.where(same, lower, 0.0), jnp.where(same, upper, 0.0)
    cum = jnp.where(rev > 0.5, _dot_exact_lhs(upper, to_sum), _dot_exact_lhs(lower, to_sum))
    if mode == "even":
        same = (r // DELTA_CHUNK) == (c // DELTA_CHUNK)
        cum_d = jnp.where(rev > 0.5, _dot_exact_lhs(jnp.where(same, upper, 0.0), to_sum),
                          _dot_exact_lhs(jnp.where(same, lower, 0.0), to_sum))
        cum = jnp.where(col < 4 * H_A, cum_d, cum)
    act_ref[...] = act
    cum_ref[...] = cum


def gate_prep(raw, params, mode):
    t = raw.shape[0]
    return pl.pallas_call(
        functools.partial(_gate_kernel, mode=mode),
        grid=(t // CHUNK,),
        in_specs=[pl.BlockSpec((CHUNK, LANES), lambda i: (i, 0)),
                  pl.BlockSpec((8, LANES), lambda i: (0, 0))],
        out_specs=[pl.BlockSpec((CHUNK, LANES), lambda i: (i, 0))] * 2,
        out_shape=[jax.ShapeDtypeStruct((t, LANES), F32)] * 2,
        compiler_params=_cparams("arbitrary"),
        name="gate_prep_" + mode,
    )(raw, params)


def _conv_silu(x, cw_ref, cb_ref, period):
    n = x.shape[0]
    row = lax.broadcasted_iota(jnp.int32, x.shape, 0) % period
    prev = jnp.where(row == 0, 0.0, pltpu.roll(x, 1, 0))
    nxt = jnp.where(row == period - 1, 0.0, pltpu.roll(x, n - 1, 0))
    y = cb_ref[...] + prev * cw_ref[0:1, :] + x * cw_ref[1:2, :] + nxt * cw_ref[2:3, :]
    return _silu(y)


def _pick_col(blk, idx):
    lane = lax.broadcasted_iota(jnp.int32, blk.shape, 1)
    return jnp.sum(jnp.where(lane == idx, blk, 0.0), axis=1, keepdims=True)


def _dir_masks(rev, n=CHUNK):
    r, c = _tri_masks(n)
    if rev:
        return c >= r, c > r
    return c <= r, c < r


def _tri_inverse(lmat, rev):
    return _tri_inverse_many([lmat], [rev])[0]


def _tri_inverse_many(lmats, revs):
    n = lmats[0].shape[0]
    r, c = _tri_masks(n)
    eye = jnp.where(r == c, 1.0, 0.0)

    def off_mask(s, rev):
        same = (r // (2 * s)) == (c // (2 * s))
        r_hi = (r // s) % 2
        c_hi = (c // s) % 2
        return same & ((r_hi == 0) & (c_hi == 1) if rev else (r_hi == 1) & (c_hi == 0))

    masks = {rev: off_mask(1, rev) for rev in set(revs)}
    ts = [eye - jnp.where(masks[rev], lm, 0.0) for lm, rev in zip(lmats, revs)]
    s = 2
    while s < n:
        masks = {rev: off_mask(s, rev) for rev in set(revs)}
        ps = [_bdot(t, jnp.where(masks[rev], lm, 0.0)) for t, lm, rev in zip(ts, lmats, revs)]
        ts = [t - _bdot(p, t) for p, t in zip(ps, ts)]
        s *= 2
    return ts


def _delta_kernel(*refs, n_chunks, period, has_state, hb):
    (q_ref, k_ref, v_ref, z_ref, cwq, cwk, cwv, cbq, cbk, cbv, act_ref, cum_ref, cumt_ref, norm_ref) = refs[:14]
    rest = refs[14:]
    if has_state:
        s0_ref, *_aliased_out, o_ref, acc_ref = rest
        sout_ref = None
    else:
        o_ref, sout_ref, acc_ref = rest
    h0 = pl.program_id(1) * hb
    dc = DELTA_CHUNK

    q = _conv_silu(q_ref[...].astype(F32), cwq, cbq, period)
    k = _conv_silu(k_ref[...].astype(F32), cwk, cbk, period)
    v = _conv_silu(v_ref[...].astype(F32), cwv, cbv, period)

    triples = [(hh, d, ci) for hh in range(hb) for d in (0, 1) for ci in range(n_chunks)]
    pre = {}
    for hh in range(hb):
        hs = slice(hh * DK_A, (hh + 1) * DK_A)
        qh, kh = q[:, hs], k[:, hs]
        qh = qh * lax.rsqrt(jnp.sum(qh * qh, axis=1, keepdims=True) + RMS_EPS) * (DK_A ** -0.5)
        kh = kh * lax.rsqrt(jnp.sum(kh * kh, axis=1, keepdims=True) + RMS_EPS)
        for d in (0, 1):
            m_incl, m_strict = _dir_masks(d == 1, dc)
            grow_all = cumt_ref[pl.ds(2 * H_A + d * H_A + h0 + hh, 1), :]
            for ci in range(n_chunks):
                sl = slice(ci * dc, (ci + 1) * dc)
                qc, kc, vc = qh[sl], kh[sl], v[sl, hs]
                beta = _pick_col(act_ref[sl, :], d * H_A + h0 + hh)
                gcol = _pick_col(cum_ref[sl, :], 2 * H_A + d * H_A + h0 + hh)
                grow = grow_all[:, sl]
                decay = jnp.exp(jnp.where(m_incl, gcol - grow, NEG))
                kb = kc * beta
                pre[hh, d, ci] = dict(
                    qc=qc, kc=kc, kb=kb, vb=vc * beta, gcol=gcol, decay=decay,
                    lmat=_bdot_nt(kb, kc) * jnp.where(m_strict, decay, 0.0),
                    attn=_bdot_nt(qc, kc) * decay)
    tinvs = _tri_inverse_many([pre[t]["lmat"] for t in triples], [t[1] == 1 for t in triples])
    for t, tinv in zip(triples, tinvs):
        p = pre[t]
        p["u"] = _bdot(tinv, p["vb"])
        if has_state or n_chunks > 1:
            p["w"] = _bdot(tinv, p["kb"] * jnp.exp(p["gcol"]))

    for hh in range(hb):
        hs = slice(hh * DK_A, (hh + 1) * DK_A)
        for d in (0, 1):
            rev = d == 1
            state = s0_ref[d, hh] if has_state else None
            order = range(n_chunks - 1, -1, -1) if rev else range(n_chunks)
            for ci in order:
                sl = slice(ci * dc, (ci + 1) * dc)
                p = pre[hh, d, ci]
                u, gcol = p["u"], p["gcol"]
                if state is not None:
                    u = u - _bdot(p["w"], state)
                o = _bdot(p["attn"], u)
                if state is not None:
                    o = o + _bdot(p["qc"] * jnp.exp(gcol), state)
                glast = gcol[0:1, :] if rev else gcol[dc - 1:dc, :]
                upd = _bdot_tn(p["kc"] * jnp.exp(glast - gcol), u)
                state = upd if state is None else state * jnp.exp(glast) + upd
                if rev:
                    acc_ref[sl, hs] = acc_ref[sl, hs] + o
                else:
                    acc_ref[sl, hs] = o
            if sout_ref is not None:
                sout_ref[d, hh] = state

    z = z_ref[...].astype(F32)
    for hh in range(hb):
        hs = slice(hh * DK_A, (hh + 1) * DK_A)
        o = acc_ref[:, hs]
        o = o * lax.rsqrt(jnp.mean(o * o, axis=1, keepdims=True) + RMS_EPS) * norm_ref[...]
        o_ref[:, hs] = (o * _silu(z[:, hs])).astype(o_ref.dtype)


def _alias_out(in_specs, args, out_buf):
    if out_buf is None:
        return {}
    in_specs.append(pl.BlockSpec(memory_space=pl.ANY))
    args.append(out_buf)
    return {len(args) - 1: 0}


def delta_mixer(proj, conv_w, conv_b, act, cum, cumt, norm, state, *, row0, n_seq, seq_len, period, hb,
                out_buf=None):
    assert row0 % seq_len == 0 and H_A % hb == 0
    n_chunks = seq_len // DELTA_CHUNK
    rb0 = row0 // seq_len
    has_state = state is not None
    w = hb * DK_A
    nq = H_A // hb
    col = lambda off: (lambda s, h: (rb0 + s, off + h))
    cw = lambda off: (lambda s, h: (0, off + h))
    in_specs = [pl.BlockSpec((seq_len, w), col(0)), pl.BlockSpec((seq_len, w), col(nq)),
                pl.BlockSpec((seq_len, w), col(2 * nq)), pl.BlockSpec((seq_len, w), col(3 * nq)),
                pl.BlockSpec((3, w), cw(0)), pl.BlockSpec((3, w), cw(nq)), pl.BlockSpec((3, w), cw(2 * nq)),
                pl.BlockSpec((1, w), cw(0)), pl.BlockSpec((1, w), cw(nq)), pl.BlockSpec((1, w), cw(2 * nq)),
                pl.BlockSpec((seq_len, LANES), lambda s, h: (rb0 + s, 0)),
                pl.BlockSpec((seq_len, LANES), lambda s, h: (rb0 + s, 0)),
                pl.BlockSpec((LANES, seq_len), lambda s, h: (0, rb0 + s)),
                pl.BlockSpec((1, DV_A), lambda s, h: (0, 0))]
    args = [proj, proj, proj, proj, conv_w, conv_w, conv_w, conv_b, conv_b, conv_b, act, cum, cumt, norm]
    o_spec = pl.BlockSpec((seq_len, w), lambda s, h: (rb0 + s, h))
    o_shape = jax.ShapeDtypeStruct((proj.shape[0], H_A * DV_A), BF16)
    st_spec = pl.BlockSpec((None, 2, hb, DK_A, DV_A), lambda s, h: (s, 0, h, 0, 0))
    if has_state:
        in_specs.append(st_spec)
        args.append(state)
        out_specs, out_shape = o_spec, o_shape
    else:
        out_specs = [o_spec, st_spec]
        out_shape = [o_shape, jax.ShapeDtypeStruct((n_seq, 2, H_A, DK_A, DV_A), F32)]
    aliases = _alias_out(in_specs, args, out_buf)
    return pl.pallas_call(
        functools.partial(_delta_kernel, n_chunks=n_chunks, period=period, has_state=has_state, hb=hb),
        grid=(n_seq, H_A // hb),
        in_specs=in_specs, out_specs=out_specs, out_shape=out_shape, input_output_aliases=aliases,
        scratch_shapes=[pltpu.VMEM((seq_len, w), F32)],
        compiler_params=_cparams("arbitrary", "arbitrary"),
        name="delta_lat" if has_state else "delta_ctx",
    )(*args)


def even_gate_params(a_log, dt_bias, b_i, b_f):
    zeros_a = jnp.zeros((2 * H_A,), F32)
    coef = jnp.concatenate([zeros_a, -jnp.exp(a_log.astype(F32)).reshape(-1), jnp.zeros((4 * H_B,), F32)])
    bias = jnp.concatenate([zeros_a, dt_bias.reshape(-1), b_i.reshape(-1), b_f.reshape(-1)]).astype(F32)
    rev = jnp.concatenate([jnp.repeat(jnp.arange(2, dtype=F32), H_A)] * 2 + [jnp.repeat(jnp.arange(2, dtype=F32), H_B)] * 2)
    p = jnp.stack([coef, bias, rev])
    return jnp.pad(p, ((0, 5), (0, LANES - EV_GATES)))


def _mlstm_kernel(*refs, n_chunks, has_state):
    (q_ref, k_ref, v_ref, og_ref, act_ref, actt_ref, cum_ref, cumt_ref, norm_ref) = refs[:9]
    rest = refs[9:]
    if has_state:
        c0_ref, n0_ref, m0_ref, *_aliased_out, o_ref, acc_ref = rest
    else:
        o_ref, cout_ref, nm_ref, acc_ref = rest
    h = pl.program_id(1)
    i_col0, f_col0 = 4 * H_A, 4 * H_A + 2 * H_B

    for d in (0, 1):
        rev = d == 1
        m_incl, _ = _dir_masks(rev)
        if has_state:
            cm, nv, m = c0_ref[d], n0_ref[d], m0_ref[d]
        else:
            cm, nv, m = None, None, jnp.zeros((1, 1), F32)
        order = range(n_chunks - 1, -1, -1) if rev else range(n_chunks)
        for ci in order:
            sl = slice(ci * CHUNK, (ci + 1) * CHUNK)
            qc = q_ref[sl, :].astype(F32) * (DK_B ** -0.5)
            kc = k_ref[sl, :].astype(F32)
            vc = v_ref[sl, :].astype(F32)
            li_col = _pick_col(act_ref[sl, :], i_col0 + d * H_B + h)
            li_row = actt_ref[pl.ds(i_col0 + d * H_B + h, 1), sl]
            b_col = _pick_col(cum_ref[sl, :], f_col0 + d * H_B + h)
            b_row = cumt_ref[pl.ds(f_col0 + d * H_B + h, 1), sl]
            dlog = jnp.where(m_incl, b_col - b_row + li_row, NEG)
            inter = b_col + m
            m_q = jnp.maximum(inter, jnp.max(dlog, axis=1, keepdims=True))
            s = _bdot_nt(qc, kc) * jnp.exp(dlog - m_q)
            num = _bdot(s, vc)
            den = jnp.sum(s, axis=1, keepdims=True)
            if cm is not None:
                w_inter = jnp.exp(inter - m_q)
                num = num + w_inter * _bdot(qc, cm)
                den = den + w_inter * jnp.sum(qc * nv, axis=1, keepdims=True)
            hout = num / jnp.maximum(jnp.abs(den), jnp.exp(-m_q))
            b_last = b_col[0:1, :] if rev else b_col[CHUNK - 1:CHUNK, :]
            wlog = b_last - b_col + li_col
            m_new = jnp.maximum(b_last + m, jnp.max(wlog, axis=0, keepdims=True))
            kw = kc * jnp.exp(wlog - m_new)
            c_upd = _bdot_tn(kw, vc)
            n_upd = jnp.sum(kw, axis=0, keepdims=True)
            if cm is not None:
                sc = jnp.exp(b_last + m - m_new)
                cm, nv = sc * cm + c_upd, sc * nv + n_upd
            else:
                cm, nv = c_upd, n_upd
            m = m_new
            if rev:
                acc_ref[sl, :] = acc_ref[sl, :] + hout
            else:
                acc_ref[sl, :] = hout
        if not has_state:
            cout_ref[d] = cm
            nm_ref[d, 0:1, :] = nv
            nm_ref[d, 1:2, :] = jnp.broadcast_to(m, (1, DK_B))
            nm_ref[d, 2:8, :] = jnp.zeros((6, DK_B), F32)

    o = acc_ref[...]
    o = o * lax.rsqrt(jnp.mean(o * o, axis=1, keepdims=True) + RMS_EPS) * norm_ref[...]
    o_ref[...] = (o * jax.nn.sigmoid(og_ref[...].astype(F32))).astype(o_ref.dtype)


def mlstm_mixer(proj, act, actt, cum, cumt, norm, state, *, row0, n_seq, seq_len, out_buf=None):
    assert row0 % seq_len == 0
    n_chunks = seq_len // CHUNK
    rb0 = row0 // seq_len
    has_state = state is not None
    q0 = (CONV_A + H_A * DV_A) // LANES
    k0 = q0 + H_B
    v0 = (CONV_A + H_A * DV_A + 2 * H_B * DK_B) // DV_B
    o0 = v0 + H_B
    col = lambda off: (lambda s, h: (rb0 + s, off + h))
    in_specs = [pl.BlockSpec((seq_len, DK_B), col(q0)), pl.BlockSpec((seq_len, DK_B), col(k0)),
                pl.BlockSpec((seq_len, DV_B), col(v0)), pl.BlockSpec((seq_len, DV_B), col(o0)),
                pl.BlockSpec((seq_len, LANES), lambda s, h: (rb0 + s, 0)),
                pl.BlockSpec((LANES, seq_len), lambda s, h: (0, rb0 + s)),
                pl.BlockSpec((seq_len, LANES), lambda s, h: (rb0 + s, 0)),
                pl.BlockSpec((LANES, seq_len), lambda s, h: (0, rb0 + s)),
                pl.BlockSpec((1, DV_B), lambda s, h: (0, 0))]
    args = [proj, proj, proj, proj, act, actt, cum, cumt, norm]
    o_spec = pl.BlockSpec((seq_len, DV_B), lambda s, h: (rb0 + s, h))
    o_shape = jax.ShapeDtypeStruct((proj.shape[0], H_B * DV_B), BF16)
    st_idx = lambda s, h: (s, 0, h, 0, 0)
    if has_state:
        c0, n0, m0 = state
        in_specs += [pl.BlockSpec((None, 2, None, DK_B, DV_B), st_idx),
                     pl.BlockSpec((None, 2, None, 1, DK_B), st_idx),
                     pl.BlockSpec((None, 2, None, 1, 1), st_idx)]
        args += [c0, n0.reshape(n_seq, 2, H_B, 1, DK_B), m0.reshape(n_seq, 2, H_B, 1, 1)]
        out_specs, out_shape = o_spec, o_shape
    else:
        out_specs = [o_spec, pl.BlockSpec((None, 2, None, DK_B, DV_B), st_idx),
                     pl.BlockSpec((None, 2, None, 8, DK_B), st_idx)]
        out_shape = [o_shape, jax.ShapeDtypeStruct((n_seq, 2, H_B, DK_B, DV_B), F32),
                     jax.ShapeDtypeStruct((n_seq, 2, H_B, 8, DK_B), F32)]
    aliases = _alias_out(in_specs, args, out_buf)
    return pl.pallas_call(
        functools.partial(_mlstm_kernel, n_chunks=n_chunks, has_state=has_state),
        grid=(n_seq, H_B),
        in_specs=in_specs, out_specs=out_specs, out_shape=out_shape, input_output_aliases=aliases,
        scratch_shapes=[pltpu.VMEM((seq_len, DV_B), F32)],
        compiler_params=_cparams("arbitrary", "arbitrary"),
        name="mlstm_lat" if has_state else "mlstm_ctx",
    )(*args)


def odd_gate_params(a_log, dt_bias):
    coef = -jnp.exp(a_log.astype(F32)).reshape(-1)
    bias = dt_bias.astype(F32).reshape(-1)
    rev = jnp.repeat(jnp.arange(2, dtype=F32), H_C)
    return jnp.pad(jnp.stack([coef, bias, rev]), ((0, 5), (0, 0)))


def _ssd_kernel(*refs, n_chunks, period, has_state):
    (z_ref, x_ref, b_ref, c_ref, cwx, cwb, cwc, cbx, cbb, cbc, dt_ref, cum_ref, cumt_ref, dskip_ref, norm_ref) = refs[:15]
    rest = refs[15:]
    if has_state:
        s0_ref, *_aliased_out, o_ref, acc_ref = rest
        sout_ref = None
    else:
        o_ref, sout_ref, acc_ref = rest
    g = pl.program_id(1)

    x = _conv_silu(x_ref[...].astype(F32), cwx, cbx, period)
    bm = _conv_silu(b_ref[...].astype(F32), cwb, cbb, period)
    cm = _conv_silu(c_ref[...].astype(F32), cwc, cbc, period)

    er = lax.broadcasted_iota(jnp.int32, (LANES, GW_C), 0)
    ec = lax.broadcasted_iota(jnp.int32, (LANES, GW_C), 1)
    tr = lax.broadcasted_iota(jnp.int32, (GW_C, LANES), 0)
    tc = lax.broadcasted_iota(jnp.int32, (GW_C, LANES), 1)
    sc_ = SSD_CHUNK
    lane_in_tile = lax.broadcasted_iota(jnp.int32, (sc_, LANES), 1)

    for d in (0, 1):
        rev = d == 1
        m_incl, _ = _dir_masks(rev, sc_)
        lo, hi, full = slice(0, sc_ // 2), slice(sc_ // 2, sc_), slice(0, sc_)
        if sc_ // 2 >= LANES:
            half_blocks = ((lo, full), (hi, hi)) if rev else ((lo, lo), (hi, full))
        else:
            half_blocks = ((full, full),)
        col0 = d * H_C + g * HG_C
        expand = jnp.where(er == col0 + ec // P_C, 1.0, 0.0)
        expand_t = tc == col0 + tr // P_C
        state = s0_ref[d].reshape(GW_C, N_C) if has_state else None
        order = range(n_chunks - 1, -1, -1) if rev else range(n_chunks)
        crows = [cumt_ref[pl.ds(col0 + hh, 1), :] for hh in range(HG_C)]
        for ci in order:
            sl = slice(ci * sc_, (ci + 1) * sc_)
            xc, bc, cc = x[sl], bm[sl], cm[sl]
            cum_blk = cum_ref[sl, :]
            cum_last = cum_blk[0:1, :] if rev else cum_blk[sc_ - 1:sc_, :]
            xdt = xc * _dot_exact_rhs(dt_ref[sl, :], expand, passes=2)
            scores_bf = _bdot_nt(cc, bc).astype(BF16)
            for hp in range(HG_C // 2):
                ps = slice(hp * LANES, (hp + 1) * LANES)
                xpair = xdt[:, ps]
                ypair = None
                for sub in (0, 1):
                    hh = 2 * hp + sub
                    cb = _pick_col(cum_blk, col0 + hh)
                    crow = crows[hh][:, sl]
                    mine = (lane_in_tile < P_C) if sub == 0 else (lane_in_tile >= P_C)
                    rhs = jnp.where(mine, xpair, 0.0)
                    parts = []
                    for rows, cols in half_blocks:
                        seg = jnp.exp2(jnp.where(m_incl[rows, cols], cb[rows] - crow[:, cols], NEG))
                        parts.append(_bdot(scores_bf[rows, cols] * seg.astype(BF16), rhs[cols]))
                    y = jnp.concatenate(parts, axis=0)
                    ypair = y if ypair is None else ypair + y
                if rev:
                    acc_ref[sl, ps] = acc_ref[sl, ps] + ypair
                else:
                    acc_ref[sl, ps] = ypair
            if state is not None:
                y_in = _bdot_nt(cc, state) * _dot_exact_rhs(jnp.exp2(cum_blk), expand, passes=2)
                acc_ref[sl, :] = acc_ref[sl, :] + y_in
            dend = _dot_exact_rhs(jnp.exp2(jnp.minimum(cum_last - cum_blk, 0.0)), expand, passes=2)
            upd = _bdot_tn(xdt * dend, bc)
            if state is not None:
                tot = jnp.sum(jnp.where(expand_t, jnp.broadcast_to(cum_last, (GW_C, LANES)), 0.0), axis=1, keepdims=True)
                state = state * jnp.exp2(tot) + upd
            else:
                state = upd
        if sout_ref is not None:
            sout_ref[d] = state.reshape(HG_C, P_C, N_C)

    y = acc_ref[...] + dskip_ref[...] * x
    y = y * _silu(z_ref[...].astype(F32))
    y = y * lax.rsqrt(jnp.mean(y * y, axis=1, keepdims=True) + RMS_EPS) * norm_ref[...]
    o_ref[...] = y.astype(o_ref.dtype)


def ssd_mixer(proj, conv_w, conv_b, dt, cum, cumt, dskip, norm, state, *, row0, n_seq, seq_len, period, out_buf=None):
    assert row0 % seq_len == 0
    n_chunks = seq_len // SSD_CHUNK
    rb0 = row0 // seq_len
    has_state = state is not None
    xb0 = D_INNER // GW_C
    bb0 = 2 * D_INNER // N_C
    cb0 = bb0 + G_C
    wb0 = D_INNER // N_C
    wc0 = wb0 + G_C
    col = lambda off: (lambda s, g: (rb0 + s, off + g))
    cw = lambda off: (lambda s, g: (0, off + g))
    in_specs = [pl.BlockSpec((seq_len, GW_C), col(0)), pl.BlockSpec((seq_len, GW_C), col(xb0)),
                pl.BlockSpec((seq_len, N_C), col(bb0)), pl.BlockSpec((seq_len, N_C), col(cb0)),
                pl.BlockSpec((3, GW_C), cw(0)), pl.BlockSpec((3, N_C), cw(wb0)), pl.BlockSpec((3, N_C), cw(wc0)),
                pl.BlockSpec((1, GW_C), cw(0)), pl.BlockSpec((1, N_C), cw(wb0)), pl.BlockSpec((1, N_C), cw(wc0)),
                pl.BlockSpec((seq_len, LANES), lambda s, g: (rb0 + s, 0)),
                pl.BlockSpec((seq_len, LANES), lambda s, g: (rb0 + s, 0)),
                pl.BlockSpec((LANES, seq_len), lambda s, g: (0, rb0 + s)),
                pl.BlockSpec((1, GW_C), cw(0)), pl.BlockSpec((1, GW_C), cw(0))]
    args = [proj, proj, proj, proj, conv_w, conv_w, conv_w, conv_b, conv_b, conv_b, dt, cum, cumt, dskip, norm]
    o_spec = pl.BlockSpec((seq_len, GW_C), lambda s, g: (rb0 + s, g))
    o_shape = jax.ShapeDtypeStruct((proj.shape[0], D_INNER), BF16)
    st_spec = pl.BlockSpec((None, 2, HG_C, P_C, N_C), lambda s, g: (s, 0, g, 0, 0))
    if has_state:
        in_specs.append(st_spec)
        args.append(state)
        out_specs, out_shape = o_spec, o_shape
    else:
        out_specs = [o_spec, st_spec]
        out_shape = [o_shape, jax.ShapeDtypeStruct((n_seq, 2, H_C, P_C, N_C), F32)]
    aliases = _alias_out(in_specs, args, out_buf)
    return pl.pallas_call(
        functools.partial(_ssd_kernel, n_chunks=n_chunks, period=period, has_state=has_state),
        grid=(n_seq, G_C),
        in_specs=in_specs, out_specs=out_specs, out_shape=out_shape, input_output_aliases=aliases,
        scratch_shapes=[pltpu.VMEM((seq_len, GW_C), F32)],
        compiler_params=_cparams("arbitrary", "arbitrary"),
        name="ssd_lat" if has_state else "ssd_ctx",
    )(*args)


def _dot3(a, b):
    a_hi = a.astype(BF16)
    a_lo = (a - a_hi.astype(F32)).astype(BF16)
    b_hi = b.astype(BF16)
    b_lo = (b - b_hi.astype(F32)).astype(BF16)
    d = lambda p, q: jnp.dot(p, q, preferred_element_type=F32)
    return d(a_hi, b_hi) + (d(a_hi, b_lo) + d(a_lo, b_hi))


def _dot3_nt(a, b):
    a_hi = a.astype(BF16)
    a_lo = (a - a_hi.astype(F32)).astype(BF16)
    b_hi = b.astype(BF16)
    b_lo = (b - b_hi.astype(F32)).astype(BF16)
    d = lambda p, q: lax.dot_general(p, q, (((1,), (1,)), ((), ())), preferred_element_type=F32)
    return d(a_hi, b_hi) + (d(a_hi, b_lo) + d(a_lo, b_hi))


def _resid_ln_kernel(*refs, n_y, lhs_widths, n_gathered, gate, sh, sc, want_h, want_logits, split_in, split_out,
                     n_ctx_blocks):
    in_ctx = pl.program_id(0) < n_ctx_blocks
    if split_in:
        x_in = jnp.where(in_ctx, refs[0][...], refs[1][...])
        refs = refs[1:]
    else:
        x_in = refs[0][...]
    y_refs = refs[1:1 + n_y]
    rest = list(refs[1 + n_y:])
    lhs_refs = [rest.pop(0) for _ in lhs_widths]
    pw_ref = rest.pop(0) if lhs_widths else None
    m_ref, mn_ref, g_ref, b_ref = (rest.pop(0) for _ in range(4))
    if n_gathered:
        gath_ref, gw_ref = rest.pop(0), rest.pop(0)
    rw_ref = rest.pop(0) if want_logits else None
    xo_refs = [rest.pop(0) for _ in range(2 if split_out else 1)]
    y = None
    for r in y_refs:
        y = r[...].astype(F32) if y is None else y + r[...].astype(F32)
    off = 0
    for lhs_ref, width in zip(lhs_refs, lhs_widths):
        part = jnp.dot(lhs_ref[...], pw_ref[off:off + width, :], preferred_element_type=F32)
        y = part if y is None else y + part
        off += width
    for kk in range(n_gathered):
        y = y + gw_ref[:, kk:kk + 1] * gath_ref[kk].astype(F32)
    v = ALPHA * x_in + m_ref[gate:gate + 1, :] * y
    mu = jnp.mean(v, axis=1, keepdims=True)
    vc = v - mu
    var = jnp.mean(vc * vc, axis=1, keepdims=True)
    xn = vc * lax.rsqrt(var + LN_EPS) * g_ref[...] + b_ref[...]
    if split_out:
        @pl.when(in_ctx)
        def _():
            xo_refs[0][...] = xn

        @pl.when(jnp.logical_not(in_ctx))
        def _():
            xo_refs[1][...] = xn
    else:
        xo_refs[0][...] = xn
    if want_h:
        hm = xn * (1.0 + mn_ref[sc:sc + 1, :]) + mn_ref[sh:sh + 1, :]
        rest.pop(0)[...] = hm.astype(BF16)
        if want_logits:
            rest.pop(0)[...] = _dot3_nt(rw_ref[...], hm)


def resid_ln(x, ys, mod, mod_next, ln_g, ln_b, router_w, *, gate, sh, sc, want_h, t_ctx, lat_len, gathered=None,
             h_rows=None, proj=None, tm=256, split_out=False):
    split_in = isinstance(x, tuple)
    d = x[0].shape[1] if split_in else x.shape[1]
    t = x[0].shape[0] + x[1].shape[0] if split_in else x.shape[0]
    n_ctx_blocks = t_ctx // tm
    ctx_blk = lambda i: (jnp.minimum(i, n_ctx_blocks - 1), 0)
    lat_blk = lambda i: (jnp.maximum(i - n_ctx_blocks, 0), 0)
    want_logits = router_w is not None
    grp = lambda i: (_group_of_block(i, tm, t_ctx, lat_len), 0, 0)
    row = pl.BlockSpec((tm, d), lambda i: (i, 0))
    vec = pl.BlockSpec((1, d), lambda i: (0, 0))
    if split_in:
        in_specs = [pl.BlockSpec((tm, d), ctx_blk), pl.BlockSpec((tm, d), lat_blk)] + [row] * len(ys)
        args = [*x, *ys]
    else:
        in_specs = [row] * (1 + len(ys))
        args = [x, *ys]
    lhs_widths = ()
    if proj is not None:
        lhs_list, pw = proj
        lhs_widths = tuple(a.shape[1] for a in lhs_list)
        in_specs += [pl.BlockSpec((tm, wd), lambda i: (i, 0)) for wd in lhs_widths]
        in_specs.append(pl.BlockSpec(pw.shape, lambda i: (0, 0), pipeline_mode=pl.Buffered(1)))
        args += [*lhs_list, pw]
    in_specs += [pl.BlockSpec((None, 6, d), grp), pl.BlockSpec((None, 6, d), grp), vec, vec]
    args += [mod, mod_next, ln_g.reshape(1, d), ln_b.reshape(1, d)]
    n_gathered = 0
    if gathered is not None:
        n_gathered = gathered[0].shape[0]
        in_specs += [pl.BlockSpec((n_gathered, tm, d), lambda i: (0, i, 0)),
                     pl.BlockSpec((tm, n_gathered), lambda i: (i, 0))]
        args += list(gathered)
    if split_out:
        out_specs = [pl.BlockSpec((tm, d), ctx_blk), pl.BlockSpec((tm, d), lat_blk)]
        out_shape = [jax.ShapeDtypeStruct((t_ctx, d), F32), jax.ShapeDtypeStruct((t - t_ctx, d), F32)]
    else:
        out_specs, out_shape = [row], [jax.ShapeDtypeStruct((t, d), F32)]
    if want_logits:
        n_e = router_w.shape[1]
        in_specs.append(pl.BlockSpec((n_e, d), lambda i: (0, 0)))
        args.append(router_w.T)
    if want_h:
        out_specs.append(row)
        out_shape.append(jax.ShapeDtypeStruct((h_rows or t, d), BF16))
    if want_logits:
        out_specs.append(pl.BlockSpec((n_e, tm), lambda i: (0, i)))
        out_shape.append(jax.ShapeDtypeStruct((n_e, t), F32))
    return pl.pallas_call(
        functools.partial(_resid_ln_kernel, n_y=len(ys), lhs_widths=lhs_widths, n_gathered=n_gathered, gate=gate, sh=sh,
                          sc=sc, want_h=want_h, want_logits=want_logits, split_in=split_in, split_out=split_out,
                          n_ctx_blocks=n_ctx_blocks),
        grid=(t // tm,),
        in_specs=in_specs, out_specs=out_specs, out_shape=out_shape,
        compiler_params=_cparams("arbitrary"),
        name="resid_ln",
    )(*args)


def _ffn_kernel(be_ref, nx_ref, nu_ref, x_ref, wg_hbm, wu_hbm, wd_hbm, *rest, layer):
    *_aliased_out, o_ref, g_f32, u_f32, d_f32, g_bf, u_bf, d_bf, sem = rest
    b = pl.program_id(0)

    def copies(e):
        return (pltpu.make_async_copy(wg_hbm.at[layer, e], g_f32, sem.at[0]),
                pltpu.make_async_copy(wu_hbm.at[layer, e], u_f32, sem.at[1]),
                pltpu.make_async_copy(wd_hbm.at[layer, e], d_f32, sem.at[2]))

    @pl.when(b < nu_ref[0])
    def _():
        e = be_ref[b]

        @pl.when(b == 0)
        def _():
            for cp in copies(e):
                cp.start()

        @pl.when((b == 0) | (e != be_ref[jnp.maximum(b - 1, 0)]))
        def _():
            for cp in copies(e):
                cp.wait()
            g_bf[...] = g_f32[...].astype(BF16)
            u_bf[...] = u_f32[...].astype(BF16)
            d_bf[...] = d_f32[...].astype(BF16)
            nxt = nx_ref[b]

            @pl.when(nxt >= 0)
            def _():
                for cp in copies(nxt):
                    cp.start()

        x = x_ref[...]
        hg = jnp.dot(x, g_bf[...], preferred_element_type=F32)
        hu = jnp.dot(x, u_bf[...], preferred_element_type=F32)
        a = (_silu(hg) * hu).astype(BF16)
        o_ref[...] = jnp.dot(a, d_bf[...], preferred_element_type=F32).astype(o_ref.dtype)


def expert_ffn(xs, blk_e, next_e, n_used, w_gate, w_up, w_down, layer, *, tm, out_rows=None, out_block0=0,
               out_buf=None, out_dtype=BF16, name="expert_ffn"):
    d = xs.shape[1]
    de = w_gate.shape[3]
    n_blk = blk_e.shape[0]
    in_specs = [pl.BlockSpec((tm, d), lambda b, be, nx, nu: (b, 0))] + [pl.BlockSpec(memory_space=pl.ANY)] * 3
    args = [blk_e, next_e, n_used, xs, w_gate, w_up, w_down]
    aliases = _alias_out(in_specs, args, out_buf)
    grid_spec = pltpu.PrefetchScalarGridSpec(
        num_scalar_prefetch=3,
        grid=(n_blk,),
        in_specs=in_specs,
        out_specs=pl.BlockSpec((tm, d), lambda b, be, nx, nu: (out_block0 + b, 0)),
        scratch_shapes=[pltpu.VMEM((d, de), F32), pltpu.VMEM((d, de), F32), pltpu.VMEM((de, d), F32),
                        pltpu.VMEM((d, de), BF16), pltpu.VMEM((d, de), BF16), pltpu.VMEM((de, d), BF16),
                        pltpu.SemaphoreType.DMA((3,))],
    )
    return pl.pallas_call(
        functools.partial(_ffn_kernel, layer=layer),
        grid_spec=grid_spec,
        out_shape=jax.ShapeDtypeStruct((out_rows or n_blk * tm, d), out_dtype),
        input_output_aliases=aliases,
        compiler_params=_cparams("arbitrary"),
        name=name,
    )(*args)


ROUTE_TM = 512
GROUP_SIZE = N_EXP // N_GROUPS


def _first_argmax(v, idx, axis, sentinel):
    mx = jnp.max(v, axis=axis, keepdims=True)
    return mx, jnp.min(jnp.where(v == mx, idx, sentinel), axis=axis, keepdims=True)


def _route_kernel(lt_ref, bias_ref, idx_ref, w_ref, rank_ref, cnt_ref, carry_ref):
    i = pl.program_id(0)
    tm = lt_ref.shape[1]

    @pl.when(i == 0)
    def _():
        carry_ref[...] = jnp.zeros_like(carry_ref)

    scores = jax.nn.sigmoid(lt_ref[...])
    biased = scores + bias_ref[...]
    b3 = biased.reshape(N_GROUPS, GROUP_SIZE, tm)
    mem = lax.broadcasted_iota(jnp.int32, b3.shape, 1).astype(F32)
    m1, first = _first_argmax(b3, mem, 1, float(GROUP_SIZE))
    m2 = jnp.max(jnp.where(mem == first, -jnp.inf, b3), axis=1, keepdims=True)
    gs = (m1 + m2).reshape(N_GROUPS, tm)
    gi = lax.broadcasted_iota(jnp.int32, gs.shape, 0).astype(F32)
    gsel = jnp.zeros(gs.shape, F32)
    cur = gs
    for _ in range(TOPK_GROUPS):
        _, pick = _first_argmax(cur, gi, 0, float(N_GROUPS))
        hit = gi == pick
        gsel = jnp.where(hit, 1.0, gsel)
        cur = jnp.where(hit, -jnp.inf, cur)
    masked = jnp.where(gsel.reshape(N_GROUPS, 1, tm) > 0.5, b3, -jnp.inf).reshape(N_EXP, tm)

    ei = lax.broadcasted_iota(jnp.int32, masked.shape, 0).astype(F32)
    picks, sel_scores = [], []
    chosen = jnp.zeros(masked.shape, F32)
    cur = masked
    for _ in range(TOP_K):
        _, pick = _first_argmax(cur, ei, 0, float(N_EXP))
        hit = ei == pick
        picks.append(pick)
        sel_scores.append(jnp.sum(jnp.where(hit, scores, 0.0), axis=0, keepdims=True))
        chosen = jnp.where(hit, 1.0, chosen)
        cur = jnp.where(hit, -jnp.inf, cur)

    r, c = _tri_masks(tm)
    before = jnp.where(r < c, 1.0, 0.0).astype(BF16)
    rank = jnp.dot(chosen.astype(BF16), before, preferred_element_type=F32) + carry_ref[...]
    carry_ref[...] = carry_ref[...] + jnp.sum(chosen, axis=1, keepdims=True)
    cnt_ref[...] = carry_ref[...]

    total = sel_scores[0]
    for s in sel_scores[1:]:
        total = total + s
    for k in range(TOP_K):
        idx_ref[k:k + 1, :] = picks[k].astype(jnp.int32)
        w_ref[k:k + 1, :] = sel_scores[k] / total * ROUTED_SCALE
        rank_ref[k:k + 1, :] = jnp.sum(jnp.where(ei == picks[k], rank, 0.0), axis=0, keepdims=True).astype(jnp.int32)


def route(logits_t, router_bias):
    n_e, t = logits_t.shape
    tm = ROUTE_TM
    kt = pl.BlockSpec((TOP_K, tm), lambda i: (0, i))
    return pl.pallas_call(
        _route_kernel,
        grid=(t // tm,),
        in_specs=[pl.BlockSpec((n_e, tm), lambda i: (0, i)), pl.BlockSpec((n_e, 1), lambda i: (0, 0))],
        out_specs=[kt, kt, kt, pl.BlockSpec((n_e, 1), lambda i: (0, 0))],
        out_shape=[jax.ShapeDtypeStruct((TOP_K, t), jnp.int32), jax.ShapeDtypeStruct((TOP_K, t), F32),
                   jax.ShapeDtypeStruct((TOP_K, t), jnp.int32), jax.ShapeDtypeStruct((n_e, 1), F32)],
        scratch_shapes=[pltpu.VMEM((n_e, 1), F32)],
        compiler_params=_cparams("arbitrary"),
        name="route",
    )(logits_t, router_bias.reshape(n_e, 1))


def moe(h, logits_t, router_bias, e_gate, e_up, e_down, s_gate, s_up, s_down, layer):
    d = h.shape[1]
    t = logits_t.shape[1]
    tm = 256
    top_e, wts, rank, counts = route(logits_t, router_bias)
    counts = counts.reshape(-1).astype(jnp.int32)
    n_assign = t * TOP_K
    padded = (counts + tm - 1) // tm * tm
    pad_end = jnp.cumsum(padded)
    pad_start = pad_end - padded
    n_blk = n_assign // tm + N_EXP
    blk_first = jnp.arange(n_blk, dtype=jnp.int32) * tm
    blk_e = jnp.minimum(jnp.sum((pad_end[None, :] <= blk_first[:, None]).astype(jnp.int32), axis=1), N_EXP - 1)
    n_used = (pad_end[-1] // tm).astype(jnp.int32).reshape(1)
    expert_ids = jnp.arange(N_EXP, dtype=jnp.int32)
    dest = jnp.sum(jnp.where(top_e[..., None] == expert_ids, pad_start, 0), axis=-1) + rank
    tok = jnp.broadcast_to(jnp.arange(t, dtype=jnp.int32), (TOP_K, t))
    filler = jnp.arange(n_blk * tm, dtype=jnp.int32) % t
    slot_tok = filler.at[dest.reshape(-1)].set(tok.reshape(-1), unique_indices=True, mode="promise_in_bounds")
    blk_ids = jnp.arange(n_blk, dtype=jnp.int32)
    run_end = jnp.sum(jnp.where(blk_e[:, None] >= expert_ids, padded, 0), axis=1) // tm
    run_end_e = jnp.sum(jnp.where(run_end[:, None] == blk_ids, blk_e, 0), axis=1)
    cb = n_blk // MOE_CHUNKS
    ys = None
    for ci in range(MOE_CHUNKS):
        b0, b1 = ci * cb, (ci + 1) * cb
        xs = h.at[slot_tok[b0 * tm:b1 * tm]].get(mode="promise_in_bounds")
        last_blk = jnp.minimum(n_used[0], b1)
        next_e = jnp.where(run_end[b0:b1] < last_blk, run_end_e[b0:b1], -1).astype(jnp.int32)
        ys = expert_ffn(xs, blk_e[b0:b1], next_e, jnp.clip(n_used - b0, 0, cb), e_gate, e_up, e_down, layer, tm=tm,
                        out_rows=n_blk * tm, out_block0=b0, out_buf=ys, name="routed_ffn")
    routed_rows = ys.at[dest.reshape(-1)].get(mode="promise_in_bounds").reshape(TOP_K, t, d)
    tm_sh = 1024
    n_sh = t // tm_sh
    shared = expert_ffn(h, jnp.zeros((n_sh,), jnp.int32), jnp.full((n_sh,), -1, jnp.int32),
                        jnp.full((1,), n_sh, jnp.int32), s_gate[:, None], s_up[:, None], s_down[:, None], layer,
                        tm=tm_sh, name="shared_ffn")
    return shared, routed_rows, wts.T


def kernel(x_prompt, x_sample, state_dn, state_ml_C, state_ml_n, state_ml_m, state_ssd, c, c_ctx,
           mod_w, mod_b, ln1_g, ln1_b, ln2_g, ln2_b, router_w, router_bias, exp_gate, exp_up, exp_down,
           sh_gate, sh_up, sh_down, ev_w_in, ev_conv_w, ev_conv_b, dn_A_log, dn_dt_bias, ml_b_i, ml_b_f,
           dn_norm, ml_norm, ev_w_out, od_w_in, od_conv_w, od_conv_b, ssd_A_log, ssd_dt_bias, ssd_D,
           ssd_norm, od_w_out):
    bp, sl, d = x_prompt.shape
    bl, ll, _ = x_sample.shape
    depth = mod_w.shape[0]
    t_ctx = bp * sl
    t_all = t_ctx + bl * ll
    x = (x_prompt.reshape(t_ctx, d), x_sample.reshape(bl * ll, d))
    cvec = jnp.concatenate([c_ctx[None], c, jnp.zeros((8 - 1 - bl, d), F32)], axis=0)
    mods = compute_mods(cvec, mod_w, mod_b)[:, :1 + bl].reshape(depth, 1 + bl, 6, d)
    geo = dict(t_ctx=t_ctx, lat_len=ll)
    ctx = dict(row0=0, n_seq=bp, seq_len=sl)
    lat = dict(row0=t_ctx, n_seq=bl, seq_len=ll)

    h = modulate(x[0], mods[0], 0, 1, row0=0, t_total=t_all, **geo)
    h = modulate(x[1], mods[0], 0, 1, row0=t_ctx, t_total=t_all, out_buf=h, **geo)
    new_dn, new_c, new_n, new_m, new_ssd = [], [], [], [], []
    for l in range(depth):
        j = l // 2
        if l % 2 == 0:
            w_in = ev_w_in[j]
            proj = matmul(h, w_in, tm=1024, tn=1024, n_out=EV_MAIN, out_dtype=BF16, name="ev_in_proj")
            graw = matmul(h, w_in, tm=1024, tn=LANES, n_out=LANES, col_block_off=EV_MAIN // LANES,
                          valid_cols=EV_GATES, name="ev_gate_proj")
            act, cum = gate_prep(graw, even_gate_params(dn_A_log[j], dn_dt_bias[j], ml_b_i[j], ml_b_f[j]), "even")
            actt, cumt = act.T, cum.T
            cw, cb, dnn, mln = ev_conv_w[j], ev_conv_b[j].reshape(1, -1), dn_norm[j].reshape(1, -1), ml_norm[j].reshape(1, -1)
            oa, s_dn = delta_mixer(proj, cw, cb, act, cum, cumt, dnn, None, period=sl, hb=4, **ctx)
            oa = delta_mixer(proj, cw, cb, act, cum, cumt, dnn, state_dn[:, j], period=GRID_W, hb=1, out_buf=oa, **lat)
            ob, s_c, s_nm = mlstm_mixer(proj, act, actt, cum, cumt, mln, None, **ctx)
            ob = mlstm_mixer(proj, act, actt, cum, cumt, mln,
                             (state_ml_C[:, j], state_ml_n[:, j], state_ml_m[:, j]), out_buf=ob, **lat)
            out_proj = ([oa, ob], ev_w_out[j].astype(BF16))
            new_dn.append(s_dn)
            new_c.append(s_c)
            new_n.append(s_nm[:, :, :, 0, :])
            new_m.append(s_nm[:, :, :, 1, 0])
        else:
            w_in = od_w_in[j]
            proj = matmul(h, w_in, tm=1024, tn=1024, n_out=OD_MAIN, out_dtype=BF16, name="od_in_proj")
            draw = matmul(h, w_in, tm=1024, tn=LANES, n_out=LANES, col_block_off=OD_MAIN // LANES, name="od_dt_proj")
            dt, cum = gate_prep(draw, odd_gate_params(ssd_A_log[j], ssd_dt_bias[j]), "odd")
            cumt = cum.T
            cw, cb = od_conv_w[j], od_conv_b[j].reshape(1, -1)
            dsk, nrm = jnp.repeat(ssd_D[j], P_C).reshape(1, -1), ssd_norm[j].reshape(1, -1)
            oc, s_ssd = ssd_mixer(proj, cw, cb, dt, cum, cumt, dsk, nrm, None, period=sl, **ctx)
            oc = ssd_mixer(proj, cw, cb, dt, cum, cumt, dsk, nrm, state_ssd[:, j], period=GRID_W, out_buf=oc, **lat)
            out_proj = ([oc], od_w_out[j].astype(BF16))
            new_ssd.append(s_ssd)
        x, h2, logits_t = resid_ln(x, [], mods[l], mods[l], ln1_g[l], ln1_b[l], router_w[l], proj=out_proj, tm=512,
                                   gate=2, sh=3, sc=4, want_h=True, h_rows=GATHER_SRC_ROWS, **geo)
        shared, routed_rows, wts = moe(h2, logits_t, router_bias[l], exp_gate, exp_up, exp_down,
                                       sh_gate, sh_up, sh_down, l)
        last = l == depth - 1
        res = resid_ln(x, [shared], mods[l], mods[min(l + 1, depth - 1)], ln2_g[l], ln2_b[l], None,
                       gate=5, sh=0, sc=1, want_h=not last, gathered=(routed_rows, wts), split_out=last, **geo)
        if last:
            x = (res[0], res[1])
        else:
            x, h = res[0], res[1]
    y_prompt = x[0].reshape(bp, sl, d)
    y_sample = x[1].reshape(bl, ll, d)
    return (y_prompt, y_sample, jnp.stack(new_dn, axis=1), jnp.stack(new_c, axis=1), jnp.stack(new_n, axis=1),
            jnp.stack(new_m, axis=1), jnp.stack(new_ssd, axis=1))
```

```python
import functools

import jax
import jax.numpy as jnp
from jax import lax
from jax.experimental import pallas as pl
from jax.experimental.pallas import tpu as pltpu

F32 = jnp.float32
BF16 = jnp.bfloat16

D_MODEL = 2048
DEPTH = 2
GRID_W = 64
ALPHA = (2 * DEPTH) ** 0.25
LN_EPS = 1e-5
RMS_EPS = 1e-6

H_A, DK_A, DV_A = 8, 128, 128
H_B, DK_B, DV_B = 4, 128, 256
CONV_A = 2 * H_A * DK_A + H_A * DV_A
EV_MAIN = CONV_A + H_A * DV_A + 2 * H_B * DK_B + 2 * H_B * DV_B
EV_GATES = 4 * H_A + 4 * H_B

D_INNER = 2 * D_MODEL
P_C, N_C, G_C = 64, 128, 8
H_C = D_INNER // P_C
HG_C = H_C // G_C
GW_C = D_INNER // G_C
OD_MAIN = 2 * D_INNER + 2 * G_C * N_C

N_EXP, TOP_K, N_GROUPS, TOPK_GROUPS = 64, 8, 8, 4
D_EXP = 512
ROUTED_SCALE = 2.5

MOE_CHUNKS = 3
GATHER_SRC_ROWS = 16384
CHUNK = 256
DELTA_CHUNK = 128
SSD_CHUNK = 256
LANES = 128
VMEM_LIMIT = 56 * 1024 * 1024
NEG = -1e30
LOG2E = 1.4426950408889634


def _cparams(*sem):
    return pltpu.CompilerParams(dimension_semantics=sem, vmem_limit_bytes=VMEM_LIMIT)


def _bdot(a, b):
    return jnp.dot(a.astype(BF16), b.astype(BF16), preferred_element_type=F32)


def _bdot_nt(a, b):
    return lax.dot_general(a.astype(BF16), b.astype(BF16), (((1,), (1,)), ((), ())), preferred_element_type=F32)


def _bdot_tn(a, b):
    return lax.dot_general(a.astype(BF16), b.astype(BF16), (((0,), (0,)), ((), ())), preferred_element_type=F32)


def _split3(a):
    hi = a.astype(BF16)
    r = a - hi.astype(F32)
    mid = r.astype(BF16)
    lo = (r - mid.astype(F32)).astype(BF16)
    return hi, mid, lo


def _dot_exact_rhs(a, b_exact, passes=3):
    hi, mid, lo = _split3(a)
    bb = b_exact.astype(BF16)
    d = lambda p: jnp.dot(p, bb, preferred_element_type=F32)
    return d(hi) + d(mid) + d(lo) if passes == 3 else d(hi) + d(mid)


def _dot_exact_lhs(a_exact, b):
    hi, mid, lo = _split3(b)
    aa = a_exact.astype(BF16)
    d = lambda p: jnp.dot(aa, p, preferred_element_type=F32)
    return d(hi) + d(mid) + d(lo)


def _silu(x):
    return x * jax.nn.sigmoid(x)


def _softplus(x):
    return jnp.maximum(x, 0.0) + jnp.log(1.0 + jnp.exp(-jnp.abs(x)))


def _group_of_block(i, tm, t_ctx, lat_len):
    return jnp.maximum(i * tm - t_ctx, -1) // lat_len + 1


def _mod_kernel(c_ref, w_ref, b_ref, o_ref):
    c = c_ref[...]
    o_ref[...] = _bdot(_silu(c), w_ref[...]) + b_ref[...]


def compute_mods(cvec, mod_w, mod_b):
    depth, d, n = mod_w.shape
    tn = 512
    return pl.pallas_call(
        _mod_kernel,
        grid=(depth, n // tn),
        in_specs=[pl.BlockSpec((8, d), lambda l, j: (0, 0)),
                  pl.BlockSpec((None, d, tn), lambda l, j: (l, 0, j)),
                  pl.BlockSpec((None, 1, tn), lambda l, j: (l, 0, j))],
        out_specs=pl.BlockSpec((None, 8, tn), lambda l, j: (l, 0, j)),
        out_shape=jax.ShapeDtypeStruct((depth, 8, n), F32),
        compiler_params=_cparams("arbitrary", "arbitrary"),
        name="mod_vectors",
    )(cvec, mod_w, mod_b.reshape(depth, 1, n))


def _modulate_kernel(x_ref, m_ref, *rest, sh, sc):
    o_ref = rest[-1]
    o_ref[...] = (x_ref[...] * (1.0 + m_ref[sc:sc + 1, :]) + m_ref[sh:sh + 1, :]).astype(o_ref.dtype)


def modulate(x_part, mod, sh, sc, *, row0, t_total, t_ctx, lat_len, out_buf=None):
    tp, d = x_part.shape
    tm = 512
    blk0 = row0 // tm
    in_specs = [pl.BlockSpec((tm, d), lambda i: (i, 0)),
                pl.BlockSpec((None, 6, d), lambda i: (_group_of_block(blk0 + i, tm, t_ctx, lat_len), 0, 0))]
    args = [x_part, mod]
    aliases = _alias_out(in_specs, args, out_buf)
    return pl.pallas_call(
        functools.partial(_modulate_kernel, sh=sh, sc=sc),
        grid=(tp // tm,),
        in_specs=in_specs,
        out_specs=pl.BlockSpec((tm, d), lambda i: (blk0 + i, 0)),
        out_shape=jax.ShapeDtypeStruct((t_total, d), BF16),
        input_output_aliases=aliases,
        compiler_params=_cparams("arbitrary"),
        name="modulate",
    )(*args)


def _matmul_kernel(x_ref, w_ref, o_ref, wbf_ref, *, valid_cols):
    @pl.when(pl.program_id(1) == 0)
    def _():
        wbf_ref[...] = w_ref[...].astype(BF16)

    y = jnp.dot(x_ref[...], wbf_ref[...], preferred_element_type=F32)
    if valid_cols is not None:
        col = lax.broadcasted_iota(jnp.int32, y.shape, 1)
        y = jnp.where(col < valid_cols, y, 0.0)
    o_ref[...] = y.astype(o_ref.dtype)


def matmul(x, w, *, tm, tn, n_out, col_block_off=0, valid_cols=None, out_dtype=F32, name="matmul"):
    m, k = x.shape
    return pl.pallas_call(
        functools.partial(_matmul_kernel, valid_cols=valid_cols),
        grid=(n_out // tn, m // tm),
        in_specs=[pl.BlockSpec((tm, k), lambda j, i: (i, 0)),
                  pl.BlockSpec((k, tn), lambda j, i: (0, j + col_block_off))],
        out_specs=pl.BlockSpec((tm, tn), lambda j, i: (i, j)),
        out_shape=jax.ShapeDtypeStruct((m, n_out), out_dtype),
        scratch_shapes=[pltpu.VMEM((k, tn), BF16)],
        compiler_params=_cparams("arbitrary", "arbitrary"),
        name=name,
    )(x, w)


def _tri_masks(n):
    r = lax.broadcasted_iota(jnp.int32, (n, n), 0)
    c = lax.broadcasted_iota(jnp.int32, (n, n), 1)
    return r, c


def _gate_kernel(raw_ref, p_ref, act_ref, cum_ref, *, mode):
    x = raw_ref[...]
    coef, bias, rev = p_ref[0:1, :], p_ref[1:2, :], p_ref[2:3, :]
    col = lax.broadcasted_iota(jnp.int32, x.shape, 1)
    xb = x + bias
    if mode == "even":
        act = jnp.where(col < 2 * H_A, jax.nn.sigmoid(xb),
                        jnp.where(col < 4 * H_A, coef * _softplus(xb),
                                  jnp.where(col < 4 * H_A + 2 * H_B, xb,
                                            jnp.minimum(xb, 0.0) - jnp.log(1.0 + jnp.exp(-jnp.abs(xb))))))
        to_sum = act
    else:
        act = _softplus(xb)
        to_sum = act * (coef * LOG2E)
    r, c = _tri_masks(CHUNK)
    lower = jnp.where(c <= r, 1.0, 0.0)
    upper = jnp.where(c >= r, 1.0, 0.0)
    if mode == "odd" and SSD_CHUNK != CHUNK:
        same = (r // SSD_CHUNK) == (c // SSD_CHUNK)
        lower, upper = jnp.where(same, lower, 0.0), jnp.where(same, upper, 0.0)
    cum = jnp.where(rev > 0.5, _dot_exact_lhs(upper, to_sum), _dot_exact_lhs(lower, to_sum))
    if mode == "even":
        same = (r // DELTA_CHUNK) == (c // DELTA_CHUNK)
        cum_d = jnp.where(rev > 0.5, _dot_exact_lhs(jnp.where(same, upper, 0.0), to_sum),
                          _dot_exact_lhs(jnp.where(same, lower, 0.0), to_sum))
        cum = jnp.where(col < 4 * H_A, cum_d, cum)
    act_ref[...] = act
    cum_ref[...] = cum


def gate_prep(raw, params, mode):
    t = raw.shape[0]
    return pl.pallas_call(
        functools.partial(_gate_kernel, mode=mode),
        grid=(t // CHUNK,),
        in_specs=[pl.BlockSpec((CHUNK, LANES), lambda i: (i, 0)),
                  pl.BlockSpec((8, LANES), lambda i: (0, 0))],
        out_specs=[pl.BlockSpec((CHUNK, LANES), lambda i: (i, 0))] * 2,
        out_shape=[jax.ShapeDtypeStruct((t, LANES), F32)] * 2,
        compiler_params=_cparams("arbitrary"),
        name="gate_prep_" + mode,
    )(raw, params)


def _conv_silu(x, cw_ref, cb_ref, period):
    n = x.shape[0]
    row = lax.broadcasted_iota(jnp.int32, x.shape, 0) % period
    prev = jnp.where(row == 0, 0.0, pltpu.roll(x, 1, 0))
    nxt = jnp.where(row == period - 1, 0.0, pltpu.roll(x, n - 1, 0))
    y = cb_ref[...] + prev * cw_ref[0:1, :] + x * cw_ref[1:2, :] + nxt * cw_ref[2:3, :]
    return _silu(y)


def _pick_col(blk, idx):
    lane = lax.broadcasted_iota(jnp.int32, blk.shape, 1)
    return jnp.sum(jnp.where(lane == idx, blk, 0.0), axis=1, keepdims=True)


def _dir_masks(rev, n=CHUNK):
    r, c = _tri_masks(n)
    if rev:
        return c >= r, c > r
    return c <= r, c < r


def _tri_inverse(lmat, rev):
    return _tri_inverse_many([lmat], [rev])[0]


def _tri_inverse_many(lmats, revs):
    n = lmats[0].shape[0]
    r, c = _tri_masks(n)
    eye = jnp.where(r == c, 1.0, 0.0)

    def off_mask(s, rev):
        same = (r // (2 * s)) == (c // (2 * s))
        r_hi = (r // s) % 2
        c_hi = (c // s) % 2
        return same & ((r_hi == 0) & (c_hi == 1) if rev else (r_hi == 1) & (c_hi == 0))

    masks = {rev: off_mask(1, rev) for rev in set(revs)}
    ts = [eye - jnp.where(masks[rev], lm, 0.0) for lm, rev in zip(lmats, revs)]
    s = 2
    while s < n:
        masks = {rev: off_mask(s, rev) for rev in set(revs)}
        ps = [_bdot(t, jnp.where(masks[rev], lm, 0.0)) for t, lm, rev in zip(ts, lmats, revs)]
        ts = [t - _bdot(p, t) for p, t in zip(ps, ts)]
        s *= 2
    return ts


def _delta_kernel(*refs, n_chunks, period, has_state, hb):
    (q_ref, k_ref, v_ref, z_ref, cwq, cwk, cwv, cbq, cbk, cbv, act_ref, cum_ref, cumt_ref, norm_ref) = refs[:14]
    rest = refs[14:]
    if has_state:
        s0_ref, *_aliased_out, o_ref, acc_ref = rest
        sout_ref = None
    else:
        o_ref, sout_ref, acc_ref = rest
    h0 = pl.program_id(1) * hb
    dc = DELTA_CHUNK

    q = _conv_silu(q_ref[...].astype(F32), cwq, cbq, period)
    k = _conv_silu(k_ref[...].astype(F32), cwk, cbk, period)
    v = _conv_silu(v_ref[...].astype(F32), cwv, cbv, period)

    triples = [(hh, d, ci) for hh in range(hb) for d in (0, 1) for ci in range(n_chunks)]
    pre = {}
    for hh in range(hb):
        hs = slice(hh * DK_A, (hh + 1) * DK_A)
        qh, kh = q[:, hs], k[:, hs]
        qh = qh * lax.rsqrt(jnp.sum(qh * qh, axis=1, keepdims=True) + RMS_EPS) * (DK_A ** -0.5)
        kh = kh * lax.rsqrt(jnp.sum(kh * kh, axis=1, keepdims=True) + RMS_EPS)
        for d in (0, 1):
            m_incl, m_strict = _dir_masks(d == 1, dc)
            grow_all = cumt_ref[pl.ds(2 * H_A + d * H_A + h0 + hh, 1), :]
            for ci in range(n_chunks):
                sl = slice(ci * dc, (ci + 1) * dc)
                qc, kc, vc = qh[sl], kh[sl], v[sl, hs]
                beta = _pick_col(act_ref[sl, :], d * H_A + h0 + hh)
                gcol = _pick_col(cum_ref[sl, :], 2 * H_A + d * H_A + h0 + hh)
                grow = grow_all[:, sl]
                decay = jnp.exp(jnp.where(m_incl, gcol - grow, NEG))
                kb = kc * beta
                pre[hh, d, ci] = dict(
                    qc=qc, kc=kc, kb=kb, vb=vc * beta, gcol=gcol, decay=decay,
                    lmat=_bdot_nt(kb, kc) * jnp.where(m_strict, decay, 0.0),
                    attn=_bdot_nt(qc, kc) * decay)
    tinvs = _tri_inverse_many([pre[t]["lmat"] for t in triples], [t[1] == 1 for t in triples])
    for t, tinv in zip(triples, tinvs):
        p = pre[t]
        p["u"] = _bdot(tinv, p["vb"])
        if has_state or n_chunks > 1:
            p["w"] = _bdot(tinv, p["kb"] * jnp.exp(p["gcol"]))

    for hh in range(hb):
        hs = slice(hh * DK_A, (hh + 1) * DK_A)
        for d in (0, 1):
            rev = d == 1
            state = s0_ref[d, hh] if has_state else None
            order = range(n_chunks - 1, -1, -1) if rev else range(n_chunks)
            for ci in order:
                sl = slice(ci * dc, (ci + 1) * dc)
                p = pre[hh, d, ci]
                u, gcol = p["u"], p["gcol"]
                if state is not None:
                    u = u - _bdot(p["w"], state)
                o = _bdot(p["attn"], u)
                if state is not None:
                    o = o + _bdot(p["qc"] * jnp.exp(gcol), state)
                glast = gcol[0:1, :] if rev else gcol[dc - 1:dc, :]
                upd = _bdot_tn(p["kc"] * jnp.exp(glast - gcol), u)
                state = upd if state is None else state * jnp.exp(glast) + upd
                if rev:
                    acc_ref[sl, hs] = acc_ref[sl, hs] + o
                else:
                    acc_ref[sl, hs] = o
            if sout_ref is not None:
                sout_ref[d, hh] = state

    z = z_ref[...].astype(F32)
    for hh in range(hb):
        hs = slice(hh * DK_A, (hh + 1) * DK_A)
        o = acc_ref[:, hs]
        o = o * lax.rsqrt(jnp.mean(o * o, axis=1, keepdims=True) + RMS_EPS) * norm_ref[...]
        o_ref[:, hs] = (o * _silu(z[:, hs])).astype(o_ref.dtype)


def _alias_out(in_specs, args, out_buf):
    if out_buf is None:
        return {}
    in_specs.append(pl.BlockSpec(memory_space=pl.ANY))
    args.append(out_buf)
    return {len(args) - 1: 0}


def delta_mixer(proj, conv_w, conv_b, act, cum, cumt, norm, state, *, row0, n_seq, seq_len, period, hb,
                out_buf=None):
    assert row0 % seq_len == 0 and H_A % hb == 0
    n_chunks = seq_len // DELTA_CHUNK
    rb0 = row0 // seq_len
    has_state = state is not None
    w = hb * DK_A
    nq = H_A // hb
    col = lambda off: (lambda s, h: (rb0 + s, off + h))
    cw = lambda off: (lambda s, h: (0, off + h))
    in_specs = [pl.BlockSpec((seq_len, w), col(0)), pl.BlockSpec((seq_len, w), col(nq)),
                pl.BlockSpec((seq_len, w), col(2 * nq)), pl.BlockSpec((seq_len, w), col(3 * nq)),
                pl.BlockSpec((3, w), cw(0)), pl.BlockSpec((3, w), cw(nq)), pl.BlockSpec((3, w), cw(2 * nq)),
                pl.BlockSpec((1, w), cw(0)), pl.BlockSpec((1, w), cw(nq)), pl.BlockSpec((1, w), cw(2 * nq)),
                pl.BlockSpec((seq_len, LANES), lambda s, h: (rb0 + s, 0)),
                pl.BlockSpec((seq_len, LANES), lambda s, h: (rb0 + s, 0)),
                pl.BlockSpec((LANES, seq_len), lambda s, h: (0, rb0 + s)),
                pl.BlockSpec((1, DV_A), lambda s, h: (0, 0))]
    args = [proj, proj, proj, proj, conv_w, conv_w, conv_w, conv_b, conv_b, conv_b, act, cum, cumt, norm]
    o_spec = pl.BlockSpec((seq_len, w), lambda s, h: (rb0 + s, h))
    o_shape = jax.ShapeDtypeStruct((proj.shape[0], H_A * DV_A), BF16)
    st_spec = pl.BlockSpec((None, 2, hb, DK_A, DV_A), lambda s, h: (s, 0, h, 0, 0))
    if has_state:
        in_specs.append(st_spec)
        args.append(state)
        out_specs, out_shape = o_spec, o_shape
    else:
        out_specs = [o_spec, st_spec]
        out_shape = [o_shape, jax.ShapeDtypeStruct((n_seq, 2, H_A, DK_A, DV_A), F32)]
    aliases = _alias_out(in_specs, args, out_buf)
    return pl.pallas_call(
        functools.partial(_delta_kernel, n_chunks=n_chunks, period=period, has_state=has_state, hb=hb),
        grid=(n_seq, H_A // hb),
        in_specs=in_specs, out_specs=out_specs, out_shape=out_shape, input_output_aliases=aliases,
        scratch_shapes=[pltpu.VMEM((seq_len, w), F32)],
        compiler_params=_cparams("arbitrary", "arbitrary"),
        name="delta_lat" if has_state else "delta_ctx",
    )(*args)


def even_gate_params(a_log, dt_bias, b_i, b_f):
    zeros_a = jnp.zeros((2 * H_A,), F32)
    coef = jnp.concatenate([zeros_a, -jnp.exp(a_log.astype(F32)).reshape(-1), jnp.zeros((4 * H_B,), F32)])
    bias = jnp.concatenate([zeros_a, dt_bias.reshape(-1), b_i.reshape(-1), b_f.reshape(-1)]).astype(F32)
    rev = jnp.concatenate([jnp.repeat(jnp.arange(2, dtype=F32), H_A)] * 2 + [jnp.repeat(jnp.arange(2, dtype=F32), H_B)] * 2)
    p = jnp.stack([coef, bias, rev])
    return jnp.pad(p, ((0, 5), (0, LANES - EV_GATES)))


def _mlstm_kernel(*refs, n_chunks, has_state):
    (q_ref, k_ref, v_ref, og_ref, act_ref, actt_ref, cum_ref, cumt_ref, norm_ref) = refs[:9]
    rest = refs[9:]
    if has_state:
        c0_ref, n0_ref, m0_ref, *_aliased_out, o_ref, acc_ref = rest
    else:
        o_ref, cout_ref, nm_ref, acc_ref = rest
    h = pl.program_id(1)
    i_col0, f_col0 = 4 * H_A, 4 * H_A + 2 * H_B

    for d in (0, 1):
        rev = d == 1
        m_incl, _ = _dir_masks(rev)
        if has_state:
            cm, nv, m = c0_ref[d], n0_ref[d], m0_ref[d]
        else:
            cm, nv, m = None, None, jnp.zeros((1, 1), F32)
        order = range(n_chunks - 1, -1, -1) if rev else range(n_chunks)
        for ci in order:
            sl = slice(ci * CHUNK, (ci + 1) * CHUNK)
            qc = q_ref[sl, :].astype(F32) * (DK_B ** -0.5)
            kc = k_ref[sl, :].astype(F32)
            vc = v_ref[sl, :].astype(F32)
            li_col = _pick_col(act_ref[sl, :], i_col0 + d * H_B + h)
            li_row = actt_ref[pl.ds(i_col0 + d * H_B + h, 1), sl]
            b_col = _pick_col(cum_ref[sl, :], f_col0 + d * H_B + h)
            b_row = cumt_ref[pl.ds(f_col0 + d * H_B + h, 1), sl]
            dlog = jnp.where(m_incl, b_col - b_row + li_row, NEG)
            inter = b_col + m
            m_q = jnp.maximum(inter, jnp.max(dlog, axis=1, keepdims=True))
            s = _bdot_nt(qc, kc) * jnp.exp(dlog - m_q)
            num = _bdot(s, vc)
            den = jnp.sum(s, axis=1, keepdims=True)
            if cm is not None:
                w_inter = jnp.exp(inter - m_q)
                num = num + w_inter * _bdot(qc, cm)
                den = den + w_inter * jnp.sum(qc * nv, axis=1, keepdims=True)
            hout = num / jnp.maximum(jnp.abs(den), jnp.exp(-m_q))
            b_last = b_col[0:1, :] if rev else b_col[CHUNK - 1:CHUNK, :]
            wlog = b_last - b_col + li_col
            m_new = jnp.maximum(b_last + m, jnp.max(wlog, axis=0, keepdims=True))
            kw = kc * jnp.exp(wlog - m_new)
            c_upd = _bdot_tn(kw, vc)
            n_upd = jnp.sum(kw, axis=0, keepdims=True)
            if cm is not None:
                sc = jnp.exp(b_last + m - m_new)
                cm, nv = sc * cm + c_upd, sc * nv + n_upd
            else:
                cm, nv = c_upd, n_upd
            m = m_new
            if rev:
                acc_ref[sl, :] = acc_ref[sl, :] + hout
            else:
                acc_ref[sl, :] = hout
        if not has_state:
            cout_ref[d] = cm
            nm_ref[d, 0:1, :] = nv
            nm_ref[d, 1:2, :] = jnp.broadcast_to(m, (1, DK_B))
            nm_ref[d, 2:8, :] = jnp.zeros((6, DK_B), F32)

    o = acc_ref[...]
    o = o * lax.rsqrt(jnp.mean(o * o, axis=1, keepdims=True) + RMS_EPS) * norm_ref[...]
    o_ref[...] = (o * jax.nn.sigmoid(og_ref[...].astype(F32))).astype(o_ref.dtype)


def mlstm_mixer(proj, act, actt, cum, cumt, norm, state, *, row0, n_seq, seq_len, out_buf=None):
    assert row0 % seq_len == 0
    n_chunks = seq_len // CHUNK
    rb0 = row0 // seq_len
    has_state = state is not None
    q0 = (CONV_A + H_A * DV_A) // LANES
    k0 = q0 + H_B
    v0 = (CONV_A + H_A * DV_A + 2 * H_B * DK_B) // DV_B
    o0 = v0 + H_B
    col = lambda off: (lambda s, h: (rb0 + s, off + h))
    in_specs = [pl.BlockSpec((seq_len, DK_B), col(q0)), pl.BlockSpec((seq_len, DK_B), col(k0)),
                pl.BlockSpec((seq_len, DV_B), col(v0)), pl.BlockSpec((seq_len, DV_B), col(o0)),
                pl.BlockSpec((seq_len, LANES), lambda s, h: (rb0 + s, 0)),
                pl.BlockSpec((LANES, seq_len), lambda s, h: (0, rb0 + s)),
                pl.BlockSpec((seq_len, LANES), lambda s, h: (rb0 + s, 0)),
                pl.BlockSpec((LANES, seq_len), lambda s, h: (0, rb0 + s)),
                pl.BlockSpec((1, DV_B), lambda s, h: (0, 0))]
    args = [proj, proj, proj, proj, act, actt, cum, cumt, norm]
    o_spec = pl.BlockSpec((seq_len, DV_B), lambda s, h: (rb0 + s, h))
    o_shape = jax.ShapeDtypeStruct((proj.shape[0], H_B * DV_B), BF16)
    st_idx = lambda s, h: (s, 0, h, 0, 0)
    if has_state:
        c0, n0, m0 = state
        in_specs += [pl.BlockSpec((None, 2, None, DK_B, DV_B), st_idx),
                     pl.BlockSpec((None, 2, None, 1, DK_B), st_idx),
                     pl.BlockSpec((None, 2, None, 1, 1), st_idx)]
        args += [c0, n0.reshape(n_seq, 2, H_B, 1, DK_B), m0.reshape(n_seq, 2, H_B, 1, 1)]
        out_specs, out_shape = o_spec, o_shape
    else:
        out_specs = [o_spec, pl.BlockSpec((None, 2, None, DK_B, DV_B), st_idx),
                     pl.BlockSpec((None, 2, None, 8, DK_B), st_idx)]
        out_shape = [o_shape, jax.ShapeDtypeStruct((n_seq, 2, H_B, DK_B, DV_B), F32),
                     jax.ShapeDtypeStruct((n_seq, 2, H_B, 8, DK_B), F32)]
    aliases = _alias_out(in_specs, args, out_buf)
    return pl.pallas_call(
        functools.partial(_mlstm_kernel, n_chunks=n_chunks, has_state=has_state),
        grid=(n_seq, H_B),
        in_specs=in_specs, out_specs=out_specs, out_shape=out_shape, input_output_aliases=aliases,
        scratch_shapes=[pltpu.VMEM((seq_len, DV_B), F32)],
        compiler_params=_cparams("arbitrary", "arbitrary"),
        name="mlstm_lat" if has_state else "mlstm_ctx",
    )(*args)


def odd_gate_params(a_log, dt_bias):
    coef = -jnp.exp(a_log.astype(F32)).reshape(-1)
    bias = dt_bias.astype(F32).reshape(-1)
    rev = jnp.repeat(jnp.arange(2, dtype=F32), H_C)
    return jnp.pad(jnp.stack([coef, bias, rev]), ((0, 5), (0, 0)))


def _ssd_kernel(*refs, n_chunks, period, has_state):
    (z_ref, x_ref, b_ref, c_ref, cwx, cwb, cwc, cbx, cbb, cbc, dt_ref, cum_ref, cumt_ref, dskip_ref, norm_ref) = refs[:15]
    rest = refs[15:]
    if has_state:
        s0_ref, *_aliased_out, o_ref, acc_ref = rest
        sout_ref = None
    else:
        o_ref, sout_ref, acc_ref = rest
    g = pl.program_id(1)

    x = _conv_silu(x_ref[...].astype(F32), cwx, cbx, period)
    bm = _conv_silu(b_ref[...].astype(F32), cwb, cbb, period)
    cm = _conv_silu(c_ref[...].astype(F32), cwc, cbc, period)

    er = lax.broadcasted_iota(jnp.int32, (LANES, GW_C), 0)
    ec = lax.broadcasted_iota(jnp.int32, (LANES, GW_C), 1)
    tr = lax.broadcasted_iota(jnp.int32, (GW_C, LANES), 0)
    tc = lax.broadcasted_iota(jnp.int32, (GW_C, LANES), 1)
    sc_ = SSD_CHUNK
    lane_in_tile = lax.broadcasted_iota(jnp.int32, (sc_, LANES), 1)

    for d in (0, 1):
        rev = d == 1
        m_incl, _ = _dir_masks(rev, sc_)
        lo, hi, full = slice(0, sc_ // 2), slice(sc_ // 2, sc_), slice(0, sc_)
        if sc_ // 2 >= LANES:
            half_blocks = ((lo, full), (hi, hi)) if rev else ((lo, lo), (hi, full))
        else:
            half_blocks = ((full, full),)
        col0 = d * H_C + g * HG_C
        expand = jnp.where(er == col0 + ec // P_C, 1.0, 0.0)
        expand_t = tc == col0 + tr // P_C
        state = s0_ref[d].reshape(GW_C, N_C) if has_state else None
        order = range(n_chunks - 1, -1, -1) if rev else range(n_chunks)
        crows = [cumt_ref[pl.ds(col0 + hh, 1), :] for hh in range(HG_C)]
        for ci in order:
            sl = slice(ci * sc_, (ci + 1) * sc_)
            xc, bc, cc = x[sl], bm[sl], cm[sl]
            cum_blk = cum_ref[sl, :]
            cum_last = cum_blk[0:1, :] if rev else cum_blk[sc_ - 1:sc_, :]
            xdt = xc * _dot_exact_rhs(dt_ref[sl, :], expand, passes=2)
            scores_bf = _bdot_nt(cc, bc).astype(BF16)
            for hp in range(HG_C // 2):
                ps = slice(hp * LANES, (hp + 1) * LANES)
                xpair = xdt[:, ps]
                ypair = None
                for sub in (0, 1):
                    hh = 2 * hp + sub
                    cb = _pick_col(cum_blk, col0 + hh)
                    crow = crows[hh][:, sl]
                    mine = (lane_in_tile < P_C) if sub == 0 else (lane_in_tile >= P_C)
                    rhs = jnp.where(mine, xpair, 0.0)
                    parts = []
                    for rows, cols in half_blocks:
                        seg = jnp.exp2(jnp.where(m_incl[rows, cols], cb[rows] - crow[:, cols], NEG))
                        parts.append(_bdot(scores_bf[rows, cols] * seg.astype(BF16), rhs[cols]))
                    y = jnp.concatenate(parts, axis=0)
                    ypair = y if ypair is None else ypair + y
                if rev:
                    acc_ref[sl, ps] = acc_ref[sl, ps] + ypair
                else:
                    acc_ref[sl, ps] = ypair
            if state is not None:
                y_in = _bdot_nt(cc, state) * _dot_exact_rhs(jnp.exp2(cum_blk), expand, passes=2)
                acc_ref[sl, :] = acc_ref[sl, :] + y_in
            dend = _dot_exact_rhs(jnp.exp2(jnp.minimum(cum_last - cum_blk, 0.0)), expand, passes=2)
            upd = _bdot_tn(xdt * dend, bc)
            if state is not None:
                tot = jnp.sum(jnp.where(expand_t, jnp.broadcast_to(cum_last, (GW_C, LANES)), 0.0), axis=1, keepdims=True)
                state = state * jnp.exp2(tot) + upd
            else:
                state = upd
        if sout_ref is not None:
            sout_ref[d] = state.reshape(HG_C, P_C, N_C)

    y = acc_ref[...] + dskip_ref[...] * x
    y = y * _silu(z_ref[...].astype(F32))
    y = y * lax.rsqrt(jnp.mean(y * y, axis=1, keepdims=True) + RMS_EPS) * norm_ref[...]
    o_ref[...] = y.astype(o_ref.dtype)


def ssd_mixer(proj, conv_w, conv_b, dt, cum, cumt, dskip, norm, state, *, row0, n_seq, seq_len, period, out_buf=None):
    assert row0 % seq_len == 0
    n_chunks = seq_len // SSD_CHUNK
    rb0 = row0 // seq_len
    has_state = state is not None
    xb0 = D_INNER // GW_C
    bb0 = 2 * D_INNER // N_C
    cb0 = bb0 + G_C
    wb0 = D_INNER // N_C
    wc0 = wb0 + G_C
    col = lambda off: (lambda s, g: (rb0 + s, off + g))
    cw = lambda off: (lambda s, g: (0, off + g))
    in_specs = [pl.BlockSpec((seq_len, GW_C), col(0)), pl.BlockSpec((seq_len, GW_C), col(xb0)),
                pl.BlockSpec((seq_len, N_C), col(bb0)), pl.BlockSpec((seq_len, N_C), col(cb0)),
                pl.BlockSpec((3, GW_C), cw(0)), pl.BlockSpec((3, N_C), cw(wb0)), pl.BlockSpec((3, N_C), cw(wc0)),
                pl.BlockSpec((1, GW_C), cw(0)), pl.BlockSpec((1, N_C), cw(wb0)), pl.BlockSpec((1, N_C), cw(wc0)),
                pl.BlockSpec((seq_len, LANES), lambda s, g: (rb0 + s, 0)),
                pl.BlockSpec((seq_len, LANES), lambda s, g: (rb0 + s, 0)),
                pl.BlockSpec((LANES, seq_len), lambda s, g: (0, rb0 + s)),
                pl.BlockSpec((1, GW_C), cw(0)), pl.BlockSpec((1, GW_C), cw(0))]
    args = [proj, proj, proj, proj, conv_w, conv_w, conv_w, conv_b, conv_b, conv_b, dt, cum, cumt, dskip, norm]
    o_spec = pl.BlockSpec((seq_len, GW_C), lambda s, g: (rb0 + s, g))
    o_shape = jax.ShapeDtypeStruct((proj.shape[0], D_INNER), BF16)
    st_spec = pl.BlockSpec((None, 2, HG_C, P_C, N_C), lambda s, g: (s, 0, g, 0, 0))
    if has_state:
        in_specs.append(st_spec)
        args.append(state)
        out_specs, out_shape = o_spec, o_shape
    else:
        out_specs = [o_spec, st_spec]
        out_shape = [o_shape, jax.ShapeDtypeStruct((n_seq, 2, H_C, P_C, N_C), F32)]
    aliases = _alias_out(in_specs, args, out_buf)
    return pl.pallas_call(
        functools.partial(_ssd_kernel, n_chunks=n_chunks, period=period, has_state=has_state),
        grid=(n_seq, G_C),
        in_specs=in_specs, out_specs=out_specs, out_shape=out_shape, input_output_aliases=aliases,
        scratch_shapes=[pltpu.VMEM((seq_len, GW_C), F32)],
        compiler_params=_cparams("arbitrary", "arbitrary"),
        name="ssd_lat" if has_state else "ssd_ctx",
    )(*args)


def _dot3(a, b):
    a_hi = a.astype(BF16)
    a_lo = (a - a_hi.astype(F32)).astype(BF16)
    b_hi = b.astype(BF16)
    b_lo = (b - b_hi.astype(F32)).astype(BF16)
    d = lambda p, q: jnp.dot(p, q, preferred_element_type=F32)
    return d(a_hi, b_hi) + (d(a_hi, b_lo) + d(a_lo, b_hi))


def _dot3_nt(a, b):
    a_hi = a.astype(BF16)
    a_lo = (a - a_hi.astype(F32)).astype(BF16)
    b_hi = b.astype(BF16)
    b_lo = (b - b_hi.astype(F32)).astype(BF16)
    d = lambda p, q: lax.dot_general(p, q, (((1,), (1,)), ((), ())), preferred_element_type=F32)
    return d(a_hi, b_hi) + (d(a_hi, b_lo) + d(a_lo, b_hi))


def _resid_ln_kernel(*refs, n_y, lhs_widths, n_gathered, gate, sh, sc, want_h, want_logits, split_in, split_out,
                     n_ctx_blocks):
    in_ctx = pl.program_id(0) < n_ctx_blocks
    if split_in:
        x_in = jnp.where(in_ctx, refs[0][...], refs[1][...])
        refs = refs[1:]
    else:
        x_in = refs[0][...]
    y_refs = refs[1:1 + n_y]
    rest = list(refs[1 + n_y:])
    lhs_refs = [rest.pop(0) for _ in lhs_widths]
    pw_ref = rest.pop(0) if lhs_widths else None
    m_ref, mn_ref, g_ref, b_ref = (rest.pop(0) for _ in range(4))
    if n_gathered:
        gath_ref, gw_ref = rest.pop(0), rest.pop(0)
    rw_ref = rest.pop(0) if want_logits else None
    xo_refs = [rest.pop(0) for _ in range(2 if split_out else 1)]
    y = None
    for r in y_refs:
        y = r[...].astype(F32) if y is None else y + r[...].astype(F32)
    off = 0
    for lhs_ref, width in zip(lhs_refs, lhs_widths):
        part = jnp.dot(lhs_ref[...], pw_ref[off:off + width, :], preferred_element_type=F32)
        y = part if y is None else y + part
        off += width
    for kk in range(n_gathered):
        y = y + gw_ref[:, kk:kk + 1] * gath_ref[kk].astype(F32)
    v = ALPHA * x_in + m_ref[gate:gate + 1, :] * y
    mu = jnp.mean(v, axis=1, keepdims=True)
    vc = v - mu
    var = jnp.mean(vc * vc, axis=1, keepdims=True)
    xn = vc * lax.rsqrt(var + LN_EPS) * g_ref[...] + b_ref[...]
    if split_out:
        @pl.when(in_ctx)
        def _():
            xo_refs[0][...] = xn

        @pl.when(jnp.logical_not(in_ctx))
        def _():
            xo_refs[1][...] = xn
    else:
        xo_refs[0][...] = xn
    if want_h:
        hm = xn * (1.0 + mn_ref[sc:sc + 1, :]) + mn_ref[sh:sh + 1, :]
        rest.pop(0)[...] = hm.astype(BF16)
        if want_logits:
            rest.pop(0)[...] = _dot3_nt(rw_ref[...], hm)


def resid_ln(x, ys, mod, mod_next, ln_g, ln_b, router_w, *, gate, sh, sc, want_h, t_ctx, lat_len, gathered=None,
             h_rows=None, proj=None, tm=256, split_out=False):
    split_in = isinstance(x, tuple)
    d = x[0].shape[1] if split_in else x.shape[1]
    t = x[0].shape[0] + x[1].shape[0] if split_in else x.shape[0]
    n_ctx_blocks = t_ctx // tm
    ctx_blk = lambda i: (jnp.minimum(i, n_ctx_blocks - 1), 0)
    lat_blk = lambda i: (jnp.maximum(i - n_ctx_blocks, 0), 0)
    want_logits = router_w is not None
    grp = lambda i: (_group_of_block(i, tm, t_ctx, lat_len), 0, 0)
    row = pl.BlockSpec((tm, d), lambda i: (i, 0))
    vec = pl.BlockSpec((1, d), lambda i: (0, 0))
    if split_in:
        in_specs = [pl.BlockSpec((tm, d), ctx_blk), pl.BlockSpec((tm, d), lat_blk)] + [row] * len(ys)
        args = [*x, *ys]
    else:
        in_specs = [row] * (1 + len(ys))
        args = [x, *ys]
    lhs_widths = ()
    if proj is not None:
        lhs_list, pw = proj
        lhs_widths = tuple(a.shape[1] for a in lhs_list)
        in_specs += [pl.BlockSpec((tm, wd), lambda i: (i, 0)) for wd in lhs_widths]
        in_specs.append(pl.BlockSpec(pw.shape, lambda i: (0, 0), pipeline_mode=pl.Buffered(1)))
        args += [*lhs_list, pw]
    in_specs += [pl.BlockSpec((None, 6, d), grp), pl.BlockSpec((None, 6, d), grp), vec, vec]
    args += [mod, mod_next, ln_g.reshape(1, d), ln_b.reshape(1, d)]
    n_gathered = 0
    if gathered is not None:
        n_gathered = gathered[0].shape[0]
        in_specs += [pl.BlockSpec((n_gathered, tm, d), lambda i: (0, i, 0)),
                     pl.BlockSpec((tm, n_gathered), lambda i: (i, 0))]
        args += list(gathered)
    if split_out:
        out_specs = [pl.BlockSpec((tm, d), ctx_blk), pl.BlockSpec((tm, d), lat_blk)]
        out_shape = [jax.ShapeDtypeStruct((t_ctx, d), F32), jax.ShapeDtypeStruct((t - t_ctx, d), F32)]
    else:
        out_specs, out_shape = [row], [jax.ShapeDtypeStruct((t, d), F32)]
    if want_logits:
        n_e = router_w.shape[1]
        in_specs.append(pl.BlockSpec((n_e, d), lambda i: (0, 0)))
        args.append(router_w.T)
    if want_h:
        out_specs.append(row)
        out_shape.append(jax.ShapeDtypeStruct((h_rows or t, d), BF16))
    if want_logits:
        out_specs.append(pl.BlockSpec((n_e, tm), lambda i: (0, i)))
        out_shape.append(jax.ShapeDtypeStruct((n_e, t), F32))
    return pl.pallas_call(
        functools.partial(_resid_ln_kernel, n_y=len(ys), lhs_widths=lhs_widths, n_gathered=n_gathered, gate=gate, sh=sh,
                          sc=sc, want_h=want_h, want_logits=want_logits, split_in=split_in, split_out=split_out,
                          n_ctx_blocks=n_ctx_blocks),
        grid=(t // tm,),
        in_specs=in_specs, out_specs=out_specs, out_shape=out_shape,
        compiler_params=_cparams("arbitrary"),
        name="resid_ln",
    )(*args)


def _ffn_kernel(be_ref, nx_ref, nu_ref, x_ref, wg_hbm, wu_hbm, wd_hbm, *rest, layer):
    *_aliased_out, o_ref, g_f32, u_f32, d_f32, g_bf, u_bf, d_bf, sem = rest
    b = pl.program_id(0)

    def copies(e):
        return (pltpu.make_async_copy(wg_hbm.at[layer, e], g_f32, sem.at[0]),
                pltpu.make_async_copy(wu_hbm.at[layer, e], u_f32, sem.at[1]),
                pltpu.make_async_copy(wd_hbm.at[layer, e], d_f32, sem.at[2]))

    @pl.when(b < nu_ref[0])
    def _():
        e = be_ref[b]

        @pl.when(b == 0)
        def _():
            for cp in copies(e):
                cp.start()

        @pl.when((b == 0) | (e != be_ref[jnp.maximum(b - 1, 0)]))
        def _():
            for cp in copies(e):
                cp.wait()
            g_bf[...] = g_f32[...].astype(BF16)
            u_bf[...] = u_f32[...].astype(BF16)
            d_bf[...] = d_f32[...].astype(BF16)
            nxt = nx_ref[b]

            @pl.when(nxt >= 0)
            def _():
                for cp in copies(nxt):
                    cp.start()

        x = x_ref[...]
        hg = jnp.dot(x, g_bf[...], preferred_element_type=F32)
        hu = jnp.dot(x, u_bf[...], preferred_element_type=F32)
        a = (_silu(hg) * hu).astype(BF16)
        o_ref[...] = jnp.dot(a, d_bf[...], preferred_element_type=F32).astype(o_ref.dtype)


def expert_ffn(xs, blk_e, next_e, n_used, w_gate, w_up, w_down, layer, *, tm, out_rows=None, out_block0=0,
               out_buf=None, out_dtype=BF16, name="expert_ffn"):
    d = xs.shape[1]
    de = w_gate.shape[3]
    n_blk = blk_e.shape[0]
    in_specs = [pl.BlockSpec((tm, d), lambda b, be, nx, nu: (b, 0))] + [pl.BlockSpec(memory_space=pl.ANY)] * 3
    args = [blk_e, next_e, n_used, xs, w_gate, w_up, w_down]
    aliases = _alias_out(in_specs, args, out_buf)
    grid_spec = pltpu.PrefetchScalarGridSpec(
        num_scalar_prefetch=3,
        grid=(n_blk,),
        in_specs=in_specs,
        out_specs=pl.BlockSpec((tm, d), lambda b, be, nx, nu: (out_block0 + b, 0)),
        scratch_shapes=[pltpu.VMEM((d, de), F32), pltpu.VMEM((d, de), F32), pltpu.VMEM((de, d), F32),
                        pltpu.VMEM((d, de), BF16), pltpu.VMEM((d, de), BF16), pltpu.VMEM((de, d), BF16),
                        pltpu.SemaphoreType.DMA((3,))],
    )
    return pl.pallas_call(
        functools.partial(_ffn_kernel, layer=layer),
        grid_spec=grid_spec,
        out_shape=jax.ShapeDtypeStruct((out_rows or n_blk * tm, d), out_dtype),
        input_output_aliases=aliases,
        compiler_params=_cparams("arbitrary"),
        name=name,
    )(*args)


ROUTE_TM = 512
GROUP_SIZE = N_EXP // N_GROUPS


def _first_argmax(v, idx, axis, sentinel):
    mx = jnp.max(v, axis=axis, keepdims=True)
    return mx, jnp.min(jnp.where(v == mx, idx, sentinel), axis=axis, keepdims=True)


def _route_kernel(lt_ref, bias_ref, idx_ref, w_ref, rank_ref, cnt_ref, carry_ref):
    i = pl.program_id(0)
    tm = lt_ref.shape[1]

    @pl.when(i == 0)
    def _():
        carry_ref[...] = jnp.zeros_like(carry_ref)

    scores = jax.nn.sigmoid(lt_ref[...])
    biased = scores + bias_ref[...]
    b3 = biased.reshape(N_GROUPS, GROUP_SIZE, tm)
    mem = lax.broadcasted_iota(jnp.int32, b3.shape, 1).astype(F32)
    m1, first = _first_argmax(b3, mem, 1, float(GROUP_SIZE))
    m2 = jnp.max(jnp.where(mem == first, -jnp.inf, b3), axis=1, keepdims=True)
    gs = (m1 + m2).reshape(N_GROUPS, tm)
    gi = lax.broadcasted_iota(jnp.int32, gs.shape, 0).astype(F32)
    gsel = jnp.zeros(gs.shape, F32)
    cur = gs
    for _ in range(TOPK_GROUPS):
        _, pick = _first_argmax(cur, gi, 0, float(N_GROUPS))
        hit = gi == pick
        gsel = jnp.where(hit, 1.0, gsel)
        cur = jnp.where(hit, -jnp.inf, cur)
    masked = jnp.where(gsel.reshape(N_GROUPS, 1, tm) > 0.5, b3, -jnp.inf).reshape(N_EXP, tm)

    ei = lax.broadcasted_iota(jnp.int32, masked.shape, 0).astype(F32)
    picks, sel_scores = [], []
    chosen = jnp.zeros(masked.shape, F32)
    cur = masked
    for _ in range(TOP_K):
        _, pick = _first_argmax(cur, ei, 0, float(N_EXP))
        hit = ei == pick
        picks.append(pick)
        sel_scores.append(jnp.sum(jnp.where(hit, scores, 0.0), axis=0, keepdims=True))
        chosen = jnp.where(hit, 1.0, chosen)
        cur = jnp.where(hit, -jnp.inf, cur)

    r, c = _tri_masks(tm)
    before = jnp.where(r < c, 1.0, 0.0).astype(BF16)
    rank = jnp.dot(chosen.astype(BF16), before, preferred_element_type=F32) + carry_ref[...]
    carry_ref[...] = carry_ref[...] + jnp.sum(chosen, axis=1, keepdims=True)
    cnt_ref[...] = carry_ref[...]

    total = sel_scores[0]
    for s in sel_scores[1:]:
        total = total + s
    for k in range(TOP_K):
        idx_ref[k:k + 1, :] = picks[k].astype(jnp.int32)
        w_ref[k:k + 1, :] = sel_scores[k] / total * ROUTED_SCALE
        rank_ref[k:k + 1, :] = jnp.sum(jnp.where(ei == picks[k], rank, 0.0), axis=0, keepdims=True).astype(jnp.int32)


def route(logits_t, router_bias):
    n_e, t = logits_t.shape
    tm = ROUTE_TM
    kt = pl.BlockSpec((TOP_K, tm), lambda i: (0, i))
    return pl.pallas_call(
        _route_kernel,
        grid=(t // tm,),
        in_specs=[pl.BlockSpec((n_e, tm), lambda i: (0, i)), pl.BlockSpec((n_e, 1), lambda i: (0, 0))],
        out_specs=[kt, kt, kt, pl.BlockSpec((n_e, 1), lambda i: (0, 0))],
        out_shape=[jax.ShapeDtypeStruct((TOP_K, t), jnp.int32), jax.ShapeDtypeStruct((TOP_K, t), F32),
                   jax.ShapeDtypeStruct((TOP_K, t), jnp.int32), jax.ShapeDtypeStruct((n_e, 1), F32)],
        scratch_shapes=[pltpu.VMEM((n_e, 1), F32)],
        compiler_params=_cparams("arbitrary"),
        name="route",
    )(logits_t, router_bias.reshape(n_e, 1))


SLOT_MAP_CHUNK = 2048


def _slot_map_kernel(dest_ref, out_ref, *, n_tokens, n_slots):
    i = pl.program_id(0)
    filler_mask = (1 << (n_tokens.bit_length() - 1)) - 1

    @pl.when(i == 0)
    def _():
        def fill(s, carry):
            out_ref[s] = s & filler_mask
            return carry
        lax.fori_loop(0, n_slots, fill, 0, unroll=16)

    base = (i % (n_tokens // SLOT_MAP_CHUNK)) * SLOT_MAP_CHUNK

    def put(j, carry):
        out_ref[dest_ref[j]] = base + j
        return carry
    lax.fori_loop(0, SLOT_MAP_CHUNK, put, 0, unroll=16)


def slot_map(dest_flat, n_tokens, n_slots):
    n = dest_flat.shape[0]
    assert n % SLOT_MAP_CHUNK == 0 and n_tokens % SLOT_MAP_CHUNK == 0
    return pl.pallas_call(
        functools.partial(_slot_map_kernel, n_tokens=n_tokens, n_slots=n_slots),
        grid=(n // SLOT_MAP_CHUNK,),
        in_specs=[pl.BlockSpec((SLOT_MAP_CHUNK,), lambda i: (i,), memory_space=pltpu.SMEM)],
        out_specs=pl.BlockSpec((n_slots,), lambda i: (0,), memory_space=pltpu.SMEM),
        out_shape=jax.ShapeDtypeStruct((n_slots,), jnp.int32),
        compiler_params=_cparams("arbitrary"),
        name="slot_map",
    )(dest_flat)


def moe(h, logits_t, router_bias, e_gate, e_up, e_down, s_gate, s_up, s_down, layer):
    d = h.shape[1]
    t = logits_t.shape[1]
    tm = 256
    top_e, wts, rank, counts = route(logits_t, router_bias)
    counts = counts.reshape(-1).astype(jnp.int32)
    n_assign = t * TOP_K
    padded = (counts + tm - 1) // tm * tm
    pad_end = jnp.cumsum(padded)
    pad_start = pad_end - padded
    n_blk = n_assign // tm + N_EXP
    blk_first = jnp.arange(n_blk, dtype=jnp.int32) * tm
    blk_e = jnp.minimum(jnp.sum((pad_end[None, :] <= blk_first[:, None]).astype(jnp.int32), axis=1), N_EXP - 1)
    n_used = (pad_end[-1] // tm).astype(jnp.int32).reshape(1)
    expert_ids = jnp.arange(N_EXP, dtype=jnp.int32)
    dest = jnp.sum(jnp.where(top_e[..., None] == expert_ids, pad_start, 0), axis=-1) + rank
    slot_tok = slot_map(dest.reshape(-1), t, n_blk * tm)
    blk_ids = jnp.arange(n_blk, dtype=jnp.int32)
    run_end = jnp.sum(jnp.where(blk_e[:, None] >= expert_ids, padded, 0), axis=1) // tm
    run_end_e = jnp.sum(jnp.where(run_end[:, None] == blk_ids, blk_e, 0), axis=1)
    cb = n_blk // MOE_CHUNKS
    ys = None
    for ci in range(MOE_CHUNKS):
        b0, b1 = ci * cb, (ci + 1) * cb
        xs = h.at[slot_tok[b0 * tm:b1 * tm]].get(mode="promise_in_bounds")
        last_blk = jnp.minimum(n_used[0], b1)
        next_e = jnp.where(run_end[b0:b1] < last_blk, run_end_e[b0:b1], -1).astype(jnp.int32)
        ys = expert_ffn(xs, blk_e[b0:b1], next_e, jnp.clip(n_used - b0, 0, cb), e_gate, e_up, e_down, layer, tm=tm,
                        out_rows=n_blk * tm, out_block0=b0, out_buf=ys, name="routed_ffn")
    routed_rows = ys.at[dest.reshape(-1)].get(mode="promise_in_bounds").reshape(TOP_K, t, d)
    tm_sh = 1024
    n_sh = t // tm_sh
    shared = expert_ffn(h, jnp.zeros((n_sh,), jnp.int32), jnp.full((n_sh,), -1, jnp.int32),
                        jnp.full((1,), n_sh, jnp.int32), s_gate[:, None], s_up[:, None], s_down[:, None], layer,
                        tm=tm_sh, name="shared_ffn")
    return shared, routed_rows, wts.T


def kernel(x_prompt, x_sample, state_dn, state_ml_C, state_ml_n, state_ml_m, state_ssd, c, c_ctx,
           mod_w, mod_b, ln1_g, ln1_b, ln2_g, ln2_b, router_w, router_bias, exp_gate, exp_up, exp_down,
           sh_gate, sh_up, sh_down, ev_w_in, ev_conv_w, ev_conv_b, dn_A_log, dn_dt_bias, ml_b_i, ml_b_f,
           dn_norm, ml_norm, ev_w_out, od_w_in, od_conv_w, od_conv_b, ssd_A_log, ssd_dt_bias, ssd_D,
           ssd_norm, od_w_out):
    bp, sl, d = x_prompt.shape
    bl, ll, _ = x_sample.shape
    depth = mod_w.shape[0]
    t_ctx = bp * sl
    t_all = t_ctx + bl * ll
    x = (x_prompt.reshape(t_ctx, d), x_sample.reshape(bl * ll, d))
    cvec = jnp.concatenate([c_ctx[None], c, jnp.zeros((8 - 1 - bl, d), F32)], axis=0)
    mods = compute_mods(cvec, mod_w, mod_b)[:, :1 + bl].reshape(depth, 1 + bl, 6, d)
    geo = dict(t_ctx=t_ctx, lat_len=ll)
    ctx = dict(row0=0, n_seq=bp, seq_len=sl)
    lat = dict(row0=t_ctx, n_seq=bl, seq_len=ll)

    h = modulate(x[0], mods[0], 0, 1, row0=0, t_total=t_all, **geo)
    h = modulate(x[1], mods[0], 0, 1, row0=t_ctx, t_total=t_all, out_buf=h, **geo)
    new_dn, new_c, new_n, new_m, new_ssd = [], [], [], [], []
    for l in range(depth):
        j = l // 2
        if l % 2 == 0:
            w_in = ev_w_in[j]
            proj = matmul(h, w_in, tm=1024, tn=1024, n_out=EV_MAIN, out_dtype=BF16, name="ev_in_proj")
            graw = matmul(h, w_in, tm=1024, tn=LANES, n_out=LANES, col_block_off=EV_MAIN // LANES,
                          valid_cols=EV_GATES, name="ev_gate_proj")
            act, cum = gate_prep(graw, even_gate_params(dn_A_log[j], dn_dt_bias[j], ml_b_i[j], ml_b_f[j]), "even")
            actt, cumt = act.T, cum.T
            cw, cb, dnn, mln = ev_conv_w[j], ev_conv_b[j].reshape(1, -1), dn_norm[j].reshape(1, -1), ml_norm[j].reshape(1, -1)
            oa, s_dn = delta_mixer(proj, cw, cb, act, cum, cumt, dnn, None, period=sl, hb=4, **ctx)
            oa = delta_mixer(proj, cw, cb, act, cum, cumt, dnn, state_dn[:, j], period=GRID_W, hb=1, out_buf=oa, **lat)
            ob, s_c, s_nm = mlstm_mixer(proj, act, actt, cum, cumt, mln, None, **ctx)
            ob = mlstm_mixer(proj, act, actt, cum, cumt, mln,
                             (state_ml_C[:, j], state_ml_n[:, j], state_ml_m[:, j]), out_buf=ob, **lat)
            out_proj = ([oa, ob], ev_w_out[j].astype(BF16))
            new_dn.append(s_dn)
            new_c.append(s_c)
            new_n.append(s_nm[:, :, :, 0, :])
            new_m.append(s_nm[:, :, :, 1, 0])
        else:
            w_in = od_w_in[j]
            proj = matmul(h, w_in, tm=1024, tn=1024, n_out=OD_MAIN, out_dtype=BF16, name="od_in_proj")
            draw = matmul(h, w_in, tm=1024, tn=LANES, n_out=LANES, col_block_off=OD_MAIN // LANES, name="od_dt_proj")
            dt, cum = gate_prep(draw, odd_gate_params(ssd_A_log[j], ssd_dt_bias[j]), "odd")
            cumt = cum.T
            cw, cb = od_conv_w[j], od_conv_b[j].reshape(1, -1)
            dsk, nrm = jnp.repeat(ssd_D[j], P_C).reshape(1, -1), ssd_norm[j].reshape(1, -1)
            oc, s_ssd = ssd_mixer(proj, cw, cb, dt, cum, cumt, dsk, nrm, None, period=sl, **ctx)
            oc = ssd_mixer(proj, cw, cb, dt, cum, cumt, dsk, nrm, state_ssd[:, j], period=GRID_W, out_buf=oc, **lat)
            out_proj = ([oc], od_w_out[j].astype(BF16))
            new_ssd.append(s_ssd)
        x, h2, logits_t = resid_ln(x, [], mods[l], mods[l], ln1_g[l], ln1_b[l], router_w[l], proj=out_proj, tm=512,
                                   gate=2, sh=3, sc=4, want_h=True, h_rows=GATHER_SRC_ROWS, **geo)
        shared, routed_rows, wts = moe(h2, logits_t, router_bias[l], exp_gate, exp_up, exp_down,
                                       sh_gate, sh_up, sh_down, l)
        last = l == depth - 1
        res = resid_ln(x, [shared], mods[l], mods[min(l + 1, depth - 1)], ln2_g[l], ln2_b[l], None,
                       gate=5, sh=0, sc=1, want_h=not last, gathered=(routed_rows, wts), split_out=last, **geo)
        if last:
            x = (res[0], res[1])
        else:
            x, h = res[0], res[1]
    y_prompt = x[0].reshape(bp, sl, d)
    y_sample = x[1].reshape(bl, ll, d)
    return (y_prompt, y_sample, jnp.stack(new_dn, axis=1), jnp.stack(new_c, axis=1), jnp.stack(new_n, axis=1),
            jnp.stack(new_m, axis=1), jnp.stack(new_ssd, axis=1))
```

```python
import functools

import jax
import jax.numpy as jnp
from jax import lax
from jax.experimental import pallas as pl
from jax.experimental.pallas import tpu as pltpu

F32 = jnp.float32
BF16 = jnp.bfloat16

D_MODEL = 2048
DEPTH = 2
GRID_W = 64
ALPHA = (2 * DEPTH) ** 0.25
LN_EPS = 1e-5
RMS_EPS = 1e-6

H_A, DK_A, DV_A = 8, 128, 128
H_B, DK_B, DV_B = 4, 128, 256
CONV_A = 2 * H_A * DK_A + H_A * DV_A
EV_MAIN = CONV_A + H_A * DV_A + 2 * H_B * DK_B + 2 * H_B * DV_B
EV_GATES = 4 * H_A + 4 * H_B

D_INNER = 2 * D_MODEL
P_C, N_C, G_C = 64, 128, 8
H_C = D_INNER // P_C
HG_C = H_C // G_C
GW_C = D_INNER // G_C
OD_MAIN = 2 * D_INNER + 2 * G_C * N_C

N_EXP, TOP_K, N_GROUPS, TOPK_GROUPS = 64, 8, 8, 4
D_EXP = 512
ROUTED_SCALE = 2.5

MOE_CHUNKS = 4
GATHER_SRC_ROWS = 16384
CHUNK = 256
DELTA_CHUNK = 128
SSD_CHUNK = 256
LANES = 128
VMEM_LIMIT = 56 * 1024 * 1024
NEG = -1e30
LOG2E = 1.4426950408889634


def _cparams(*sem):
    return pltpu.CompilerParams(dimension_semantics=sem, vmem_limit_bytes=VMEM_LIMIT)


def _bdot(a, b):
    return jnp.dot(a.astype(BF16), b.astype(BF16), preferred_element_type=F32)


def _bdot_nt(a, b):
    return lax.dot_general(a.astype(BF16), b.astype(BF16), (((1,), (1,)), ((), ())), preferred_element_type=F32)


def _bdot_tn(a, b):
    return lax.dot_general(a.astype(BF16), b.astype(BF16), (((0,), (0,)), ((), ())), preferred_element_type=F32)


def _split3(a):
    hi = a.astype(BF16)
    r = a - hi.astype(F32)
    mid = r.astype(BF16)
    lo = (r - mid.astype(F32)).astype(BF16)
    return hi, mid, lo


def _dot_exact_rhs(a, b_exact, passes=3):
    hi, mid, lo = _split3(a)
    bb = b_exact.astype(BF16)
    d = lambda p: jnp.dot(p, bb, preferred_element_type=F32)
    return d(hi) + d(mid) + d(lo) if passes == 3 else d(hi) + d(mid)


def _dot_exact_lhs(a_exact, b):
    hi, mid, lo = _split3(b)
    aa = a_exact.astype(BF16)
    d = lambda p: jnp.dot(aa, p, preferred_element_type=F32)
    return d(hi) + d(mid) + d(lo)


def _silu(x):
    return x * jax.nn.sigmoid(x)


def _softplus(x):
    return jnp.maximum(x, 0.0) + jnp.log(1.0 + jnp.exp(-jnp.abs(x)))


def _group_of_block(i, tm, t_ctx, lat_len):
    return jnp.maximum(i * tm - t_ctx, -1) // lat_len + 1


def _mod_kernel(c_ref, w_ref, b_ref, o_ref):
    c = c_ref[...]
    o_ref[...] = _bdot(_silu(c), w_ref[...]) + b_ref[...]


def compute_mods(cvec, mod_w, mod_b):
    depth, d, n = mod_w.shape
    tn = 512
    return pl.pallas_call(
        _mod_kernel,
        grid=(depth, n // tn),
        in_specs=[pl.BlockSpec((8, d), lambda l, j: (0, 0)),
                  pl.BlockSpec((None, d, tn), lambda l, j: (l, 0, j)),
                  pl.BlockSpec((None, 1, tn), lambda l, j: (l, 0, j))],
        out_specs=pl.BlockSpec((None, 8, tn), lambda l, j: (l, 0, j)),
        out_shape=jax.ShapeDtypeStruct((depth, 8, n), F32),
        compiler_params=_cparams("arbitrary", "arbitrary"),
        name="mod_vectors",
    )(cvec, mod_w, mod_b.reshape(depth, 1, n))


def _modulate_kernel(x_ref, m_ref, *rest, sh, sc):
    o_ref = rest[-1]
    o_ref[...] = (x_ref[...] * (1.0 + m_ref[sc:sc + 1, :]) + m_ref[sh:sh + 1, :]).astype(o_ref.dtype)


def modulate(x_part, mod, sh, sc, *, row0, t_total, t_ctx, lat_len, out_buf=None):
    tp, d = x_part.shape
    tm = 512
    blk0 = row0 // tm
    in_specs = [pl.BlockSpec((tm, d), lambda i: (i, 0)),
                pl.BlockSpec((None, 6, d), lambda i: (_group_of_block(blk0 + i, tm, t_ctx, lat_len), 0, 0))]
    args = [x_part, mod]
    aliases = _alias_out(in_specs, args, out_buf)
    return pl.pallas_call(
        functools.partial(_modulate_kernel, sh=sh, sc=sc),
        grid=(tp // tm,),
        in_specs=in_specs,
        out_specs=pl.BlockSpec((tm, d), lambda i: (blk0 + i, 0)),
        out_shape=jax.ShapeDtypeStruct((t_total, d), BF16),
        input_output_aliases=aliases,
        compiler_params=_cparams("arbitrary"),
        name="modulate",
    )(*args)


def _matmul_kernel(x_ref, w_ref, o_ref, wbf_ref, *, valid_cols):
    @pl.when(pl.program_id(1) == 0)
    def _():
        wbf_ref[...] = w_ref[...].astype(BF16)

    y = jnp.dot(x_ref[...], wbf_ref[...], preferred_element_type=F32)
    if valid_cols is not None:
        col = lax.broadcasted_iota(jnp.int32, y.shape, 1)
        y = jnp.where(col < valid_cols, y, 0.0)
    o_ref[...] = y.astype(o_ref.dtype)


def matmul(x, w, *, tm, tn, n_out, col_block_off=0, valid_cols=None, out_dtype=F32, name="matmul"):
    m, k = x.shape
    return pl.pallas_call(
        functools.partial(_matmul_kernel, valid_cols=valid_cols),
        grid=(n_out // tn, m // tm),
        in_specs=[pl.BlockSpec((tm, k), lambda j, i: (i, 0)),
                  pl.BlockSpec((k, tn), lambda j, i: (0, j + col_block_off))],
        out_specs=pl.BlockSpec((tm, tn), lambda j, i: (i, j)),
        out_shape=jax.ShapeDtypeStruct((m, n_out), out_dtype),
        scratch_shapes=[pltpu.VMEM((k, tn), BF16)],
        compiler_params=_cparams("arbitrary", "arbitrary"),
        name=name,
    )(x, w)


def _tri_masks(n):
    r = lax.broadcasted_iota(jnp.int32, (n, n), 0)
    c = lax.broadcasted_iota(jnp.int32, (n, n), 1)
    return r, c


def _gate_kernel(raw_ref, p_ref, act_ref, cum_ref, *, mode):
    x = raw_ref[...]
    coef, bias, rev = p_ref[0:1, :], p_ref[1:2, :], p_ref[2:3, :]
    col = lax.broadcasted_iota(jnp.int32, x.shape, 1)
    xb = x + bias
    if mode == "even":
        act = jnp.where(col < 2 * H_A, jax.nn.sigmoid(xb),
                        jnp.where(col < 4 * H_A, coef * _softplus(xb),
                                  jnp.where(col < 4 * H_A + 2 * H_B, xb,
                                            jnp.minimum(xb, 0.0) - jnp.log(1.0 + jnp.exp(-jnp.abs(xb))))))
        to_sum = act
    else:
        act = _softplus(xb)
        to_sum = act * (coef * LOG2E)
    r, c = _tri_masks(CHUNK)
    lower = jnp.where(c <= r, 1.0, 0.0)
    upper = jnp.where(c >= r, 1.0, 0.0)
    if mode == "odd" and SSD_CHUNK != CHUNK:
        same = (r // SSD_CHUNK) == (c // SSD_CHUNK)
        lower, upper = jnp.where(same, lower, 0.0), jnp.where(same, upper, 0.0)
    cum = jnp.where(rev > 0.5, _dot_exact_lhs(upper, to_sum), _dot_exact_lhs(lower, to_sum))
    if mode == "even":
        same = (r // DELTA_CHUNK) == (c // DELTA_CHUNK)
        cum_d = jnp.where(rev > 0.5, _dot_exact_lhs(jnp.where(same, upper, 0.0), to_sum),
                          _dot_exact_lhs(jnp.where(same, lower, 0.0), to_sum))
        cum = jnp.where(col < 4 * H_A, cum_d, cum)
    act_ref[...] = act
    cum_ref[...] = cum


def gate_prep(raw, params, mode):
    t = raw.shape[0]
    return pl.pallas_call(
        functools.partial(_gate_kernel, mode=mode),
        grid=(t // CHUNK,),
        in_specs=[pl.BlockSpec((CHUNK, LANES), lambda i: (i, 0)),
                  pl.BlockSpec((8, LANES), lambda i: (0, 0))],
        out_specs=[pl.BlockSpec((CHUNK, LANES), lambda i: (i, 0))] * 2,
        out_shape=[jax.ShapeDtypeStruct((t, LANES), F32)] * 2,
        compiler_params=_cparams("arbitrary"),
        name="gate_prep_" + mode,
    )(raw, params)


def _conv_silu(x, cw_ref, cb_ref, period):
    n = x.shape[0]
    row = lax.broadcasted_iota(jnp.int32, x.shape, 0) % period
    prev = jnp.where(row == 0, 0.0, pltpu.roll(x, 1, 0))
    nxt = jnp.where(row == period - 1, 0.0, pltpu.roll(x, n - 1, 0))
    y = cb_ref[...] + prev * cw_ref[0:1, :] + x * cw_ref[1:2, :] + nxt * cw_ref[2:3, :]
    return _silu(y)


def _pick_col(blk, idx):
    lane = lax.broadcasted_iota(jnp.int32, blk.shape, 1)
    return jnp.sum(jnp.where(lane == idx, blk, 0.0), axis=1, keepdims=True)


def _dir_masks(rev, n=CHUNK):
    r, c = _tri_masks(n)
    if rev:
        return c >= r, c > r
    return c <= r, c < r


def _tri_inverse(lmat, rev):
    return _tri_inverse_many([lmat], [rev])[0]


def _tri_inverse_many(lmats, revs):
    n = lmats[0].shape[0]
    r, c = _tri_masks(n)
    eye = jnp.where(r == c, 1.0, 0.0)

    def off_mask(s, rev):
        same = (r // (2 * s)) == (c // (2 * s))
        r_hi = (r // s) % 2
        c_hi = (c // s) % 2
        return same & ((r_hi == 0) & (c_hi == 1) if rev else (r_hi == 1) & (c_hi == 0))

    masks = {rev: off_mask(1, rev) for rev in set(revs)}
    ts = [eye - jnp.where(masks[rev], lm, 0.0) for lm, rev in zip(lmats, revs)]
    s = 2
    while s < n:
        masks = {rev: off_mask(s, rev) for rev in set(revs)}
        ps = [_bdot(t, jnp.where(masks[rev], lm, 0.0)) for t, lm, rev in zip(ts, lmats, revs)]
        ts = [t - _bdot(p, t) for p, t in zip(ps, ts)]
        s *= 2
    return ts


def _delta_kernel(*refs, n_chunks, period, has_state, hb):
    (q_ref, k_ref, v_ref, z_ref, cwq, cwk, cwv, cbq, cbk, cbv, act_ref, cum_ref, cumt_ref, norm_ref) = refs[:14]
    rest = refs[14:]
    if has_state:
        s0_ref, *_aliased_out, o_ref, acc_ref = rest
        sout_ref = None
    else:
        o_ref, sout_ref, acc_ref = rest
    h0 = pl.program_id(1) * hb
    dc = DELTA_CHUNK

    q = _conv_silu(q_ref[...].astype(F32), cwq, cbq, period)
    k = _conv_silu(k_ref[...].astype(F32), cwk, cbk, period)
    v = _conv_silu(v_ref[...].astype(F32), cwv, cbv, period)

    triples = [(hh, d, ci) for hh in range(hb) for d in (0, 1) for ci in range(n_chunks)]
    pre = {}
    for hh in range(hb):
        hs = slice(hh * DK_A, (hh + 1) * DK_A)
        qh, kh = q[:, hs], k[:, hs]
        qh = qh * lax.rsqrt(jnp.sum(qh * qh, axis=1, keepdims=True) + RMS_EPS) * (DK_A ** -0.5)
        kh = kh * lax.rsqrt(jnp.sum(kh * kh, axis=1, keepdims=True) + RMS_EPS)
        for d in (0, 1):
            m_incl, m_strict = _dir_masks(d == 1, dc)
            grow_all = cumt_ref[pl.ds(2 * H_A + d * H_A + h0 + hh, 1), :]
            for ci in range(n_chunks):
                sl = slice(ci * dc, (ci + 1) * dc)
                qc, kc, vc = qh[sl], kh[sl], v[sl, hs]
                beta = _pick_col(act_ref[sl, :], d * H_A + h0 + hh)
                gcol = _pick_col(cum_ref[sl, :], 2 * H_A + d * H_A + h0 + hh)
                grow = grow_all[:, sl]
                decay = jnp.exp(jnp.where(m_incl, gcol - grow, NEG))
                kb = kc * beta
                pre[hh, d, ci] = dict(
                    qc=qc, kc=kc, kb=kb, vb=vc * beta, gcol=gcol, decay=decay,
                    lmat=_bdot_nt(kb, kc) * jnp.where(m_strict, decay, 0.0),
                    attn=_bdot_nt(qc, kc) * decay)
    tinvs = _tri_inverse_many([pre[t]["lmat"] for t in triples], [t[1] == 1 for t in triples])
    for t, tinv in zip(triples, tinvs):
        p = pre[t]
        p["u"] = _bdot(tinv, p["vb"])
        if has_state or n_chunks > 1:
            p["w"] = _bdot(tinv, p["kb"] * jnp.exp(p["gcol"]))

    for hh in range(hb):
        hs = slice(hh * DK_A, (hh + 1) * DK_A)
        for d in (0, 1):
            rev = d == 1
            state = s0_ref[d, hh] if has_state else None
            order = range(n_chunks - 1, -1, -1) if rev else range(n_chunks)
            for ci in order:
                sl = slice(ci * dc, (ci + 1) * dc)
                p = pre[hh, d, ci]
                u, gcol = p["u"], p["gcol"]
                if state is not None:
                    u = u - _bdot(p["w"], state)
                o = _bdot(p["attn"], u)
                if state is not None:
                    o = o + _bdot(p["qc"] * jnp.exp(gcol), state)
                glast = gcol[0:1, :] if rev else gcol[dc - 1:dc, :]
                upd = _bdot_tn(p["kc"] * jnp.exp(glast - gcol), u)
                state = upd if state is None else state * jnp.exp(glast) + upd
                if rev:
                    acc_ref[sl, hs] = acc_ref[sl, hs] + o
                else:
                    acc_ref[sl, hs] = o
            if sout_ref is not None:
                sout_ref[d, hh] = state

    z = z_ref[...].astype(F32)
    for hh in range(hb):
        hs = slice(hh * DK_A, (hh + 1) * DK_A)
        o = acc_ref[:, hs]
        o = o * lax.rsqrt(jnp.mean(o * o, axis=1, keepdims=True) + RMS_EPS) * norm_ref[...]
        o_ref[:, hs] = (o * _silu(z[:, hs])).astype(o_ref.dtype)


def _alias_out(in_specs, args, out_buf):
    if out_buf is None:
        return {}
    in_specs.append(pl.BlockSpec(memory_space=pl.ANY))
    args.append(out_buf)
    return {len(args) - 1: 0}


def delta_mixer(proj, conv_w, conv_b, act, cum, cumt, norm, state, *, row0, n_seq, seq_len, period, hb,
                out_buf=None):
    assert row0 % seq_len == 0 and H_A % hb == 0
    n_chunks = seq_len // DELTA_CHUNK
    rb0 = row0 // seq_len
    has_state = state is not None
    w = hb * DK_A
    nq = H_A // hb
    col = lambda off: (lambda s, h: (rb0 + s, off + h))
    cw = lambda off: (lambda s, h: (0, off + h))
    in_specs = [pl.BlockSpec((seq_len, w), col(0)), pl.BlockSpec((seq_len, w), col(nq)),
                pl.BlockSpec((seq_len, w), col(2 * nq)), pl.BlockSpec((seq_len, w), col(3 * nq)),
                pl.BlockSpec((3, w), cw(0)), pl.BlockSpec((3, w), cw(nq)), pl.BlockSpec((3, w), cw(2 * nq)),
                pl.BlockSpec((1, w), cw(0)), pl.BlockSpec((1, w), cw(nq)), pl.BlockSpec((1, w), cw(2 * nq)),
                pl.BlockSpec((seq_len, LANES), lambda s, h: (rb0 + s, 0)),
                pl.BlockSpec((seq_len, LANES), lambda s, h: (rb0 + s, 0)),
                pl.BlockSpec((LANES, seq_len), lambda s, h: (0, rb0 + s)),
                pl.BlockSpec((1, DV_A), lambda s, h: (0, 0))]
    args = [proj, proj, proj, proj, conv_w, conv_w, conv_w, conv_b, conv_b, conv_b, act, cum, cumt, norm]
    o_spec = pl.BlockSpec((seq_len, w), lambda s, h: (rb0 + s, h))
    o_shape = jax.ShapeDtypeStruct((proj.shape[0], H_A * DV_A), BF16)
    st_spec = pl.BlockSpec((None, 2, hb, DK_A, DV_A), lambda s, h: (s, 0, h, 0, 0))
    if has_state:
        in_specs.append(st_spec)
        args.append(state)
        out_specs, out_shape = o_spec, o_shape
    else:
        out_specs = [o_spec, st_spec]
        out_shape = [o_shape, jax.ShapeDtypeStruct((n_seq, 2, H_A, DK_A, DV_A), F32)]
    aliases = _alias_out(in_specs, args, out_buf)
    return pl.pallas_call(
        functools.partial(_delta_kernel, n_chunks=n_chunks, period=period, has_state=has_state, hb=hb),
        grid=(n_seq, H_A // hb),
        in_specs=in_specs, out_specs=out_specs, out_shape=out_shape, input_output_aliases=aliases,
        scratch_shapes=[pltpu.VMEM((seq_len, w), F32)],
        compiler_params=_cparams("arbitrary", "arbitrary"),
        name="delta_lat" if has_state else "delta_ctx",
    )(*args)


def even_gate_params(a_log, dt_bias, b_i, b_f):
    zeros_a = jnp.zeros((2 * H_A,), F32)
    coef = jnp.concatenate([zeros_a, -jnp.exp(a_log.astype(F32)).reshape(-1), jnp.zeros((4 * H_B,), F32)])
    bias = jnp.concatenate([zeros_a, dt_bias.reshape(-1), b_i.reshape(-1), b_f.reshape(-1)]).astype(F32)
    rev = jnp.concatenate([jnp.repeat(jnp.arange(2, dtype=F32), H_A)] * 2 + [jnp.repeat(jnp.arange(2, dtype=F32), H_B)] * 2)
    p = jnp.stack([coef, bias, rev])
    return jnp.pad(p, ((0, 5), (0, LANES - EV_GATES)))


def _mlstm_kernel(*refs, n_chunks, has_state):
    (q_ref, k_ref, v_ref, og_ref, act_ref, actt_ref, cum_ref, cumt_ref, norm_ref) = refs[:9]
    rest = refs[9:]
    if has_state:
        c0_ref, n0_ref, m0_ref, *_aliased_out, o_ref, acc_ref = rest
    else:
        o_ref, cout_ref, nm_ref, acc_ref = rest
    h = pl.program_id(1)
    i_col0, f_col0 = 4 * H_A, 4 * H_A + 2 * H_B

    for d in (0, 1):
        rev = d == 1
        m_incl, _ = _dir_masks(rev)
        if has_state:
            cm, nv, m = c0_ref[d], n0_ref[d], m0_ref[d]
        else:
            cm, nv, m = None, None, jnp.zeros((1, 1), F32)
        order = range(n_chunks - 1, -1, -1) if rev else range(n_chunks)
        for ci in order:
            sl = slice(ci * CHUNK, (ci + 1) * CHUNK)
            qc = q_ref[sl, :].astype(F32) * (DK_B ** -0.5)
            kc = k_ref[sl, :].astype(F32)
            vc = v_ref[sl, :].astype(F32)
            li_col = _pick_col(act_ref[sl, :], i_col0 + d * H_B + h)
            li_row = actt_ref[pl.ds(i_col0 + d * H_B + h, 1), sl]
            b_col = _pick_col(cum_ref[sl, :], f_col0 + d * H_B + h)
            b_row = cumt_ref[pl.ds(f_col0 + d * H_B + h, 1), sl]
            dlog = jnp.where(m_incl, b_col - b_row + li_row, NEG)
            inter = b_col + m
            m_q = jnp.maximum(inter, jnp.max(dlog, axis=1, keepdims=True))
            s = _bdot_nt(qc, kc) * jnp.exp(dlog - m_q)
            num = _bdot(s, vc)
            den = jnp.sum(s, axis=1, keepdims=True)
            if cm is not None:
                w_inter = jnp.exp(inter - m_q)
                num = num + w_inter * _bdot(qc, cm)
                den = den + w_inter * jnp.sum(qc * nv, axis=1, keepdims=True)
            hout = num / jnp.maximum(jnp.abs(den), jnp.exp(-m_q))
            b_last = b_col[0:1, :] if rev else b_col[CHUNK - 1:CHUNK, :]
            wlog = b_last - b_col + li_col
            m_new = jnp.maximum(b_last + m, jnp.max(wlog, axis=0, keepdims=True))
            kw = kc * jnp.exp(wlog - m_new)
            c_upd = _bdot_tn(kw, vc)
            n_upd = jnp.sum(kw, axis=0, keepdims=True)
            if cm is not None:
                sc = jnp.exp(b_last + m - m_new)
                cm, nv = sc * cm + c_upd, sc * nv + n_upd
            else:
                cm, nv = c_upd, n_upd
            m = m_new
            if rev:
                acc_ref[sl, :] = acc_ref[sl, :] + hout
            else:
                acc_ref[sl, :] = hout
        if not has_state:
            cout_ref[d] = cm
            nm_ref[d, 0:1, :] = nv
            nm_ref[d, 1:2, :] = jnp.broadcast_to(m, (1, DK_B))
            nm_ref[d, 2:8, :] = jnp.zeros((6, DK_B), F32)

    o = acc_ref[...]
    o = o * lax.rsqrt(jnp.mean(o * o, axis=1, keepdims=True) + RMS_EPS) * norm_ref[...]
    o_ref[...] = (o * jax.nn.sigmoid(og_ref[...].astype(F32))).astype(o_ref.dtype)


def mlstm_mixer(proj, act, actt, cum, cumt, norm, state, *, row0, n_seq, seq_len, out_buf=None):
    assert row0 % seq_len == 0
    n_chunks = seq_len // CHUNK
    rb0 = row0 // seq_len
    has_state = state is not None
    q0 = (CONV_A + H_A * DV_A) // LANES
    k0 = q0 + H_B
    v0 = (CONV_A + H_A * DV_A + 2 * H_B * DK_B) // DV_B
    o0 = v0 + H_B
    col = lambda off: (lambda s, h: (rb0 + s, off + h))
    in_specs = [pl.BlockSpec((seq_len, DK_B), col(q0)), pl.BlockSpec((seq_len, DK_B), col(k0)),
                pl.BlockSpec((seq_len, DV_B), col(v0)), pl.BlockSpec((seq_len, DV_B), col(o0)),
                pl.BlockSpec((seq_len, LANES), lambda s, h: (rb0 + s, 0)),
                pl.BlockSpec((LANES, seq_len), lambda s, h: (0, rb0 + s)),
                pl.BlockSpec((seq_len, LANES), lambda s, h: (rb0 + s, 0)),
                pl.BlockSpec((LANES, seq_len), lambda s, h: (0, rb0 + s)),
                pl.BlockSpec((1, DV_B), lambda s, h: (0, 0))]
    args = [proj, proj, proj, proj, act, actt, cum, cumt, norm]
    o_spec = pl.BlockSpec((seq_len, DV_B), lambda s, h: (rb0 + s, h))
    o_shape = jax.ShapeDtypeStruct((proj.shape[0], H_B * DV_B), BF16)
    st_idx = lambda s, h: (s, 0, h, 0, 0)
    if has_state:
        c0, n0, m0 = state
        in_specs += [pl.BlockSpec((None, 2, None, DK_B, DV_B), st_idx),
                     pl.BlockSpec((None, 2, None, 1, DK_B), st_idx),
                     pl.BlockSpec((None, 2, None, 1, 1), st_idx)]
        args += [c0, n0.reshape(n_seq, 2, H_B, 1, DK_B), m0.reshape(n_seq, 2, H_B, 1, 1)]
        out_specs, out_shape = o_spec, o_shape
    else:
        out_specs = [o_spec, pl.BlockSpec((None, 2, None, DK_B, DV_B), st_idx),
                     pl.BlockSpec((None, 2, None, 8, DK_B), st_idx)]
        out_shape = [o_shape, jax.ShapeDtypeStruct((n_seq, 2, H_B, DK_B, DV_B), F32),
                     jax.ShapeDtypeStruct((n_seq, 2, H_B, 8, DK_B), F32)]
    aliases = _alias_out(in_specs, args, out_buf)
    return pl.pallas_call(
        functools.partial(_mlstm_kernel, n_chunks=n_chunks, has_state=has_state),
        grid=(n_seq, H_B),
        in_specs=in_specs, out_specs=out_specs, out_shape=out_shape, input_output_aliases=aliases,
        scratch_shapes=[pltpu.VMEM((seq_len, DV_B), F32)],
        compiler_params=_cparams("arbitrary", "arbitrary"),
        name="mlstm_lat" if has_state else "mlstm_ctx",
    )(*args)


def odd_gate_params(a_log, dt_bias):
    coef = -jnp.exp(a_log.astype(F32)).reshape(-1)
    bias = dt_bias.astype(F32).reshape(-1)
    rev = jnp.repeat(jnp.arange(2, dtype=F32), H_C)
    return jnp.pad(jnp.stack([coef, bias, rev]), ((0, 5), (0, 0)))


def _ssd_kernel(*refs, n_chunks, period, has_state):
    (z_ref, x_ref, b_ref, c_ref, cwx, cwb, cwc, cbx, cbb, cbc, dt_ref, cum_ref, cumt_ref, dskip_ref, norm_ref) = refs[:15]
    rest = refs[15:]
    if has_state:
        s0_ref, *_aliased_out, o_ref, acc_ref = rest
        sout_ref = None
    else:
        o_ref, sout_ref, acc_ref = rest
    g = pl.program_id(1)

    x = _conv_silu(x_ref[...].astype(F32), cwx, cbx, period)
    bm = _conv_silu(b_ref[...].astype(F32), cwb, cbb, period)
    cm = _conv_silu(c_ref[...].astype(F32), cwc, cbc, period)

    er = lax.broadcasted_iota(jnp.int32, (LANES, GW_C), 0)
    ec = lax.broadcasted_iota(jnp.int32, (LANES, GW_C), 1)
    tr = lax.broadcasted_iota(jnp.int32, (GW_C, LANES), 0)
    tc = lax.broadcasted_iota(jnp.int32, (GW_C, LANES), 1)
    sc_ = SSD_CHUNK
    lane_in_tile = lax.broadcasted_iota(jnp.int32, (sc_, LANES), 1)

    for d in (0, 1):
        rev = d == 1
        m_incl, _ = _dir_masks(rev, sc_)
        lo, hi, full = slice(0, sc_ // 2), slice(sc_ // 2, sc_), slice(0, sc_)
        if sc_ // 2 >= LANES:
            half_blocks = ((lo, full), (hi, hi)) if rev else ((lo, lo), (hi, full))
        else:
            half_blocks = ((full, full),)
        col0 = d * H_C + g * HG_C
        expand = jnp.where(er == col0 + ec // P_C, 1.0, 0.0)
        expand_t = tc == col0 + tr // P_C
        state = s0_ref[d].reshape(GW_C, N_C) if has_state else None
        order = range(n_chunks - 1, -1, -1) if rev else range(n_chunks)
        crows = [cumt_ref[pl.ds(col0 + hh, 1), :] for hh in range(HG_C)]
        for ci in order:
            sl = slice(ci * sc_, (ci + 1) * sc_)
            xc, bc, cc = x[sl], bm[sl], cm[sl]
            cum_blk = cum_ref[sl, :]
            cum_last = cum_blk[0:1, :] if rev else cum_blk[sc_ - 1:sc_, :]
            xdt = xc * _dot_exact_rhs(dt_ref[sl, :], expand, passes=2)
            scores_bf = _bdot_nt(cc, bc).astype(BF16)
            for hp in range(HG_C // 2):
                ps = slice(hp * LANES, (hp + 1) * LANES)
                xpair = xdt[:, ps]
                ypair = None
                for sub in (0, 1):
                    hh = 2 * hp + sub
                    cb = _pick_col(cum_blk, col0 + hh)
                    crow = crows[hh][:, sl]
                    mine = (lane_in_tile < P_C) if sub == 0 else (lane_in_tile >= P_C)
                    rhs = jnp.where(mine, xpair, 0.0)
                    parts = []
                    for rows, cols in half_blocks:
                        seg = jnp.exp2(jnp.where(m_incl[rows, cols], cb[rows] - crow[:, cols], NEG))
                        parts.append(_bdot(scores_bf[rows, cols] * seg.astype(BF16), rhs[cols]))
                    y = jnp.concatenate(parts, axis=0)
                    ypair = y if ypair is None else ypair + y
                if rev:
                    acc_ref[sl, ps] = acc_ref[sl, ps] + ypair
                else:
                    acc_ref[sl, ps] = ypair
            if state is not None:
                y_in = _bdot_nt(cc, state) * _dot_exact_rhs(jnp.exp2(cum_blk), expand, passes=2)
                acc_ref[sl, :] = acc_ref[sl, :] + y_in
            dend = _dot_exact_rhs(jnp.exp2(jnp.minimum(cum_last - cum_blk, 0.0)), expand, passes=2)
            upd = _bdot_tn(xdt * dend, bc)
            if state is not None:
                tot = jnp.sum(jnp.where(expand_t, jnp.broadcast_to(cum_last, (GW_C, LANES)), 0.0), axis=1, keepdims=True)
                state = state * jnp.exp2(tot) + upd
            else:
                state = upd
        if sout_ref is not None:
            sout_ref[d] = state.reshape(HG_C, P_C, N_C)

    y = acc_ref[...] + dskip_ref[...] * x
    y = y * _silu(z_ref[...].astype(F32))
    y = y * lax.rsqrt(jnp.mean(y * y, axis=1, keepdims=True) + RMS_EPS) * norm_ref[...]
    o_ref[...] = y.astype(o_ref.dtype)


def ssd_mixer(proj, conv_w, conv_b, dt, cum, cumt, dskip, norm, state, *, row0, n_seq, seq_len, period, out_buf=None):
    assert row0 % seq_len == 0
    n_chunks = seq_len // SSD_CHUNK
    rb0 = row0 // seq_len
    has_state = state is not None
    xb0 = D_INNER // GW_C
    bb0 = 2 * D_INNER // N_C
    cb0 = bb0 + G_C
    wb0 = D_INNER // N_C
    wc0 = wb0 + G_C
    col = lambda off: (lambda s, g: (rb0 + s, off + g))
    cw = lambda off: (lambda s, g: (0, off + g))
    in_specs = [pl.BlockSpec((seq_len, GW_C), col(0)), pl.BlockSpec((seq_len, GW_C), col(xb0)),
                pl.BlockSpec((seq_len, N_C), col(bb0)), pl.BlockSpec((seq_len, N_C), col(cb0)),
                pl.BlockSpec((3, GW_C), cw(0)), pl.BlockSpec((3, N_C), cw(wb0)), pl.BlockSpec((3, N_C), cw(wc0)),
                pl.BlockSpec((1, GW_C), cw(0)), pl.BlockSpec((1, N_C), cw(wb0)), pl.BlockSpec((1, N_C), cw(wc0)),
                pl.BlockSpec((seq_len, LANES), lambda s, g: (rb0 + s, 0)),
                pl.BlockSpec((seq_len, LANES), lambda s, g: (rb0 + s, 0)),
                pl.BlockSpec((LANES, seq_len), lambda s, g: (0, rb0 + s)),
                pl.BlockSpec((1, GW_C), cw(0)), pl.BlockSpec((1, GW_C), cw(0))]
    args = [proj, proj, proj, proj, conv_w, conv_w, conv_w, conv_b, conv_b, conv_b, dt, cum, cumt, dskip, norm]
    o_spec = pl.BlockSpec((seq_len, GW_C), lambda s, g: (rb0 + s, g))
    o_shape = jax.ShapeDtypeStruct((proj.shape[0], D_INNER), BF16)
    st_spec = pl.BlockSpec((None, 2, HG_C, P_C, N_C), lambda s, g: (s, 0, g, 0, 0))
    if has_state:
        in_specs.append(st_spec)
        args.append(state)
        out_specs, out_shape = o_spec, o_shape
    else:
        out_specs = [o_spec, st_spec]
        out_shape = [o_shape, jax.ShapeDtypeStruct((n_seq, 2, H_C, P_C, N_C), F32)]
    aliases = _alias_out(in_specs, args, out_buf)
    return pl.pallas_call(
        functools.partial(_ssd_kernel, n_chunks=n_chunks, period=period, has_state=has_state),
        grid=(n_seq, G_C),
        in_specs=in_specs, out_specs=out_specs, out_shape=out_shape, input_output_aliases=aliases,
        scratch_shapes=[pltpu.VMEM((seq_len, GW_C), F32)],
        compiler_params=_cparams("arbitrary", "arbitrary"),
        name="ssd_lat" if has_state else "ssd_ctx",
    )(*args)


def _dot3(a, b):
    a_hi = a.astype(BF16)
    a_lo = (a - a_hi.astype(F32)).astype(BF16)
    b_hi = b.astype(BF16)
    b_lo = (b - b_hi.astype(F32)).astype(BF16)
    d = lambda p, q: jnp.dot(p, q, preferred_element_type=F32)
    return d(a_hi, b_hi) + (d(a_hi, b_lo) + d(a_lo, b_hi))


def _dot3_nt(a, b):
    a_hi = a.astype(BF16)
    a_lo = (a - a_hi.astype(F32)).astype(BF16)
    b_hi = b.astype(BF16)
    b_lo = (b - b_hi.astype(F32)).astype(BF16)
    d = lambda p, q: lax.dot_general(p, q, (((1,), (1,)), ((), ())), preferred_element_type=F32)
    return d(a_hi, b_hi) + (d(a_hi, b_lo) + d(a_lo, b_hi))


def _resid_ln_kernel(*refs, n_y, lhs_widths, n_gathered, gate, sh, sc, want_h, want_logits, split_in, split_out,
                     n_ctx_blocks):
    in_ctx = pl.program_id(0) < n_ctx_blocks
    if split_in:
        x_in = jnp.where(in_ctx, refs[0][...], refs[1][...])
        refs = refs[1:]
    else:
        x_in = refs[0][...]
    y_refs = refs[1:1 + n_y]
    rest = list(refs[1 + n_y:])
    lhs_refs = [rest.pop(0) for _ in lhs_widths]
    pw_ref = rest.pop(0) if lhs_widths else None
    m_ref, mn_ref, g_ref, b_ref = (rest.pop(0) for _ in range(4))
    if n_gathered:
        gath_ref, gw_ref = rest.pop(0), rest.pop(0)
    rw_ref = rest.pop(0) if want_logits else None
    xo_refs = [rest.pop(0) for _ in range(2 if split_out else 1)]
    y = None
    for r in y_refs:
        y = r[...].astype(F32) if y is None else y + r[...].astype(F32)
    off = 0
    for lhs_ref, width in zip(lhs_refs, lhs_widths):
        part = jnp.dot(lhs_ref[...], pw_ref[off:off + width, :], preferred_element_type=F32)
        y = part if y is None else y + part
        off += width
    for kk in range(n_gathered):
        y = y + gw_ref[:, kk:kk + 1] * gath_ref[kk].astype(F32)
    v = ALPHA * x_in + m_ref[gate:gate + 1, :] * y
    mu = jnp.mean(v, axis=1, keepdims=True)
    vc = v - mu
    var = jnp.mean(vc * vc, axis=1, keepdims=True)
    xn = vc * lax.rsqrt(var + LN_EPS) * g_ref[...] + b_ref[...]
    if split_out:
        @pl.when(in_ctx)
        def _():
            xo_refs[0][...] = xn

        @pl.when(jnp.logical_not(in_ctx))
        def _():
            xo_refs[1][...] = xn
    else:
        xo_refs[0][...] = xn
    if want_h:
        hm = xn * (1.0 + mn_ref[sc:sc + 1, :]) + mn_ref[sh:sh + 1, :]
        rest.pop(0)[...] = hm.astype(BF16)
        if want_logits:
            rest.pop(0)[...] = _dot3_nt(rw_ref[...], hm)


def resid_ln(x, ys, mod, mod_next, ln_g, ln_b, router_w, *, gate, sh, sc, want_h, t_ctx, lat_len, gathered=None,
             h_rows=None, proj=None, tm=256, split_out=False):
    split_in = isinstance(x, tuple)
    d = x[0].shape[1] if split_in else x.shape[1]
    t = x[0].shape[0] + x[1].shape[0] if split_in else x.shape[0]
    n_ctx_blocks = t_ctx // tm
    ctx_blk = lambda i: (jnp.minimum(i, n_ctx_blocks - 1), 0)
    lat_blk = lambda i: (jnp.maximum(i - n_ctx_blocks, 0), 0)
    want_logits = router_w is not None
    grp = lambda i: (_group_of_block(i, tm, t_ctx, lat_len), 0, 0)
    row = pl.BlockSpec((tm, d), lambda i: (i, 0))
    vec = pl.BlockSpec((1, d), lambda i: (0, 0))
    if split_in:
        in_specs = [pl.BlockSpec((tm, d), ctx_blk), pl.BlockSpec((tm, d), lat_blk)] + [row] * len(ys)
        args = [*x, *ys]
    else:
        in_specs = [row] * (1 + len(ys))
        args = [x, *ys]
    lhs_widths = ()
    if proj is not None:
        lhs_list, pw = proj
        lhs_widths = tuple(a.shape[1] for a in lhs_list)
        in_specs += [pl.BlockSpec((tm, wd), lambda i: (i, 0)) for wd in lhs_widths]
        in_specs.append(pl.BlockSpec(pw.shape, lambda i: (0, 0), pipeline_mode=pl.Buffered(1)))
        args += [*lhs_list, pw]
    in_specs += [pl.BlockSpec((None, 6, d), grp), pl.BlockSpec((None, 6, d), grp), vec, vec]
    args += [mod, mod_next, ln_g.reshape(1, d), ln_b.reshape(1, d)]
    n_gathered = 0
    if gathered is not None:
        n_gathered = gathered[0].shape[0]
        in_specs += [pl.BlockSpec((n_gathered, tm, d), lambda i: (0, i, 0)),
                     pl.BlockSpec((tm, n_gathered), lambda i: (i, 0))]
        args += list(gathered)
    if split_out:
        out_specs = [pl.BlockSpec((tm, d), ctx_blk), pl.BlockSpec((tm, d), lat_blk)]
        out_shape = [jax.ShapeDtypeStruct((t_ctx, d), F32), jax.ShapeDtypeStruct((t - t_ctx, d), F32)]
    else:
        out_specs, out_shape = [row], [jax.ShapeDtypeStruct((t, d), F32)]
    if want_logits:
        n_e = router_w.shape[1]
        in_specs.append(pl.BlockSpec((n_e, d), lambda i: (0, 0)))
        args.append(router_w.T)
    if want_h:
        out_specs.append(row)
        out_shape.append(jax.ShapeDtypeStruct((h_rows or t, d), BF16))
    if want_logits:
        out_specs.append(pl.BlockSpec((n_e, tm), lambda i: (0, i)))
        out_shape.append(jax.ShapeDtypeStruct((n_e, t), F32))
    return pl.pallas_call(
        functools.partial(_resid_ln_kernel, n_y=len(ys), lhs_widths=lhs_widths, n_gathered=n_gathered, gate=gate, sh=sh,
                          sc=sc, want_h=want_h, want_logits=want_logits, split_in=split_in, split_out=split_out,
                          n_ctx_blocks=n_ctx_blocks),
        grid=(t // tm,),
        in_specs=in_specs, out_specs=out_specs, out_shape=out_shape,
        compiler_params=_cparams("arbitrary"),
        name="resid_ln",
    )(*args)


def _ffn_kernel(be_ref, nx_ref, nu_ref, x_ref, wg_hbm, wu_hbm, wd_hbm, *rest, layer):
    *_aliased_out, o_ref, g_f32, u_f32, d_f32, g_bf, u_bf, d_bf, sem = rest
    b = pl.program_id(0)

    def copies(e):
        return (pltpu.make_async_copy(wg_hbm.at[layer, e], g_f32, sem.at[0]),
                pltpu.make_async_copy(wu_hbm.at[layer, e], u_f32, sem.at[1]),
                pltpu.make_async_copy(wd_hbm.at[layer, e], d_f32, sem.at[2]))

    @pl.when(b < nu_ref[0])
    def _():
        e = be_ref[b]

        @pl.when(b == 0)
        def _():
            for cp in copies(e):
                cp.start()

        @pl.when((b == 0) | (e != be_ref[jnp.maximum(b - 1, 0)]))
        def _():
            for cp in copies(e):
                cp.wait()
            g_bf[...] = g_f32[...].astype(BF16)
            u_bf[...] = u_f32[...].astype(BF16)
            d_bf[...] = d_f32[...].astype(BF16)
            nxt = nx_ref[b]

            @pl.when(nxt >= 0)
            def _():
                for cp in copies(nxt):
                    cp.start()

        x = x_ref[...]
        hg = jnp.dot(x, g_bf[...], preferred_element_type=F32)
        hu = jnp.dot(x, u_bf[...], preferred_element_type=F32)
        a = (_silu(hg) * hu).astype(BF16)
        o_ref[...] = jnp.dot(a, d_bf[...], preferred_element_type=F32).astype(o_ref.dtype)


def expert_ffn(xs, blk_e, next_e, n_used, w_gate, w_up, w_down, layer, *, tm, out_rows=None, out_block0=0,
               out_buf=None, out_dtype=BF16, name="expert_ffn"):
    d = xs.shape[1]
    de = w_gate.shape[3]
    n_blk = blk_e.shape[0]
    in_specs = [pl.BlockSpec((tm, d), lambda b, be, nx, nu: (b, 0))] + [pl.BlockSpec(memory_space=pl.ANY)] * 3
    args = [blk_e, next_e, n_used, xs, w_gate, w_up, w_down]
    aliases = _alias_out(in_specs, args, out_buf)
    grid_spec = pltpu.PrefetchScalarGridSpec(
        num_scalar_prefetch=3,
        grid=(n_blk,),
        in_specs=in_specs,
        out_specs=pl.BlockSpec((tm, d), lambda b, be, nx, nu: (out_block0 + b, 0)),
        scratch_shapes=[pltpu.VMEM((d, de), F32), pltpu.VMEM((d, de), F32), pltpu.VMEM((de, d), F32),
                        pltpu.VMEM((d, de), BF16), pltpu.VMEM((d, de), BF16), pltpu.VMEM((de, d), BF16),
                        pltpu.SemaphoreType.DMA((3,))],
    )
    return pl.pallas_call(
        functools.partial(_ffn_kernel, layer=layer),
        grid_spec=grid_spec,
        out_shape=jax.ShapeDtypeStruct((out_rows or n_blk * tm, d), out_dtype),
        input_output_aliases=aliases,
        compiler_params=_cparams("arbitrary"),
        name=name,
    )(*args)


ROUTE_TM = 512
GROUP_SIZE = N_EXP // N_GROUPS


def _first_argmax(v, idx, axis, sentinel):
    mx = jnp.max(v, axis=axis, keepdims=True)
    return mx, jnp.min(jnp.where(v == mx, idx, sentinel), axis=axis, keepdims=True)


def _route_kernel(lt_ref, bias_ref, idx_ref, w_ref, rank_ref, cnt_ref, carry_ref):
    i = pl.program_id(0)
    tm = lt_ref.shape[1]

    @pl.when(i == 0)
    def _():
        carry_ref[...] = jnp.zeros_like(carry_ref)

    scores = jax.nn.sigmoid(lt_ref[...])
    biased = scores + bias_ref[...]
    b3 = biased.reshape(N_GROUPS, GROUP_SIZE, tm)
    mem = lax.broadcasted_iota(jnp.int32, b3.shape, 1).astype(F32)
    m1, first = _first_argmax(b3, mem, 1, float(GROUP_SIZE))
    m2 = jnp.max(jnp.where(mem == first, -jnp.inf, b3), axis=1, keepdims=True)
    gs = (m1 + m2).reshape(N_GROUPS, tm)
    gi = lax.broadcasted_iota(jnp.int32, gs.shape, 0).astype(F32)
    gsel = jnp.zeros(gs.shape, F32)
    cur = gs
    for _ in range(TOPK_GROUPS):
        _, pick = _first_argmax(cur, gi, 0, float(N_GROUPS))
        hit = gi == pick
        gsel = jnp.where(hit, 1.0, gsel)
        cur = jnp.where(hit, -jnp.inf, cur)
    masked = jnp.where(gsel.reshape(N_GROUPS, 1, tm) > 0.5, b3, -jnp.inf).reshape(N_EXP, tm)

    ei = lax.broadcasted_iota(jnp.int32, masked.shape, 0).astype(F32)
    picks, sel_scores = [], []
    chosen = jnp.zeros(masked.shape, F32)
    cur = masked
    for _ in range(TOP_K):
        _, pick = _first_argmax(cur, ei, 0, float(N_EXP))
        hit = ei == pick
        picks.append(pick)
        sel_scores.append(jnp.sum(jnp.where(hit, scores, 0.0), axis=0, keepdims=True))
        chosen = jnp.where(hit, 1.0, chosen)
        cur = jnp.where(hit, -jnp.inf, cur)

    r, c = _tri_masks(tm)
    before = jnp.where(r < c, 1.0, 0.0).astype(BF16)
    rank = jnp.dot(chosen.astype(BF16), before, preferred_element_type=F32) + carry_ref[...]
    carry_ref[...] = carry_ref[...] + jnp.sum(chosen, axis=1, keepdims=True)
    cnt_ref[...] = carry_ref[...]

    total = sel_scores[0]
    for s in sel_scores[1:]:
        total = total + s
    for k in range(TOP_K):
        idx_ref[k:k + 1, :] = picks[k].astype(jnp.int32)
        w_ref[k:k + 1, :] = sel_scores[k] / total * ROUTED_SCALE
        rank_ref[k:k + 1, :] = jnp.sum(jnp.where(ei == picks[k], rank, 0.0), axis=0, keepdims=True).astype(jnp.int32)


def route(logits_t, router_bias):
    n_e, t = logits_t.shape
    tm = ROUTE_TM
    kt = pl.BlockSpec((TOP_K, tm), lambda i: (0, i))
    return pl.pallas_call(
        _route_kernel,
        grid=(t // tm,),
        in_specs=[pl.BlockSpec((n_e, tm), lambda i: (0, i)), pl.BlockSpec((n_e, 1), lambda i: (0, 0))],
        out_specs=[kt, kt, kt, pl.BlockSpec((n_e, 1), lambda i: (0, 0))],
        out_shape=[jax.ShapeDtypeStruct((TOP_K, t), jnp.int32), jax.ShapeDtypeStruct((TOP_K, t), F32),
                   jax.ShapeDtypeStruct((TOP_K, t), jnp.int32), jax.ShapeDtypeStruct((n_e, 1), F32)],
        scratch_shapes=[pltpu.VMEM((n_e, 1), F32)],
        compiler_params=_cparams("arbitrary"),
        name="route",
    )(logits_t, router_bias.reshape(n_e, 1))


SLOT_MAP_CHUNK = 2048


def _slot_map_kernel(dest_ref, out_ref, *, n_tokens, n_slots):
    i = pl.program_id(0)
    filler_mask = (1 << (n_tokens.bit_length() - 1)) - 1

    @pl.when(i == 0)
    def _():
        def fill(s, carry):
            out_ref[s] = s & filler_mask
            return carry
        lax.fori_loop(0, n_slots, fill, 0, unroll=16)

    base = (i % (n_tokens // SLOT_MAP_CHUNK)) * SLOT_MAP_CHUNK

    def put(j, carry):
        out_ref[dest_ref[j]] = base + j
        return carry
    lax.fori_loop(0, SLOT_MAP_CHUNK, put, 0, unroll=16)


def slot_map(dest_flat, n_tokens, n_slots):
    n = dest_flat.shape[0]
    assert n % SLOT_MAP_CHUNK == 0 and n_tokens % SLOT_MAP_CHUNK == 0
    return pl.pallas_call(
        functools.partial(_slot_map_kernel, n_tokens=n_tokens, n_slots=n_slots),
        grid=(n // SLOT_MAP_CHUNK,),
        in_specs=[pl.BlockSpec((SLOT_MAP_CHUNK,), lambda i: (i,), memory_space=pltpu.SMEM)],
        out_specs=pl.BlockSpec((n_slots,), lambda i: (0,), memory_space=pltpu.SMEM),
        out_shape=jax.ShapeDtypeStruct((n_slots,), jnp.int32),
        compiler_params=_cparams("arbitrary"),
        name="slot_map",
    )(dest_flat)


def moe(h, logits_t, router_bias, e_gate, e_up, e_down, s_gate, s_up, s_down, layer):
    d = h.shape[1]
    t = logits_t.shape[1]
    tm = 256
    top_e, wts, rank, counts = route(logits_t, router_bias)
    counts = counts.reshape(-1).astype(jnp.int32)
    n_assign = t * TOP_K
    padded = (counts + tm - 1) // tm * tm
    pad_end = jnp.cumsum(padded)
    pad_start = pad_end - padded
    n_blk = n_assign // tm + N_EXP
    blk_first = jnp.arange(n_blk, dtype=jnp.int32) * tm
    blk_e = jnp.minimum(jnp.sum((pad_end[None, :] <= blk_first[:, None]).astype(jnp.int32), axis=1), N_EXP - 1)
    n_used = (pad_end[-1] // tm).astype(jnp.int32).reshape(1)
    expert_ids = jnp.arange(N_EXP, dtype=jnp.int32)
    dest = jnp.sum(jnp.where(top_e[..., None] == expert_ids, pad_start, 0), axis=-1) + rank
    slot_tok = slot_map(dest.reshape(-1), t, n_blk * tm)
    blk_ids = jnp.arange(n_blk, dtype=jnp.int32)
    run_end = jnp.sum(jnp.where(blk_e[:, None] >= expert_ids, padded, 0), axis=1) // tm
    run_end_e = jnp.sum(jnp.where(run_end[:, None] == blk_ids, blk_e, 0), axis=1)
    cb = n_blk // MOE_CHUNKS
    ys = None
    for ci in range(MOE_CHUNKS):
        b0, b1 = ci * cb, (ci + 1) * cb
        xs = h.at[slot_tok[b0 * tm:b1 * tm]].get(mode="promise_in_bounds")
        last_blk = jnp.minimum(n_used[0], b1)
        next_e = jnp.where(run_end[b0:b1] < last_blk, run_end_e[b0:b1], -1).astype(jnp.int32)
        ys = expert_ffn(xs, blk_e[b0:b1], next_e, jnp.clip(n_used - b0, 0, cb), e_gate, e_up, e_down, layer, tm=tm,
                        out_rows=n_blk * tm, out_block0=b0, out_buf=ys, name="routed_ffn")
    routed_rows = ys.at[dest.reshape(-1)].get(mode="promise_in_bounds").reshape(TOP_K, t, d)
    tm_sh = 1024
    n_sh = t // tm_sh
    shared = expert_ffn(h, jnp.zeros((n_sh,), jnp.int32), jnp.full((n_sh,), -1, jnp.int32),
                        jnp.full((1,), n_sh, jnp.int32), s_gate[:, None], s_up[:, None], s_down[:, None], layer,
                        tm=tm_sh, name="shared_ffn")
    return shared, routed_rows, wts.T


def kernel(x_prompt, x_sample, state_dn, state_ml_C, state_ml_n, state_ml_m, state_ssd, c, c_ctx,
           mod_w, mod_b, ln1_g, ln1_b, ln2_g, ln2_b, router_w, router_bias, exp_gate, exp_up, exp_down,
           sh_gate, sh_up, sh_down, ev_w_in, ev_conv_w, ev_conv_b, dn_A_log, dn_dt_bias, ml_b_i, ml_b_f,
           dn_norm, ml_norm, ev_w_out, od_w_in, od_conv_w, od_conv_b, ssd_A_log, ssd_dt_bias, ssd_D,
           ssd_norm, od_w_out):
    bp, sl, d = x_prompt.shape
    bl, ll, _ = x_sample.shape
    depth = mod_w.shape[0]
    t_ctx = bp * sl
    t_all = t_ctx + bl * ll
    x = (x_prompt.reshape(t_ctx, d), x_sample.reshape(bl * ll, d))
    cvec = jnp.concatenate([c_ctx[None], c, jnp.zeros((8 - 1 - bl, d), F32)], axis=0)
    mods = compute_mods(cvec, mod_w, mod_b)[:, :1 + bl].reshape(depth, 1 + bl, 6, d)
    geo = dict(t_ctx=t_ctx, lat_len=ll)
    ctx = dict(row0=0, n_seq=bp, seq_len=sl)
    lat = dict(row0=t_ctx, n_seq=bl, seq_len=ll)

    h = modulate(x[0], mods[0], 0, 1, row0=0, t_total=t_all, **geo)
    h = modulate(x[1], mods[0], 0, 1, row0=t_ctx, t_total=t_all, out_buf=h, **geo)
    new_dn, new_c, new_n, new_m, new_ssd = [], [], [], [], []
    for l in range(depth):
        j = l // 2
        if l % 2 == 0:
            w_in = ev_w_in[j]
            proj = matmul(h, w_in, tm=1024, tn=1024, n_out=EV_MAIN, out_dtype=BF16, name="ev_in_proj")
            graw = matmul(h, w_in, tm=1024, tn=LANES, n_out=LANES, col_block_off=EV_MAIN // LANES,
                          valid_cols=EV_GATES, name="ev_gate_proj")
            act, cum = gate_prep(graw, even_gate_params(dn_A_log[j], dn_dt_bias[j], ml_b_i[j], ml_b_f[j]), "even")
            actt, cumt = act.T, cum.T
            cw, cb, dnn, mln = ev_conv_w[j], ev_conv_b[j].reshape(1, -1), dn_norm[j].reshape(1, -1), ml_norm[j].reshape(1, -1)
            oa, s_dn = delta_mixer(proj, cw, cb, act, cum, cumt, dnn, None, period=sl, hb=4, **ctx)
            oa = delta_mixer(proj, cw, cb, act, cum, cumt, dnn, state_dn[:, j], period=GRID_W, hb=1, out_buf=oa, **lat)
            ob, s_c, s_nm = mlstm_mixer(proj, act, actt, cum, cumt, mln, None, **ctx)
            ob = mlstm_mixer(proj, act, actt, cum, cumt, mln,
                             (state_ml_C[:, j], state_ml_n[:, j], state_ml_m[:, j]), out_buf=ob, **lat)
            out_proj = ([oa, ob], ev_w_out[j].astype(BF16))
            new_dn.append(s_dn)
            new_c.append(s_c)
            new_n.append(s_nm[:, :, :, 0, :])
            new_m.append(s_nm[:, :, :, 1, 0])
        else:
            w_in = od_w_in[j]
            proj = matmul(h, w_in, tm=1024, tn=1024, n_out=OD_MAIN, out_dtype=BF16, name="od_in_proj")
            draw = matmul(h, w_in, tm=1024, tn=LANES, n_out=LANES, col_block_off=OD_MAIN // LANES, name="od_dt_proj")
            dt, cum = gate_prep(draw, odd_gate_params(ssd_A_log[j], ssd_dt_bias[j]), "odd")
            cumt = cum.T
            cw, cb = od_conv_w[j], od_conv_b[j].reshape(1, -1)
            dsk, nrm = jnp.repeat(ssd_D[j], P_C).reshape(1, -1), ssd_norm[j].reshape(1, -1)
            oc, s_ssd = ssd_mixer(proj, cw, cb, dt, cum, cumt, dsk, nrm, None, period=sl, **ctx)
            oc = ssd_mixer(proj, cw, cb, dt, cum, cumt, dsk, nrm, state_ssd[:, j], period=GRID_W, out_buf=oc, **lat)
            out_proj = ([oc], od_w_out[j].astype(BF16))
            new_ssd.append(s_ssd)
        x, h2, logits_t = resid_ln(x, [], mods[l], mods[l], ln1_g[l], ln1_b[l], router_w[l], proj=out_proj, tm=512,
                                   gate=2, sh=3, sc=4, want_h=True, h_rows=GATHER_SRC_ROWS, **geo)
        shared, routed_rows, wts = moe(h2, logits_t, router_bias[l], exp_gate, exp_up, exp_down,
                                       sh_gate, sh_up, sh_down, l)
        last = l == depth - 1
        res = resid_ln(x, [shared], mods[l], mods[min(l + 1, depth - 1)], ln2_g[l], ln2_b[l], None,
                       gate=5, sh=0, sc=1, want_h=not last, gathered=(routed_rows, wts), split_out=last, **geo)
        if last:
            x = (res[0], res[1])
        else:
            x, h = res[0], res[1]
    y_prompt = x[0].reshape(bp, sl, d)
    y_sample = x[1].reshape(bl, ll, d)
    return (y_prompt, y_sample, jnp.stack(new_dn, axis=1), jnp.stack(new_c, axis=1), jnp.stack(new_n, axis=1),
            jnp.stack(new_m, axis=1), jnp.stack(new_ssd, axis=1))
```
